```python
import jax, jax.numpy as jnp
from jax import lax
import numpy as np

D_MODEL = 2048
BATCH = 8
SEQ = 8192
DEPTH = 4

CHUNK = 64

D_MIX = D_MODEL
D_CONV = D_MIX // 2
D_LRU = D_MIX - D_CONV
CONV_WIDTH = 31
LRU_CONV_WIDTH = 4
LRU_HEADS = 8
LRU_HEAD_DIM = D_LRU // LRU_HEADS
LRU_C = 8.0
D_IN = 2 * D_CONV + D_CONV + D_LRU + D_LRU
RMS_EPS = 1e-6
LN_EPS = 1e-5

kernel_name = "hybrid_conformer_conv_rglru_trunk"


def rms_norm(x, g):
    xf = x.astype(jnp.float32)
    y = xf * lax.rsqrt(jnp.mean(xf * xf, axis=-1, keepdims=True) + RMS_EPS)
    return (y * g.astype(jnp.float32)).astype(x.dtype)


def layer_norm(x, g, b):
    xf = x.astype(jnp.float32)
    mu = jnp.mean(xf, axis=-1, keepdims=True)
    xc = xf - mu
    var = jnp.mean(xc * xc, axis=-1, keepdims=True)
    y = xc * lax.rsqrt(var + LN_EPS)
    return (y * g.astype(jnp.float32) + b.astype(jnp.float32)).astype(x.dtype)


def causal_depthwise_conv(x, w, b):
    k = w.shape[0]
    out = lax.conv_general_dilated(
        x, w[:, None, :].astype(x.dtype), window_strides=(1,), padding=[(k - 1, 0)],
        dimension_numbers=("NWC", "WIO", "NWC"), feature_group_count=x.shape[-1])
    return out + b.astype(x.dtype)


def block_diag_linear(x, w, b):
    bsz, s, _ = x.shape
    xh = x.reshape(bsz, s, LRU_HEADS, LRU_HEAD_DIM)
    y = jnp.einsum("bshi,hij->bshj", xh, w.astype(x.dtype))
    return y.reshape(bsz, s, D_LRU) + b.astype(x.dtype)


def conformer_conv_branch(v, gate, dw_w, dw_b, ln_g, ln_b):
    c = v * jax.nn.sigmoid(gate)
    c = causal_depthwise_conv(c, dw_w, dw_b)
    c = layer_norm(c, ln_g, ln_b)
    return jax.nn.silu(c)


def rglru_branch(xl, conv_w, conv_b, wa, ba, wx, bx, lam):
    xc = causal_depthwise_conv(xl, conv_w, conv_b)
    xf = xc.astype(jnp.float32)
    r = jax.nn.sigmoid(block_diag_linear(xc, wa, ba).astype(jnp.float32))
    i = jax.nn.sigmoid(block_diag_linear(xc, wx, bx).astype(jnp.float32))
    log_a = LRU_C * r * jax.nn.log_sigmoid(lam.astype(jnp.float32))
    a = jnp.exp(log_a)
    mult = jnp.sqrt(-jnp.expm1(2.0 * log_a))
    bterm = mult * (i * xf)

    def combine(left, right):
        a1, b1 = left
        a2, b2 = right
        return a1 * a2, a2 * b1 + b2

    _, h = lax.associative_scan(combine, (a, bterm), axis=1)
    return h.astype(xl.dtype)


def _fwd_setup_inputs(seed: int = 0) -> dict:
    key = jax.random.key(seed)
    ks = jax.random.split(key, 20)
    f32 = jnp.float32
    x = jax.random.normal(ks[0], (BATCH, SEQ, D_MODEL), f32)
    norm_g = 1.0 + 0.02 * jax.random.normal(ks[1], (DEPTH, D_MODEL), f32)
    w_in = jax.random.normal(ks[2], (DEPTH, D_MODEL, D_IN), f32) * D_MODEL ** -0.5
    conv_dw_w = jax.random.normal(ks[3], (DEPTH, CONV_WIDTH, D_CONV), f32) * CONV_WIDTH ** -0.5
    conv_dw_b = 0.02 * jax.random.normal(ks[4], (DEPTH, D_CONV), f32)
    conv_ln_g = 1.0 + 0.02 * jax.random.normal(ks[5], (DEPTH, D_CONV), f32)
    conv_ln_b = 0.02 * jax.random.normal(ks[6], (DEPTH, D_CONV), f32)
    lru_conv_w = jax.random.normal(ks[7], (DEPTH, LRU_CONV_WIDTH, D_LRU), f32) * LRU_CONV_WIDTH ** -0.5
    lru_conv_b = 0.02 * jax.random.normal(ks[8], (DEPTH, D_LRU), f32)
    lru_wa = jax.random.normal(ks[9], (DEPTH, LRU_HEADS, LRU_HEAD_DIM, LRU_HEAD_DIM), f32) * LRU_HEAD_DIM ** -0.5
    lru_ba = 0.02 * jax.random.normal(ks[10], (DEPTH, D_LRU), f32)
    lru_wx = jax.random.normal(ks[11], (DEPTH, LRU_HEADS, LRU_HEAD_DIM, LRU_HEAD_DIM), f32) * LRU_HEAD_DIM ** -0.5
    lru_bx = 0.02 * jax.random.normal(ks[12], (DEPTH, D_LRU), f32)
    a_c = jax.random.uniform(ks[13], (DEPTH, D_LRU), f32, 0.9, 0.999)
    p = a_c ** (1.0 / LRU_C)
    lru_lambda = jnp.log(p) - jnp.log1p(-p)
    w_out = jax.random.normal(ks[14], (DEPTH, D_MIX, D_MODEL), f32) * D_MIX ** -0.5
    final_g = 1.0 + 0.02 * jax.random.normal(ks[15], (D_MODEL,), f32)
    return {"x": x, "norm_g": norm_g, "w_in": w_in,
            "conv_dw_w": conv_dw_w, "conv_dw_b": conv_dw_b, "conv_ln_g": conv_ln_g, "conv_ln_b": conv_ln_b,
            "lru_conv_w": lru_conv_w, "lru_conv_b": lru_conv_b, "lru_wa": lru_wa, "lru_ba": lru_ba,
            "lru_wx": lru_wx, "lru_bx": lru_bx, "lru_lambda": lru_lambda,
            "w_out": w_out, "final_g": final_g}


def _fwd_reference(x, norm_g, w_in, conv_dw_w, conv_dw_b, conv_ln_g, conv_ln_b,
              lru_conv_w, lru_conv_b, lru_wa, lru_ba, lru_wx, lru_bx, lru_lambda,
              w_out, final_g):
    splits = [D_CONV, 2 * D_CONV, 3 * D_CONV, 3 * D_CONV + D_LRU]
    for l in range(DEPTH):
        h = rms_norm(x, norm_g[l])
        u = jnp.einsum("bsd,de->bse", h, w_in[l].astype(h.dtype))
        glu_v, glu_g, z_conv, x_lru, z_lru = jnp.split(u, splits, axis=-1)
        y_conv = conformer_conv_branch(glu_v, glu_g, conv_dw_w[l], conv_dw_b[l],
                                       conv_ln_g[l], conv_ln_b[l]) * jax.nn.silu(z_conv)
        y_lru = rglru_branch(x_lru, lru_conv_w[l], lru_conv_b[l], lru_wa[l], lru_ba[l],
                             lru_wx[l], lru_bx[l], lru_lambda[l]) * jax.nn.silu(z_lru)
        y = jnp.concatenate([y_conv, y_lru], axis=-1)
        x = x + jnp.einsum("bse,ed->bsd", y, w_out[l].astype(y.dtype))
    return rms_norm(x, final_g)


import jax as _jax
import jax.numpy as _jnp

TWIN_FORMAT = 'train_step'
FWD_PARAMS = ['x', 'norm_g', 'w_in', 'conv_dw_w', 'conv_dw_b', 'conv_ln_g', 'conv_ln_b', 'lru_conv_w', 'lru_conv_b', 'lru_wa', 'lru_ba', 'lru_wx', 'lru_bx', 'lru_lambda', 'w_out', 'final_g']
TWIN_WEIGHTS = ['norm_g', 'w_in', 'conv_dw_w', 'conv_dw_b', 'conv_ln_g', 'conv_ln_b', 'lru_conv_w', 'lru_conv_b', 'lru_wa', 'lru_ba', 'lru_wx', 'lru_bx', 'lru_lambda', 'w_out', 'final_g']
TWIN_DIFF_INPUT = 'x'
TWIN_INPUTS = ['x', 'norm_g', 'w_in', 'conv_dw_w', 'conv_dw_b', 'conv_ln_g', 'conv_ln_b', 'lru_conv_w', 'lru_conv_b', 'lru_wa', 'lru_ba', 'lru_wx', 'lru_bx', 'lru_lambda', 'w_out', 'final_g', 'loss_target', 'm_norm_g', 'm_w_in', 'm_conv_dw_w', 'm_conv_dw_b', 'm_conv_ln_g', 'm_conv_ln_b', 'm_lru_conv_w', 'm_lru_conv_b', 'm_lru_wa', 'm_lru_ba', 'm_lru_wx', 'm_lru_bx', 'm_lru_lambda', 'm_w_out', 'm_final_g', 'v_norm_g', 'v_w_in', 'v_conv_dw_w', 'v_conv_dw_b', 'v_conv_ln_g', 'v_conv_ln_b', 'v_lru_conv_w', 'v_lru_conv_b', 'v_lru_wa', 'v_lru_ba', 'v_lru_wx', 'v_lru_bx', 'v_lru_lambda', 'v_w_out', 'v_final_g']
TWIN_OUTPUTS = ['loss', 'grad_x', 'grad_norm_g', 'grad_w_in', 'grad_conv_dw_w', 'grad_conv_dw_b', 'grad_conv_ln_g', 'grad_conv_ln_b', 'grad_lru_conv_w', 'grad_lru_conv_b', 'grad_lru_wa', 'grad_lru_ba', 'grad_lru_wx', 'grad_lru_bx', 'grad_lru_lambda', 'grad_w_out', 'grad_final_g', 'delta_norm_g', 'delta_w_in', 'delta_conv_dw_w', 'delta_conv_dw_b', 'delta_conv_ln_g', 'delta_conv_ln_b', 'delta_lru_conv_w', 'delta_lru_conv_b', 'delta_lru_wa', 'delta_lru_ba', 'delta_lru_wx', 'delta_lru_bx', 'delta_lru_lambda', 'delta_w_out', 'delta_final_g', 'new_m_norm_g', 'new_m_w_in', 'new_m_conv_dw_w', 'new_m_conv_dw_b', 'new_m_conv_ln_g', 'new_m_conv_ln_b', 'new_m_lru_conv_w', 'new_m_lru_conv_b', 'new_m_lru_wa', 'new_m_lru_ba', 'new_m_lru_wx', 'new_m_lru_bx', 'new_m_lru_lambda', 'new_m_w_out', 'new_m_final_g', 'new_v_norm_g', 'new_v_w_in', 'new_v_conv_dw_w', 'new_v_conv_dw_b', 'new_v_conv_ln_g', 'new_v_conv_ln_b', 'new_v_lru_conv_w', 'new_v_lru_conv_b', 'new_v_lru_wa', 'new_v_lru_ba', 'new_v_lru_wx', 'new_v_lru_bx', 'new_v_lru_lambda', 'new_v_w_out', 'new_v_final_g']
TWIN_LEAF_KINDS = {'loss': 'loss', 'grad_x': 'grad_x', 'grad_norm_g': 'grad_w', 'grad_w_in': 'grad_w', 'grad_conv_dw_w': 'grad_w', 'grad_conv_dw_b': 'grad_w', 'grad_conv_ln_g': 'grad_w', 'grad_conv_ln_b': 'grad_w', 'grad_lru_conv_w': 'grad_w', 'grad_lru_conv_b': 'grad_w', 'grad_lru_wa': 'grad_w', 'grad_lru_ba': 'grad_w', 'grad_lru_wx': 'grad_w', 'grad_lru_bx': 'grad_w', 'grad_lru_lambda': 'grad_w', 'grad_w_out': 'grad_w', 'grad_final_g': 'grad_w', 'delta_norm_g': 'delta_w', 'delta_w_in': 'delta_w', 'delta_conv_dw_w': 'delta_w', 'delta_conv_dw_b': 'delta_w', 'delta_conv_ln_g': 'delta_w', 'delta_conv_ln_b': 'delta_w', 'delta_lru_conv_w': 'delta_w', 'delta_lru_conv_b': 'delta_w', 'delta_lru_wa': 'delta_w', 'delta_lru_ba': 'delta_w', 'delta_lru_wx': 'delta_w', 'delta_lru_bx': 'delta_w', 'delta_lru_lambda': 'delta_w', 'delta_w_out': 'delta_w', 'delta_final_g': 'delta_w', 'new_m_norm_g': 'new_m', 'new_m_w_in': 'new_m', 'new_m_conv_dw_w': 'new_m', 'new_m_conv_dw_b': 'new_m', 'new_m_conv_ln_g': 'new_m', 'new_m_conv_ln_b': 'new_m', 'new_m_lru_conv_w': 'new_m', 'new_m_lru_conv_b': 'new_m', 'new_m_lru_wa': 'new_m', 'new_m_lru_ba': 'new_m', 'new_m_lru_wx': 'new_m', 'new_m_lru_bx': 'new_m', 'new_m_lru_lambda': 'new_m', 'new_m_w_out': 'new_m', 'new_m_final_g': 'new_m', 'new_v_norm_g': 'new_v', 'new_v_w_in': 'new_v', 'new_v_conv_dw_w': 'new_v', 'new_v_conv_dw_b': 'new_v', 'new_v_conv_ln_g': 'new_v', 'new_v_conv_ln_b': 'new_v', 'new_v_lru_conv_w': 'new_v', 'new_v_lru_conv_b': 'new_v', 'new_v_lru_wa': 'new_v', 'new_v_lru_ba': 'new_v', 'new_v_lru_wx': 'new_v', 'new_v_lru_bx': 'new_v', 'new_v_lru_lambda': 'new_v', 'new_v_w_out': 'new_v', 'new_v_final_g': 'new_v'}


def _forward(args):
    return _fwd_reference(*[args[k] for k in FWD_PARAMS])


def _output_shape():
    def fwd():
        inp = _fwd_setup_inputs(0)
        return _fwd_reference(*[inp[k] for k in FWD_PARAMS])
    out = _jax.eval_shape(fwd)
    return out.shape, out.dtype

N_MICROBATCH = 1
ADAM_LR = 0.001
ADAM_B1 = 0.9
ADAM_B2 = 0.999
ADAM_EPS = 1e-08
ADAM_WD = 0.01
ADAM_STEP = 10
PER_EXAMPLE_BATCH_AXIS = {'x': 0, 'loss_target': 0}
SHARED_INPUTS = []
_WEIGHT_DTYPES = {'norm_g': _jnp.float32, 'w_in': _jnp.float32, 'conv_dw_w': _jnp.float32, 'conv_dw_b': _jnp.float32, 'conv_ln_g': _jnp.float32, 'conv_ln_b': _jnp.float32, 'lru_conv_w': _jnp.float32, 'lru_conv_b': _jnp.float32, 'lru_wa': _jnp.float32, 'lru_ba': _jnp.float32, 'lru_wx': _jnp.float32, 'lru_bx': _jnp.float32, 'lru_lambda': _jnp.float32, 'w_out': _jnp.float32, 'final_g': _jnp.float32}
MOMENT_SCALE = {'norm_g': 7.999718e-02, 'w_in': 5.039580e-02, 'conv_dw_w': 4.430477e-02, 'conv_dw_b': 1.092557e-01, 'conv_ln_g': 5.986191e-02, 'conv_ln_b': 6.049817e-02, 'lru_conv_w': 7.894011e-02, 'lru_conv_b': 1.010659e+00, 'lru_wa': 2.391150e-02, 'lru_ba': 2.053012e-02, 'lru_wx': 4.378386e-02, 'lru_bx': 2.809613e-02, 'lru_lambda': 4.368618e-02, 'w_out': 6.345238e-02, 'final_g': 3.199780e+01}


def _to_microbatches(a, axis):
    t = _jnp.moveaxis(a, axis, 0)
    t = t.reshape((N_MICROBATCH, t.shape[0] // N_MICROBATCH) + t.shape[1:])
    return _jnp.moveaxis(t, 1, axis + 1)


def setup_inputs(seed: int = 0) -> dict:
    inp = _fwd_setup_inputs(seed)
    key = _jax.random.fold_in(_jax.random.key(seed), 7919)
    shape, _ = _output_shape()
    out = dict(inp)
    out["loss_target"] = _jax.random.normal(_jax.random.fold_in(key, 0), shape, _jnp.float32)
    for i, name in enumerate(TWIN_WEIGHTS):
        w = inp[name].astype(_jnp.float32)
        if MOMENT_SCALE is None:
            s = _jnp.sqrt(_jnp.mean(_jnp.square(w)) + 1e-30)
        else:
            s = MOMENT_SCALE[name]
        km, kv = _jax.random.split(_jax.random.fold_in(key, i + 1))
        out[name] = w
        out["m_" + name] = s * _jax.random.normal(km, w.shape, _jnp.float32)
        out["v_" + name] = (s * s) * _jax.random.uniform(kv, w.shape, _jnp.float32, 0.5, 1.5)
    if N_MICROBATCH > 1:
        for name, axis in PER_EXAMPLE_BATCH_AXIS.items():
            out[name] = _to_microbatches(out[name], axis)
    return {'x': out['x'], 'norm_g': out['norm_g'], 'w_in': out['w_in'], 'conv_dw_w': out['conv_dw_w'], 'conv_dw_b': out['conv_dw_b'], 'conv_ln_g': out['conv_ln_g'], 'conv_ln_b': out['conv_ln_b'], 'lru_conv_w': out['lru_conv_w'], 'lru_conv_b': out['lru_conv_b'], 'lru_wa': out['lru_wa'], 'lru_ba': out['lru_ba'], 'lru_wx': out['lru_wx'], 'lru_bx': out['lru_bx'], 'lru_lambda': out['lru_lambda'], 'w_out': out['w_out'], 'final_g': out['final_g'], 'loss_target': out['loss_target'], 'm_norm_g': out['m_norm_g'], 'm_w_in': out['m_w_in'], 'm_conv_dw_w': out['m_conv_dw_w'], 'm_conv_dw_b': out['m_conv_dw_b'], 'm_conv_ln_g': out['m_conv_ln_g'], 'm_conv_ln_b': out['m_conv_ln_b'], 'm_lru_conv_w': out['m_lru_conv_w'], 'm_lru_conv_b': out['m_lru_conv_b'], 'm_lru_wa': out['m_lru_wa'], 'm_lru_ba': out['m_lru_ba'], 'm_lru_wx': out['m_lru_wx'], 'm_lru_bx': out['m_lru_bx'], 'm_lru_lambda': out['m_lru_lambda'], 'm_w_out': out['m_w_out'], 'm_final_g': out['m_final_g'], 'v_norm_g': out['v_norm_g'], 'v_w_in': out['v_w_in'], 'v_conv_dw_w': out['v_conv_dw_w'], 'v_conv_dw_b': out['v_conv_dw_b'], 'v_conv_ln_g': out['v_conv_ln_g'], 'v_conv_ln_b': out['v_conv_ln_b'], 'v_lru_conv_w': out['v_lru_conv_w'], 'v_lru_conv_b': out['v_lru_conv_b'], 'v_lru_wa': out['v_lru_wa'], 'v_lru_ba': out['v_lru_ba'], 'v_lru_wx': out['v_lru_wx'], 'v_lru_bx': out['v_lru_bx'], 'v_lru_lambda': out['v_lru_lambda'], 'v_w_out': out['v_w_out'], 'v_final_g': out['v_final_g']}


def _loss(weights, diff, rest, loss_target):
    with _jax.named_scope("forward"):
        args = {**rest, TWIN_DIFF_INPUT: diff, **{k: w.astype(_WEIGHT_DTYPES[k]) for k, w in weights.items()}}
        y = _forward(args)
    with _jax.named_scope("loss_head"):
        err = _jnp.square(y.astype(_jnp.float32) - loss_target)
        return 0.5 * _jnp.sum(_jnp.mean(err, axis=-1)) if err.ndim else 0.5 * err


def _adamw(w, g, m, v):
    m = ADAM_B1 * m + (1.0 - ADAM_B1) * g
    v = ADAM_B2 * v + (1.0 - ADAM_B2) * _jnp.square(g)
    m_hat = m / (1.0 - ADAM_B1 ** ADAM_STEP)
    v_hat = v / (1.0 - ADAM_B2 ** ADAM_STEP)
    delta = -ADAM_LR * (m_hat / (_jnp.sqrt(v_hat) + ADAM_EPS) + ADAM_WD * w)
    return delta, m, v


def reference(x, norm_g, w_in, conv_dw_w, conv_dw_b, conv_ln_g, conv_ln_b, lru_conv_w, lru_conv_b, lru_wa, lru_ba, lru_wx, lru_bx, lru_lambda, w_out, final_g, loss_target, m_norm_g, m_w_in, m_conv_dw_w, m_conv_dw_b, m_conv_ln_g, m_conv_ln_b, m_lru_conv_w, m_lru_conv_b, m_lru_wa, m_lru_ba, m_lru_wx, m_lru_bx, m_lru_lambda, m_w_out, m_final_g, v_norm_g, v_w_in, v_conv_dw_w, v_conv_dw_b, v_conv_ln_g, v_conv_ln_b, v_lru_conv_w, v_lru_conv_b, v_lru_wa, v_lru_ba, v_lru_wx, v_lru_bx, v_lru_lambda, v_w_out, v_final_g):
    given = dict(x=x, norm_g=norm_g, w_in=w_in, conv_dw_w=conv_dw_w, conv_dw_b=conv_dw_b, conv_ln_g=conv_ln_g, conv_ln_b=conv_ln_b, lru_conv_w=lru_conv_w, lru_conv_b=lru_conv_b, lru_wa=lru_wa, lru_ba=lru_ba, lru_wx=lru_wx, lru_bx=lru_bx, lru_lambda=lru_lambda, w_out=w_out, final_g=final_g, loss_target=loss_target, m_norm_g=m_norm_g, m_w_in=m_w_in, m_conv_dw_w=m_conv_dw_w, m_conv_dw_b=m_conv_dw_b, m_conv_ln_g=m_conv_ln_g, m_conv_ln_b=m_conv_ln_b, m_lru_conv_w=m_lru_conv_w, m_lru_conv_b=m_lru_conv_b, m_lru_wa=m_lru_wa, m_lru_ba=m_lru_ba, m_lru_wx=m_lru_wx, m_lru_bx=m_lru_bx, m_lru_lambda=m_lru_lambda, m_w_out=m_w_out, m_final_g=m_final_g, v_norm_g=v_norm_g, v_w_in=v_w_in, v_conv_dw_w=v_conv_dw_w, v_conv_dw_b=v_conv_dw_b, v_conv_ln_g=v_conv_ln_g, v_conv_ln_b=v_conv_ln_b, v_lru_conv_w=v_lru_conv_w, v_lru_conv_b=v_lru_conv_b, v_lru_wa=v_lru_wa, v_lru_ba=v_lru_ba, v_lru_wx=v_lru_wx, v_lru_bx=v_lru_bx, v_lru_lambda=v_lru_lambda, v_w_out=v_w_out, v_final_g=v_final_g)
    weights = {n: given[n] for n in TWIN_WEIGHTS}
    shared = {n: given[n] for n in SHARED_INPUTS}
    per_example = {n: given[n] for n in ['x']}
    grad_fn = _jax.value_and_grad(_loss, argnums=(0, 1))

    def one_microbatch(ex, loss_target):
        ex = dict(ex)
        diff = ex.pop(TWIN_DIFF_INPUT)
        return grad_fn(weights, diff, {**shared, **ex}, loss_target)

    if N_MICROBATCH == 1:
        loss, (grad_w, grad_x) = one_microbatch(per_example, given["loss_target"])
    else:
        def body(carry, xs):
            loss_sum, grad_sum = carry
            l_k, (gw_k, gx_k) = one_microbatch(xs[0], xs[1])
            with _jax.named_scope("update"):
                return (loss_sum + l_k, _jax.tree.map(_jnp.add, grad_sum, gw_k)), gx_k

        init = (_jnp.zeros((), _jnp.float32), _jax.tree.map(_jnp.zeros_like, weights))
        (loss, grad_w), grad_x = _jax.lax.scan(body, init, (per_example, given["loss_target"]))
    with _jax.named_scope("update"):
        delta_w, new_m, new_v = {}, {}, {}
        for n in TWIN_WEIGHTS:
            delta_w[n], new_m[n], new_v[n] = _adamw(weights[n], grad_w[n], given["m_" + n], given["v_" + n])
    return (loss, grad_x, *[grad_w[n] for n in TWIN_WEIGHTS], *[delta_w[n] for n in TWIN_WEIGHTS],
            *[new_m[n] for n in TWIN_WEIGHTS], *[new_v[n] for n in TWIN_WEIGHTS])
```

```python
import functools

import jax
import jax.numpy as jnp
from jax import lax
from jax.experimental import pallas as pl
from jax.experimental.pallas import tpu as pltpu

F32 = jnp.float32
BF16 = jnp.bfloat16
MESH = pl.DeviceIdType.MESH
AXES = ("x", "y", "c")
N_DEV = 8

D_MODEL = 2048
D_BR = 1024
D_IN = 5 * D_BR
SHARD_IN = D_IN // N_DEV
SHARD_OUT = D_MODEL // N_DEV
KW = 31
KW4 = 4
HEADS = 8
HD = 128
LRU_C = 8.0
RMS_EPS = 1e-6
LN_EPS = 1e-5
HALO = 32
HALO4 = 8
ROW_CHUNK = 16

ADAM_LR = 0.001
ADAM_B1 = 0.9
ADAM_B2 = 0.999
ADAM_EPS = 1e-08
ADAM_WD = 0.01
ADAM_STEP = 10

VMEM_LIMIT = 56 * 1024 * 1024


def _cparams(n_grid):
    return pltpu.CompilerParams(dimension_semantics=("arbitrary",) * n_grid, vmem_limit_bytes=VMEM_LIMIT)


def _sig(x):
    return jax.nn.sigmoid(x)


def _dsilu(z, sz):
    return sz * (1.0 + z * (1.0 - sz))


def _expm1(x):
    small = jnp.abs(x) < 0.01
    series = x * (1.0 + x * (0.5 + x * (1.0 / 6.0 + x * (1.0 / 24.0))))
    return jnp.where(small, series, jnp.exp(x) - 1.0)


def _log_sigmoid(x):
    e = jnp.exp(-jnp.abs(x))
    l1p = jnp.where(e < 0.01, e * (1.0 - e * (0.5 - e * (1.0 / 3.0))), jnp.log(1.0 + e))
    return jnp.minimum(x, 0.0) - l1p


def _colsum(x):
    return jnp.sum(x, axis=0, keepdims=True)


def _my_pos():
    return lax.axis_index("x"), lax.axis_index("y"), lax.axis_index("c")


def _cast_bf16(x, name):
    nl, r, c = x.shape
    tr = min(r, 512)

    def body(x_ref, o_ref):
        o_ref[...] = x_ref[...].astype(BF16)

    spec = pl.BlockSpec((None, tr, c), lambda l, i: (l, i, 0))
    return pl.pallas_call(
        body, name=name, grid=(nl, r // tr), in_specs=[spec], out_specs=spec,
        out_shape=jax.ShapeDtypeStruct(x.shape, BF16), compiler_params=_cparams(2))(x)


def _all_gather(shards):
    nt = len(shards)

    def body(*refs):
        srcs, outs = refs[:nt], refs[nt:2 * nt]
        send_sems, recv_sems, local_sems = refs[2 * nt:]
        x, y, c = _my_pos()
        me, sibling = (x, y, c), (x, y, 1 - c)
        chips = [(1 - x, y), (x, 1 - y), (1 - x, 1 - y)]

        def slot(p):
            return 4 * p[0] + 2 * p[1] + p[2]

        def copy(t, k, block, to, own=False):
            dst = outs[t].at[slot(block)]
            return pltpu.make_async_remote_copy(
                src_ref=srcs[t] if own else dst, dst_ref=dst, send_sem=send_sems.at[t, k], recv_sem=recv_sems.at[t, k],
                device_id=to, device_id_type=MESH)

        mine = [pltpu.make_async_copy(srcs[t], outs[t].at[slot(me)], local_sems.at[t]) for t in range(nt)]
        for cp in mine:
            cp.start()
        first = []
        for t in range(nt):
            first.append(copy(t, 0, me, sibling, own=True))
            first += [copy(t, 1 + j, me, (*chip, c), own=True) for j, chip in enumerate(chips)]
        for cp in first:
            cp.start()
        passed = []
        for j, chip in enumerate(chips):
            for t in range(nt):
                copy(t, 1 + j, (*chip, c), me).wait_recv()
                fwd = copy(t, 4 + j, (*chip, c), sibling)
                fwd.start()
                passed.append(fwd)
        for t in range(nt):
            copy(t, 0, sibling, me).wait_recv()
            for j, chip in enumerate(chips):
                copy(t, 4 + j, (*chip, 1 - c), me).wait_recv()
        for cp in first + passed:
            cp.wait_send()
        for cp in mine:
            cp.wait()

    any_spec = pl.BlockSpec(memory_space=pl.ANY)
    return pl.pallas_call(
        body, name="weight_all_gather",
        out_shape=[jax.ShapeDtypeStruct((N_DEV,) + s.shape, s.dtype) for s in shards],
        in_specs=[any_spec] * nt, out_specs=[any_spec] * nt,
        scratch_shapes=[pltpu.SemaphoreType.DMA((nt, 7)), pltpu.SemaphoreType.DMA((nt, 7)),
                        pltpu.SemaphoreType.DMA((nt,))],
    )(*shards)


def _in_proj(x, g_row, w_all, layer):
    s = x.shape[0]
    tm = min(s, 512)

    def body(x_ref, g_ref, w_ref, h_ref, u_ref):
        @pl.when(pl.program_id(1) == 0)
        def _():
            xf = x_ref[...]
            rstd = lax.rsqrt(jnp.mean(xf * xf, axis=-1, keepdims=True) + RMS_EPS)
            h_ref[...] = (xf * rstd * g_ref[...]).astype(BF16)

        u_ref[...] = jnp.dot(h_ref[...], w_ref[...], preferred_element_type=F32)

    return pl.pallas_call(
        body, name="in_proj", grid=(s // tm, N_DEV),
        in_specs=[pl.BlockSpec((tm, D_MODEL), lambda i, j: (i, 0)),
                  pl.BlockSpec((1, D_MODEL), lambda i, j: (0, 0)),
                  pl.BlockSpec((None, None, D_MODEL, SHARD_IN), lambda i, j: (j, layer, 0, 0))],
        out_specs=[pl.BlockSpec((tm, D_MODEL), lambda i, j: (i, 0)),
                   pl.BlockSpec((tm, SHARD_IN), lambda i, j: (i, j))],
        out_shape=[jax.ShapeDtypeStruct((s, D_MODEL), BF16), jax.ShapeDtypeStruct((s, D_IN), F32)],
        compiler_params=_cparams(2))(x, g_row, w_all)


def _shift_copies(cs_ref, buf):
    n = buf.shape[0]
    cs_ref[0] = buf
    for sft in range(1, 8):
        cs_ref[sft] = pltpu.roll(buf, n - sft, 0)


def _conv_taps(cs_ref, w_ref, q_ref, t_rows, offs):
    def chunk(r, carry):
        r0 = pl.multiple_of(r * ROW_CHUNK, ROW_CHUNK)
        acc = jnp.zeros((ROW_CHUNK, D_BR), F32)
        for k, off in enumerate(offs):
            acc = acc + w_ref[k:k + 1, :] * cs_ref[off % 8, pl.ds(r0 + (off // 8) * 8, ROW_CHUNK), :]
        q_ref[pl.ds(r0, ROW_CHUNK), :] = acc
        return carry

    lax.fori_loop(0, t_rows // ROW_CHUNK, chunk, 0)


def _scan_fwd(a, b):
    t_rows = a.shape[0]
    row = lax.broadcasted_iota(jnp.int32, a.shape, 0)
    d = 1
    while d < t_rows:
        keep = row >= d
        a_s = jnp.where(keep, pltpu.roll(a, d, 0), 1.0)
        b_s = jnp.where(keep, pltpu.roll(b, d, 0), 0.0)
        b = a * b_s + b
        a = a * a_s
        d *= 2
    return a, b


def _scan_rev(a, b):
    t_rows = a.shape[0]
    row = lax.broadcasted_iota(jnp.int32, a.shape, 0)
    d = 1
    while d < t_rows:
        keep = row < t_rows - d
        a_s = jnp.where(keep, pltpu.roll(a, t_rows - d, 0), 1.0)
        b_s = jnp.where(keep, pltpu.roll(b, t_rows - d, 0), 0.0)
        b = a * b_s + b
        a = a * a_s
        d *= 2
    return a, b


def _heads_matmul(x_bf, w_ref):
    return jnp.concatenate(
        [jnp.dot(x_bf[:, h * HD:(h + 1) * HD], w_ref[h], preferred_element_type=F32) for h in range(HEADS)], axis=1)


def _heads_matmul_t(d_bf, w_ref):
    return jnp.concatenate(
        [lax.dot_general(d_bf[:, h * HD:(h + 1) * HD], w_ref[h], (((1,), (1,)), ((), ())), preferred_element_type=F32)
         for h in range(HEADS)], axis=1)


def _conv_branch_fwd(v, g, cbuf_ref, cs_ref, q_ref, w_ref, b31, ln_g, ln_b, t_rows):
    sg = _sig(g)
    c = v * sg
    cbuf_ref[pl.ds(HALO, t_rows), :] = c
    _shift_copies(cs_ref, cbuf_ref[...])
    _conv_taps(cs_ref, w_ref, q_ref, t_rows, [HALO - (KW - 1) + k for k in range(KW)])
    q = q_ref[...] + b31
    mu = jnp.mean(q, axis=-1, keepdims=True)
    xc = q - mu
    var = jnp.mean(xc * xc, axis=-1, keepdims=True)
    rstd = lax.rsqrt(var + LN_EPS)
    n = xc * rstd
    p = n * ln_g + ln_b
    sp = _sig(p)
    return c, sg, n, rstd, p, sp


def _lru_gates(xl, xbuf_ref, w4_ref, b4, wa_ref, ba, wx_ref, bx, lam, t_rows):
    xbuf_ref[pl.ds(HALO4, t_rows), :] = xl
    xb = xbuf_ref[...]
    n = t_rows + HALO4
    xc = b4 + w4_ref[3:4, :] * xl
    for k in range(KW4 - 1):
        off = HALO4 - (KW4 - 1) + k
        xc = xc + w4_ref[k:k + 1, :] * pltpu.roll(xb, n - off, 0)[0:t_rows]
    xc_bf = xc.astype(BF16)
    r = _sig(_heads_matmul(xc_bf, wa_ref) + ba)
    ig = _sig(_heads_matmul(xc_bf, wx_ref) + bx)
    log_s = _log_sigmoid(lam)
    la = LRU_C * r * log_s
    a = jnp.exp(la)
    m = jnp.sqrt(-_expm1(2.0 * la))
    return xb, xc, xc_bf, r, ig, log_s, a, m


def _mixer_fwd(u, w31, b31, ln_g, ln_b, w4, b4, wa_bf, ba, wx_bf, bx, lam, layer, t_rows):
    s = u.shape[0]
    nb = s // t_rows

    def body(u_ref, w31_ref, b31_ref, lng_ref, lnb_ref, w4_ref, b4_ref, wa_ref, ba_ref, wx_ref, bx_ref, lam_ref,
             y_ref, hb_ref, cbuf_ref, cs_ref, q_ref, xbuf_ref, hcar_ref):
        @pl.when(pl.program_id(0) == 0)
        def _():
            cbuf_ref[pl.ds(0, HALO), :] = jnp.zeros((HALO, D_BR), F32)
            xbuf_ref[pl.ds(0, HALO4), :] = jnp.zeros((HALO4, D_BR), F32)
            hcar_ref[...] = jnp.zeros_like(hcar_ref)

        zc = u_ref[:, 2 * D_BR:3 * D_BR]
        c, _, _, _, p, sp = _conv_branch_fwd(u_ref[:, 0:D_BR], u_ref[:, D_BR:2 * D_BR], cbuf_ref, cs_ref, q_ref,
                                             w31_ref, b31_ref[...], lng_ref[...], lnb_ref[...], t_rows)
        cbuf_ref[pl.ds(0, HALO), :] = c[t_rows - HALO:t_rows]
        y_ref[:, 0:D_BR] = (p * sp * (zc * _sig(zc))).astype(BF16)

        xl = u_ref[:, 3 * D_BR:4 * D_BR]
        zl = u_ref[:, 4 * D_BR:5 * D_BR]
        _, xc, _, _, ig, _, a, m = _lru_gates(xl, xbuf_ref, w4_ref, b4_ref[...], wa_ref, ba_ref[...], wx_ref,
                                              bx_ref[...], lam_ref[...], t_rows)
        xbuf_ref[pl.ds(0, HALO4), :] = xl[t_rows - HALO4:t_rows]
        a_cum, h_loc = _scan_fwd(a, m * (ig * xc))
        h_in = hcar_ref[...]
        hb_ref[...] = h_in
        h = h_loc + a_cum * h_in
        q_ref[...] = h
        hcar_ref[...] = q_ref[pl.ds(t_rows - 1, 1), :]
        y_ref[:, D_BR:2 * D_BR] = (h * (zl * _sig(zl))).astype(BF16)

    row1 = lambda i: (layer, 0, 0)
    return pl.pallas_call(
        body, name="mixer_fwd", grid=(nb,),
        in_specs=[pl.BlockSpec((t_rows, D_IN), lambda i: (i, 0)),
                  pl.BlockSpec((None, KW, D_BR), row1),
                  pl.BlockSpec((None, 1, D_BR), row1), pl.BlockSpec((None, 1, D_BR), row1),
                  pl.BlockSpec((None, 1, D_BR), row1),
                  pl.BlockSpec((None, KW4, D_BR), row1), pl.BlockSpec((None, 1, D_BR), row1),
                  pl.BlockSpec((None, HEADS, HD, HD), lambda i: (layer, 0, 0, 0)), pl.BlockSpec((None, 1, D_BR), row1),
                  pl.BlockSpec((None, HEADS, HD, HD), lambda i: (layer, 0, 0, 0)), pl.BlockSpec((None, 1, D_BR), row1),
                  pl.BlockSpec((None, 1, D_BR), row1)],
        out_specs=[pl.BlockSpec((t_rows, 2 * D_BR), lambda i: (i, 0)),
                   pl.BlockSpec((None, 1, D_BR), lambda i: (i, 0, 0))],
        out_shape=[jax.ShapeDtypeStruct((s, 2 * D_BR), BF16), jax.ShapeDtypeStruct((nb, 1, D_BR), F32)],
        scratch_shapes=[pltpu.VMEM((t_rows + HALO, D_BR), F32), pltpu.VMEM((8, t_rows + HALO, D_BR), F32),
                        pltpu.VMEM((t_rows, D_BR), F32), pltpu.VMEM((t_rows + HALO4, D_BR), F32),
                        pltpu.VMEM((1, D_BR), F32)],
        compiler_params=_cparams(1))(u, w31, b31, ln_g, ln_b, w4, b4, wa_bf, ba, wx_bf, bx, lam)


def _out_proj(x, y, wo_all, layer):
    s = x.shape[0]
    tm = min(s, 512)
    tn = 1024

    def body(x_ref, y_ref, w_ref, o_ref):
        o_ref[...] = x_ref[...] + jnp.dot(y_ref[...], w_ref[...], preferred_element_type=F32)

    return pl.pallas_call(
        body, name="out_proj", grid=(s // tm, D_MODEL // tn),
        in_specs=[pl.BlockSpec((tm, tn), lambda i, j: (i, j)),
                  pl.BlockSpec((tm, D_MODEL), lambda i, j: (i, 0)),
                  pl.BlockSpec((None, D_MODEL, tn), lambda i, j: (layer, 0, j))],
        out_specs=pl.BlockSpec((tm, tn), lambda i, j: (i, j)),
        out_shape=jax.ShapeDtypeStruct((s, D_MODEL), F32), compiler_params=_cparams(2))(x, y, wo_all)


def _loss_head(x, g_row, target):
    s = x.shape[0]
    tm = min(s, 512)

    def body(x_ref, g_ref, t_ref, loss_ref, dx_ref, dg_ref):
        @pl.when(pl.program_id(0) == 0)
        def _():
            loss_ref[...] = jnp.zeros_like(loss_ref)
            dg_ref[...] = jnp.zeros_like(dg_ref)

        xf = x_ref[...]
        g = g_ref[...]
        rstd = lax.rsqrt(jnp.mean(xf * xf, axis=-1, keepdims=True) + RMS_EPS)
        n = xf * rstd
        err = n * g - t_ref[...]
        loss_ref[...] += 0.5 * jnp.sum(jnp.mean(err * err, axis=-1, keepdims=True))
        dy = err * (1.0 / D_MODEL)
        dg_ref[...] += _colsum(dy * n)
        dn = dy * g
        dx_ref[...] = rstd * (dn - n * jnp.mean(dn * n, axis=-1, keepdims=True))

    return pl.pallas_call(
        body, name="loss_head", grid=(s // tm,),
        in_specs=[pl.BlockSpec((tm, D_MODEL), lambda i: (i, 0)), pl.BlockSpec((1, D_MODEL), lambda i: (0, 0)),
                  pl.BlockSpec((tm, D_MODEL), lambda i: (i, 0))],
        out_specs=[pl.BlockSpec((8, 128), lambda i: (0, 0)), pl.BlockSpec((tm, D_MODEL), lambda i: (i, 0)),
                   pl.BlockSpec((1, D_MODEL), lambda i: (0, 0))],
        out_shape=[jax.ShapeDtypeStruct((8, 128), F32), jax.ShapeDtypeStruct((s, D_MODEL), F32),
                   jax.ShapeDtypeStruct((1, D_MODEL), F32)],
        compiler_params=_cparams(1))(x, g_row, target)


def _out_proj_bwd_x(dx, wo_all, layer):
    s = dx.shape[0]
    tm = min(s, 512)
    tn = 1024

    def body(dx_ref, w_ref, dy_ref, dxb_ref):
        @pl.when(pl.program_id(1) == 0)
        def _():
            dxb_ref[...] = dx_ref[...].astype(BF16)

        dy_ref[...] = lax.dot_general(dxb_ref[...], w_ref[...], (((1,), (1,)), ((), ())), preferred_element_type=F32)

    return pl.pallas_call(
        body, name="out_proj_bwd_x", grid=(s // tm, D_MODEL // tn),
        in_specs=[pl.BlockSpec((tm, D_MODEL), lambda i, j: (i, 0)),
                  pl.BlockSpec((None, tn, D_MODEL), lambda i, j: (layer, j, 0))],
        out_specs=[pl.BlockSpec((tm, tn), lambda i, j: (i, j)), pl.BlockSpec((tm, D_MODEL), lambda i, j: (i, 0))],
        out_shape=[jax.ShapeDtypeStruct((s, D_MODEL), F32), jax.ShapeDtypeStruct((s, D_MODEL), BF16)],
        compiler_params=_cparams(2))(dx, wo_all)


def _owner_slot(j):
    return (j % 2) * 4 + j // 2


def _weight_grad(lhs, rhs, buf, layer, n_layers, m_blk, n_blk, by_rows, name):
    s = lhs.shape[0]
    tk = min(s, 512)

    def body(*refs):
        l_ref, r_ref = refs[0], refs[1]
        o_ref = refs[-1]

        @pl.when(pl.program_id(1) == 0)
        def _():
            o_ref[...] = jnp.zeros_like(o_ref)

        o_ref[...] += lax.dot_general(l_ref[...], r_ref[...], (((0,), (0,)), ((), ())), preferred_element_type=F32)

    if by_rows:
        in_specs = [pl.BlockSpec((tk, m_blk), lambda j, k: (k, j)), pl.BlockSpec((tk, n_blk), lambda j, k: (k, 0))]
    else:
        in_specs = [pl.BlockSpec((tk, m_blk), lambda j, k: (k, 0)), pl.BlockSpec((tk, n_blk), lambda j, k: (k, j))]
    args = [lhs, rhs]
    aliases = {}
    if buf is not None:
        in_specs.append(pl.BlockSpec(memory_space=pl.ANY))
        args.append(buf)
        aliases = {2: 0}
    return pl.pallas_call(
        body, name=name, grid=(N_DEV, s // tk), in_specs=in_specs,
        out_specs=pl.BlockSpec((None, None, m_blk, n_blk), lambda j, k: (_owner_slot(j), layer, 0, 0)),
        out_shape=jax.ShapeDtypeStruct((N_DEV, n_layers, m_blk, n_blk), F32),
        input_output_aliases=aliases, compiler_params=_cparams(2))(*args)


def _in_proj_bwd_x(du, w_all, x, g_row, dx_next, layer):
    s = x.shape[0]
    tm = min(s, 512)

    def body(du_ref, w_ref, x_ref, g_ref, dxn_ref, dx_ref, dg_ref, acc_ref):
        i, j = pl.program_id(0), pl.program_id(1)

        @pl.when((i == 0) & (j == 0))
        def _():
            dg_ref[...] = jnp.zeros_like(dg_ref)

        @pl.when(j == 0)
        def _():
            acc_ref[...] = jnp.zeros_like(acc_ref)

        acc_ref[...] += lax.dot_general(du_ref[...], w_ref[...], (((1,), (1,)), ((), ())), preferred_element_type=F32)

        @pl.when(j == N_DEV - 1)
        def _():
            xf = x_ref[...]
            dh = acc_ref[...]
            rstd = lax.rsqrt(jnp.mean(xf * xf, axis=-1, keepdims=True) + RMS_EPS)
            n = xf * rstd
            dg_ref[...] += _colsum(dh * n)
            dn = dh * g_ref[...]
            dx_ref[...] = dxn_ref[...] + rstd * (dn - n * jnp.mean(dn * n, axis=-1, keepdims=True))

    return pl.pallas_call(
        body, name="in_proj_bwd_x", grid=(s // tm, N_DEV),
        in_specs=[pl.BlockSpec((tm, SHARD_IN), lambda i, j: (i, j)),
                  pl.BlockSpec((None, None, D_MODEL, SHARD_IN), lambda i, j: (j, layer, 0, 0)),
                  pl.BlockSpec((tm, D_MODEL), lambda i, j: (i, 0)),
                  pl.BlockSpec((1, D_MODEL), lambda i, j: (0, 0)),
                  pl.BlockSpec((tm, D_MODEL), lambda i, j: (i, 0))],
        out_specs=[pl.BlockSpec((tm, D_MODEL), lambda i, j: (i, 0)), pl.BlockSpec((1, D_MODEL), lambda i, j: (0, 0))],
        out_shape=[jax.ShapeDtypeStruct((s, D_MODEL), F32), jax.ShapeDtypeStruct((1, D_MODEL), F32)],
        scratch_shapes=[pltpu.VMEM((tm, D_MODEL), F32)],
        compiler_params=_cparams(2))(du, w_all, x, g_row, dx_next)


PG_B31, PG_LNG, PG_LNB, PG_B4, PG_BA, PG_BX, PG_LAM, PG_W4 = 0, 1, 2, 3, 4, 5, 6, 8
PG_ROWS = 16


def _mixer_bwd(u, dy, hb, w31, b31, ln_g, ln_b, w4, b4, wa_bf, ba, wx_bf, bx, lam, layer, t_rows):
    s = u.shape[0]
    nb = s // t_rows
    conv_offs = [HALO - (KW - 1) + k for k in range(KW)]

    def body(u_ref, uh_ref, xh_ref, dy_ref, hb_ref, w31_ref, b31_ref, lng_ref, lnb_ref, w4_ref, b4_ref,
             wa_ref, ba_ref, wx_ref, bx_ref, lam_ref,
             du_ref, pg_ref, dw31_ref, dwa_ref, dwx_ref,
             cbuf_ref, cs_ref, q_ref, dqbuf_ref, dwacc_ref, xbuf_ref, dxcbuf_ref, acar_ref, gcar_ref):
        step = pl.program_id(0)
        blk = nb - 1 - step
        not_first = jnp.where(blk == 0, 0.0, 1.0)

        @pl.when(step == 0)
        def _():
            pg_ref[...] = jnp.zeros_like(pg_ref)
            dwa_ref[...] = jnp.zeros_like(dwa_ref)
            dwx_ref[...] = jnp.zeros_like(dwx_ref)
            dwacc_ref[...] = jnp.zeros_like(dwacc_ref)
            dqbuf_ref[pl.ds(t_rows, HALO), :] = jnp.zeros((HALO, D_BR), F32)
            dxcbuf_ref[pl.ds(t_rows, HALO4), :] = jnp.zeros((HALO4, D_BR), F32)
            acar_ref[...] = jnp.zeros_like(acar_ref)
            gcar_ref[...] = jnp.zeros_like(gcar_ref)

        def add_row(r, val):
            pg_ref[r:r + 1, :] += val

        v = u_ref[:, 0:D_BR]
        g = u_ref[:, D_BR:2 * D_BR]
        zc = u_ref[:, 2 * D_BR:3 * D_BR]
        dyc = dy_ref[:, 0:D_BR]
        cbuf_ref[pl.ds(0, HALO), :] = (uh_ref[:, 0:D_BR] * _sig(uh_ref[:, D_BR:2 * D_BR])) * not_first
        ln_gv = lng_ref[...]
        _, sg, n, rstd, p, sp = _conv_branch_fwd(v, g, cbuf_ref, cs_ref, q_ref, w31_ref, b31_ref[...], ln_gv,
                                                 lnb_ref[...], t_rows)
        sz = _sig(zc)
        du_ref[:, 2 * D_BR:3 * D_BR] = (dyc * (p * sp) * _dsilu(zc, sz)).astype(BF16)
        dp = dyc * (zc * sz) * _dsilu(p, sp)
        add_row(PG_LNG, _colsum(dp * n))
        add_row(PG_LNB, _colsum(dp))
        dn = dp * ln_gv
        dq = rstd * (dn - jnp.mean(dn, axis=-1, keepdims=True) - n * jnp.mean(dn * n, axis=-1, keepdims=True))
        add_row(PG_B31, _colsum(dq))
        dqbuf_ref[pl.ds(0, t_rows), :] = dq

        def dw_chunk(r, carry):
            r0 = pl.multiple_of(r * ROW_CHUNK, ROW_CHUNK)
            dqc = dqbuf_ref[pl.ds(r0, ROW_CHUNK), :]
            for k, off in enumerate(conv_offs):
                prod = dqc * cs_ref[off % 8, pl.ds(r0 + (off // 8) * 8, ROW_CHUNK), :]
                part = prod[0:8]
                for piece in range(1, ROW_CHUNK // 8):
                    part = part + prod[8 * piece:8 * piece + 8]
                dwacc_ref[k] += part
            return carry

        lax.fori_loop(0, t_rows // ROW_CHUNK, dw_chunk, 0)

        _shift_copies(cs_ref, dqbuf_ref[...])
        _conv_taps(cs_ref, w31_ref, q_ref, t_rows, [KW - 1 - k for k in range(KW)])
        dqbuf_ref[pl.ds(t_rows, HALO), :] = dq[0:HALO]
        dc = q_ref[...]
        du_ref[:, 0:D_BR] = (dc * sg).astype(BF16)
        du_ref[:, D_BR:2 * D_BR] = (dc * v * sg * (1.0 - sg)).astype(BF16)

        xl = u_ref[:, 3 * D_BR:4 * D_BR]
        zl = u_ref[:, 4 * D_BR:5 * D_BR]
        dyl = dy_ref[:, D_BR:2 * D_BR]
        xbuf_ref[pl.ds(0, HALO4), :] = xh_ref[...] * not_first
        xb, xc, xc_bf, r, ig, log_s, a, m = _lru_gates(xl, xbuf_ref, w4_ref, b4_ref[...], wa_ref, ba_ref[...], wx_ref,
                                                        bx_ref[...], lam_ref[...], t_rows)
        row = lax.broadcasted_iota(jnp.int32, (t_rows, D_BR), 0)
        a_cum, h_loc = _scan_fwd(a, m * (ig * xc))
        h_in = hb_ref[...]
        h = h_loc + a_cum * h_in
        h_prev = jnp.where(row >= 1, pltpu.roll(h, 1, 0), h_in)
        szl = _sig(zl)
        du_ref[:, 4 * D_BR:5 * D_BR] = (dyl * h * _dsilu(zl, szl)).astype(BF16)
        a_next = jnp.where(row < t_rows - 1, pltpu.roll(a, t_rows - 1, 0), acar_ref[...])
        an_cum, g_loc = _scan_rev(a_next, dyl * (zl * szl))
        gs = g_loc + an_cum * gcar_ref[...]
        q_ref[...] = gs
        gcar_ref[...] = q_ref[pl.ds(0, 1), :]
        q_ref[...] = a
        acar_ref[...] = q_ref[pl.ds(0, 1), :]

        dm = gs * ig * xc
        di = gs * m * xc
        dla = gs * h_prev * a - dm * (a * a / m)
        add_row(PG_LAM, _colsum(dla * r) * LRU_C)
        dra = dla * (LRU_C * log_s) * r * (1.0 - r)
        dia = di * ig * (1.0 - ig)
        add_row(PG_BA, _colsum(dra))
        add_row(PG_BX, _colsum(dia))
        dra_bf = dra.astype(BF16)
        dia_bf = dia.astype(BF16)
        for hd in range(HEADS):
            sl = slice(hd * HD, (hd + 1) * HD)
            dwa_ref[hd] += lax.dot_general(xc_bf[:, sl], dra_bf[:, sl], (((0,), (0,)), ((), ())),
                                           preferred_element_type=F32)
            dwx_ref[hd] += lax.dot_general(xc_bf[:, sl], dia_bf[:, sl], (((0,), (0,)), ((), ())),
                                           preferred_element_type=F32)
        dxc = gs * m * ig + _heads_matmul_t(dra_bf, wa_ref) + _heads_matmul_t(dia_bf, wx_ref)
        add_row(PG_B4, _colsum(dxc))
        n4 = t_rows + HALO4
        add_row(PG_W4 + 3, _colsum(dxc * xl))
        for k in range(KW4 - 1):
            off = HALO4 - (KW4 - 1) + k
            add_row(PG_W4 + k, _colsum(dxc * pltpu.roll(xb, n4 - off, 0)[0:t_rows]))
        dxcbuf_ref[pl.ds(0, t_rows), :] = dxc
        db = dxcbuf_ref[...]
        dxl = w4_ref[3:4, :] * dxc
        for k in range(KW4 - 1):
            dxl = dxl + w4_ref[k:k + 1, :] * pltpu.roll(db, n4 - (KW4 - 1 - k), 0)[0:t_rows]
        dxcbuf_ref[pl.ds(t_rows, HALO4), :] = dxc[0:HALO4]
        du_ref[:, 3 * D_BR:4 * D_BR] = dxl.astype(BF16)

        @pl.when(step == nb - 1)
        def _():
            pg_ref[PG_LAM:PG_LAM + 1, :] = pg_ref[PG_LAM:PG_LAM + 1, :] * _sig(-lam_ref[...])
            dw31_ref[...] = jnp.zeros_like(dw31_ref)
            for k in range(KW):
                dw31_ref[k:k + 1, :] = jnp.sum(dwacc_ref[k], axis=0, keepdims=True)

    row1 = lambda i: (layer, 0, 0)
    const2 = lambda i: (0, 0)
    const3 = lambda i: (0, 0, 0)
    return pl.pallas_call(
        body, name="mixer_bwd", grid=(nb,),
        in_specs=[pl.BlockSpec((t_rows, D_IN), lambda i: (nb - 1 - i, 0)),
                  pl.BlockSpec((HALO, 2 * D_BR), lambda i: (jnp.maximum((nb - 1 - i) * (t_rows // HALO) - 1, 0), 0)),
                  pl.BlockSpec((HALO4, D_BR), lambda i: (jnp.maximum((nb - 1 - i) * (t_rows // HALO4) - 1, 0), 3)),
                  pl.BlockSpec((t_rows, 2 * D_BR), lambda i: (nb - 1 - i, 0)),
                  pl.BlockSpec((None, 1, D_BR), lambda i: (nb - 1 - i, 0, 0)),
                  pl.BlockSpec((None, KW, D_BR), row1),
                  pl.BlockSpec((None, 1, D_BR), row1), pl.BlockSpec((None, 1, D_BR), row1),
                  pl.BlockSpec((None, 1, D_BR), row1),
                  pl.BlockSpec((None, KW4, D_BR), row1), pl.BlockSpec((None, 1, D_BR), row1),
                  pl.BlockSpec((None, HEADS, HD, HD), lambda i: (layer, 0, 0, 0)), pl.BlockSpec((None, 1, D_BR), row1),
                  pl.BlockSpec((None, HEADS, HD, HD), lambda i: (layer, 0, 0, 0)), pl.BlockSpec((None, 1, D_BR), row1),
                  pl.BlockSpec((None, 1, D_BR), row1)],
        out_specs=[pl.BlockSpec((t_rows, D_IN), lambda i: (nb - 1 - i, 0)),
                   pl.BlockSpec((PG_ROWS, D_BR), const2), pl.BlockSpec((32, D_BR), const2),
                   pl.BlockSpec((HEADS, HD, HD), const3), pl.BlockSpec((HEADS, HD, HD), const3)],
        out_shape=[jax.ShapeDtypeStruct((s, D_IN), BF16), jax.ShapeDtypeStruct((PG_ROWS, D_BR), F32),
                   jax.ShapeDtypeStruct((32, D_BR), F32), jax.ShapeDtypeStruct((HEADS, HD, HD), F32),
                   jax.ShapeDtypeStruct((HEADS, HD, HD), F32)],
        scratch_shapes=[pltpu.VMEM((t_rows + HALO, D_BR), F32), pltpu.VMEM((8, t_rows + HALO, D_BR), F32),
                        pltpu.VMEM((t_rows, D_BR), F32), pltpu.VMEM((t_rows + HALO, D_BR), F32),
                        pltpu.VMEM((KW, 8, D_BR), F32), pltpu.VMEM((t_rows + HALO4, D_BR), F32),
                        pltpu.VMEM((t_rows + HALO4, D_BR), F32), pltpu.VMEM((1, D_BR), F32),
                        pltpu.VMEM((1, D_BR), F32)],
        compiler_params=_cparams(1))(u, u, u, dy, hb, w31, b31, ln_g, ln_b, w4, b4, wa_bf, ba, wx_bf, bx, lam)


def _exchange_half(src, axis, name):
    h = src.shape[0] // 2

    def body(src_ref, recv_ref, send_sem, recv_sem):
        pos = list(_my_pos())
        me = pos[axis]
        pos[axis] = 1 - me
        cp = pltpu.make_async_remote_copy(
            src_ref=src_ref.at[pl.ds((1 - me) * h, h)], dst_ref=recv_ref, send_sem=send_sem, recv_sem=recv_sem,
            device_id=tuple(pos), device_id_type=MESH)
        cp.start()
        cp.wait()

    any_spec = pl.BlockSpec(memory_space=pl.ANY)
    return pl.pallas_call(
        body, name=name, out_shape=jax.ShapeDtypeStruct((h,) + src.shape[1:], src.dtype),
        in_specs=[any_spec], out_specs=any_spec,
        scratch_shapes=[pltpu.SemaphoreType.DMA(()), pltpu.SemaphoreType.DMA(())])(src)


def _add_kept_half(src, recv, keep, name):
    h, r, c = recv.shape
    tr = min(r, 256)

    def body(keep_ref, s_ref, r_ref, o_ref):
        o_ref[...] = s_ref[...] + r_ref[...]

    grid_spec = pltpu.PrefetchScalarGridSpec(
        num_scalar_prefetch=1, grid=(h, r // tr),
        in_specs=[pl.BlockSpec((None, tr, c), lambda b, i, kp: (kp[0] * h + b, i, 0)),
                  pl.BlockSpec((None, tr, c), lambda b, i, kp: (b, i, 0))],
        out_specs=pl.BlockSpec((None, tr, c), lambda b, i, kp: (b, i, 0)))
    return pl.pallas_call(
        body, name=name, grid_spec=grid_spec, out_shape=jax.ShapeDtypeStruct(recv.shape, F32),
        compiler_params=_cparams(2))(keep, src, recv)


def _reduce_scatter(buf, tag):
    x, y, c = _my_pos()
    for stage, (axis, me) in enumerate(((2, c), (0, x), (1, y))):
        recv = _exchange_half(buf, axis, f"rs_{tag}_exchange{stage}")
        buf = _add_kept_half(buf, recv, jnp.reshape(me, (1,)).astype(jnp.int32), f"rs_{tag}_add{stage}")
    return buf[0]


def _all_reduce_small(p):
    def body(p_ref, o_ref, r0_ref, r1_ref, r2_ref, send_sems, recv_sems):
        x, y, c = _my_pos()
        peers = [(x, y, 1 - c), (1 - x, y, c), (x, 1 - y, c)]
        o_ref[...] = p_ref[...]
        for k, (peer, r_ref) in enumerate(zip(peers, (r0_ref, r1_ref, r2_ref))):
            cp = pltpu.make_async_remote_copy(
                src_ref=o_ref, dst_ref=r_ref, send_sem=send_sems.at[k], recv_sem=recv_sems.at[k],
                device_id=peer, device_id_type=MESH)
            cp.start()
            cp.wait()
            o_ref[...] = o_ref[...] + r_ref[...]

    vm = pl.BlockSpec(memory_space=pltpu.VMEM)
    return pl.pallas_call(
        body, name="small_all_reduce", out_shape=jax.ShapeDtypeStruct(p.shape, F32), in_specs=[vm], out_specs=vm,
        scratch_shapes=[pltpu.VMEM(p.shape, F32)] * 3 + [pltpu.SemaphoreType.DMA((3,)), pltpu.SemaphoreType.DMA((3,))],
        compiler_params=pltpu.CompilerParams(vmem_limit_bytes=VMEM_LIMIT))(p)


def _adamw(w, g, m, v, name):
    r, c = w.shape
    tr = r
    for cand in (512, 256, 128, 64, 32, 16, 8):
        if r % cand == 0 and cand * c * 4 <= (2 << 20):
            tr = cand
            break

    def body(w_ref, g_ref, m_ref, v_ref, d_ref, mo_ref, vo_ref):
        gv = g_ref[...]
        m_new = ADAM_B1 * m_ref[...] + (1.0 - ADAM_B1) * gv
        v_new = ADAM_B2 * v_ref[...] + (1.0 - ADAM_B2) * (gv * gv)
        m_hat = m_new / (1.0 - ADAM_B1 ** ADAM_STEP)
        v_hat = v_new / (1.0 - ADAM_B2 ** ADAM_STEP)
        d_ref[...] = -ADAM_LR * (m_hat / (jnp.sqrt(v_hat) + ADAM_EPS) + ADAM_WD * w_ref[...])
        mo_ref[...] = m_new
        vo_ref[...] = v_new

    spec = pl.BlockSpec((tr, c), lambda i: (i, 0))
    shape = jax.ShapeDtypeStruct((r, c), F32)
    return pl.pallas_call(
        body, name=name, grid=(r // tr,), in_specs=[spec] * 4, out_specs=[spec] * 3, out_shape=[shape] * 3,
        compiler_params=_cparams(1))(w, g, m, v)


def _pack_rows(parts):
    flat = jnp.concatenate([jnp.reshape(p, (-1, D_BR)) for p in parts], axis=0)
    pad = (-flat.shape[0]) % 64
    if pad:
        flat = jnp.concatenate([flat, jnp.zeros((pad, D_BR), F32)], axis=0)
    return flat


def _unpack_rows(flat, shapes):
    out, r0 = [], 0
    for shp in shapes:
        n = 1
        for d in shp:
            n *= d
        rows = n // D_BR
        out.append(jnp.reshape(flat[r0:r0 + rows], shp))
        r0 += rows
    return out


def kernel(x, norm_g, w_in, conv_dw_w, conv_dw_b, conv_ln_g, conv_ln_b, lru_conv_w, lru_conv_b, lru_wa, lru_ba, lru_wx, lru_bx, lru_lambda, w_out, final_g, loss_target, m_norm_g, m_w_in, m_conv_dw_w, m_conv_dw_b, m_conv_ln_g, m_conv_ln_b, m_lru_conv_w, m_lru_conv_b, m_lru_wa, m_lru_ba, m_lru_wx, m_lru_bx, m_lru_lambda, m_w_out, m_final_g, v_norm_g, v_w_in, v_conv_dw_w, v_conv_dw_b, v_conv_ln_g, v_conv_ln_b, v_lru_conv_w, v_lru_conv_b, v_lru_wa, v_lru_ba, v_lru_wx, v_lru_bx, v_lru_lambda, v_w_out, v_final_g):
    n_layers = norm_g.shape[0]
    s = x.shape[1]
    t_rows = min(s, 128)
    xs = jnp.reshape(x, (s, D_MODEL))
    target = jnp.reshape(loss_target, (s, D_MODEL))
    dev = 4 * lax.axis_index("x") + 2 * lax.axis_index("y") + lax.axis_index("c")

    w_in_all, w_out_all, w31_all, w4_all = _all_gather(
        [_cast_bf16(w_in, "cast_w_in"), _cast_bf16(w_out, "cast_w_out"), conv_dw_w, lru_conv_w])
    wo_full = jnp.reshape(jnp.transpose(w_out_all, (1, 0, 2, 3)), (n_layers, D_MODEL, D_MODEL))
    w31_full = jnp.reshape(jnp.transpose(w31_all, (1, 2, 0, 3)), (n_layers, KW, D_BR))
    w4_full = jnp.reshape(jnp.transpose(w4_all, (1, 2, 0, 3)), (n_layers, KW4, D_BR))
    wa_bf = lru_wa.astype(BF16)
    wx_bf = lru_wx.astype(BF16)
    row3 = lambda p: jnp.reshape(p, (n_layers, 1, -1))
    mixer_params = (w31_full, row3(conv_dw_b), row3(conv_ln_g), row3(conv_ln_b), w4_full, row3(lru_conv_b),
                    wa_bf, row3(lru_ba), wx_bf, row3(lru_bx), row3(lru_lambda))

    saved = []
    act = xs
    for l in range(n_layers):
        h, u = _in_proj(act, norm_g[l:l + 1], w_in_all, l)
        y, hb = _mixer_fwd(u, *mixer_params, l, t_rows)
        saved.append((act, h, u, y, hb))
        act = _out_proj(act, y, wo_full, l)
    loss_part, dx, d_final_g = _loss_head(act, jnp.reshape(final_g, (1, D_MODEL)), target)
    loss = lax.psum(loss_part[0, 0], AXES)

    g_in = g_out = None
    small = [None] * n_layers
    for l in reversed(range(n_layers)):
        x_l, h, u, y, hb = saved[l]
        dy, dxb = _out_proj_bwd_x(dx, wo_full, l)
        g_out = _weight_grad(y, dxb, g_out, l, n_layers, SHARD_OUT, D_MODEL, True, "w_out_grad")
        du, pg, dw31, dwa, dwx = _mixer_bwd(u, dy, hb, *mixer_params, l, t_rows)
        g_in = _weight_grad(h, du, g_in, l, n_layers, D_MODEL, SHARD_IN, False, "w_in_grad")
        dx, d_norm = _in_proj_bwd_x(du, w_in_all, x_l, norm_g[l:l + 1], dx, l)
        small[l] = (d_norm, pg, dw31, dwa, dwx)
    grad_x = jnp.reshape(dx, x.shape)

    grad_w_in = jnp.reshape(
        _reduce_scatter(jnp.reshape(g_in, (N_DEV, n_layers * D_MODEL, SHARD_IN)), "w_in"), w_in.shape)
    grad_w_out = jnp.reshape(
        _reduce_scatter(jnp.reshape(g_out, (N_DEV, n_layers * SHARD_OUT, D_MODEL)), "w_out"), w_out.shape)

    stack = lambda f: jnp.stack([f(small[l]) for l in range(n_layers)])
    pg_all = stack(lambda t: t[1])
    rep_names_shapes = [
        (stack(lambda t: t[0][0]), norm_g.shape), (pg_all[:, PG_B31], conv_dw_b.shape),
        (pg_all[:, PG_LNG], conv_ln_g.shape), (pg_all[:, PG_LNB], conv_ln_b.shape),
        (pg_all[:, PG_B4], lru_conv_b.shape), (stack(lambda t: t[3]), lru_wa.shape), (pg_all[:, PG_BA], lru_ba.shape),
        (stack(lambda t: t[4]), lru_wx.shape), (pg_all[:, PG_BX], lru_bx.shape), (pg_all[:, PG_LAM], lru_lambda.shape),
        (d_final_g, final_g.shape)]
    shard_parts = [(stack(lambda t: t[2][0:KW]), (n_layers, KW, D_BR)),
                   (pg_all[:, PG_W4:PG_W4 + KW4], (n_layers, KW4, D_BR))]
    all_parts = rep_names_shapes + shard_parts
    reduced = _unpack_rows(_all_reduce_small(_pack_rows([p for p, _ in all_parts])), [shp for _, shp in all_parts])
    rep_grads = reduced[:len(rep_names_shapes)]
    grad_dw = lax.dynamic_slice_in_dim(reduced[-2], dev * HD, HD, axis=2)
    grad_w4 = lax.dynamic_slice_in_dim(reduced[-1], dev * HD, HD, axis=2)

    def adam_nd(w, g, m, v, name):
        two_d = (-1, w.shape[-1])
        outs = _adamw(*(jnp.reshape(t, two_d) for t in (w, g, m, v)), name)
        return [jnp.reshape(o, w.shape) for o in outs]

    upd = {}
    upd["w_in"] = adam_nd(w_in, grad_w_in, m_w_in, v_w_in, "adamw_w_in")
    upd["w_out"] = adam_nd(w_out, grad_w_out, m_w_out, v_w_out, "adamw_w_out")
    upd["conv_dw_w"] = adam_nd(conv_dw_w, grad_dw, m_conv_dw_w, v_conv_dw_w, "adamw_conv_dw_w")
    upd["lru_conv_w"] = adam_nd(lru_conv_w, grad_w4, m_lru_conv_w, v_lru_conv_w, "adamw_lru_conv_w")
    rep_w = [norm_g, conv_dw_b, conv_ln_g, conv_ln_b, lru_conv_b, lru_wa, lru_ba, lru_wx, lru_bx, lru_lambda, final_g]
    rep_m = [m_norm_g, m_conv_dw_b, m_conv_ln_g, m_conv_ln_b, m_lru_conv_b, m_lru_wa, m_lru_ba, m_lru_wx, m_lru_bx,
             m_lru_lambda, m_final_g]
    rep_v = [v_norm_g, v_conv_dw_b, v_conv_ln_g, v_conv_ln_b, v_lru_conv_b, v_lru_wa, v_lru_ba, v_lru_wx, v_lru_bx,
             v_lru_lambda, v_final_g]
    rep_shapes = [w.shape for w in rep_w]
    packed = _adamw(_pack_rows(rep_w), _pack_rows(rep_grads), _pack_rows(rep_m), _pack_rows(rep_v), "adamw_small")
    rep_out = [_unpack_rows(o, rep_shapes) for o in packed]
    rep_keys = ["norm_g", "conv_dw_b", "conv_ln_g", "conv_ln_b", "lru_conv_b", "lru_wa", "lru_ba", "lru_wx", "lru_bx",
                "lru_lambda", "final_g"]
    grads = {"w_in": grad_w_in, "w_out": grad_w_out, "conv_dw_w": grad_dw, "lru_conv_w": grad_w4}
    for i, key in enumerate(rep_keys):
        grads[key] = rep_grads[i]
        upd[key] = [rep_out[0][i], rep_out[1][i], rep_out[2][i]]

    order = ["norm_g", "w_in", "conv_dw_w", "conv_dw_b", "conv_ln_g", "conv_ln_b", "lru_conv_w", "lru_conv_b", "lru_wa",
             "lru_ba", "lru_wx", "lru_bx", "lru_lambda", "w_out", "final_g"]
    return (loss, grad_x, *[grads[k] for k in order], *[upd[k][0] for k in order], *[upd[k][1] for k in order],
            *[upd[k][2] for k in order])
```

```python
import functools

import jax
import jax.numpy as jnp
from jax import lax
from jax.experimental import pallas as pl
from jax.experimental.pallas import tpu as pltpu

F32 = jnp.float32
BF16 = jnp.bfloat16
MESH = pl.DeviceIdType.MESH
AXES = ("x", "y", "c")
N_DEV = 8

D_MODEL = 2048
D_BR = 1024
D_IN = 5 * D_BR
SHARD_IN = D_IN // N_DEV
SHARD_OUT = D_MODEL // N_DEV
KW = 31
KW4 = 4
HEADS = 8
HD = 128
LRU_C = 8.0
RMS_EPS = 1e-6
LN_EPS = 1e-5
SUBLANES = 8
HALO = 32
HALO4 = 8
ROW_CHUNK = 16

ADAM_LR = 0.001
ADAM_B1 = 0.9
ADAM_B2 = 0.999
ADAM_EPS = 1e-08
ADAM_WD = 0.01
ADAM_STEP = 10

VMEM_LIMIT = 56 * 1024 * 1024

ANY = pl.BlockSpec(memory_space=pl.ANY)


def _cparams(n_grid):
    return pltpu.CompilerParams(dimension_semantics=("arbitrary",) * n_grid, vmem_limit_bytes=VMEM_LIMIT)


def _resident(block_shape, index_map):
    return pl.BlockSpec(block_shape, index_map, pipeline_mode=pl.Buffered(1))


def _sig(x):
    return jax.nn.sigmoid(x)


def _dsilu(z, sz):
    return sz * (1.0 + z * (1.0 - sz))


def _expm1(x):
    small = jnp.abs(x) < 0.01
    series = x * (1.0 + x * (0.5 + x * (1.0 / 6.0 + x * (1.0 / 24.0))))
    return jnp.where(small, series, jnp.exp(x) - 1.0)


def _log_sigmoid(x):
    e = jnp.exp(-jnp.abs(x))
    l1p = jnp.where(e < 0.01, e * (1.0 - e * (0.5 - e * (1.0 / 3.0))), jnp.log(1.0 + e))
    return jnp.minimum(x, 0.0) - l1p


def _colsum(x):
    return jnp.sum(x, axis=0, keepdims=True)


def _my_pos():
    return lax.axis_index("x"), lax.axis_index("y"), lax.axis_index("c")


class _GatherJob:
    def __init__(self, shards):
        self.arrays = list(shards)
        nt = self.nt = len(self.arrays)
        self.in_specs = [ANY] * nt
        self.out_shape = [jax.ShapeDtypeStruct((N_DEV,) + s.shape, s.dtype) for s in self.arrays]
        self.out_specs = [ANY] * nt
        self.scratch = [pltpu.SemaphoreType.DMA((nt, 7)), pltpu.SemaphoreType.DMA((nt, 7)),
                        pltpu.SemaphoreType.DMA((nt,))]

    def _plan(self, srcs, outs, scr):
        send_sems, recv_sems, local_sems = scr
        x, y, c = _my_pos()
        me, sibling = (x, y, c), (x, y, 1 - c)
        chips = [(1 - x, y), (x, 1 - y), (1 - x, 1 - y)]

        def slot(p):
            return 4 * p[0] + 2 * p[1] + p[2]

        def copy(t, k, block, to, own=False):
            dst = outs[t].at[slot(block)]
            return pltpu.make_async_remote_copy(
                src_ref=srcs[t] if own else dst, dst_ref=dst, send_sem=send_sems.at[t, k], recv_sem=recv_sems.at[t, k],
                device_id=to, device_id_type=MESH)

        mine = [pltpu.make_async_copy(srcs[t], outs[t].at[slot(me)], local_sems.at[t]) for t in range(self.nt)]
        first = []
        for t in range(self.nt):
            first.append(copy(t, 0, me, sibling, own=True))
            first += [copy(t, 1 + j, me, (*chip, c), own=True) for j, chip in enumerate(chips)]
        return me, sibling, chips, c, copy, mine, first

    def start(self, srcs, outs, scr):
        _, _, _, _, _, mine, first = self._plan(srcs, outs, scr)
        for cp in mine + first:
            cp.start()

    def finish(self, srcs, outs, scr):
        me, sibling, chips, c, copy, mine, first = self._plan(srcs, outs, scr)
        passed = []
        for j, chip in enumerate(chips):
            for t in range(self.nt):
                copy(t, 1 + j, (*chip, c), me).wait_recv()
                fwd = copy(t, 4 + j, (*chip, c), sibling)
                fwd.start()
                passed.append(fwd)
        for t in range(self.nt):
            copy(t, 0, sibling, me).wait_recv()
            for j, chip in enumerate(chips):
                copy(t, 4 + j, (*chip, 1 - c), me).wait_recv()
        for cp in first + passed:
            cp.wait_send()
        for cp in mine:
            cp.wait()


class _ExchangeJob:
    def __init__(self, srcs, axis):
        self.arrays = list(srcs)
        self.axis = axis
        nt = self.nt = len(self.arrays)
        self.half = [s.shape[0] // 2 for s in self.arrays]
        self.in_specs = [ANY] * nt
        self.out_shape = [jax.ShapeDtypeStruct((h,) + s.shape[1:], s.dtype) for h, s in zip(self.half, self.arrays)]
        self.out_specs = [ANY] * nt
        self.scratch = [pltpu.SemaphoreType.DMA((nt,)), pltpu.SemaphoreType.DMA((nt,))]

    def _copies(self, srcs, outs, scr):
        send_sems, recv_sems = scr
        pos = list(_my_pos())
        me = pos[self.axis]
        pos[self.axis] = 1 - me
        return [pltpu.make_async_remote_copy(
            src_ref=srcs[t].at[pl.ds((1 - me) * self.half[t], self.half[t])], dst_ref=outs[t],
            send_sem=send_sems.at[t], recv_sem=recv_sems.at[t], device_id=tuple(pos), device_id_type=MESH)
            for t in range(self.nt)]

    def start(self, srcs, outs, scr):
        for cp in self._copies(srcs, outs, scr):
            cp.start()

    def finish(self, srcs, outs, scr):
        for cp in self._copies(srcs, outs, scr):
            cp.wait()


def _run_job(job, name):
    def body(*refs):
        ins, outs, scr = refs[:job.nt], refs[job.nt:2 * job.nt], refs[2 * job.nt:]
        job.start(ins, outs, scr)
        job.finish(ins, outs, scr)

    return pl.pallas_call(body, name=name, out_shape=job.out_shape, in_specs=job.in_specs, out_specs=job.out_specs,
                          scratch_shapes=job.scratch)(*job.arrays)


def _hosted_call(body, *, name, grid, in_specs, out_specs, out_shape, scratch_shapes, args, job):
    n_in, n_out, n_scr = len(in_specs), len(out_specs), len(scratch_shapes)
    if job is None:
        outs = pl.pallas_call(body, name=name, grid=grid, in_specs=in_specs, out_specs=out_specs, out_shape=out_shape,
                              scratch_shapes=scratch_shapes, compiler_params=_cparams(len(grid)))(*args)
        return list(outs), None
    nt = job.nt

    def full_body(*refs):
        own_in, job_in = refs[:n_in], refs[n_in:n_in + nt]
        base = n_in + nt
        own_out, job_out = refs[base:base + n_out], refs[base + n_out:base + n_out + nt]
        base += n_out + nt
        own_scr, job_scr = refs[base:base + n_scr], refs[base + n_scr:]
        ids = [pl.program_id(a) for a in range(len(grid))]
        is_first = functools.reduce(jnp.logical_and, [i == 0 for i in ids])
        is_last = functools.reduce(jnp.logical_and, [i == g - 1 for i, g in zip(ids, grid)])

        @pl.when(is_first)
        def _():
            job.start(job_in, job_out, job_scr)

        body(*own_in, *own_out, *own_scr)

        @pl.when(is_last)
        def _():
            job.finish(job_in, job_out, job_scr)

    outs = pl.pallas_call(
        full_body, name=name, grid=grid, in_specs=list(in_specs) + job.in_specs,
        out_specs=list(out_specs) + job.out_specs, out_shape=list(out_shape) + job.out_shape,
        scratch_shapes=list(scratch_shapes) + job.scratch, compiler_params=_cparams(len(grid)))(*args, *job.arrays)
    return list(outs[:n_out]), list(outs[n_out:])


def _cast_bf16(x, name):
    nl, r, c = x.shape
    tr = min(r, 512)

    def body(x_ref, o_ref):
        o_ref[...] = x_ref[...].astype(BF16)

    spec = pl.BlockSpec((None, tr, c), lambda l, i: (l, i, 0))
    return pl.pallas_call(
        body, name=name, grid=(nl, r // tr), in_specs=[spec], out_specs=spec,
        out_shape=jax.ShapeDtypeStruct(x.shape, BF16), compiler_params=_cparams(2))(x)


def _in_proj(x, g_row, w_all, job):
    s = x.shape[0]
    tm = min(s, 512)

    def body(x_ref, g_ref, w_ref, h_ref, u_ref):
        @pl.when(pl.program_id(1) == 0)
        def _():
            xf = x_ref[...]
            rstd = lax.rsqrt(jnp.mean(xf * xf, axis=-1, keepdims=True) + RMS_EPS)
            h_ref[...] = (xf * rstd * g_ref[...]).astype(BF16)

        u_ref[...] = jnp.dot(h_ref[...], w_ref[pl.program_id(1)], preferred_element_type=F32)

    own, extra = _hosted_call(
        body, name="in_proj", grid=(s // tm, N_DEV),
        in_specs=[pl.BlockSpec((tm, D_MODEL), lambda i, j: (i, 0)),
                  pl.BlockSpec((1, D_MODEL), lambda i, j: (0, 0)),
                  _resident((N_DEV, D_MODEL, SHARD_IN), lambda i, j: (0, 0, 0))],
        out_specs=[pl.BlockSpec((tm, D_MODEL), lambda i, j: (i, 0)),
                   pl.BlockSpec((tm, SHARD_IN), lambda i, j: (i, j))],
        out_shape=[jax.ShapeDtypeStruct((s, D_MODEL), BF16), jax.ShapeDtypeStruct((s, D_IN), F32)],
        scratch_shapes=[], args=(x, g_row, w_all), job=job)
    return own[0], own[1], extra


def _shift_copies(cs_ref, buf):
    n = buf.shape[0]
    cs_ref[0] = buf
    for sft in range(1, 8):
        cs_ref[sft] = pltpu.roll(buf, n - sft, 0)


def _conv_taps(cs_ref, w_ref, q_ref, t_rows, offs):
    def chunk(r, carry):
        r0 = pl.multiple_of(r * ROW_CHUNK, ROW_CHUNK)
        acc = jnp.zeros((ROW_CHUNK, D_BR), F32)
        for k, off in enumerate(offs):
            acc = acc + w_ref[k:k + 1, :] * cs_ref[off % 8, pl.ds(r0 + (off // 8) * 8, ROW_CHUNK), :]
        q_ref[pl.ds(r0, ROW_CHUNK), :] = acc
        return carry

    lax.fori_loop(0, t_rows // ROW_CHUNK, chunk, 0)


def _scan_fwd(a, b, h_in):
    t_rows = a.shape[0]
    row8 = lax.broadcasted_iota(jnp.int32, a.shape, 0) & (SUBLANES - 1)
    d = 1
    while d < SUBLANES:
        keep = row8 >= d
        a_s = jnp.where(keep, pltpu.roll(a, d, 0), 1.0)
        b_s = jnp.where(keep, pltpu.roll(b, d, 0), 0.0)
        b = a * b_s + b
        a = a * a_s
        d *= 2
    carry = h_in
    groups = []
    for grp in range(t_rows // SUBLANES):
        rows = slice(grp * SUBLANES, (grp + 1) * SUBLANES)
        h_g = b[rows] + a[rows] * carry
        groups.append(h_g)
        carry = h_g[SUBLANES - 1:SUBLANES]
    return jnp.concatenate(groups, axis=0)


def _scan_rev(a, b, g_in):
    t_rows = a.shape[0]
    row8 = lax.broadcasted_iota(jnp.int32, a.shape, 0) & (SUBLANES - 1)
    d = 1
    while d < SUBLANES:
        keep = row8 < SUBLANES - d
        a_s = jnp.where(keep, pltpu.roll(a, t_rows - d, 0), 1.0)
        b_s = jnp.where(keep, pltpu.roll(b, t_rows - d, 0), 0.0)
        b = a * b_s + b
        a = a * a_s
        d *= 2
    carry = g_in
    groups = []
    for grp in reversed(range(t_rows // SUBLANES)):
        rows = slice(grp * SUBLANES, (grp + 1) * SUBLANES)
        g_g = b[rows] + a[rows] * carry
        groups.append(g_g)
        carry = g_g[0:1]
    return jnp.concatenate(groups[::-1], axis=0)


def _heads_matmul(x_bf, w_ref):
    return jnp.concatenate(
        [jnp.dot(x_bf[:, h * HD:(h + 1) * HD], w_ref[h], preferred_element_type=F32) for h in range(HEADS)], axis=1)


def _heads_matmul_t(d_bf, w_ref):
    return jnp.concatenate(
        [lax.dot_general(d_bf[:, h * HD:(h + 1) * HD], w_ref[h], (((1,), (1,)), ((), ())), preferred_element_type=F32)
         for h in range(HEADS)], axis=1)


def _layer_norm_swish(q, ln_g, ln_b):
    mu = jnp.mean(q, axis=-1, keepdims=True)
    xc = q - mu
    var = jnp.mean(xc * xc, axis=-1, keepdims=True)
    rstd = lax.rsqrt(var + LN_EPS)
    n = xc * rstd
    p = n * ln_g + ln_b
    return n, rstd, p, _sig(p)


def _lru_gates(xl, xbuf_ref, w4_ref, b4, wa_ref, ba, wx_ref, bx, lam, t_rows):
    xbuf_ref[pl.ds(HALO4, t_rows), :] = xl
    xb = xbuf_ref[...]
    n = t_rows + HALO4
    xc = b4 + w4_ref[3:4, :] * xl
    for k in range(KW4 - 1):
        off = HALO4 - (KW4 - 1) + k
        xc = xc + w4_ref[k:k + 1, :] * pltpu.roll(xb, n - off, 0)[0:t_rows]
    xc_bf = xc.astype(BF16)
    r = _sig(_heads_matmul(xc_bf, wa_ref) + ba)
    ig = _sig(_heads_matmul(xc_bf, wx_ref) + bx)
    log_s = _log_sigmoid(lam)
    la = LRU_C * r * log_s
    a = jnp.exp(la)
    m = jnp.sqrt(-_expm1(2.0 * la))
    return xb, xc, xc_bf, r, ig, log_s, a, m


def _mixer_specs(layer):
    row1 = lambda i: (layer, 0, 0)
    heads = lambda i: (layer, 0, 0, 0)
    return [pl.BlockSpec((None, KW, D_BR), row1),
            pl.BlockSpec((None, 1, D_BR), row1), pl.BlockSpec((None, 1, D_BR), row1),
            pl.BlockSpec((None, 1, D_BR), row1),
            pl.BlockSpec((None, KW4, D_BR), row1), pl.BlockSpec((None, 1, D_BR), row1),
            pl.BlockSpec((None, HEADS, HD, HD), heads), pl.BlockSpec((None, 1, D_BR), row1),
            pl.BlockSpec((None, HEADS, HD, HD), heads), pl.BlockSpec((None, 1, D_BR), row1),
            pl.BlockSpec((None, 1, D_BR), row1)]


def _mixer_fwd(u, params, layer, t_rows):
    s = u.shape[0]
    nb = s // t_rows

    def body(u_ref, w31_ref, b31_ref, lng_ref, lnb_ref, w4_ref, b4_ref, wa_ref, ba_ref, wx_ref, bx_ref, lam_ref,
             y_ref, q_out_ref, h_out_ref, hb_ref, cbuf_ref, cs_ref, xbuf_ref, hcar_ref):
        @pl.when(pl.program_id(0) == 0)
        def _():
            cbuf_ref[pl.ds(0, HALO), :] = jnp.zeros((HALO, D_BR), F32)
            xbuf_ref[pl.ds(0, HALO4), :] = jnp.zeros((HALO4, D_BR), F32)
            hcar_ref[...] = jnp.zeros_like(hcar_ref)

        zc = u_ref[:, 2 * D_BR:3 * D_BR]
        c = u_ref[:, 0:D_BR] * _sig(u_ref[:, D_BR:2 * D_BR])
        cbuf_ref[pl.ds(HALO, t_rows), :] = c
        _shift_copies(cs_ref, cbuf_ref[...])
        _conv_taps(cs_ref, w31_ref, q_out_ref, t_rows, [HALO - (KW - 1) + k for k in range(KW)])
        cbuf_ref[pl.ds(0, HALO), :] = c[t_rows - HALO:t_rows]
        q = q_out_ref[...] + b31_ref[...]
        q_out_ref[...] = q
        _, _, p, sp = _layer_norm_swish(q, lng_ref[...], lnb_ref[...])
        y_ref[:, 0:D_BR] = (p * sp * (zc * _sig(zc))).astype(BF16)

        xl = u_ref[:, 3 * D_BR:4 * D_BR]
        zl = u_ref[:, 4 * D_BR:5 * D_BR]
        _, xc, _, _, ig, _, a, m = _lru_gates(xl, xbuf_ref, w4_ref, b4_ref[...], wa_ref, ba_ref[...], wx_ref,
                                              bx_ref[...], lam_ref[...], t_rows)
        xbuf_ref[pl.ds(0, HALO4), :] = xl[t_rows - HALO4:t_rows]
        h_in = hcar_ref[...]
        hb_ref[...] = h_in
        h = _scan_fwd(a, m * (ig * xc), h_in)
        h_out_ref[...] = h
        hcar_ref[...] = h_out_ref[pl.ds(t_rows - 1, 1), :]
        y_ref[:, D_BR:2 * D_BR] = (h * (zl * _sig(zl))).astype(BF16)

    blk = pl.BlockSpec((t_rows, D_BR), lambda i: (i, 0))
    return pl.pallas_call(
        body, name="mixer_fwd", grid=(nb,),
        in_specs=[pl.BlockSpec((t_rows, D_IN), lambda i: (i, 0))] + _mixer_specs(layer),
        out_specs=[pl.BlockSpec((t_rows, 2 * D_BR), lambda i: (i, 0)), blk, blk,
                   pl.BlockSpec((None, 1, D_BR), lambda i: (i, 0, 0))],
        out_shape=[jax.ShapeDtypeStruct((s, 2 * D_BR), BF16), jax.ShapeDtypeStruct((s, D_BR), F32),
                   jax.ShapeDtypeStruct((s, D_BR), F32), jax.ShapeDtypeStruct((nb, 1, D_BR), F32)],
        scratch_shapes=[pltpu.VMEM((t_rows + HALO, D_BR), F32), pltpu.VMEM((8, t_rows + HALO, D_BR), F32),
                        pltpu.VMEM((t_rows + HALO4, D_BR), F32), pltpu.VMEM((1, D_BR), F32)],
        compiler_params=_cparams(1))(u, *params)


def _out_proj(x, y, wo):
    s = x.shape[0]
    tm = min(s, 512)

    def body(x_ref, y_ref, w_ref, o_ref):
        o_ref[...] = x_ref[...] + jnp.dot(y_ref[...], w_ref[...], preferred_element_type=F32)

    blk = pl.BlockSpec((tm, D_MODEL), lambda i: (i, 0))
    return pl.pallas_call(
        body, name="out_proj", grid=(s // tm,),
        in_specs=[blk, blk, _resident((D_MODEL, D_MODEL), lambda i: (0, 0))],
        out_specs=blk,
        out_shape=jax.ShapeDtypeStruct((s, D_MODEL), F32), compiler_params=_cparams(1))(x, y, wo)


def _loss_head(x, g_row, target):
    s = x.shape[0]
    tm = min(s, 512)

    def body(x_ref, g_ref, t_ref, loss_ref, dx_ref, dg_ref):
        @pl.when(pl.program_id(0) == 0)
        def _():
            loss_ref[...] = jnp.zeros_like(loss_ref)
            dg_ref[...] = jnp.zeros_like(dg_ref)

        xf = x_ref[...]
        g = g_ref[...]
        rstd = lax.rsqrt(jnp.mean(xf * xf, axis=-1, keepdims=True) + RMS_EPS)
        n = xf * rstd
        err = n * g - t_ref[...]
        loss_ref[...] += 0.5 * jnp.sum(jnp.mean(err * err, axis=-1, keepdims=True))
        dy = err * (1.0 / D_MODEL)
        dg_ref[...] += _colsum(dy * n)
        dn = dy * g
        dx_ref[...] = rstd * (dn - n * jnp.mean(dn * n, axis=-1, keepdims=True))

    return pl.pallas_call(
        body, name="loss_head", grid=(s // tm,),
        in_specs=[pl.BlockSpec((tm, D_MODEL), lambda i: (i, 0)), pl.BlockSpec((1, D_MODEL), lambda i: (0, 0)),
                  pl.BlockSpec((tm, D_MODEL), lambda i: (i, 0))],
        out_specs=[pl.BlockSpec((8, 128), lambda i: (0, 0)), pl.BlockSpec((tm, D_MODEL), lambda i: (i, 0)),
                   pl.BlockSpec((1, D_MODEL), lambda i: (0, 0))],
        out_shape=[jax.ShapeDtypeStruct((8, 128), F32), jax.ShapeDtypeStruct((s, D_MODEL), F32),
                   jax.ShapeDtypeStruct((1, D_MODEL), F32)],
        compiler_params=_cparams(1))(x, g_row, target)


def _out_proj_bwd_x(dx, wo, job):
    s = dx.shape[0]
    tm = min(s, 512)

    def body(dx_ref, w_ref, dy_ref, dxb_ref):
        dxb = dx_ref[...].astype(BF16)
        dxb_ref[...] = dxb
        dy_ref[...] = lax.dot_general(dxb, w_ref[...], (((1,), (1,)), ((), ())), preferred_element_type=F32)

    blk = pl.BlockSpec((tm, D_MODEL), lambda i: (i, 0))
    own, extra = _hosted_call(
        body, name="out_proj_bwd_x", grid=(s // tm,),
        in_specs=[blk, _resident((D_MODEL, D_MODEL), lambda i: (0, 0))],
        out_specs=[blk, blk],
        out_shape=[jax.ShapeDtypeStruct((s, D_MODEL), F32), jax.ShapeDtypeStruct((s, D_MODEL), BF16)],
        scratch_shapes=[], args=(dx, wo), job=job)
    return own[0], own[1], extra


def _w_in_grad(h, du, job):
    s = h.shape[0]
    tk = min(s, 512)
    nk = s // tk

    def body(h_ref, du_ref, o_ref, acc_ref):
        k = pl.program_id(1)

        @pl.when(k == 0)
        def _():
            acc_ref[...] = jnp.zeros_like(acc_ref)

        acc_ref[...] += lax.dot_general(h_ref[...], du_ref[...], (((0,), (0,)), ((), ())), preferred_element_type=F32)

        @pl.when(k == nk - 1)
        def _():
            o_ref[0] = acc_ref[:, 0:SHARD_IN].astype(BF16)
            o_ref[1] = acc_ref[:, SHARD_IN:2 * SHARD_IN].astype(BF16)

    own, extra = _hosted_call(
        body, name="w_in_grad", grid=(N_DEV // 2, nk),
        in_specs=[pl.BlockSpec((tk, D_MODEL), lambda q, k: (k, 0)),
                  pl.BlockSpec((tk, 2 * SHARD_IN), lambda q, k: (k, q))],
        out_specs=[pl.BlockSpec((2, None, D_MODEL, SHARD_IN), lambda q, k: (0, q, 0, 0))],
        out_shape=[jax.ShapeDtypeStruct((2, N_DEV // 2, D_MODEL, SHARD_IN), BF16)],
        scratch_shapes=[pltpu.VMEM((D_MODEL, 2 * SHARD_IN), F32)], args=(h, du), job=job)
    return jnp.reshape(own[0], (N_DEV, D_MODEL, SHARD_IN)), extra


def _w_out_grad(y, dxb, job):
    s = y.shape[0]
    tk = min(s, 512)
    nk = s // tk
    tn = 512

    def body(y_ref, dx_ref, o_ref, acc_ref):
        k = pl.program_id(1)

        @pl.when(k == 0)
        def _():
            acc_ref[...] = jnp.zeros_like(acc_ref)

        acc_ref[...] += lax.dot_general(y_ref[...], dx_ref[...], (((0,), (0,)), ((), ())), preferred_element_type=F32)

        @pl.when(k == nk - 1)
        def _():
            for j in range(N_DEV):
                slot = (j % 2) * 4 + j // 2
                o_ref[slot] = acc_ref[pl.ds(j * SHARD_OUT, SHARD_OUT), :].astype(BF16)

    own, extra = _hosted_call(
        body, name="w_out_grad", grid=(D_MODEL // tn, nk),
        in_specs=[pl.BlockSpec((tk, D_MODEL), lambda n, k: (k, 0)),
                  pl.BlockSpec((tk, tn), lambda n, k: (k, n))],
        out_specs=[pl.BlockSpec((N_DEV, SHARD_OUT, tn), lambda n, k: (0, 0, n))],
        out_shape=[jax.ShapeDtypeStruct((N_DEV, SHARD_OUT, D_MODEL), BF16)],
        scratch_shapes=[pltpu.VMEM((D_MODEL, tn), F32)], args=(y, dxb), job=job)
    return own[0], extra


def _in_proj_bwd_x(du, w_all, x, g_row, dx_next):
    s = x.shape[0]
    tm = min(s, 512)

    def body(du_ref, w_ref, x_ref, g_ref, dxn_ref, dx_ref, dg_ref, acc_ref):
        i, j = pl.program_id(0), pl.program_id(1)

        @pl.when((i == 0) & (j == 0))
        def _():
            dg_ref[...] = jnp.zeros_like(dg_ref)

        @pl.when(j == 0)
        def _():
            acc_ref[...] = jnp.zeros_like(acc_ref)

        acc_ref[...] += lax.dot_general(du_ref[...], w_ref[j], (((1,), (1,)), ((), ())), preferred_element_type=F32)

        @pl.when(j == N_DEV - 1)
        def _():
            xf = x_ref[...]
            dh = acc_ref[...]
            rstd = lax.rsqrt(jnp.mean(xf * xf, axis=-1, keepdims=True) + RMS_EPS)
            n = xf * rstd
            dg_ref[...] += _colsum(dh * n)
            dn = dh * g_ref[...]
            dx_ref[...] = dxn_ref[...] + rstd * (dn - n * jnp.mean(dn * n, axis=-1, keepdims=True))

    return pl.pallas_call(
        body, name="in_proj_bwd_x", grid=(s // tm, N_DEV),
        in_specs=[pl.BlockSpec((tm, SHARD_IN), lambda i, j: (i, j)),
                  _resident((N_DEV, D_MODEL, SHARD_IN), lambda i, j: (0, 0, 0)),
                  pl.BlockSpec((tm, D_MODEL), lambda i, j: (i, 0), pipeline_mode=pl.Buffered(1)),
                  pl.BlockSpec((1, D_MODEL), lambda i, j: (0, 0)),
                  pl.BlockSpec((tm, D_MODEL), lambda i, j: (i, 0), pipeline_mode=pl.Buffered(1))],
        out_specs=[pl.BlockSpec((tm, D_MODEL), lambda i, j: (i, 0)), pl.BlockSpec((1, D_MODEL), lambda i, j: (0, 0))],
        out_shape=[jax.ShapeDtypeStruct((s, D_MODEL), F32), jax.ShapeDtypeStruct((1, D_MODEL), F32)],
        scratch_shapes=[pltpu.VMEM((tm, D_MODEL), F32)],
        compiler_params=_cparams(2))(du, w_all, x, g_row, dx_next)


PG_B31, PG_LNG, PG_LNB, PG_B4, PG_BA, PG_BX, PG_LAM, PG_W4 = 0, 1, 2, 3, 4, 5, 6, 8
PG_ROWS = 16


def _mixer_bwd(u, q_saved, h_saved, dy, hb, params, layer, t_rows):
    s = u.shape[0]
    nb = s // t_rows
    conv_offs = [HALO - (KW - 1) + k for k in range(KW)]

    def body(u_ref, uh_ref, xh_ref, q_ref, h_ref, dy_ref, hb_ref, w31_ref, b31_ref, lng_ref, lnb_ref, w4_ref, b4_ref,
             wa_ref, ba_ref, wx_ref, bx_ref, lam_ref,
             du_ref, pg_ref, dw31_ref, dwa_ref, dwx_ref,
             cbuf_ref, cs_ref, dc_ref, dqbuf_ref, dwacc_ref, xbuf_ref, dxcbuf_ref, acar_ref, gcar_ref):
        step = pl.program_id(0)
        blk = nb - 1 - step
        not_first = jnp.where(blk == 0, 0.0, 1.0)

        @pl.when(step == 0)
        def _():
            pg_ref[...] = jnp.zeros_like(pg_ref)
            dwa_ref[...] = jnp.zeros_like(dwa_ref)
            dwx_ref[...] = jnp.zeros_like(dwx_ref)
            dwacc_ref[...] = jnp.zeros_like(dwacc_ref)
            dqbuf_ref[pl.ds(t_rows, HALO), :] = jnp.zeros((HALO, D_BR), F32)
            dxcbuf_ref[pl.ds(t_rows, HALO4), :] = jnp.zeros((HALO4, D_BR), F32)
            acar_ref[...] = jnp.zeros_like(acar_ref)
            gcar_ref[...] = jnp.zeros_like(gcar_ref)

        def add_row(r, val):
            pg_ref[r:r + 1, :] += val

        v = u_ref[:, 0:D_BR]
        g = u_ref[:, D_BR:2 * D_BR]
        zc = u_ref[:, 2 * D_BR:3 * D_BR]
        dyc = dy_ref[:, 0:D_BR]
        sg = _sig(g)
        cbuf_ref[pl.ds(0, HALO), :] = (uh_ref[:, 0:D_BR] * _sig(uh_ref[:, D_BR:2 * D_BR])) * not_first
        cbuf_ref[pl.ds(HALO, t_rows), :] = v * sg
        _shift_copies(cs_ref, cbuf_ref[...])
        ln_gv = lng_ref[...]
        n, rstd, p, sp = _layer_norm_swish(q_ref[...], ln_gv, lnb_ref[...])
        sz = _sig(zc)
        du_ref[:, 2 * D_BR:3 * D_BR] = (dyc * (p * sp) * _dsilu(zc, sz)).astype(BF16)
        dp = dyc * (zc * sz) * _dsilu(p, sp)
        add_row(PG_LNG, _colsum(dp * n))
        add_row(PG_LNB, _colsum(dp))
        dn = dp * ln_gv
        dq = rstd * (dn - jnp.mean(dn, axis=-1, keepdims=True) - n * jnp.mean(dn * n, axis=-1, keepdims=True))
        add_row(PG_B31, _colsum(dq))
        dqbuf_ref[pl.ds(0, t_rows), :] = dq

        def dw_chunk(r, carry):
            r0 = pl.multiple_of(r * ROW_CHUNK, ROW_CHUNK)
            dqc = dqbuf_ref[pl.ds(r0, ROW_CHUNK), :]
            for k, off in enumerate(conv_offs):
                prod = dqc * cs_ref[off % 8, pl.ds(r0 + (off // 8) * 8, ROW_CHUNK), :]
                part = prod[0:8]
                for piece in range(1, ROW_CHUNK // 8):
                    part = part + prod[8 * piece:8 * piece + 8]
                dwacc_ref[k] += part
            return carry

        lax.fori_loop(0, t_rows // ROW_CHUNK, dw_chunk, 0)

        _shift_copies(cs_ref, dqbuf_ref[...])
        _conv_taps(cs_ref, w31_ref, dc_ref, t_rows, [KW - 1 - k for k in range(KW)])
        dqbuf_ref[pl.ds(t_rows, HALO), :] = dq[0:HALO]
        dc = dc_ref[...]
        du_ref[:, 0:D_BR] = (dc * sg).astype(BF16)
        du_ref[:, D_BR:2 * D_BR] = (dc * v * sg * (1.0 - sg)).astype(BF16)

        xl = u_ref[:, 3 * D_BR:4 * D_BR]
        zl = u_ref[:, 4 * D_BR:5 * D_BR]
        dyl = dy_ref[:, D_BR:2 * D_BR]
        xbuf_ref[pl.ds(0, HALO4), :] = xh_ref[...] * not_first
        xb, xc, xc_bf, r, ig, log_s, a, m = _lru_gates(xl, xbuf_ref, w4_ref, b4_ref[...], wa_ref, ba_ref[...], wx_ref,
                                                        bx_ref[...], lam_ref[...], t_rows)
        row = lax.broadcasted_iota(jnp.int32, (t_rows, D_BR), 0)
        h = h_ref[...]
        h_prev = jnp.where(row >= 1, pltpu.roll(h, 1, 0), hb_ref[...])
        szl = _sig(zl)
        du_ref[:, 4 * D_BR:5 * D_BR] = (dyl * h * _dsilu(zl, szl)).astype(BF16)
        a_next = jnp.where(row < t_rows - 1, pltpu.roll(a, t_rows - 1, 0), acar_ref[...])
        gs = _scan_rev(a_next, dyl * (zl * szl), gcar_ref[...])
        dc_ref[...] = gs
        gcar_ref[...] = dc_ref[pl.ds(0, 1), :]
        dc_ref[...] = a
        acar_ref[...] = dc_ref[pl.ds(0, 1), :]

        dm = gs * ig * xc
        di = gs * m * xc
        dla = gs * h_prev * a - dm * (a * a / m)
        add_row(PG_LAM, _colsum(dla * r) * LRU_C)
        dra = dla * (LRU_C * log_s) * r * (1.0 - r)
        dia = di * ig * (1.0 - ig)
        add_row(PG_BA, _colsum(dra))
        add_row(PG_BX, _colsum(dia))
        dra_bf = dra.astype(BF16)
        dia_bf = dia.astype(BF16)
        for hd in range(HEADS):
            sl = slice(hd * HD, (hd + 1) * HD)
            dwa_ref[hd] += lax.dot_general(xc_bf[:, sl], dra_bf[:, sl], (((0,), (0,)), ((), ())),
                                           preferred_element_type=F32)
            dwx_ref[hd] += lax.dot_general(xc_bf[:, sl], dia_bf[:, sl], (((0,), (0,)), ((), ())),
                                           preferred_element_type=F32)
        dxc = gs * m * ig + _heads_matmul_t(dra_bf, wa_ref) + _heads_matmul_t(dia_bf, wx_ref)
        add_row(PG_B4, _colsum(dxc))
        n4 = t_rows + HALO4
        add_row(PG_W4 + 3, _colsum(dxc * xl))
        for k in range(KW4 - 1):
            off = HALO4 - (KW4 - 1) + k
            add_row(PG_W4 + k, _colsum(dxc * pltpu.roll(xb, n4 - off, 0)[0:t_rows]))
        dxcbuf_ref[pl.ds(0, t_rows), :] = dxc
        db = dxcbuf_ref[...]
        dxl = w4_ref[3:4, :] * dxc
        for k in range(KW4 - 1):
            dxl = dxl + w4_ref[k:k + 1, :] * pltpu.roll(db, n4 - (KW4 - 1 - k), 0)[0:t_rows]
        dxcbuf_ref[pl.ds(t_rows, HALO4), :] = dxc[0:HALO4]
        du_ref[:, 3 * D_BR:4 * D_BR] = dxl.astype(BF16)

        @pl.when(step == nb - 1)
        def _():
            pg_ref[PG_LAM:PG_LAM + 1, :] = pg_ref[PG_LAM:PG_LAM + 1, :] * _sig(-lam_ref[...])
            dw31_ref[...] = jnp.zeros_like(dw31_ref)
            for k in range(KW):
                dw31_ref[k:k + 1, :] = jnp.sum(dwacc_ref[k], axis=0, keepdims=True)

    const2 = lambda i: (0, 0)
    const3 = lambda i: (0, 0, 0)
    rev = lambda i: (nb - 1 - i, 0)
    return pl.pallas_call(
        body, name="mixer_bwd", grid=(nb,),
        in_specs=[pl.BlockSpec((t_rows, D_IN), rev),
                  pl.BlockSpec((HALO, 2 * D_BR), lambda i: (jnp.maximum((nb - 1 - i) * (t_rows // HALO) - 1, 0), 0)),
                  pl.BlockSpec((HALO4, D_BR), lambda i: (jnp.maximum((nb - 1 - i) * (t_rows // HALO4) - 1, 0), 3)),
                  pl.BlockSpec((t_rows, D_BR), rev), pl.BlockSpec((t_rows, D_BR), rev),
                  pl.BlockSpec((t_rows, 2 * D_BR), rev),
                  pl.BlockSpec((None, 1, D_BR), lambda i: (nb - 1 - i, 0, 0))] + _mixer_specs(layer),
        out_specs=[pl.BlockSpec((t_rows, D_IN), rev),
                   pl.BlockSpec((PG_ROWS, D_BR), const2), pl.BlockSpec((32, D_BR), const2),
                   pl.BlockSpec((HEADS, HD, HD), const3), pl.BlockSpec((HEADS, HD, HD), const3)],
        out_shape=[jax.ShapeDtypeStruct((s, D_IN), BF16), jax.ShapeDtypeStruct((PG_ROWS, D_BR), F32),
                   jax.ShapeDtypeStruct((32, D_BR), F32), jax.ShapeDtypeStruct((HEADS, HD, HD), F32),
                   jax.ShapeDtypeStruct((HEADS, HD, HD), F32)],
        scratch_shapes=[pltpu.VMEM((t_rows + HALO, D_BR), F32), pltpu.VMEM((8, t_rows + HALO, D_BR), F32),
                        pltpu.VMEM((t_rows, D_BR), F32), pltpu.VMEM((t_rows + HALO, D_BR), F32),
                        pltpu.VMEM((KW, 8, D_BR), F32), pltpu.VMEM((t_rows + HALO4, D_BR), F32),
                        pltpu.VMEM((t_rows + HALO4, D_BR), F32), pltpu.VMEM((1, D_BR), F32),
                        pltpu.VMEM((1, D_BR), F32)],
        compiler_params=_cparams(1))(u, u, u, q_saved, h_saved, dy, hb, *params)


def _add_kept_half(src, recv, keep, out_dtype, name):
    h, r, c = recv.shape
    tr = min(r, 256)

    def body(keep_ref, s_ref, r_ref, o_ref):
        o_ref[...] = (s_ref[...].astype(F32) + r_ref[...].astype(F32)).astype(out_dtype)

    grid_spec = pltpu.PrefetchScalarGridSpec(
        num_scalar_prefetch=1, grid=(h, r // tr),
        in_specs=[pl.BlockSpec((None, tr, c), lambda b, i, kp: (kp[0] * h + b, i, 0)),
                  pl.BlockSpec((None, tr, c), lambda b, i, kp: (b, i, 0))],
        out_specs=pl.BlockSpec((None, tr, c), lambda b, i, kp: (b, i, 0)))
    return pl.pallas_call(
        body, name=name, grid_spec=grid_spec, out_shape=jax.ShapeDtypeStruct(recv.shape, out_dtype),
        compiler_params=_cparams(2))(keep, src, recv)


class _PendingReduce:
    STAGE_AXES = (2, 0, 1)

    def __init__(self, bufs):
        self.bufs = list(bufs)
        self.stage = 0

    def job(self):
        return _ExchangeJob(self.bufs, self.STAGE_AXES[self.stage])

    def absorb(self, recvs):
        me = _my_pos()[self.STAGE_AXES[self.stage]]
        keep = jnp.reshape(me, (1,)).astype(jnp.int32)
        last = self.stage == 2
        self.bufs = [_add_kept_half(b, r, keep, F32 if last else BF16, f"rs_add{self.stage}_{t}")
                     for t, (b, r) in enumerate(zip(self.bufs, recvs))]
        self.stage += 1

    def finish_alone(self):
        while self.stage < 3:
            job = self.job()
            self.absorb(_run_job(job, f"rs_exchange{self.stage}"))
        return [b[0] for b in self.bufs]


def _all_reduce_small(p):
    def body(p_ref, o_ref, r0_ref, r1_ref, r2_ref, send_sems, recv_sems):
        x, y, c = _my_pos()
        peers = [(x, y, 1 - c), (1 - x, y, c), (x, 1 - y, c)]
        o_ref[...] = p_ref[...]
        for k, (peer, r_ref) in enumerate(zip(peers, (r0_ref, r1_ref, r2_ref))):
            cp = pltpu.make_async_remote_copy(
                src_ref=o_ref, dst_ref=r_ref, send_sem=send_sems.at[k], recv_sem=recv_sems.at[k],
                device_id=peer, device_id_type=MESH)
            cp.start()
            cp.wait()
            o_ref[...] = o_ref[...] + r_ref[...]

    vm = pl.BlockSpec(memory_space=pltpu.VMEM)
    return pl.pallas_call(
        body, name="small_all_reduce", out_shape=jax.ShapeDtypeStruct(p.shape, F32), in_specs=[vm], out_specs=vm,
        scratch_shapes=[pltpu.VMEM(p.shape, F32)] * 3 + [pltpu.SemaphoreType.DMA((3,)), pltpu.SemaphoreType.DMA((3,))],
        compiler_params=pltpu.CompilerParams(vmem_limit_bytes=VMEM_LIMIT))(p)


def _adamw(w, g, m, v, name):
    r, c = w.shape
    tr = r
    for cand in (512, 256, 128, 64, 32, 16, 8):
        if r % cand == 0 and cand * c * 4 <= (2 << 20):
            tr = cand
            break

    def body(w_ref, g_ref, m_ref, v_ref, d_ref, mo_ref, vo_ref):
        gv = g_ref[...]
        m_new = ADAM_B1 * m_ref[...] + (1.0 - ADAM_B1) * gv
        v_new = ADAM_B2 * v_ref[...] + (1.0 - ADAM_B2) * (gv * gv)
        m_hat = m_new / (1.0 - ADAM_B1 ** ADAM_STEP)
        v_hat = v_new / (1.0 - ADAM_B2 ** ADAM_STEP)
        d_ref[...] = -ADAM_LR * (m_hat / (jnp.sqrt(v_hat) + ADAM_EPS) + ADAM_WD * w_ref[...])
        mo_ref[...] = m_new
        vo_ref[...] = v_new

    spec = pl.BlockSpec((tr, c), lambda i: (i, 0))
    shape = jax.ShapeDtypeStruct((r, c), F32)
    return pl.pallas_call(
        body, name=name, grid=(r // tr,), in_specs=[spec] * 4, out_specs=[spec] * 3, out_shape=[shape] * 3,
        compiler_params=_cparams(1))(w, g, m, v)


def _pack_rows(parts):
    flat = jnp.concatenate([jnp.reshape(p, (-1, D_BR)) for p in parts], axis=0)
    pad = (-flat.shape[0]) % 64
    if pad:
        flat = jnp.concatenate([flat, jnp.zeros((pad, D_BR), F32)], axis=0)
    return flat


def _unpack_rows(flat, shapes):
    out, r0 = [], 0
    for shp in shapes:
        n = 1
        for d in shp:
            n *= d
        rows = n // D_BR
        out.append(jnp.reshape(flat[r0:r0 + rows], shp))
        r0 += rows
    return out


def kernel(x, norm_g, w_in, conv_dw_w, conv_dw_b, conv_ln_g, conv_ln_b, lru_conv_w, lru_conv_b, lru_wa, lru_ba, lru_wx, lru_bx, lru_lambda, w_out, final_g, loss_target, m_norm_g, m_w_in, m_conv_dw_w, m_conv_dw_b, m_conv_ln_g, m_conv_ln_b, m_lru_conv_w, m_lru_conv_b, m_lru_wa, m_lru_ba, m_lru_wx, m_lru_bx, m_lru_lambda, m_w_out, m_final_g, v_norm_g, v_w_in, v_conv_dw_w, v_conv_dw_b, v_conv_ln_g, v_conv_ln_b, v_lru_conv_w, v_lru_conv_b, v_lru_wa, v_lru_ba, v_lru_wx, v_lru_bx, v_lru_lambda, v_w_out, v_final_g):
    n_layers = norm_g.shape[0]
    s = x.shape[1]
    t_rows = min(s, 128)
    xs = jnp.reshape(x, (s, D_MODEL))
    target = jnp.reshape(loss_target, (s, D_MODEL))
    dev = 4 * lax.axis_index("x") + 2 * lax.axis_index("y") + lax.axis_index("c")

    w_in_bf = _cast_bf16(w_in, "cast_w_in")
    w_out_bf = _cast_bf16(w_out, "cast_w_out")
    w_in_l, w_out_l, w31_all, w4_all = _run_job(
        _GatherJob([w_in_bf[0], w_out_bf[0], conv_dw_w, lru_conv_w]), "weight_all_gather0")
    w31_full = jnp.reshape(jnp.transpose(w31_all, (1, 2, 0, 3)), (n_layers, KW, D_BR))
    w4_full = jnp.reshape(jnp.transpose(w4_all, (1, 2, 0, 3)), (n_layers, KW4, D_BR))
    row3 = lambda p: jnp.reshape(p, (n_layers, 1, -1))
    mixer_params = (w31_full, row3(conv_dw_b), row3(conv_ln_g), row3(conv_ln_b), w4_full, row3(lru_conv_b),
                    lru_wa.astype(BF16), row3(lru_ba), lru_wx.astype(BF16), row3(lru_bx), row3(lru_lambda))

    saved = []
    act = xs
    for l in range(n_layers):
        job = _GatherJob([w_in_bf[l + 1], w_out_bf[l + 1]]) if l + 1 < n_layers else None
        h, u, gathered = _in_proj(act, norm_g[l:l + 1], w_in_l, job)
        y, q_sv, h_sv, hb = _mixer_fwd(u, mixer_params, l, t_rows)
        wo = jnp.reshape(w_out_l, (D_MODEL, D_MODEL))
        saved.append((act, h, u, y, q_sv, h_sv, hb, w_in_l, wo))
        act = _out_proj(act, y, wo)
        if gathered is not None:
            w_in_l, w_out_l = gathered
    loss_part, dx, d_final_g = _loss_head(act, jnp.reshape(final_g, (1, D_MODEL)), target)
    loss = lax.psum(loss_part[0, 0], AXES)

    pending = None
    reduced_big = [None] * n_layers
    small = [None] * n_layers
    for l in reversed(range(n_layers)):
        x_l, h, u, y, q_sv, h_sv, hb, w_in_l, wo = saved[l]
        dy, dxb, recvs = _out_proj_bwd_x(dx, wo, pending.job() if pending else None)
        if pending:
            pending.absorb(recvs)
        g_out, recvs = _w_out_grad(y, dxb, pending.job() if pending else None)
        if pending:
            pending.absorb(recvs)
        du, pg, dw31, dwa, dwx = _mixer_bwd(u, q_sv, h_sv, dy, hb, mixer_params, l, t_rows)
        g_in, recvs = _w_in_grad(h, du, pending.job() if pending else None)
        if pending:
            pending.absorb(recvs)
            reduced_big[l + 1] = [b[0] for b in pending.bufs]
        dx, d_norm = _in_proj_bwd_x(du, w_in_l, x_l, norm_g[l:l + 1], dx)
        small[l] = (d_norm, pg, dw31, dwa, dwx)
        pending = _PendingReduce([g_in, g_out])
    reduced_big[0] = pending.finish_alone()
    grad_x = jnp.reshape(dx, x.shape)
    grad_w_in = jnp.stack([r[0] for r in reduced_big])
    grad_w_out = jnp.stack([r[1] for r in reduced_big])

    stack = lambda f: jnp.stack([f(small[l]) for l in range(n_layers)])
    pg_all = stack(lambda t: t[1])
    rep_parts = [
        (stack(lambda t: t[0][0]), norm_g.shape), (pg_all[:, PG_B31], conv_dw_b.shape),
        (pg_all[:, PG_LNG], conv_ln_g.shape), (pg_all[:, PG_LNB], conv_ln_b.shape),
        (pg_all[:, PG_B4], lru_conv_b.shape), (stack(lambda t: t[3]), lru_wa.shape), (pg_all[:, PG_BA], lru_ba.shape),
        (stack(lambda t: t[4]), lru_wx.shape), (pg_all[:, PG_BX], lru_bx.shape), (pg_all[:, PG_LAM], lru_lambda.shape),
        (d_final_g, final_g.shape)]
    shard_parts = [(stack(lambda t: t[2][0:KW]), (n_layers, KW, D_BR)),
                   (pg_all[:, PG_W4:PG_W4 + KW4], (n_layers, KW4, D_BR))]
    all_parts = rep_parts + shard_parts
    reduced = _unpack_rows(_all_reduce_small(_pack_rows([p for p, _ in all_parts])), [shp for _, shp in all_parts])
    rep_grads = reduced[:len(rep_parts)]
    grad_dw = lax.dynamic_slice_in_dim(reduced[-2], dev * HD, HD, axis=2)
    grad_w4 = lax.dynamic_slice_in_dim(reduced[-1], dev * HD, HD, axis=2)

    def adam_nd(w, g, m, v, name):
        two_d = (-1, w.shape[-1])
        outs = _adamw(*(jnp.reshape(t, two_d) for t in (w, g, m, v)), name)
        return [jnp.reshape(o, w.shape) for o in outs]

    upd = {}
    upd["w_in"] = adam_nd(w_in, grad_w_in, m_w_in, v_w_in, "adamw_w_in")
    upd["w_out"] = adam_nd(w_out, grad_w_out, m_w_out, v_w_out, "adamw_w_out")
    upd["conv_dw_w"] = adam_nd(conv_dw_w, grad_dw, m_conv_dw_w, v_conv_dw_w, "adamw_conv_dw_w")
    upd["lru_conv_w"] = adam_nd(lru_conv_w, grad_w4, m_lru_conv_w, v_lru_conv_w, "adamw_lru_conv_w")
    rep_w = [norm_g, conv_dw_b, conv_ln_g, conv_ln_b, lru_conv_b, lru_wa, lru_ba, lru_wx, lru_bx, lru_lambda, final_g]
    rep_m = [m_norm_g, m_conv_dw_b, m_conv_ln_g, m_conv_ln_b, m_lru_conv_b, m_lru_wa, m_lru_ba, m_lru_wx, m_lru_bx,
             m_lru_lambda, m_final_g]
    rep_v = [v_norm_g, v_conv_dw_b, v_conv_ln_g, v_conv_ln_b, v_lru_conv_b, v_lru_wa, v_lru_ba, v_lru_wx, v_lru_bx,
             v_lru_lambda, v_final_g]
    rep_shapes = [w.shape for w in rep_w]
    packed = _adamw(_pack_rows(rep_w), _pack_rows(rep_grads), _pack_rows(rep_m), _pack_rows(rep_v), "adamw_small")
    rep_out = [_unpack_rows(o, rep_shapes) for o in packed]
    rep_keys = ["norm_g", "conv_dw_b", "conv_ln_g", "conv_ln_b", "lru_conv_b", "lru_wa", "lru_ba", "lru_wx", "lru_bx",
                "lru_lambda", "final_g"]
    grads = {"w_in": grad_w_in, "w_out": grad_w_out, "conv_dw_w": grad_dw, "lru_conv_w": grad_w4}
    for i, key in enumerate(rep_keys):
        grads[key] = rep_grads[i]
        upd[key] = [rep_out[0][i], rep_out[1][i], rep_out[2][i]]

    order = ["norm_g", "w_in", "conv_dw_w", "conv_dw_b", "conv_ln_g", "conv_ln_b", "lru_conv_w", "lru_conv_b", "lru_wa",
             "lru_ba", "lru_wx", "lru_bx", "lru_lambda", "w_out", "final_g"]
    return (loss, grad_x, *[grads[k] for k in order], *[upd[k][0] for k in order], *[upd[k][1] for k in order],
            *[upd[k][2] for k in order])
```

```python
import functools

import jax
import jax.numpy as jnp
from jax import lax
from jax.experimental import pallas as pl
from jax.experimental.pallas import tpu as pltpu

F32 = jnp.float32
BF16 = jnp.bfloat16
MESH = pl.DeviceIdType.MESH
AXES = ("x", "y", "c")
N_DEV = 8

D_MODEL = 2048
D_BR = 1024
D_IN = 5 * D_BR
SHARD_IN = D_IN // N_DEV
SHARD_OUT = D_MODEL // N_DEV
KW = 31
KW4 = 4
HEADS = 8
HD = 128
LRU_C = 8.0
RMS_EPS = 1e-6
LN_EPS = 1e-5
SUBLANES = 8
HALO = 32
HALO4 = 8
ROW_CHUNK = 16

ADAM_LR = 0.001
ADAM_B1 = 0.9
ADAM_B2 = 0.999
ADAM_EPS = 1e-08
ADAM_WD = 0.01
ADAM_STEP = 10

VMEM_LIMIT = 56 * 1024 * 1024

ANY = pl.BlockSpec(memory_space=pl.ANY)


def _cparams(n_grid):
    return pltpu.CompilerParams(dimension_semantics=("arbitrary",) * n_grid, vmem_limit_bytes=VMEM_LIMIT)


def _resident(block_shape, index_map):
    return pl.BlockSpec(block_shape, index_map, pipeline_mode=pl.Buffered(1))


def _sig(x):
    return jax.nn.sigmoid(x)


def _dsilu(z, sz):
    return sz * (1.0 + z * (1.0 - sz))


def _expm1(x):
    small = jnp.abs(x) < 0.01
    series = x * (1.0 + x * (0.5 + x * (1.0 / 6.0 + x * (1.0 / 24.0))))
    return jnp.where(small, series, jnp.exp(x) - 1.0)


def _log_sigmoid(x):
    e = jnp.exp(-jnp.abs(x))
    l1p = jnp.where(e < 0.01, e * (1.0 - e * (0.5 - e * (1.0 / 3.0))), jnp.log(1.0 + e))
    return jnp.minimum(x, 0.0) - l1p


def _colsum(x):
    return jnp.sum(x, axis=0, keepdims=True)


def _my_pos():
    return lax.axis_index("x"), lax.axis_index("y"), lax.axis_index("c")


class _GatherJob:
    def __init__(self, shards):
        self.arrays = list(shards)
        nt = self.nt = len(self.arrays)
        self.in_specs = [ANY] * nt
        self.out_shape = [jax.ShapeDtypeStruct((N_DEV,) + s.shape, s.dtype) for s in self.arrays]
        self.out_specs = [ANY] * nt
        self.scratch = [pltpu.SemaphoreType.DMA((nt, 7)), pltpu.SemaphoreType.DMA((nt, 7)),
                        pltpu.SemaphoreType.DMA((nt,))]

    def _plan(self, srcs, outs, scr):
        send_sems, recv_sems, local_sems = scr
        x, y, c = _my_pos()
        me, sibling = (x, y, c), (x, y, 1 - c)
        chips = [(1 - x, y), (x, 1 - y), (1 - x, 1 - y)]

        def slot(p):
            return 4 * p[0] + 2 * p[1] + p[2]

        def copy(t, k, block, to, own=False):
            dst = outs[t].at[slot(block)]
            return pltpu.make_async_remote_copy(
                src_ref=srcs[t] if own else dst, dst_ref=dst, send_sem=send_sems.at[t, k], recv_sem=recv_sems.at[t, k],
                device_id=to, device_id_type=MESH)

        mine = [pltpu.make_async_copy(srcs[t], outs[t].at[slot(me)], local_sems.at[t]) for t in range(self.nt)]
        first = []
        for t in range(self.nt):
            first.append(copy(t, 0, me, sibling, own=True))
            first += [copy(t, 1 + j, me, (*chip, c), own=True) for j, chip in enumerate(chips)]
        return me, sibling, chips, c, copy, mine, first

    def start(self, srcs, outs, scr):
        _, _, _, _, _, mine, first = self._plan(srcs, outs, scr)
        for cp in mine + first:
            cp.start()

    def finish(self, srcs, outs, scr):
        me, sibling, chips, c, copy, mine, first = self._plan(srcs, outs, scr)
        passed = []
        for j, chip in enumerate(chips):
            for t in range(self.nt):
                copy(t, 1 + j, (*chip, c), me).wait_recv()
                fwd = copy(t, 4 + j, (*chip, c), sibling)
                fwd.start()
                passed.append(fwd)
        for t in range(self.nt):
            copy(t, 0, sibling, me).wait_recv()
            for j, chip in enumerate(chips):
                copy(t, 4 + j, (*chip, 1 - c), me).wait_recv()
        for cp in first + passed:
            cp.wait_send()
        for cp in mine:
            cp.wait()


class _ExchangeJob:
    def __init__(self, srcs, axis):
        self.arrays = list(srcs)
        self.axis = axis
        nt = self.nt = len(self.arrays)
        self.half = [s.shape[0] // 2 for s in self.arrays]
        self.in_specs = [ANY] * nt
        self.out_shape = [jax.ShapeDtypeStruct((h,) + s.shape[1:], s.dtype) for h, s in zip(self.half, self.arrays)]
        self.out_specs = [ANY] * nt
        self.scratch = [pltpu.SemaphoreType.DMA((nt,)), pltpu.SemaphoreType.DMA((nt,))]

    def _copies(self, srcs, outs, scr):
        send_sems, recv_sems = scr
        pos = list(_my_pos())
        me = pos[self.axis]
        pos[self.axis] = 1 - me
        return [pltpu.make_async_remote_copy(
            src_ref=srcs[t].at[pl.ds((1 - me) * self.half[t], self.half[t])], dst_ref=outs[t],
            send_sem=send_sems.at[t], recv_sem=recv_sems.at[t], device_id=tuple(pos), device_id_type=MESH)
            for t in range(self.nt)]

    def start(self, srcs, outs, scr):
        for cp in self._copies(srcs, outs, scr):
            cp.start()

    def finish(self, srcs, outs, scr):
        for cp in self._copies(srcs, outs, scr):
            cp.wait()


def _run_job(job, name):
    def body(*refs):
        ins, outs, scr = refs[:job.nt], refs[job.nt:2 * job.nt], refs[2 * job.nt:]
        job.start(ins, outs, scr)
        job.finish(ins, outs, scr)

    return pl.pallas_call(body, name=name, out_shape=job.out_shape, in_specs=job.in_specs, out_specs=job.out_specs,
                          scratch_shapes=job.scratch)(*job.arrays)


def _hosted_call(body, *, name, grid, in_specs, out_specs, out_shape, scratch_shapes, args, job):
    n_in, n_out, n_scr = len(in_specs), len(out_specs), len(scratch_shapes)
    if job is None:
        outs = pl.pallas_call(body, name=name, grid=grid, in_specs=in_specs, out_specs=out_specs, out_shape=out_shape,
                              scratch_shapes=scratch_shapes, compiler_params=_cparams(len(grid)))(*args)
        return list(outs), None
    nt = job.nt

    def full_body(*refs):
        own_in, job_in = refs[:n_in], refs[n_in:n_in + nt]
        base = n_in + nt
        own_out, job_out = refs[base:base + n_out], refs[base + n_out:base + n_out + nt]
        base += n_out + nt
        own_scr, job_scr = refs[base:base + n_scr], refs[base + n_scr:]
        ids = [pl.program_id(a) for a in range(len(grid))]
        is_first = functools.reduce(jnp.logical_and, [i == 0 for i in ids])
        is_last = functools.reduce(jnp.logical_and, [i == g - 1 for i, g in zip(ids, grid)])

        @pl.when(is_first)
        def _():
            job.start(job_in, job_out, job_scr)

        body(*own_in, *own_out, *own_scr)

        @pl.when(is_last)
        def _():
            job.finish(job_in, job_out, job_scr)

    outs = pl.pallas_call(
        full_body, name=name, grid=grid, in_specs=list(in_specs) + job.in_specs,
        out_specs=list(out_specs) + job.out_specs, out_shape=list(out_shape) + job.out_shape,
        scratch_shapes=list(scratch_shapes) + job.scratch, compiler_params=_cparams(len(grid)))(*args, *job.arrays)
    return list(outs[:n_out]), list(outs[n_out:])


def _cast_bf16(x, name):
    nl, r, c = x.shape
    tr = min(r, 512)

    def body(x_ref, o_ref):
        o_ref[...] = x_ref[...].astype(BF16)

    spec = pl.BlockSpec((None, tr, c), lambda l, i: (l, i, 0))
    return pl.pallas_call(
        body, name=name, grid=(nl, r // tr), in_specs=[spec], out_specs=spec,
        out_shape=jax.ShapeDtypeStruct(x.shape, BF16), compiler_params=_cparams(2))(x)


def _w_in_rows(w_all):
    def body(i_ref, o_ref):
        o_ref[...] = i_ref[...]

    return pl.pallas_call(
        body, name="w_in_rows", grid=(N_DEV,),
        in_specs=[pl.BlockSpec((None, D_MODEL, SHARD_IN), lambda j: (j, 0, 0))],
        out_specs=pl.BlockSpec((D_MODEL, SHARD_IN), lambda j: (0, j)),
        out_shape=jax.ShapeDtypeStruct((D_MODEL, D_IN), BF16), compiler_params=_cparams(1))(w_all)


def _in_proj(x, g_row, w_full, job):
    s = x.shape[0]
    tm = min(s, 1024)
    tn = 2 * SHARD_IN

    def body(x_ref, g_ref, w_ref, h_ref, u_ref):
        @pl.when(pl.program_id(1) == 0)
        def _():
            xf = x_ref[...]
            rstd = lax.rsqrt(jnp.mean(xf * xf, axis=-1, keepdims=True) + RMS_EPS)
            h_ref[...] = (xf * rstd * g_ref[...]).astype(BF16)

        u_ref[...] = jnp.dot(h_ref[...], w_ref[...], preferred_element_type=F32)

    own, extra = _hosted_call(
        body, name="in_proj", grid=(s // tm, D_IN // tn),
        in_specs=[pl.BlockSpec((tm, D_MODEL), lambda i, j: (i, 0)),
                  pl.BlockSpec((1, D_MODEL), lambda i, j: (0, 0)),
                  pl.BlockSpec((D_MODEL, tn), lambda i, j: (0, j))],
        out_specs=[pl.BlockSpec((tm, D_MODEL), lambda i, j: (i, 0)),
                   pl.BlockSpec((tm, tn), lambda i, j: (i, j))],
        out_shape=[jax.ShapeDtypeStruct((s, D_MODEL), BF16), jax.ShapeDtypeStruct((s, D_IN), F32)],
        scratch_shapes=[], args=(x, g_row, w_full), job=job)
    return own[0], own[1], extra


def _shift_copies(cs_ref, buf):
    n = buf.shape[0]
    cs_ref[0] = buf
    for sft in range(1, 8):
        cs_ref[sft] = pltpu.roll(buf, n - sft, 0)


def _conv_taps(cs_ref, w_ref, q_ref, t_rows, offs):
    def chunk(r, carry):
        r0 = pl.multiple_of(r * ROW_CHUNK, ROW_CHUNK)
        acc = jnp.zeros((ROW_CHUNK, D_BR), F32)
        for k, off in enumerate(offs):
            acc = acc + w_ref[k:k + 1, :] * cs_ref[off % 8, pl.ds(r0 + (off // 8) * 8, ROW_CHUNK), :]
        q_ref[pl.ds(r0, ROW_CHUNK), :] = acc
        return carry

    lax.fori_loop(0, t_rows // ROW_CHUNK, chunk, 0)


def _scan_fwd(a, b, h_in):
    t_rows = a.shape[0]
    row8 = lax.broadcasted_iota(jnp.int32, a.shape, 0) & (SUBLANES - 1)
    d = 1
    while d < SUBLANES:
        keep = row8 >= d
        a_s = jnp.where(keep, pltpu.roll(a, d, 0), 1.0)
        b_s = jnp.where(keep, pltpu.roll(b, d, 0), 0.0)
        b = a * b_s + b
        a = a * a_s
        d *= 2
    carry = h_in
    groups = []
    for grp in range(t_rows // SUBLANES):
        rows = slice(grp * SUBLANES, (grp + 1) * SUBLANES)
        h_g = b[rows] + a[rows] * carry
        groups.append(h_g)
        carry = h_g[SUBLANES - 1:SUBLANES]
    return jnp.concatenate(groups, axis=0)


def _scan_rev(a, b, g_in):
    t_rows = a.shape[0]
    row8 = lax.broadcasted_iota(jnp.int32, a.shape, 0) & (SUBLANES - 1)
    d = 1
    while d < SUBLANES:
        keep = row8 < SUBLANES - d
        a_s = jnp.where(keep, pltpu.roll(a, t_rows - d, 0), 1.0)
        b_s = jnp.where(keep, pltpu.roll(b, t_rows - d, 0), 0.0)
        b = a * b_s + b
        a = a * a_s
        d *= 2
    carry = g_in
    groups = []
    for grp in reversed(range(t_rows // SUBLANES)):
        rows = slice(grp * SUBLANES, (grp + 1) * SUBLANES)
        g_g = b[rows] + a[rows] * carry
        groups.append(g_g)
        carry = g_g[0:1]
    return jnp.concatenate(groups[::-1], axis=0)


def _heads_matmul(x_bf, w_ref):
    return jnp.concatenate(
        [jnp.dot(x_bf[:, h * HD:(h + 1) * HD], w_ref[h], preferred_element_type=F32) for h in range(HEADS)], axis=1)


def _heads_matmul_t(d_bf, w_ref):
    return jnp.concatenate(
        [lax.dot_general(d_bf[:, h * HD:(h + 1) * HD], w_ref[h], (((1,), (1,)), ((), ())), preferred_element_type=F32)
         for h in range(HEADS)], axis=1)


def _layer_norm_swish(q, ln_g, ln_b):
    mu = jnp.mean(q, axis=-1, keepdims=True)
    xc = q - mu
    var = jnp.mean(xc * xc, axis=-1, keepdims=True)
    rstd = lax.rsqrt(var + LN_EPS)
    n = xc * rstd
    p = n * ln_g + ln_b
    return n, rstd, p, _sig(p)


def _lru_gates(xl, xbuf_ref, w4_ref, b4, wa_ref, ba, wx_ref, bx, lam, t_rows):
    xbuf_ref[pl.ds(HALO4, t_rows), :] = xl
    xb = xbuf_ref[...]
    n = t_rows + HALO4
    xc = b4 + w4_ref[3:4, :] * xl
    for k in range(KW4 - 1):
        off = HALO4 - (KW4 - 1) + k
        xc = xc + w4_ref[k:k + 1, :] * pltpu.roll(xb, n - off, 0)[0:t_rows]
    xc_bf = xc.astype(BF16)
    r = _sig(_heads_matmul(xc_bf, wa_ref) + ba)
    ig = _sig(_heads_matmul(xc_bf, wx_ref) + bx)
    log_s = _log_sigmoid(lam)
    la = LRU_C * r * log_s
    a = jnp.exp(la)
    m = jnp.sqrt(-_expm1(2.0 * la))
    return xb, xc, xc_bf, r, ig, log_s, a, m


def _mixer_specs(layer):
    row1 = lambda i: (layer, 0, 0)
    heads = lambda i: (layer, 0, 0, 0)
    return [pl.BlockSpec((None, KW, D_BR), row1),
            pl.BlockSpec((None, 1, D_BR), row1), pl.BlockSpec((None, 1, D_BR), row1),
            pl.BlockSpec((None, 1, D_BR), row1),
            pl.BlockSpec((None, KW4, D_BR), row1), pl.BlockSpec((None, 1, D_BR), row1),
            pl.BlockSpec((None, HEADS, HD, HD), heads), pl.BlockSpec((None, 1, D_BR), row1),
            pl.BlockSpec((None, HEADS, HD, HD), heads), pl.BlockSpec((None, 1, D_BR), row1),
            pl.BlockSpec((None, 1, D_BR), row1)]


def _mixer_fwd(u, params, layer, t_rows):
    s = u.shape[0]
    nb = s // t_rows

    def body(u_ref, w31_ref, b31_ref, lng_ref, lnb_ref, w4_ref, b4_ref, wa_ref, ba_ref, wx_ref, bx_ref, lam_ref,
             y_ref, q_out_ref, h_out_ref, hb_ref, cbuf_ref, cs_ref, xbuf_ref, hcar_ref):
        @pl.when(pl.program_id(0) == 0)
        def _():
            cbuf_ref[pl.ds(0, HALO), :] = jnp.zeros((HALO, D_BR), F32)
            xbuf_ref[pl.ds(0, HALO4), :] = jnp.zeros((HALO4, D_BR), F32)
            hcar_ref[...] = jnp.zeros_like(hcar_ref)

        zc = u_ref[:, 2 * D_BR:3 * D_BR]
        c = u_ref[:, 0:D_BR] * _sig(u_ref[:, D_BR:2 * D_BR])
        cbuf_ref[pl.ds(HALO, t_rows), :] = c
        _shift_copies(cs_ref, cbuf_ref[...])
        _conv_taps(cs_ref, w31_ref, q_out_ref, t_rows, [HALO - (KW - 1) + k for k in range(KW)])
        cbuf_ref[pl.ds(0, HALO), :] = c[t_rows - HALO:t_rows]
        q = q_out_ref[...] + b31_ref[...]
        q_out_ref[...] = q
        _, _, p, sp = _layer_norm_swish(q, lng_ref[...], lnb_ref[...])
        y_ref[:, 0:D_BR] = (p * sp * (zc * _sig(zc))).astype(BF16)

        xl = u_ref[:, 3 * D_BR:4 * D_BR]
        zl = u_ref[:, 4 * D_BR:5 * D_BR]
        _, xc, _, _, ig, _, a, m = _lru_gates(xl, xbuf_ref, w4_ref, b4_ref[...], wa_ref, ba_ref[...], wx_ref,
                                              bx_ref[...], lam_ref[...], t_rows)
        xbuf_ref[pl.ds(0, HALO4), :] = xl[t_rows - HALO4:t_rows]
        h_in = hcar_ref[...]
        hb_ref[...] = h_in
        h = _scan_fwd(a, m * (ig * xc), h_in)
        h_out_ref[...] = h
        hcar_ref[...] = h_out_ref[pl.ds(t_rows - 1, 1), :]
        y_ref[:, D_BR:2 * D_BR] = (h * (zl * _sig(zl))).astype(BF16)

    blk = pl.BlockSpec((t_rows, D_BR), lambda i: (i, 0))
    return pl.pallas_call(
        body, name="mixer_fwd", grid=(nb,),
        in_specs=[pl.BlockSpec((t_rows, D_IN), lambda i: (i, 0))] + _mixer_specs(layer),
        out_specs=[pl.BlockSpec((t_rows, 2 * D_BR), lambda i: (i, 0)), blk, blk,
                   pl.BlockSpec((None, 1, D_BR), lambda i: (i, 0, 0))],
        out_shape=[jax.ShapeDtypeStruct((s, 2 * D_BR), BF16), jax.ShapeDtypeStruct((s, D_BR), F32),
                   jax.ShapeDtypeStruct((s, D_BR), F32), jax.ShapeDtypeStruct((nb, 1, D_BR), F32)],
        scratch_shapes=[pltpu.VMEM((t_rows + HALO, D_BR), F32), pltpu.VMEM((8, t_rows + HALO, D_BR), F32),
                        pltpu.VMEM((t_rows + HALO4, D_BR), F32), pltpu.VMEM((1, D_BR), F32)],
        compiler_params=_cparams(1))(u, *params)


def _out_proj(x, y, wo):
    s = x.shape[0]
    tm = min(s, 512)

    def body(x_ref, y_ref, w_ref, o_ref):
        o_ref[...] = x_ref[...] + jnp.dot(y_ref[...], w_ref[...], preferred_element_type=F32)

    blk = pl.BlockSpec((tm, D_MODEL), lambda i: (i, 0))
    return pl.pallas_call(
        body, name="out_proj", grid=(s // tm,),
        in_specs=[blk, blk, _resident((D_MODEL, D_MODEL), lambda i: (0, 0))],
        out_specs=blk,
        out_shape=jax.ShapeDtypeStruct((s, D_MODEL), F32), compiler_params=_cparams(1))(x, y, wo)


def _loss_head(x, g_row, target):
    s = x.shape[0]
    tm = min(s, 512)

    def body(x_ref, g_ref, t_ref, loss_ref, dx_ref, dg_ref):
        @pl.when(pl.program_id(0) == 0)
        def _():
            loss_ref[...] = jnp.zeros_like(loss_ref)
            dg_ref[...] = jnp.zeros_like(dg_ref)

        xf = x_ref[...]
        g = g_ref[...]
        rstd = lax.rsqrt(jnp.mean(xf * xf, axis=-1, keepdims=True) + RMS_EPS)
        n = xf * rstd
        err = n * g - t_ref[...]
        loss_ref[...] += 0.5 * jnp.sum(jnp.mean(err * err, axis=-1, keepdims=True))
        dy = err * (1.0 / D_MODEL)
        dg_ref[...] += _colsum(dy * n)
        dn = dy * g
        dx_ref[...] = rstd * (dn - n * jnp.mean(dn * n, axis=-1, keepdims=True))

    return pl.pallas_call(
        body, name="loss_head", grid=(s // tm,),
        in_specs=[pl.BlockSpec((tm, D_MODEL), lambda i: (i, 0)), pl.BlockSpec((1, D_MODEL), lambda i: (0, 0)),
                  pl.BlockSpec((tm, D_MODEL), lambda i: (i, 0))],
        out_specs=[pl.BlockSpec((8, 128), lambda i: (0, 0)), pl.BlockSpec((tm, D_MODEL), lambda i: (i, 0)),
                   pl.BlockSpec((1, D_MODEL), lambda i: (0, 0))],
        out_shape=[jax.ShapeDtypeStruct((8, 128), F32), jax.ShapeDtypeStruct((s, D_MODEL), F32),
                   jax.ShapeDtypeStruct((1, D_MODEL), F32)],
        compiler_params=_cparams(1))(x, g_row, target)


def _out_proj_bwd_x(dx, wo, job):
    s = dx.shape[0]
    tm = min(s, 512)

    def body(dx_ref, w_ref, dy_ref, dxb_ref):
        dxb = dx_ref[...].astype(BF16)
        dxb_ref[...] = dxb
        dy_ref[...] = lax.dot_general(dxb, w_ref[...], (((1,), (1,)), ((), ())), preferred_element_type=F32)

    blk = pl.BlockSpec((tm, D_MODEL), lambda i: (i, 0))
    own, extra = _hosted_call(
        body, name="out_proj_bwd_x", grid=(s // tm,),
        in_specs=[blk, _resident((D_MODEL, D_MODEL), lambda i: (0, 0))],
        out_specs=[blk, blk],
        out_shape=[jax.ShapeDtypeStruct((s, D_MODEL), F32), jax.ShapeDtypeStruct((s, D_MODEL), BF16)],
        scratch_shapes=[], args=(dx, wo), job=job)
    return own[0], own[1], extra


def _w_in_grad(h, du, job):
    s = h.shape[0]
    tk = min(s, 512)
    nk = s // tk

    def body(h_ref, du_ref, o_ref, acc_ref):
        k = pl.program_id(1)

        @pl.when(k == 0)
        def _():
            acc_ref[...] = jnp.zeros_like(acc_ref)

        acc_ref[...] += lax.dot_general(h_ref[...], du_ref[...], (((0,), (0,)), ((), ())), preferred_element_type=F32)

        @pl.when(k == nk - 1)
        def _():
            o_ref[0] = acc_ref[:, 0:SHARD_IN].astype(BF16)
            o_ref[1] = acc_ref[:, SHARD_IN:2 * SHARD_IN].astype(BF16)

    own, extra = _hosted_call(
        body, name="w_in_grad", grid=(N_DEV // 2, nk),
        in_specs=[pl.BlockSpec((tk, D_MODEL), lambda q, k: (k, 0)),
                  pl.BlockSpec((tk, 2 * SHARD_IN), lambda q, k: (k, q))],
        out_specs=[pl.BlockSpec((2, None, D_MODEL, SHARD_IN), lambda q, k: (0, q, 0, 0))],
        out_shape=[jax.ShapeDtypeStruct((2, N_DEV // 2, D_MODEL, SHARD_IN), BF16)],
        scratch_shapes=[pltpu.VMEM((D_MODEL, 2 * SHARD_IN), F32)], args=(h, du), job=job)
    return jnp.reshape(own[0], (N_DEV, D_MODEL, SHARD_IN)), extra


def _w_out_grad(y, dxb, job):
    s = y.shape[0]
    tk = min(s, 512)
    nk = s // tk
    tn = 512

    def body(y_ref, dx_ref, o_ref, acc_ref):
        k = pl.program_id(1)

        @pl.when(k == 0)
        def _():
            acc_ref[...] = jnp.zeros_like(acc_ref)

        acc_ref[...] += lax.dot_general(y_ref[...], dx_ref[...], (((0,), (0,)), ((), ())), preferred_element_type=F32)

        @pl.when(k == nk - 1)
        def _():
            for j in range(N_DEV):
                slot = (j % 2) * 4 + j // 2
                o_ref[slot] = acc_ref[pl.ds(j * SHARD_OUT, SHARD_OUT), :].astype(BF16)

    own, extra = _hosted_call(
        body, name="w_out_grad", grid=(D_MODEL // tn, nk),
        in_specs=[pl.BlockSpec((tk, D_MODEL), lambda n, k: (k, 0)),
                  pl.BlockSpec((tk, tn), lambda n, k: (k, n))],
        out_specs=[pl.BlockSpec((N_DEV, SHARD_OUT, tn), lambda n, k: (0, 0, n))],
        out_shape=[jax.ShapeDtypeStruct((N_DEV, SHARD_OUT, D_MODEL), BF16)],
        scratch_shapes=[pltpu.VMEM((D_MODEL, tn), F32)], args=(y, dxb), job=job)
    return own[0], extra


def _in_proj_bwd_x(du, w_full, x, g_row, dx_next):
    s = x.shape[0]
    tm = min(s, 512)
    tn = 512
    nn = D_MODEL // tn

    def body(du_ref, w_ref, x_ref, g_ref, dxn_ref, dx_ref, dg_ref, dh_ref):
        i, j = pl.program_id(0), pl.program_id(1)

        @pl.when((i == 0) & (j == 0))
        def _():
            dg_ref[...] = jnp.zeros_like(dg_ref)

        dh_ref[j] = lax.dot_general(du_ref[...], w_ref[...], (((1,), (1,)), ((), ())), preferred_element_type=F32)

        @pl.when(j == nn - 1)
        def _():
            xf = x_ref[...]
            dh = jnp.concatenate([dh_ref[k] for k in range(nn)], axis=1)
            rstd = lax.rsqrt(jnp.mean(xf * xf, axis=-1, keepdims=True) + RMS_EPS)
            n = xf * rstd
            dg_ref[...] += _colsum(dh * n)
            dn = dh * g_ref[...]
            dx_ref[...] = dxn_ref[...] + rstd * (dn - n * jnp.mean(dn * n, axis=-1, keepdims=True))

    return pl.pallas_call(
        body, name="in_proj_bwd_x", grid=(s // tm, nn),
        in_specs=[pl.BlockSpec((tm, D_IN), lambda i, j: (i, 0)),
                  pl.BlockSpec((tn, D_IN), lambda i, j: (j, 0)),
                  pl.BlockSpec((tm, D_MODEL), lambda i, j: (i, 0), pipeline_mode=pl.Buffered(1)),
                  pl.BlockSpec((1, D_MODEL), lambda i, j: (0, 0)),
                  pl.BlockSpec((tm, D_MODEL), lambda i, j: (i, 0), pipeline_mode=pl.Buffered(1))],
        out_specs=[pl.BlockSpec((tm, D_MODEL), lambda i, j: (i, 0)), pl.BlockSpec((1, D_MODEL), lambda i, j: (0, 0))],
        out_shape=[jax.ShapeDtypeStruct((s, D_MODEL), F32), jax.ShapeDtypeStruct((1, D_MODEL), F32)],
        scratch_shapes=[pltpu.VMEM((nn, tm, tn), F32)],
        compiler_params=_cparams(2))(du, w_full, x, g_row, dx_next)


PG_B31, PG_LNG, PG_LNB, PG_B4, PG_BA, PG_BX, PG_LAM, PG_W4 = 0, 1, 2, 3, 4, 5, 6, 8
PG_ROWS = 16


def _mixer_bwd(u, q_saved, h_saved, dy, hb, params, layer, t_rows):
    s = u.shape[0]
    nb = s // t_rows

    def body(u_ref, xh_ref, q_ref, h_ref, dy_ref, hb_ref, w31_ref, b31_ref, lng_ref, lnb_ref, w4_ref, b4_ref,
             wa_ref, ba_ref, wx_ref, bx_ref, lam_ref,
             du_ref, pg_ref, dw31_ref, dwa_ref, dwx_ref,
             cbuf_ref, cs_ref, dc_ref, dqbuf_ref, dwacc_ref, xbuf_ref, dxcbuf_ref, acar_ref, gcar_ref):
        step = pl.program_id(0)
        blk = nb - 1 - step
        not_first = jnp.where(blk == 0, 0.0, 1.0)

        @pl.when(step == 0)
        def _():
            pg_ref[...] = jnp.zeros_like(pg_ref)
            dwa_ref[...] = jnp.zeros_like(dwa_ref)
            dwx_ref[...] = jnp.zeros_like(dwx_ref)
            dwacc_ref[...] = jnp.zeros_like(dwacc_ref)
            dqbuf_ref[pl.ds(t_rows, HALO), :] = jnp.zeros((HALO, D_BR), F32)
            dxcbuf_ref[pl.ds(t_rows, HALO4), :] = jnp.zeros((HALO4, D_BR), F32)
            acar_ref[...] = jnp.zeros_like(acar_ref)
            gcar_ref[...] = jnp.zeros_like(gcar_ref)

        def add_row(r, val):
            pg_ref[r:r + 1, :] += val

        v = u_ref[:, 0:D_BR]
        g = u_ref[:, D_BR:2 * D_BR]
        zc = u_ref[:, 2 * D_BR:3 * D_BR]
        dyc = dy_ref[:, 0:D_BR]
        sg = _sig(g)
        cbuf_ref[...] = v * sg
        ln_gv = lng_ref[...]
        n, rstd, p, sp = _layer_norm_swish(q_ref[...], ln_gv, lnb_ref[...])
        sz = _sig(zc)
        du_ref[:, 2 * D_BR:3 * D_BR] = (dyc * (p * sp) * _dsilu(zc, sz)).astype(BF16)
        dp = dyc * (zc * sz) * _dsilu(p, sp)
        add_row(PG_LNG, _colsum(dp * n))
        add_row(PG_LNB, _colsum(dp))
        dn = dp * ln_gv
        dq = rstd * (dn - jnp.mean(dn, axis=-1, keepdims=True) - n * jnp.mean(dn * n, axis=-1, keepdims=True))
        add_row(PG_B31, _colsum(dq))
        dqbuf_ref[pl.ds(0, t_rows), :] = dq

        _shift_copies(cs_ref, dqbuf_ref[...])

        def conv_chunk(r, carry):
            r0 = pl.multiple_of(r * ROW_CHUNK, ROW_CHUNK)
            cc = cbuf_ref[pl.ds(r0, ROW_CHUNK), :]
            acc = jnp.zeros((ROW_CHUNK, D_BR), F32)
            for k in range(KW):
                off = KW - 1 - k
                ahead = cs_ref[off % 8, pl.ds(r0 + (off // 8) * 8, ROW_CHUNK), :]
                acc = acc + w31_ref[k:k + 1, :] * ahead
                prod = cc * ahead
                part = prod[0:8]
                for piece in range(1, ROW_CHUNK // 8):
                    part = part + prod[8 * piece:8 * piece + 8]
                dwacc_ref[k] += part
            dc_ref[pl.ds(r0, ROW_CHUNK), :] = acc
            return carry

        lax.fori_loop(0, t_rows // ROW_CHUNK, conv_chunk, 0)
        dqbuf_ref[pl.ds(t_rows, HALO), :] = dq[0:HALO]
        dc = dc_ref[...]
        du_ref[:, 0:D_BR] = (dc * sg).astype(BF16)
        du_ref[:, D_BR:2 * D_BR] = (dc * v * sg * (1.0 - sg)).astype(BF16)

        xl = u_ref[:, 3 * D_BR:4 * D_BR]
        zl = u_ref[:, 4 * D_BR:5 * D_BR]
        dyl = dy_ref[:, D_BR:2 * D_BR]
        xbuf_ref[pl.ds(0, HALO4), :] = xh_ref[...] * not_first
        xb, xc, xc_bf, r, ig, log_s, a, m = _lru_gates(xl, xbuf_ref, w4_ref, b4_ref[...], wa_ref, ba_ref[...], wx_ref,
                                                        bx_ref[...], lam_ref[...], t_rows)
        row = lax.broadcasted_iota(jnp.int32, (t_rows, D_BR), 0)
        h = h_ref[...]
        h_prev = jnp.where(row >= 1, pltpu.roll(h, 1, 0), hb_ref[...])
        szl = _sig(zl)
        du_ref[:, 4 * D_BR:5 * D_BR] = (dyl * h * _dsilu(zl, szl)).astype(BF16)
        a_next = jnp.where(row < t_rows - 1, pltpu.roll(a, t_rows - 1, 0), acar_ref[...])
        gs = _scan_rev(a_next, dyl * (zl * szl), gcar_ref[...])
        dc_ref[...] = gs
        gcar_ref[...] = dc_ref[pl.ds(0, 1), :]
        dc_ref[...] = a
        acar_ref[...] = dc_ref[pl.ds(0, 1), :]

        dm = gs * ig * xc
        di = gs * m * xc
        dla = gs * h_prev * a - dm * (a * a / m)
        add_row(PG_LAM, _colsum(dla * r) * LRU_C)
        dra = dla * (LRU_C * log_s) * r * (1.0 - r)
        dia = di * ig * (1.0 - ig)
        add_row(PG_BA, _colsum(dra))
        add_row(PG_BX, _colsum(dia))
        dra_bf = dra.astype(BF16)
        dia_bf = dia.astype(BF16)
        for hd in range(HEADS):
            sl = slice(hd * HD, (hd + 1) * HD)
            dwa_ref[hd] += lax.dot_general(xc_bf[:, sl], dra_bf[:, sl], (((0,), (0,)), ((), ())),
                                           preferred_element_type=F32)
            dwx_ref[hd] += lax.dot_general(xc_bf[:, sl], dia_bf[:, sl], (((0,), (0,)), ((), ())),
                                           preferred_element_type=F32)
        dxc = gs * m * ig + _heads_matmul_t(dra_bf, wa_ref) + _heads_matmul_t(dia_bf, wx_ref)
        add_row(PG_B4, _colsum(dxc))
        n4 = t_rows + HALO4
        add_row(PG_W4 + 3, _colsum(dxc * xl))
        dxcbuf_ref[pl.ds(0, t_rows), :] = dxc
        db = dxcbuf_ref[...]
        dxl = w4_ref[3:4, :] * dxc
        for k in range(KW4 - 1):
            ahead = pltpu.roll(db, n4 - (KW4 - 1 - k), 0)[0:t_rows]
            dxl = dxl + w4_ref[k:k + 1, :] * ahead
            add_row(PG_W4 + k, _colsum(xl * ahead))
        dxcbuf_ref[pl.ds(t_rows, HALO4), :] = dxc[0:HALO4]
        du_ref[:, 3 * D_BR:4 * D_BR] = dxl.astype(BF16)

        @pl.when(step == nb - 1)
        def _():
            pg_ref[PG_LAM:PG_LAM + 1, :] = pg_ref[PG_LAM:PG_LAM + 1, :] * _sig(-lam_ref[...])
            dw31_ref[...] = jnp.zeros_like(dw31_ref)
            for k in range(KW):
                dw31_ref[k:k + 1, :] = jnp.sum(dwacc_ref[k], axis=0, keepdims=True)

    const2 = lambda i: (0, 0)
    const3 = lambda i: (0, 0, 0)
    rev = lambda i: (nb - 1 - i, 0)
    return pl.pallas_call(
        body, name="mixer_bwd", grid=(nb,),
        in_specs=[pl.BlockSpec((t_rows, D_IN), rev),
                  pl.BlockSpec((HALO4, D_BR), lambda i: (jnp.maximum((nb - 1 - i) * (t_rows // HALO4) - 1, 0), 3)),
                  pl.BlockSpec((t_rows, D_BR), rev), pl.BlockSpec((t_rows, D_BR), rev),
                  pl.BlockSpec((t_rows, 2 * D_BR), rev),
                  pl.BlockSpec((None, 1, D_BR), lambda i: (nb - 1 - i, 0, 0))] + _mixer_specs(layer),
        out_specs=[pl.BlockSpec((t_rows, D_IN), rev),
                   pl.BlockSpec((PG_ROWS, D_BR), const2), pl.BlockSpec((32, D_BR), const2),
                   pl.BlockSpec((HEADS, HD, HD), const3), pl.BlockSpec((HEADS, HD, HD), const3)],
        out_shape=[jax.ShapeDtypeStruct((s, D_IN), BF16), jax.ShapeDtypeStruct((PG_ROWS, D_BR), F32),
                   jax.ShapeDtypeStruct((32, D_BR), F32), jax.ShapeDtypeStruct((HEADS, HD, HD), F32),
                   jax.ShapeDtypeStruct((HEADS, HD, HD), F32)],
        scratch_shapes=[pltpu.VMEM((t_rows, D_BR), F32), pltpu.VMEM((8, t_rows + HALO, D_BR), F32),
                        pltpu.VMEM((t_rows, D_BR), F32), pltpu.VMEM((t_rows + HALO, D_BR), F32),
                        pltpu.VMEM((KW, 8, D_BR), F32), pltpu.VMEM((t_rows + HALO4, D_BR), F32),
                        pltpu.VMEM((t_rows + HALO4, D_BR), F32), pltpu.VMEM((1, D_BR), F32),
                        pltpu.VMEM((1, D_BR), F32)],
        compiler_params=_cparams(1))(u, u, q_saved, h_saved, dy, hb, *params)


def _add_kept_half(src, recv, keep, out_dtype, name):
    h, r, c = recv.shape
    tr = min(r, 256)

    def body(keep_ref, s_ref, r_ref, o_ref):
        o_ref[...] = (s_ref[...].astype(F32) + r_ref[...].astype(F32)).astype(out_dtype)

    grid_spec = pltpu.PrefetchScalarGridSpec(
        num_scalar_prefetch=1, grid=(h, r // tr),
        in_specs=[pl.BlockSpec((None, tr, c), lambda b, i, kp: (kp[0] * h + b, i, 0)),
                  pl.BlockSpec((None, tr, c), lambda b, i, kp: (b, i, 0))],
        out_specs=pl.BlockSpec((None, tr, c), lambda b, i, kp: (b, i, 0)))
    return pl.pallas_call(
        body, name=name, grid_spec=grid_spec, out_shape=jax.ShapeDtypeStruct(recv.shape, out_dtype),
        compiler_params=_cparams(2))(keep, src, recv)


class _PendingReduce:
    STAGE_AXES = (2, 0, 1)

    def __init__(self, bufs):
        self.bufs = list(bufs)
        self.stage = 0

    def job(self):
        return _ExchangeJob(self.bufs, self.STAGE_AXES[self.stage])

    def absorb(self, recvs):
        me = _my_pos()[self.STAGE_AXES[self.stage]]
        keep = jnp.reshape(me, (1,)).astype(jnp.int32)
        last = self.stage == 2
        self.bufs = [_add_kept_half(b, r, keep, F32 if last else BF16, f"rs_add{self.stage}_{t}")
                     for t, (b, r) in enumerate(zip(self.bufs, recvs))]
        self.stage += 1

    def finish_alone(self):
        while self.stage < 3:
            job = self.job()
            self.absorb(_run_job(job, f"rs_exchange{self.stage}"))
        return [b[0] for b in self.bufs]


def _all_reduce_small(p):
    def body(p_ref, o_ref, r0_ref, r1_ref, r2_ref, send_sems, recv_sems):
        x, y, c = _my_pos()
        peers = [(x, y, 1 - c), (1 - x, y, c), (x, 1 - y, c)]
        o_ref[...] = p_ref[...]
        for k, (peer, r_ref) in enumerate(zip(peers, (r0_ref, r1_ref, r2_ref))):
            cp = pltpu.make_async_remote_copy(
                src_ref=o_ref, dst_ref=r_ref, send_sem=send_sems.at[k], recv_sem=recv_sems.at[k],
                device_id=peer, device_id_type=MESH)
            cp.start()
            cp.wait()
            o_ref[...] = o_ref[...] + r_ref[...]

    vm = pl.BlockSpec(memory_space=pltpu.VMEM)
    return pl.pallas_call(
        body, name="small_all_reduce", out_shape=jax.ShapeDtypeStruct(p.shape, F32), in_specs=[vm], out_specs=vm,
        scratch_shapes=[pltpu.VMEM(p.shape, F32)] * 3 + [pltpu.SemaphoreType.DMA((3,)), pltpu.SemaphoreType.DMA((3,))],
        compiler_params=pltpu.CompilerParams(vmem_limit_bytes=VMEM_LIMIT))(p)


def _adamw(w, g, m, v, name):
    r, c = w.shape
    tr = r
    for cand in (512, 256, 128, 64, 32, 16, 8):
        if r % cand == 0 and cand * c * 4 <= (2 << 20):
            tr = cand
            break

    def body(w_ref, g_ref, m_ref, v_ref, d_ref, mo_ref, vo_ref):
        gv = g_ref[...]
        m_new = ADAM_B1 * m_ref[...] + (1.0 - ADAM_B1) * gv
        v_new = ADAM_B2 * v_ref[...] + (1.0 - ADAM_B2) * (gv * gv)
        m_hat = m_new / (1.0 - ADAM_B1 ** ADAM_STEP)
        v_hat = v_new / (1.0 - ADAM_B2 ** ADAM_STEP)
        d_ref[...] = -ADAM_LR * (m_hat / (jnp.sqrt(v_hat) + ADAM_EPS) + ADAM_WD * w_ref[...])
        mo_ref[...] = m_new
        vo_ref[...] = v_new

    spec = pl.BlockSpec((tr, c), lambda i: (i, 0))
    shape = jax.ShapeDtypeStruct((r, c), F32)
    return pl.pallas_call(
        body, name=name, grid=(r // tr,), in_specs=[spec] * 4, out_specs=[spec] * 3, out_shape=[shape] * 3,
        compiler_params=_cparams(1))(w, g, m, v)


def _pack_rows(parts):
    flat = jnp.concatenate([jnp.reshape(p, (-1, D_BR)) for p in parts], axis=0)
    pad = (-flat.shape[0]) % 64
    if pad:
        flat = jnp.concatenate([flat, jnp.zeros((pad, D_BR), F32)], axis=0)
    return flat


def _unpack_rows(flat, shapes):
    out, r0 = [], 0
    for shp in shapes:
        n = 1
        for d in shp:
            n *= d
        rows = n // D_BR
        out.append(jnp.reshape(flat[r0:r0 + rows], shp))
        r0 += rows
    return out


def kernel(x, norm_g, w_in, conv_dw_w, conv_dw_b, conv_ln_g, conv_ln_b, lru_conv_w, lru_conv_b, lru_wa, lru_ba, lru_wx, lru_bx, lru_lambda, w_out, final_g, loss_target, m_norm_g, m_w_in, m_conv_dw_w, m_conv_dw_b, m_conv_ln_g, m_conv_ln_b, m_lru_conv_w, m_lru_conv_b, m_lru_wa, m_lru_ba, m_lru_wx, m_lru_bx, m_lru_lambda, m_w_out, m_final_g, v_norm_g, v_w_in, v_conv_dw_w, v_conv_dw_b, v_conv_ln_g, v_conv_ln_b, v_lru_conv_w, v_lru_conv_b, v_lru_wa, v_lru_ba, v_lru_wx, v_lru_bx, v_lru_lambda, v_w_out, v_final_g):
    n_layers = norm_g.shape[0]
    s = x.shape[1]
    t_rows = min(s, 128)
    xs = jnp.reshape(x, (s, D_MODEL))
    target = jnp.reshape(loss_target, (s, D_MODEL))
    dev = 4 * lax.axis_index("x") + 2 * lax.axis_index("y") + lax.axis_index("c")

    w_in_bf = _cast_bf16(w_in, "cast_w_in")
    w_out_bf = _cast_bf16(w_out, "cast_w_out")
    w_in_l, w31_all, w4_all = _run_job(_GatherJob([w_in_bf[0], conv_dw_w, lru_conv_w]), "weight_all_gather0")
    w_out_l = None
    w31_full = jnp.reshape(jnp.transpose(w31_all, (1, 2, 0, 3)), (n_layers, KW, D_BR))
    w4_full = jnp.reshape(jnp.transpose(w4_all, (1, 2, 0, 3)), (n_layers, KW4, D_BR))
    row3 = lambda p: jnp.reshape(p, (n_layers, 1, -1))
    mixer_params = (w31_full, row3(conv_dw_b), row3(conv_ln_g), row3(conv_ln_b), w4_full, row3(lru_conv_b),
                    lru_wa.astype(BF16), row3(lru_ba), lru_wx.astype(BF16), row3(lru_bx), row3(lru_lambda))

    saved = []
    act = xs
    for l in range(n_layers):
        wanted = [w_out_bf[0]] if l == 0 else []
        if l + 1 < n_layers:
            wanted += [w_in_bf[l + 1], w_out_bf[l + 1]]
        w_full = _w_in_rows(w_in_l)
        h, u, gathered = _in_proj(act, norm_g[l:l + 1], w_full, _GatherJob(wanted) if wanted else None)
        if l == 0:
            w_out_l, gathered = gathered[0], gathered[1:]
        y, q_sv, h_sv, hb = _mixer_fwd(u, mixer_params, l, t_rows)
        wo = jnp.reshape(w_out_l, (D_MODEL, D_MODEL))
        saved.append((act, h, u, y, q_sv, h_sv, hb, w_full, wo))
        act = _out_proj(act, y, wo)
        if gathered:
            w_in_l, w_out_l = gathered
    loss_part, dx, d_final_g = _loss_head(act, jnp.reshape(final_g, (1, D_MODEL)), target)
    loss = lax.psum(loss_part[0, 0], AXES)

    pending = None
    reduced_big = [None] * n_layers
    small = [None] * n_layers
    for l in reversed(range(n_layers)):
        x_l, h, u, y, q_sv, h_sv, hb, w_full, wo = saved[l]
        dy, dxb, recvs = _out_proj_bwd_x(dx, wo, pending.job() if pending else None)
        if pending:
            pending.absorb(recvs)
        g_out, recvs = _w_out_grad(y, dxb, pending.job() if pending else None)
        if pending:
            pending.absorb(recvs)
        du, pg, dw31, dwa, dwx = _mixer_bwd(u, q_sv, h_sv, dy, hb, mixer_params, l, t_rows)
        g_in, recvs = _w_in_grad(h, du, pending.job() if pending else None)
        if pending:
            pending.absorb(recvs)
            reduced_big[l + 1] = [b[0] for b in pending.bufs]
        dx, d_norm = _in_proj_bwd_x(du, w_full, x_l, norm_g[l:l + 1], dx)
        small[l] = (d_norm, pg, dw31, dwa, dwx)
        pending = _PendingReduce([g_in, g_out])
    reduced_big[0] = pending.finish_alone()
    grad_x = jnp.reshape(dx, x.shape)
    grad_w_in = jnp.stack([r[0] for r in reduced_big])
    grad_w_out = jnp.stack([r[1] for r in reduced_big])

    stack = lambda f: jnp.stack([f(small[l]) for l in range(n_layers)])
    pg_all = stack(lambda t: t[1])
    rep_parts = [
        (stack(lambda t: t[0][0]), norm_g.shape), (pg_all[:, PG_B31], conv_dw_b.shape),
        (pg_all[:, PG_LNG], conv_ln_g.shape), (pg_all[:, PG_LNB], conv_ln_b.shape),
        (pg_all[:, PG_B4], lru_conv_b.shape), (stack(lambda t: t[3]), lru_wa.shape), (pg_all[:, PG_BA], lru_ba.shape),
        (stack(lambda t: t[4]), lru_wx.shape), (pg_all[:, PG_BX], lru_bx.shape), (pg_all[:, PG_LAM], lru_lambda.shape),
        (d_final_g, final_g.shape)]
    shard_parts = [(stack(lambda t: t[2][0:KW]), (n_layers, KW, D_BR)),
                   (pg_all[:, PG_W4:PG_W4 + KW4], (n_layers, KW4, D_BR))]
    all_parts = rep_parts + shard_parts
    reduced = _unpack_rows(_all_reduce_small(_pack_rows([p for p, _ in all_parts])), [shp for _, shp in all_parts])
    rep_grads = reduced[:len(rep_parts)]
    grad_dw = lax.dynamic_slice_in_dim(reduced[-2], dev * HD, HD, axis=2)
    grad_w4 = lax.dynamic_slice_in_dim(reduced[-1], dev * HD, HD, axis=2)

    def adam_nd(w, g, m, v, name):
        two_d = (-1, w.shape[-1])
        outs = _adamw(*(jnp.reshape(t, two_d) for t in (w, g, m, v)), name)
        return [jnp.reshape(o, w.shape) for o in outs]

    upd = {}
    upd["w_in"] = adam_nd(w_in, grad_w_in, m_w_in, v_w_in, "adamw_w_in")
    upd["w_out"] = adam_nd(w_out, grad_w_out, m_w_out, v_w_out, "adamw_w_out")
    upd["conv_dw_w"] = adam_nd(conv_dw_w, grad_dw, m_conv_dw_w, v_conv_dw_w, "adamw_conv_dw_w")
    upd["lru_conv_w"] = adam_nd(lru_conv_w, grad_w4, m_lru_conv_w, v_lru_conv_w, "adamw_lru_conv_w")
    rep_w = [norm_g, conv_dw_b, conv_ln_g, conv_ln_b, lru_conv_b, lru_wa, lru_ba, lru_wx, lru_bx, lru_lambda, final_g]
    rep_m = [m_norm_g, m_conv_dw_b, m_conv_ln_g, m_conv_ln_b, m_lru_conv_b, m_lru_wa, m_lru_ba, m_lru_wx, m_lru_bx,
             m_lru_lambda, m_final_g]
    rep_v = [v_norm_g, v_conv_dw_b, v_conv_ln_g, v_conv_ln_b, v_lru_conv_b, v_lru_wa, v_lru_ba, v_lru_wx, v_lru_bx,
             v_lru_lambda, v_final_g]
    rep_shapes = [w.shape for w in rep_w]
    packed = _adamw(_pack_rows(rep_w), _pack_rows(rep_grads), _pack_rows(rep_m), _pack_rows(rep_v), "adamw_small")
    rep_out = [_unpack_rows(o, rep_shapes) for o in packed]
    rep_keys = ["norm_g", "conv_dw_b", "conv_ln_g", "conv_ln_b", "lru_conv_b", "lru_wa", "lru_ba", "lru_wx", "lru_bx",
                "lru_lambda", "final_g"]
    grads = {"w_in": grad_w_in, "w_out": grad_w_out, "conv_dw_w": grad_dw, "lru_conv_w": grad_w4}
    for i, key in enumerate(rep_keys):
        grads[key] = rep_grads[i]
        upd[key] = [rep_out[0][i], rep_out[1][i], rep_out[2][i]]

    order = ["norm_g", "w_in", "conv_dw_w", "conv_dw_b", "conv_ln_g", "conv_ln_b", "lru_conv_w", "lru_conv_b", "lru_wa",
             "lru_ba", "lru_wx", "lru_bx", "lru_lambda", "w_out", "final_g"]
    return (loss, grad_x, *[grads[k] for k in order], *[upd[k][0] for k in order], *[upd[k][1] for k in order],
            *[upd[k][2] for k in order])
```

```python
import functools

import jax
import jax.numpy as jnp
from jax import lax
from jax.experimental import pallas as pl
from jax.experimental.pallas import tpu as pltpu

F32 = jnp.float32
BF16 = jnp.bfloat16
MESH = pl.DeviceIdType.MESH
AXES = ("x", "y", "c")
N_DEV = 8

D_MODEL = 2048
D_BR = 1024
D_IN = 5 * D_BR
SHARD_IN = D_IN // N_DEV
SHARD_OUT = D_MODEL // N_DEV
KW = 31
KW4 = 4
HEADS = 8
HD = 128
LRU_C = 8.0
RMS_EPS = 1e-6
LN_EPS = 1e-5
SUBLANES = 8
HALO = 32
HALO4 = 8
ROW_CHUNK = 16

ADAM_LR = 0.001
ADAM_B1 = 0.9
ADAM_B2 = 0.999
ADAM_EPS = 1e-08
ADAM_WD = 0.01
ADAM_STEP = 10

VMEM_LIMIT = 60 * 1024 * 1024

ANY = pl.BlockSpec(memory_space=pl.ANY)


def _cparams(n_grid):
    return pltpu.CompilerParams(dimension_semantics=("arbitrary",) * n_grid, vmem_limit_bytes=VMEM_LIMIT)


def _resident(block_shape, index_map):
    return pl.BlockSpec(block_shape, index_map, pipeline_mode=pl.Buffered(1))


def _sig(x):
    return 0.5 * jnp.tanh(0.5 * x) + 0.5


def _dsilu(z, sz):
    return sz * (1.0 + z * (1.0 - sz))


def _expm1(x):
    small = jnp.abs(x) < 0.01
    series = x * (1.0 + x * (0.5 + x * (1.0 / 6.0 + x * (1.0 / 24.0))))
    return jnp.where(small, series, jnp.exp(x) - 1.0)


def _log_sigmoid(x):
    e = jnp.exp(-jnp.abs(x))
    l1p = jnp.where(e < 0.01, e * (1.0 - e * (0.5 - e * (1.0 / 3.0))), jnp.log(1.0 + e))
    return jnp.minimum(x, 0.0) - l1p


def _colsum(x):
    return jnp.sum(x, axis=0, keepdims=True)


def _my_pos():
    return lax.axis_index("x"), lax.axis_index("y"), lax.axis_index("c")


class _GatherJob:
    def __init__(self, shards):
        self.arrays = list(shards)
        nt = self.nt = len(self.arrays)
        self.in_specs = [ANY] * nt
        self.out_shape = [jax.ShapeDtypeStruct((N_DEV,) + s.shape, s.dtype) for s in self.arrays]
        self.out_specs = [ANY] * nt
        self.scratch = [pltpu.SemaphoreType.DMA((nt, 7)), pltpu.SemaphoreType.DMA((nt, 7)),
                        pltpu.SemaphoreType.DMA((nt,))]

    def _plan(self, srcs, outs, scr):
        send_sems, recv_sems, local_sems = scr
        x, y, c = _my_pos()
        me, sibling = (x, y, c), (x, y, 1 - c)
        chips = [(1 - x, y), (x, 1 - y), (1 - x, 1 - y)]

        def slot(p):
            return 4 * p[0] + 2 * p[1] + p[2]

        def copy(t, k, block, to, own=False):
            dst = outs[t].at[slot(block)]
            return pltpu.make_async_remote_copy(
                src_ref=srcs[t] if own else dst, dst_ref=dst, send_sem=send_sems.at[t, k], recv_sem=recv_sems.at[t, k],
                device_id=to, device_id_type=MESH)

        mine = [pltpu.make_async_copy(srcs[t], outs[t].at[slot(me)], local_sems.at[t]) for t in range(self.nt)]
        first = []
        for t in range(self.nt):
            first.append(copy(t, 0, me, sibling, own=True))
            first += [copy(t, 1 + j, me, (*chip, c), own=True) for j, chip in enumerate(chips)]
        return me, sibling, chips, c, copy, mine, first

    def start(self, srcs, outs, scr):
        _, _, _, _, _, mine, first = self._plan(srcs, outs, scr)
        for cp in mine + first:
            cp.start()

    def finish(self, srcs, outs, scr):
        me, sibling, chips, c, copy, mine, first = self._plan(srcs, outs, scr)
        passed = []
        for j, chip in enumerate(chips):
            for t in range(self.nt):
                copy(t, 1 + j, (*chip, c), me).wait_recv()
                fwd = copy(t, 4 + j, (*chip, c), sibling)
                fwd.start()
                passed.append(fwd)
        for t in range(self.nt):
            copy(t, 0, sibling, me).wait_recv()
            for j, chip in enumerate(chips):
                copy(t, 4 + j, (*chip, 1 - c), me).wait_recv()
        for cp in first + passed:
            cp.wait_send()
        for cp in mine:
            cp.wait()


class _ExchangeJob:
    def __init__(self, srcs, axis):
        self.arrays = list(srcs)
        self.axis = axis
        nt = self.nt = len(self.arrays)
        self.half = [s.shape[0] // 2 for s in self.arrays]
        self.in_specs = [ANY] * nt
        self.out_shape = [jax.ShapeDtypeStruct((h,) + s.shape[1:], s.dtype) for h, s in zip(self.half, self.arrays)]
        self.out_specs = [ANY] * nt
        self.scratch = [pltpu.SemaphoreType.DMA((nt,)), pltpu.SemaphoreType.DMA((nt,))]

    def _copies(self, srcs, outs, scr):
        send_sems, recv_sems = scr
        pos = list(_my_pos())
        me = pos[self.axis]
        pos[self.axis] = 1 - me
        return [pltpu.make_async_remote_copy(
            src_ref=srcs[t].at[pl.ds((1 - me) * self.half[t], self.half[t])], dst_ref=outs[t],
            send_sem=send_sems.at[t], recv_sem=recv_sems.at[t], device_id=tuple(pos), device_id_type=MESH)
            for t in range(self.nt)]

    def start(self, srcs, outs, scr):
        for cp in self._copies(srcs, outs, scr):
            cp.start()

    def finish(self, srcs, outs, scr):
        for cp in self._copies(srcs, outs, scr):
            cp.wait()


def _run_job(job, name):
    def body(*refs):
        ins, outs, scr = refs[:job.nt], refs[job.nt:2 * job.nt], refs[2 * job.nt:]
        job.start(ins, outs, scr)
        job.finish(ins, outs, scr)

    return pl.pallas_call(body, name=name, out_shape=job.out_shape, in_specs=job.in_specs, out_specs=job.out_specs,
                          scratch_shapes=job.scratch)(*job.arrays)


def _hosted_call(body, *, name, grid, in_specs, out_specs, out_shape, scratch_shapes, args, job):
    n_in, n_out, n_scr = len(in_specs), len(out_specs), len(scratch_shapes)
    if job is None:
        outs = pl.pallas_call(body, name=name, grid=grid, in_specs=in_specs, out_specs=out_specs, out_shape=out_shape,
                              scratch_shapes=scratch_shapes, compiler_params=_cparams(len(grid)))(*args)
        return list(outs), None
    nt = job.nt

    def full_body(*refs):
        own_in, job_in = refs[:n_in], refs[n_in:n_in + nt]
        base = n_in + nt
        own_out, job_out = refs[base:base + n_out], refs[base + n_out:base + n_out + nt]
        base += n_out + nt
        own_scr, job_scr = refs[base:base + n_scr], refs[base + n_scr:]
        ids = [pl.program_id(a) for a in range(len(grid))]
        is_first = functools.reduce(jnp.logical_and, [i == 0 for i in ids])
        is_last = functools.reduce(jnp.logical_and, [i == g - 1 for i, g in zip(ids, grid)])

        @pl.when(is_first)
        def _():
            job.start(job_in, job_out, job_scr)

        body(*own_in, *own_out, *own_scr)

        @pl.when(is_last)
        def _():
            job.finish(job_in, job_out, job_scr)

    outs = pl.pallas_call(
        full_body, name=name, grid=grid, in_specs=list(in_specs) + job.in_specs,
        out_specs=list(out_specs) + job.out_specs, out_shape=list(out_shape) + job.out_shape,
        scratch_shapes=list(scratch_shapes) + job.scratch, compiler_params=_cparams(len(grid)))(*args, *job.arrays)
    return list(outs[:n_out]), list(outs[n_out:])


def _cast_bf16(x, name):
    nl, r, c = x.shape
    tr = min(r, 512)

    def body(x_ref, o_ref):
        o_ref[...] = x_ref[...].astype(BF16)

    spec = pl.BlockSpec((None, tr, c), lambda l, i: (l, i, 0))
    return pl.pallas_call(
        body, name=name, grid=(nl, r // tr), in_specs=[spec], out_specs=spec,
        out_shape=jax.ShapeDtypeStruct(x.shape, BF16), compiler_params=_cparams(2))(x)


def _w_in_rows(w_all):
    def body(i_ref, o_ref):
        o_ref[...] = i_ref[...]

    return pl.pallas_call(
        body, name="w_in_rows", grid=(N_DEV,),
        in_specs=[pl.BlockSpec((None, D_MODEL, SHARD_IN), lambda j: (j, 0, 0))],
        out_specs=pl.BlockSpec((D_MODEL, SHARD_IN), lambda j: (0, j)),
        out_shape=jax.ShapeDtypeStruct((D_MODEL, D_IN), BF16), compiler_params=_cparams(1))(w_all)


def _in_proj(x, g_row, w_full, job):
    s = x.shape[0]
    tm = min(s, 1024)
    tn = 2 * SHARD_IN

    def body(x_ref, g_ref, w_ref, h_ref, u_ref):
        @pl.when(pl.program_id(1) == 0)
        def _():
            xf = x_ref[...]
            rstd = lax.rsqrt(jnp.mean(xf * xf, axis=-1, keepdims=True) + RMS_EPS)
            h_ref[...] = (xf * rstd * g_ref[...]).astype(BF16)

        u_ref[...] = jnp.dot(h_ref[...], w_ref[...], preferred_element_type=F32)

    own, extra = _hosted_call(
        body, name="in_proj", grid=(s // tm, D_IN // tn),
        in_specs=[pl.BlockSpec((tm, D_MODEL), lambda i, j: (i, 0)),
                  pl.BlockSpec((1, D_MODEL), lambda i, j: (0, 0)),
                  pl.BlockSpec((D_MODEL, tn), lambda i, j: (0, j))],
        out_specs=[pl.BlockSpec((tm, D_MODEL), lambda i, j: (i, 0)),
                   pl.BlockSpec((tm, tn), lambda i, j: (i, j))],
        out_shape=[jax.ShapeDtypeStruct((s, D_MODEL), BF16), jax.ShapeDtypeStruct((s, D_IN), F32)],
        scratch_shapes=[], args=(x, g_row, w_full), job=job)
    return own[0], own[1], extra


def _shift_copies(cs_ref, buf):
    n = buf.shape[0]
    cs_ref[0] = buf
    for sft in range(1, 8):
        cs_ref[sft] = pltpu.roll(buf, n - sft, 0)


def _spread_taps(wb_ref, w_ref):
    for k in range(KW):
        wb_ref[k] = jnp.broadcast_to(w_ref[k:k + 1, :], (SUBLANES, D_BR))


def _conv_taps(cs_ref, wb_ref, q_ref, t_rows, offs):
    groups = ROW_CHUNK // SUBLANES

    def chunk(r, carry):
        r0 = pl.multiple_of(r * ROW_CHUNK, ROW_CHUNK)
        accs = [jnp.zeros((SUBLANES, D_BR), F32) for _ in range(groups)]
        for k, off in enumerate(offs):
            wv = wb_ref[k]
            ahead = cs_ref[off % 8, pl.ds(r0 + (off // 8) * 8, ROW_CHUNK), :]
            accs = [acc + wv * ahead[SUBLANES * g:SUBLANES * (g + 1)] for g, acc in enumerate(accs)]
        q_ref[pl.ds(r0, ROW_CHUNK), :] = jnp.concatenate(accs, axis=0)
        return carry

    lax.fori_loop(0, t_rows // ROW_CHUNK, chunk, 0)


def _scan_fwd(a, b, h_in):
    t_rows = a.shape[0]
    row8 = lax.broadcasted_iota(jnp.int32, a.shape, 0) & (SUBLANES - 1)
    d = 1
    while d < SUBLANES:
        keep = row8 >= d
        a_s = jnp.where(keep, pltpu.roll(a, d, 0), 1.0)
        b_s = jnp.where(keep, pltpu.roll(b, d, 0), 0.0)
        b = a * b_s + b
        a = a * a_s
        d *= 2
    carry = h_in
    groups = []
    for grp in range(t_rows // SUBLANES):
        rows = slice(grp * SUBLANES, (grp + 1) * SUBLANES)
        h_g = b[rows] + a[rows] * carry
        groups.append(h_g)
        carry = h_g[SUBLANES - 1:SUBLANES]
    return jnp.concatenate(groups, axis=0)


def _scan_rev(a, b, g_in):
    t_rows = a.shape[0]
    row8 = lax.broadcasted_iota(jnp.int32, a.shape, 0) & (SUBLANES - 1)
    d = 1
    while d < SUBLANES:
        keep = row8 < SUBLANES - d
        a_s = jnp.where(keep, pltpu.roll(a, t_rows - d, 0), 1.0)
        b_s = jnp.where(keep, pltpu.roll(b, t_rows - d, 0), 0.0)
        b = a * b_s + b
        a = a * a_s
        d *= 2
    carry = g_in
    groups = []
    for grp in reversed(range(t_rows // SUBLANES)):
        rows = slice(grp * SUBLANES, (grp + 1) * SUBLANES)
        g_g = b[rows] + a[rows] * carry
        groups.append(g_g)
        carry = g_g[0:1]
    return jnp.concatenate(groups[::-1], axis=0)


def _heads_matmul(x_bf, w_ref):
    return jnp.concatenate(
        [jnp.dot(x_bf[:, h * HD:(h + 1) * HD], w_ref[h], preferred_element_type=F32) for h in range(HEADS)], axis=1)


def _heads_matmul_t(d_bf, w_ref):
    return jnp.concatenate(
        [lax.dot_general(d_bf[:, h * HD:(h + 1) * HD], w_ref[h], (((1,), (1,)), ((), ())), preferred_element_type=F32)
         for h in range(HEADS)], axis=1)


def _layer_norm_swish(q, ln_g, ln_b):
    mu = jnp.mean(q, axis=-1, keepdims=True)
    xc = q - mu
    var = jnp.mean(xc * xc, axis=-1, keepdims=True)
    rstd = lax.rsqrt(var + LN_EPS)
    n = xc * rstd
    p = n * ln_g + ln_b
    return n, rstd, p, _sig(p)


def _lru_gates(xl, xbuf_ref, w4_ref, b4, wa_ref, ba, wx_ref, bx, lam, t_rows):
    xbuf_ref[pl.ds(HALO4, t_rows), :] = xl
    xb = xbuf_ref[...]
    n = t_rows + HALO4
    xc = b4 + w4_ref[3:4, :] * xl
    for k in range(KW4 - 1):
        off = HALO4 - (KW4 - 1) + k
        xc = xc + w4_ref[k:k + 1, :] * pltpu.roll(xb, n - off, 0)[0:t_rows]
    xc_bf = xc.astype(BF16)
    r = _sig(_heads_matmul(xc_bf, wa_ref) + ba)
    ig = _sig(_heads_matmul(xc_bf, wx_ref) + bx)
    log_s = _log_sigmoid(lam)
    la = LRU_C * r * log_s
    a = jnp.exp(la)
    m = jnp.sqrt(-_expm1(2.0 * la))
    return xb, xc, xc_bf, r, ig, log_s, a, m


def _mixer_specs(layer):
    row1 = lambda i: (layer, 0, 0)
    heads = lambda i: (layer, 0, 0, 0)
    return [pl.BlockSpec((None, KW, D_BR), row1),
            pl.BlockSpec((None, 1, D_BR), row1), pl.BlockSpec((None, 1, D_BR), row1),
            pl.BlockSpec((None, 1, D_BR), row1),
            pl.BlockSpec((None, KW4, D_BR), row1), pl.BlockSpec((None, 1, D_BR), row1),
            pl.BlockSpec((None, HEADS, HD, HD), heads), pl.BlockSpec((None, 1, D_BR), row1),
            pl.BlockSpec((None, HEADS, HD, HD), heads), pl.BlockSpec((None, 1, D_BR), row1),
            pl.BlockSpec((None, 1, D_BR), row1)]


def _mixer_fwd(u, params, layer, t_rows):
    s = u.shape[0]
    nb = s // t_rows

    def body(u_ref, w31_ref, b31_ref, lng_ref, lnb_ref, w4_ref, b4_ref, wa_ref, ba_ref, wx_ref, bx_ref, lam_ref,
             y_ref, q_out_ref, h_out_ref, hb_ref, cbuf_ref, cs_ref, xbuf_ref, hcar_ref, wb_ref):
        @pl.when(pl.program_id(0) == 0)
        def _():
            _spread_taps(wb_ref, w31_ref)
            cbuf_ref[pl.ds(0, HALO), :] = jnp.zeros((HALO, D_BR), F32)
            xbuf_ref[pl.ds(0, HALO4), :] = jnp.zeros((HALO4, D_BR), F32)
            hcar_ref[...] = jnp.zeros_like(hcar_ref)

        zc = u_ref[:, 2 * D_BR:3 * D_BR]
        c = u_ref[:, 0:D_BR] * _sig(u_ref[:, D_BR:2 * D_BR])
        cbuf_ref[pl.ds(HALO, t_rows), :] = c
        _shift_copies(cs_ref, cbuf_ref[...])
        _conv_taps(cs_ref, wb_ref, q_out_ref, t_rows, [HALO - (KW - 1) + k for k in range(KW)])
        cbuf_ref[pl.ds(0, HALO), :] = c[t_rows - HALO:t_rows]
        q = q_out_ref[...] + b31_ref[...]
        q_out_ref[...] = q
        _, _, p, sp = _layer_norm_swish(q, lng_ref[...], lnb_ref[...])
        y_ref[:, 0:D_BR] = (p * sp * (zc * _sig(zc))).astype(BF16)

        xl = u_ref[:, 3 * D_BR:4 * D_BR]
        zl = u_ref[:, 4 * D_BR:5 * D_BR]
        _, xc, _, _, ig, _, a, m = _lru_gates(xl, xbuf_ref, w4_ref, b4_ref[...], wa_ref, ba_ref[...], wx_ref,
                                              bx_ref[...], lam_ref[...], t_rows)
        xbuf_ref[pl.ds(0, HALO4), :] = xl[t_rows - HALO4:t_rows]
        h_in = hcar_ref[...]
        hb_ref[...] = h_in
        h = _scan_fwd(a, m * (ig * xc), h_in)
        h_out_ref[...] = h
        hcar_ref[...] = h_out_ref[pl.ds(t_rows - 1, 1), :]
        y_ref[:, D_BR:2 * D_BR] = (h * (zl * _sig(zl))).astype(BF16)

    blk = pl.BlockSpec((t_rows, D_BR), lambda i: (i, 0))
    return pl.pallas_call(
        body, name="mixer_fwd", grid=(nb,),
        in_specs=[pl.BlockSpec((t_rows, D_IN), lambda i: (i, 0))] + _mixer_specs(layer),
        out_specs=[pl.BlockSpec((t_rows, 2 * D_BR), lambda i: (i, 0)), blk, blk,
                   pl.BlockSpec((None, 1, D_BR), lambda i: (i, 0, 0))],
        out_shape=[jax.ShapeDtypeStruct((s, 2 * D_BR), BF16), jax.ShapeDtypeStruct((s, D_BR), F32),
                   jax.ShapeDtypeStruct((s, D_BR), F32), jax.ShapeDtypeStruct((nb, 1, D_BR), F32)],
        scratch_shapes=[pltpu.VMEM((t_rows + HALO, D_BR), F32), pltpu.VMEM((8, t_rows + HALO, D_BR), F32),
                        pltpu.VMEM((t_rows + HALO4, D_BR), F32), pltpu.VMEM((1, D_BR), F32),
                        pltpu.VMEM((KW, SUBLANES, D_BR), F32)],
        compiler_params=_cparams(1))(u, *params)


def _out_proj(x, y, wo):
    s = x.shape[0]
    tm = min(s, 512)

    def body(x_ref, y_ref, w_ref, o_ref):
        o_ref[...] = x_ref[...] + jnp.dot(y_ref[...], w_ref[...], preferred_element_type=F32)

    blk = pl.BlockSpec((tm, D_MODEL), lambda i: (i, 0))
    return pl.pallas_call(
        body, name="out_proj", grid=(s // tm,),
        in_specs=[blk, blk, _resident((D_MODEL, D_MODEL), lambda i: (0, 0))],
        out_specs=blk,
        out_shape=jax.ShapeDtypeStruct((s, D_MODEL), F32), compiler_params=_cparams(1))(x, y, wo)


def _loss_head(x, g_row, target):
    s = x.shape[0]
    tm = min(s, 512)

    def body(x_ref, g_ref, t_ref, loss_ref, dx_ref, dg_ref):
        @pl.when(pl.program_id(0) == 0)
        def _():
            loss_ref[...] = jnp.zeros_like(loss_ref)
            dg_ref[...] = jnp.zeros_like(dg_ref)

        xf = x_ref[...]
        g = g_ref[...]
        rstd = lax.rsqrt(jnp.mean(xf * xf, axis=-1, keepdims=True) + RMS_EPS)
        n = xf * rstd
        err = n * g - t_ref[...]
        loss_ref[...] += 0.5 * jnp.sum(jnp.mean(err * err, axis=-1, keepdims=True))
        dy = err * (1.0 / D_MODEL)
        dg_ref[...] += _colsum(dy * n)
        dn = dy * g
        dx_ref[...] = rstd * (dn - n * jnp.mean(dn * n, axis=-1, keepdims=True))

    return pl.pallas_call(
        body, name="loss_head", grid=(s // tm,),
        in_specs=[pl.BlockSpec((tm, D_MODEL), lambda i: (i, 0)), pl.BlockSpec((1, D_MODEL), lambda i: (0, 0)),
                  pl.BlockSpec((tm, D_MODEL), lambda i: (i, 0))],
        out_specs=[pl.BlockSpec((8, 128), lambda i: (0, 0)), pl.BlockSpec((tm, D_MODEL), lambda i: (i, 0)),
                   pl.BlockSpec((1, D_MODEL), lambda i: (0, 0))],
        out_shape=[jax.ShapeDtypeStruct((8, 128), F32), jax.ShapeDtypeStruct((s, D_MODEL), F32),
                   jax.ShapeDtypeStruct((1, D_MODEL), F32)],
        compiler_params=_cparams(1))(x, g_row, target)


def _out_proj_bwd_x(dx, wo, job):
    s = dx.shape[0]
    tm = min(s, 512)

    def body(dx_ref, w_ref, dy_ref, dxb_ref):
        dxb = dx_ref[...].astype(BF16)
        dxb_ref[...] = dxb
        dy_ref[...] = lax.dot_general(dxb, w_ref[...], (((1,), (1,)), ((), ())), preferred_element_type=F32)

    blk = pl.BlockSpec((tm, D_MODEL), lambda i: (i, 0))
    own, extra = _hosted_call(
        body, name="out_proj_bwd_x", grid=(s // tm,),
        in_specs=[blk, _resident((D_MODEL, D_MODEL), lambda i: (0, 0))],
        out_specs=[blk, blk],
        out_shape=[jax.ShapeDtypeStruct((s, D_MODEL), F32), jax.ShapeDtypeStruct((s, D_MODEL), BF16)],
        scratch_shapes=[], args=(dx, wo), job=job)
    return own[0], own[1], extra


def _w_in_grad(h, du, job):
    s = h.shape[0]
    tk = min(s, 512)
    nk = s // tk

    def body(h_ref, du_ref, o_ref, acc_ref):
        k = pl.program_id(1)

        @pl.when(k == 0)
        def _():
            acc_ref[...] = jnp.zeros_like(acc_ref)

        acc_ref[...] += lax.dot_general(h_ref[...], du_ref[...], (((0,), (0,)), ((), ())), preferred_element_type=F32)

        @pl.when(k == nk - 1)
        def _():
            o_ref[0] = acc_ref[:, 0:SHARD_IN].astype(BF16)
            o_ref[1] = acc_ref[:, SHARD_IN:2 * SHARD_IN].astype(BF16)

    own, extra = _hosted_call(
        body, name="w_in_grad", grid=(N_DEV // 2, nk),
        in_specs=[pl.BlockSpec((tk, D_MODEL), lambda q, k: (k, 0)),
                  pl.BlockSpec((tk, 2 * SHARD_IN), lambda q, k: (k, q))],
        out_specs=[pl.BlockSpec((2, None, D_MODEL, SHARD_IN), lambda q, k: (0, q, 0, 0))],
        out_shape=[jax.ShapeDtypeStruct((2, N_DEV // 2, D_MODEL, SHARD_IN), BF16)],
        scratch_shapes=[pltpu.VMEM((D_MODEL, 2 * SHARD_IN), F32)], args=(h, du), job=job)
    return jnp.reshape(own[0], (N_DEV, D_MODEL, SHARD_IN)), extra


def _w_out_grad(y, dxb, job):
    s = y.shape[0]
    tk = min(s, 512)
    nk = s // tk
    tn = 512

    def body(y_ref, dx_ref, o_ref, acc_ref):
        k = pl.program_id(1)

        @pl.when(k == 0)
        def _():
            acc_ref[...] = jnp.zeros_like(acc_ref)

        acc_ref[...] += lax.dot_general(y_ref[...], dx_ref[...], (((0,), (0,)), ((), ())), preferred_element_type=F32)

        @pl.when(k == nk - 1)
        def _():
            for j in range(N_DEV):
                slot = (j % 2) * 4 + j // 2
                o_ref[slot] = acc_ref[pl.ds(j * SHARD_OUT, SHARD_OUT), :].astype(BF16)

    own, extra = _hosted_call(
        body, name="w_out_grad", grid=(D_MODEL // tn, nk),
        in_specs=[pl.BlockSpec((tk, D_MODEL), lambda n, k: (k, 0)),
                  pl.BlockSpec((tk, tn), lambda n, k: (k, n))],
        out_specs=[pl.BlockSpec((N_DEV, SHARD_OUT, tn), lambda n, k: (0, 0, n))],
        out_shape=[jax.ShapeDtypeStruct((N_DEV, SHARD_OUT, D_MODEL), BF16)],
        scratch_shapes=[pltpu.VMEM((D_MODEL, tn), F32)], args=(y, dxb), job=job)
    return own[0], extra


def _in_proj_bwd_x(du, w_full, x, g_row, dx_next, job):
    s = x.shape[0]
    tm = min(s, 512)
    tn = 512
    nn = D_MODEL // tn

    def body(du_ref, w_ref, x_ref, g_ref, dxn_ref, dx_ref, dg_ref, dh_ref):
        i, j = pl.program_id(0), pl.program_id(1)

        @pl.when((i == 0) & (j == 0))
        def _():
            dg_ref[...] = jnp.zeros_like(dg_ref)

        dh_ref[j] = lax.dot_general(du_ref[...], w_ref[...], (((1,), (1,)), ((), ())), preferred_element_type=F32)

        @pl.when(j == nn - 1)
        def _():
            xf = x_ref[...]
            dh = jnp.concatenate([dh_ref[k] for k in range(nn)], axis=1)
            rstd = lax.rsqrt(jnp.mean(xf * xf, axis=-1, keepdims=True) + RMS_EPS)
            n = xf * rstd
            dg_ref[...] += _colsum(dh * n)
            dn = dh * g_ref[...]
            dx_ref[...] = dxn_ref[...] + rstd * (dn - n * jnp.mean(dn * n, axis=-1, keepdims=True))

    own, extra = _hosted_call(
        body, name="in_proj_bwd_x", grid=(s // tm, nn),
        in_specs=[pl.BlockSpec((tm, D_IN), lambda i, j: (i, 0)),
                  pl.BlockSpec((tn, D_IN), lambda i, j: (j, 0)),
                  pl.BlockSpec((tm, D_MODEL), lambda i, j: (i, 0), pipeline_mode=pl.Buffered(1)),
                  pl.BlockSpec((1, D_MODEL), lambda i, j: (0, 0)),
                  pl.BlockSpec((tm, D_MODEL), lambda i, j: (i, 0))],
        out_specs=[pl.BlockSpec((tm, D_MODEL), lambda i, j: (i, 0)), pl.BlockSpec((1, D_MODEL), lambda i, j: (0, 0))],
        out_shape=[jax.ShapeDtypeStruct((s, D_MODEL), F32), jax.ShapeDtypeStruct((1, D_MODEL), F32)],
        scratch_shapes=[pltpu.VMEM((nn, tm, tn), F32)],
        args=(du, w_full, x, g_row, dx_next), job=job)
    return own[0], own[1], extra


PG_B31, PG_LNG, PG_LNB, PG_B4, PG_BA, PG_BX, PG_LAM, PG_W4 = 0, 1, 2, 3, 4, 5, 6, 8
PG_ROWS = 16


def _mixer_bwd(u, q_saved, h_saved, dy, hb, params, layer, t_rows):
    s = u.shape[0]
    nb = s // t_rows

    def body(u_ref, xh_ref, q_ref, h_ref, dy_ref, hb_ref, w31_ref, b31_ref, lng_ref, lnb_ref, w4_ref, b4_ref,
             wa_ref, ba_ref, wx_ref, bx_ref, lam_ref,
             du_ref, pg_ref, dw31_ref, dwa_ref, dwx_ref,
             cbuf_ref, cs_ref, dc_ref, dqbuf_ref, dwacc_ref, xbuf_ref, dxcbuf_ref, acar_ref, gcar_ref, wb_ref):
        step = pl.program_id(0)
        blk = nb - 1 - step
        not_first = jnp.where(blk == 0, 0.0, 1.0)

        @pl.when(step == 0)
        def _():
            _spread_taps(wb_ref, w31_ref)
            pg_ref[...] = jnp.zeros_like(pg_ref)
            dwa_ref[...] = jnp.zeros_like(dwa_ref)
            dwx_ref[...] = jnp.zeros_like(dwx_ref)
            dwacc_ref[...] = jnp.zeros_like(dwacc_ref)
            dqbuf_ref[pl.ds(t_rows, HALO), :] = jnp.zeros((HALO, D_BR), F32)
            dxcbuf_ref[pl.ds(t_rows, HALO4), :] = jnp.zeros((HALO4, D_BR), F32)
            acar_ref[...] = jnp.zeros_like(acar_ref)
            gcar_ref[...] = jnp.zeros_like(gcar_ref)

        def add_row(r, val):
            pg_ref[r:r + 1, :] += val

        v = u_ref[:, 0:D_BR]
        g = u_ref[:, D_BR:2 * D_BR]
        zc = u_ref[:, 2 * D_BR:3 * D_BR]
        dyc = dy_ref[:, 0:D_BR]
        sg = _sig(g)
        cbuf_ref[...] = v * sg
        ln_gv = lng_ref[...]
        n, rstd, p, sp = _layer_norm_swish(q_ref[...], ln_gv, lnb_ref[...])
        sz = _sig(zc)
        du_ref[:, 2 * D_BR:3 * D_BR] = (dyc * (p * sp) * _dsilu(zc, sz)).astype(BF16)
        dp = dyc * (zc * sz) * _dsilu(p, sp)
        add_row(PG_LNG, _colsum(dp * n))
        add_row(PG_LNB, _colsum(dp))
        dn = dp * ln_gv
        dq = rstd * (dn - jnp.mean(dn, axis=-1, keepdims=True) - n * jnp.mean(dn * n, axis=-1, keepdims=True))
        add_row(PG_B31, _colsum(dq))
        dqbuf_ref[pl.ds(0, t_rows), :] = dq

        _shift_copies(cs_ref, dqbuf_ref[...])

        groups = ROW_CHUNK // SUBLANES

        def conv_chunk(r, carry):
            r0 = pl.multiple_of(r * ROW_CHUNK, ROW_CHUNK)
            cc = cbuf_ref[pl.ds(r0, ROW_CHUNK), :]
            accs = [jnp.zeros((SUBLANES, D_BR), F32) for _ in range(groups)]
            for k in range(KW):
                off = KW - 1 - k
                wv = wb_ref[k]
                ahead = cs_ref[off % 8, pl.ds(r0 + (off // 8) * 8, ROW_CHUNK), :]
                accs = [acc + wv * ahead[SUBLANES * g:SUBLANES * (g + 1)] for g, acc in enumerate(accs)]
                prod = cc * ahead
                part = prod[0:SUBLANES]
                for g in range(1, groups):
                    part = part + prod[SUBLANES * g:SUBLANES * (g + 1)]
                dwacc_ref[k] += part
            dc_ref[pl.ds(r0, ROW_CHUNK), :] = jnp.concatenate(accs, axis=0)
            return carry

        lax.fori_loop(0, t_rows // ROW_CHUNK, conv_chunk, 0)
        dqbuf_ref[pl.ds(t_rows, HALO), :] = dq[0:HALO]
        dc = dc_ref[...]
        du_ref[:, 0:D_BR] = (dc * sg).astype(BF16)
        du_ref[:, D_BR:2 * D_BR] = (dc * v * sg * (1.0 - sg)).astype(BF16)

        xl = u_ref[:, 3 * D_BR:4 * D_BR]
        zl = u_ref[:, 4 * D_BR:5 * D_BR]
        dyl = dy_ref[:, D_BR:2 * D_BR]
        xbuf_ref[pl.ds(0, HALO4), :] = xh_ref[...] * not_first
        xb, xc, xc_bf, r, ig, log_s, a, m = _lru_gates(xl, xbuf_ref, w4_ref, b4_ref[...], wa_ref, ba_ref[...], wx_ref,
                                                        bx_ref[...], lam_ref[...], t_rows)
        row = lax.broadcasted_iota(jnp.int32, (t_rows, D_BR), 0)
        h = h_ref[...]
        h_prev = jnp.where(row >= 1, pltpu.roll(h, 1, 0), hb_ref[...])
        szl = _sig(zl)
        du_ref[:, 4 * D_BR:5 * D_BR] = (dyl * h * _dsilu(zl, szl)).astype(BF16)
        a_next = jnp.where(row < t_rows - 1, pltpu.roll(a, t_rows - 1, 0), acar_ref[...])
        gs = _scan_rev(a_next, dyl * (zl * szl), gcar_ref[...])
        dc_ref[...] = gs
        gcar_ref[...] = dc_ref[pl.ds(0, 1), :]
        dc_ref[...] = a
        acar_ref[...] = dc_ref[pl.ds(0, 1), :]

        dm = gs * ig * xc
        di = gs * m * xc
        dla = gs * h_prev * a - dm * (a * a / m)
        add_row(PG_LAM, _colsum(dla * r) * LRU_C)
        dra = dla * (LRU_C * log_s) * r * (1.0 - r)
        dia = di * ig * (1.0 - ig)
        add_row(PG_BA, _colsum(dra))
        add_row(PG_BX, _colsum(dia))
        dra_bf = dra.astype(BF16)
        dia_bf = dia.astype(BF16)
        for hd in range(HEADS):
            sl = slice(hd * HD, (hd + 1) * HD)
            dwa_ref[hd] += lax.dot_general(xc_bf[:, sl], dra_bf[:, sl], (((0,), (0,)), ((), ())),
                                           preferred_element_type=F32)
            dwx_ref[hd] += lax.dot_general(xc_bf[:, sl], dia_bf[:, sl], (((0,), (0,)), ((), ())),
                                           preferred_element_type=F32)
        dxc = gs * m * ig + _heads_matmul_t(dra_bf, wa_ref) + _heads_matmul_t(dia_bf, wx_ref)
        add_row(PG_B4, _colsum(dxc))
        n4 = t_rows + HALO4
        add_row(PG_W4 + 3, _colsum(dxc * xl))
        dxcbuf_ref[pl.ds(0, t_rows), :] = dxc
        db = dxcbuf_ref[...]
        dxl = w4_ref[3:4, :] * dxc
        for k in range(KW4 - 1):
            ahead = pltpu.roll(db, n4 - (KW4 - 1 - k), 0)[0:t_rows]
            dxl = dxl + w4_ref[k:k + 1, :] * ahead
            add_row(PG_W4 + k, _colsum(xl * ahead))
        dxcbuf_ref[pl.ds(t_rows, HALO4), :] = dxc[0:HALO4]
        du_ref[:, 3 * D_BR:4 * D_BR] = dxl.astype(BF16)

        @pl.when(step == nb - 1)
        def _():
            pg_ref[PG_LAM:PG_LAM + 1, :] = pg_ref[PG_LAM:PG_LAM + 1, :] * _sig(-lam_ref[...])
            dw31_ref[...] = jnp.zeros_like(dw31_ref)
            for k in range(KW):
                dw31_ref[k:k + 1, :] = jnp.sum(dwacc_ref[k], axis=0, keepdims=True)

    const2 = lambda i: (0, 0)
    const3 = lambda i: (0, 0, 0)
    rev = lambda i: (nb - 1 - i, 0)
    return pl.pallas_call(
        body, name="mixer_bwd", grid=(nb,),
        in_specs=[pl.BlockSpec((t_rows, D_IN), rev),
                  pl.BlockSpec((HALO4, D_BR), lambda i: (jnp.maximum((nb - 1 - i) * (t_rows // HALO4) - 1, 0), 3)),
                  pl.BlockSpec((t_rows, D_BR), rev), pl.BlockSpec((t_rows, D_BR), rev),
                  pl.BlockSpec((t_rows, 2 * D_BR), rev),
                  pl.BlockSpec((None, 1, D_BR), lambda i: (nb - 1 - i, 0, 0))] + _mixer_specs(layer),
        out_specs=[pl.BlockSpec((t_rows, D_IN), rev),
                   pl.BlockSpec((PG_ROWS, D_BR), const2), pl.BlockSpec((32, D_BR), const2),
                   pl.BlockSpec((HEADS, HD, HD), const3), pl.BlockSpec((HEADS, HD, HD), const3)],
        out_shape=[jax.ShapeDtypeStruct((s, D_IN), BF16), jax.ShapeDtypeStruct((PG_ROWS, D_BR), F32),
                   jax.ShapeDtypeStruct((32, D_BR), F32), jax.ShapeDtypeStruct((HEADS, HD, HD), F32),
                   jax.ShapeDtypeStruct((HEADS, HD, HD), F32)],
        scratch_shapes=[pltpu.VMEM((t_rows, D_BR), F32), pltpu.VMEM((8, t_rows + HALO, D_BR), F32),
                        pltpu.VMEM((t_rows, D_BR), F32), pltpu.VMEM((t_rows + HALO, D_BR), F32),
                        pltpu.VMEM((KW, 8, D_BR), F32), pltpu.VMEM((t_rows + HALO4, D_BR), F32),
                        pltpu.VMEM((t_rows + HALO4, D_BR), F32), pltpu.VMEM((1, D_BR), F32),
                        pltpu.VMEM((1, D_BR), F32), pltpu.VMEM((KW, SUBLANES, D_BR), F32)],
        compiler_params=_cparams(1))(u, u, q_saved, h_saved, dy, hb, *params)


def _add_kept_half(src, recv, keep, out_dtype, name):
    h, r, c = recv.shape
    tr = min(r, 1024)

    def body(keep_ref, s_ref, r_ref, o_ref):
        o_ref[...] = (s_ref[...].astype(F32) + r_ref[...].astype(F32)).astype(out_dtype)

    grid_spec = pltpu.PrefetchScalarGridSpec(
        num_scalar_prefetch=1, grid=(h, r // tr),
        in_specs=[pl.BlockSpec((None, tr, c), lambda b, i, kp: (kp[0] * h + b, i, 0)),
                  pl.BlockSpec((None, tr, c), lambda b, i, kp: (b, i, 0))],
        out_specs=pl.BlockSpec((None, tr, c), lambda b, i, kp: (b, i, 0)))
    return pl.pallas_call(
        body, name=name, grid_spec=grid_spec, out_shape=jax.ShapeDtypeStruct(recv.shape, out_dtype),
        compiler_params=_cparams(2))(keep, src, recv)


class _PendingReduce:
    STAGE_AXES = (2, 0, 1)

    def __init__(self, bufs):
        self.bufs = list(bufs)
        self.stage = 0

    def job(self):
        return _ExchangeJob(self.bufs, self.STAGE_AXES[self.stage])

    def absorb(self, recvs):
        me = _my_pos()[self.STAGE_AXES[self.stage]]
        keep = jnp.reshape(me, (1,)).astype(jnp.int32)
        last = self.stage == 2
        self.bufs = [_add_kept_half(b, r, keep, F32 if last else BF16, f"rs_add{self.stage}_{t}")
                     for t, (b, r) in enumerate(zip(self.bufs, recvs))]
        self.stage += 1

    def finish_alone(self):
        while self.stage < 3:
            job = self.job()
            self.absorb(_run_job(job, f"rs_exchange{self.stage}"))
        return [b[0] for b in self.bufs]


def _all_reduce_small(p):
    def body(p_ref, o_ref, r0_ref, r1_ref, r2_ref, send_sems, recv_sems):
        x, y, c = _my_pos()
        peers = [(x, y, 1 - c), (1 - x, y, c), (x, 1 - y, c)]
        o_ref[...] = p_ref[...]
        for k, (peer, r_ref) in enumerate(zip(peers, (r0_ref, r1_ref, r2_ref))):
            cp = pltpu.make_async_remote_copy(
                src_ref=o_ref, dst_ref=r_ref, send_sem=send_sems.at[k], recv_sem=recv_sems.at[k],
                device_id=peer, device_id_type=MESH)
            cp.start()
            cp.wait()
            o_ref[...] = o_ref[...] + r_ref[...]

    vm = pl.BlockSpec(memory_space=pltpu.VMEM)
    return pl.pallas_call(
        body, name="small_all_reduce", out_shape=jax.ShapeDtypeStruct(p.shape, F32), in_specs=[vm], out_specs=vm,
        scratch_shapes=[pltpu.VMEM(p.shape, F32)] * 3 + [pltpu.SemaphoreType.DMA((3,)), pltpu.SemaphoreType.DMA((3,))],
        compiler_params=pltpu.CompilerParams(vmem_limit_bytes=VMEM_LIMIT))(p)


def _adamw(w, g, m, v, name):
    r, c = w.shape
    tr = r
    for cand in (512, 256, 128, 64, 32, 16, 8):
        if r % cand == 0 and cand * c * 4 <= (2 << 20):
            tr = cand
            break

    def body(w_ref, g_ref, m_ref, v_ref, d_ref, mo_ref, vo_ref):
        gv = g_ref[...]
        m_new = ADAM_B1 * m_ref[...] + (1.0 - ADAM_B1) * gv
        v_new = ADAM_B2 * v_ref[...] + (1.0 - ADAM_B2) * (gv * gv)
        m_hat = m_new / (1.0 - ADAM_B1 ** ADAM_STEP)
        v_hat = v_new / (1.0 - ADAM_B2 ** ADAM_STEP)
        d_ref[...] = -ADAM_LR * (m_hat / (jnp.sqrt(v_hat) + ADAM_EPS) + ADAM_WD * w_ref[...])
        mo_ref[...] = m_new
        vo_ref[...] = v_new

    spec = pl.BlockSpec((tr, c), lambda i: (i, 0))
    shape = jax.ShapeDtypeStruct((r, c), F32)
    return pl.pallas_call(
        body, name=name, grid=(r // tr,), in_specs=[spec] * 4, out_specs=[spec] * 3, out_shape=[shape] * 3,
        compiler_params=_cparams(1))(w, g, m, v)


def _pack_rows(parts):
    flat = jnp.concatenate([jnp.reshape(p, (-1, D_BR)) for p in parts], axis=0)
    pad = (-flat.shape[0]) % 64
    if pad:
        flat = jnp.concatenate([flat, jnp.zeros((pad, D_BR), F32)], axis=0)
    return flat


def _unpack_rows(flat, shapes):
    out, r0 = [], 0
    for shp in shapes:
        n = 1
        for d in shp:
            n *= d
        rows = n // D_BR
        out.append(jnp.reshape(flat[r0:r0 + rows], shp))
        r0 += rows
    return out


def kernel(x, norm_g, w_in, conv_dw_w, conv_dw_b, conv_ln_g, conv_ln_b, lru_conv_w, lru_conv_b, lru_wa, lru_ba, lru_wx, lru_bx, lru_lambda, w_out, final_g, loss_target, m_norm_g, m_w_in, m_conv_dw_w, m_conv_dw_b, m_conv_ln_g, m_conv_ln_b, m_lru_conv_w, m_lru_conv_b, m_lru_wa, m_lru_ba, m_lru_wx, m_lru_bx, m_lru_lambda, m_w_out, m_final_g, v_norm_g, v_w_in, v_conv_dw_w, v_conv_dw_b, v_conv_ln_g, v_conv_ln_b, v_lru_conv_w, v_lru_conv_b, v_lru_wa, v_lru_ba, v_lru_wx, v_lru_bx, v_lru_lambda, v_w_out, v_final_g):
    n_layers = norm_g.shape[0]
    s = x.shape[1]
    t_rows = min(s, 128)
    xs = jnp.reshape(x, (s, D_MODEL))
    target = jnp.reshape(loss_target, (s, D_MODEL))
    dev = 4 * lax.axis_index("x") + 2 * lax.axis_index("y") + lax.axis_index("c")

    w_in_bf = _cast_bf16(w_in, "cast_w_in")
    w_out_bf = _cast_bf16(w_out, "cast_w_out")
    w_in_l, w31_all, w4_all = _run_job(_GatherJob([w_in_bf[0], conv_dw_w, lru_conv_w]), "weight_all_gather0")
    w_out_l = None
    w31_full = jnp.reshape(jnp.transpose(w31_all, (1, 2, 0, 3)), (n_layers, KW, D_BR))
    w4_full = jnp.reshape(jnp.transpose(w4_all, (1, 2, 0, 3)), (n_layers, KW4, D_BR))
    row3 = lambda p: jnp.reshape(p, (n_layers, 1, -1))
    mixer_params = (w31_full, row3(conv_dw_b), row3(conv_ln_g), row3(conv_ln_b), w4_full, row3(lru_conv_b),
                    lru_wa.astype(BF16), row3(lru_ba), lru_wx.astype(BF16), row3(lru_bx), row3(lru_lambda))

    saved = []
    act = xs
    for l in range(n_layers):
        wanted = [w_out_bf[0]] if l == 0 else []
        if l + 1 < n_layers:
            wanted += [w_in_bf[l + 1], w_out_bf[l + 1]]
        w_full = _w_in_rows(w_in_l)
        h, u, gathered = _in_proj(act, norm_g[l:l + 1], w_full, _GatherJob(wanted) if wanted else None)
        if l == 0:
            w_out_l, gathered = gathered[0], gathered[1:]
        y, q_sv, h_sv, hb = _mixer_fwd(u, mixer_params, l, t_rows)
        wo = jnp.reshape(w_out_l, (D_MODEL, D_MODEL))
        saved.append((act, h, u, y, q_sv, h_sv, hb, w_full, wo))
        act = _out_proj(act, y, wo)
        if gathered:
            w_in_l, w_out_l = gathered
    loss_part, dx, d_final_g = _loss_head(act, jnp.reshape(final_g, (1, D_MODEL)), target)
    loss = lax.psum(loss_part[0, 0], AXES)

    pending = None
    reduced_big = [None] * n_layers
    small = [None] * n_layers
    for l in reversed(range(n_layers)):
        x_l, h, u, y, q_sv, h_sv, hb, w_full, wo = saved[l]
        dy, dxb, recvs = _out_proj_bwd_x(dx, wo, pending.job() if pending else None)
        if pending:
            pending.absorb(recvs)
            reduced_big[l + 1] = [b[0] for b in pending.bufs]
        g_out, _ = _w_out_grad(y, dxb, None)
        du, pg, dw31, dwa, dwx = _mixer_bwd(u, q_sv, h_sv, dy, hb, mixer_params, l, t_rows)
        g_in, _ = _w_in_grad(h, du, None)
        pending = _PendingReduce([g_in, g_out])
        pending.absorb(_run_job(pending.job(), "rs_exchange_c"))
        dx, d_norm, recvs = _in_proj_bwd_x(du, w_full, x_l, norm_g[l:l + 1], dx, pending.job())
        pending.absorb(recvs)
        small[l] = (d_norm, pg, dw31, dwa, dwx)
    reduced_big[0] = pending.finish_alone()
    grad_x = jnp.reshape(dx, x.shape)
    grad_w_in = jnp.stack([r[0] for r in reduced_big])
    grad_w_out = jnp.stack([r[1] for r in reduced_big])

    stack = lambda f: jnp.stack([f(small[l]) for l in range(n_layers)])
    pg_all = stack(lambda t: t[1])
    rep_parts = [
        (stack(lambda t: t[0][0]), norm_g.shape), (pg_all[:, PG_B31], conv_dw_b.shape),
        (pg_all[:, PG_LNG], conv_ln_g.shape), (pg_all[:, PG_LNB], conv_ln_b.shape),
        (pg_all[:, PG_B4], lru_conv_b.shape), (stack(lambda t: t[3]), lru_wa.shape), (pg_all[:, PG_BA], lru_ba.shape),
        (stack(lambda t: t[4]), lru_wx.shape), (pg_all[:, PG_BX], lru_bx.shape), (pg_all[:, PG_LAM], lru_lambda.shape),
        (d_final_g, final_g.shape)]
    shard_parts = [(stack(lambda t: t[2][0:KW]), (n_layers, KW, D_BR)),
                   (pg_all[:, PG_W4:PG_W4 + KW4], (n_layers, KW4, D_BR))]
    all_parts = rep_parts + shard_parts
    reduced = _unpack_rows(_all_reduce_small(_pack_rows([p for p, _ in all_parts])), [shp for _, shp in all_parts])
    rep_grads = reduced[:len(rep_parts)]
    grad_dw = lax.dynamic_slice_in_dim(reduced[-2], dev * HD, HD, axis=2)
    grad_w4 = lax.dynamic_slice_in_dim(reduced[-1], dev * HD, HD, axis=2)

    def adam_nd(w, g, m, v, name):
        two_d = (-1, w.shape[-1])
        outs = _adamw(*(jnp.reshape(t, two_d) for t in (w, g, m, v)), name)
        return [jnp.reshape(o, w.shape) for o in outs]

    upd = {}
    upd["w_in"] = adam_nd(w_in, grad_w_in, m_w_in, v_w_in, "adamw_w_in")
    upd["w_out"] = adam_nd(w_out, grad_w_out, m_w_out, v_w_out, "adamw_w_out")
    upd["conv_dw_w"] = adam_nd(conv_dw_w, grad_dw, m_conv_dw_w, v_conv_dw_w, "adamw_conv_dw_w")
    upd["lru_conv_w"] = adam_nd(lru_conv_w, grad_w4, m_lru_conv_w, v_lru_conv_w, "adamw_lru_conv_w")
    rep_w = [norm_g, conv_dw_b, conv_ln_g, conv_ln_b, lru_conv_b, lru_wa, lru_ba, lru_wx, lru_bx, lru_lambda, final_g]
    rep_m = [m_norm_g, m_conv_dw_b, m_conv_ln_g, m_conv_ln_b, m_lru_conv_b, m_lru_wa, m_lru_ba, m_lru_wx, m_lru_bx,
             m_lru_lambda, m_final_g]
    rep_v = [v_norm_g, v_conv_dw_b, v_conv_ln_g, v_conv_ln_b, v_lru_conv_b, v_lru_wa, v_lru_ba, v_lru_wx, v_lru_bx,
             v_lru_lambda, v_final_g]
    rep_shapes = [w.shape for w in rep_w]
    packed = _adamw(_pack_rows(rep_w), _pack_rows(rep_grads), _pack_rows(rep_m), _pack_rows(rep_v), "adamw_small")
    rep_out = [_unpack_rows(o, rep_shapes) for o in packed]
    rep_keys = ["norm_g", "conv_dw_b", "conv_ln_g", "conv_ln_b", "lru_conv_b", "lru_wa", "lru_ba", "lru_wx", "lru_bx",
                "lru_lambda", "final_g"]
    grads = {"w_in": grad_w_in, "w_out": grad_w_out, "conv_dw_w": grad_dw, "lru_conv_w": grad_w4}
    for i, key in enumerate(rep_keys):
        grads[key] = rep_grads[i]
        upd[key] = [rep_out[0][i], rep_out[1][i], rep_out[2][i]]

    order = ["norm_g", "w_in", "conv_dw_w", "conv_dw_b", "conv_ln_g", "conv_ln_b", "lru_conv_w", "lru_conv_b", "lru_wa",
             "lru_ba", "lru_wx", "lru_bx", "lru_lambda", "w_out", "final_g"]
    return (loss, grad_x, *[grads[k] for k in order], *[upd[k][0] for k in order], *[upd[k][1] for k in order],
            *[upd[k][2] for k in order])
```

```python
import functools

import jax
import jax.numpy as jnp
from jax import lax
from jax.experimental import pallas as pl
from jax.experimental.pallas import tpu as pltpu

F32 = jnp.float32
BF16 = jnp.bfloat16
MESH = pl.DeviceIdType.MESH
AXES = ("x", "y", "c")
N_DEV = 8

D_MODEL = 2048
D_BR = 1024
D_IN = 5 * D_BR
SHARD_IN = D_IN // N_DEV
SHARD_OUT = D_MODEL // N_DEV
KW = 31
KW4 = 4
HEADS = 8
HD = 128
LRU_C = 8.0
RMS_EPS = 1e-6
LN_EPS = 1e-5
SUBLANES = 8
HALO = 32
HALO4 = 8
ROW_CHUNK = 16

ADAM_LR = 0.001
ADAM_B1 = 0.9
ADAM_B2 = 0.999
ADAM_EPS = 1e-08
ADAM_WD = 0.01
ADAM_STEP = 10

VMEM_LIMIT = 60 * 1024 * 1024

ANY = pl.BlockSpec(memory_space=pl.ANY)


def _cparams(n_grid):
    return pltpu.CompilerParams(dimension_semantics=("arbitrary",) * n_grid, vmem_limit_bytes=VMEM_LIMIT)


def _resident(block_shape, index_map):
    return pl.BlockSpec(block_shape, index_map, pipeline_mode=pl.Buffered(1))


def _sig(x):
    return 0.5 * jnp.tanh(0.5 * x) + 0.5


def _dsilu(z, sz):
    return sz * (1.0 + z * (1.0 - sz))


def _expm1(x):
    small = jnp.abs(x) < 0.01
    series = x * (1.0 + x * (0.5 + x * (1.0 / 6.0 + x * (1.0 / 24.0))))
    return jnp.where(small, series, jnp.exp(x) - 1.0)


def _log_sigmoid(x):
    e = jnp.exp(-jnp.abs(x))
    l1p = jnp.where(e < 0.01, e * (1.0 - e * (0.5 - e * (1.0 / 3.0))), jnp.log(1.0 + e))
    return jnp.minimum(x, 0.0) - l1p


def _colsum(x):
    return jnp.sum(x, axis=0, keepdims=True)


def _my_pos():
    return lax.axis_index("x"), lax.axis_index("y"), lax.axis_index("c")


class _GatherJob:
    def __init__(self, shards):
        self.arrays = list(shards)
        nt = self.nt = len(self.arrays)
        self.in_specs = [ANY] * nt
        self.out_shape = [jax.ShapeDtypeStruct((N_DEV,) + s.shape, s.dtype) for s in self.arrays]
        self.out_specs = [ANY] * nt
        self.scratch = [pltpu.SemaphoreType.DMA((nt, 7)), pltpu.SemaphoreType.DMA((nt, 7)),
                        pltpu.SemaphoreType.DMA((nt,))]

    def _plan(self, srcs, outs, scr):
        send_sems, recv_sems, local_sems = scr
        x, y, c = _my_pos()
        me, sibling = (x, y, c), (x, y, 1 - c)
        chips = [(1 - x, y), (x, 1 - y), (1 - x, 1 - y)]

        def slot(p):
            return 4 * p[0] + 2 * p[1] + p[2]

        def copy(t, k, block, to, own=False):
            dst = outs[t].at[slot(block)]
            return pltpu.make_async_remote_copy(
                src_ref=srcs[t] if own else dst, dst_ref=dst, send_sem=send_sems.at[t, k], recv_sem=recv_sems.at[t, k],
                device_id=to, device_id_type=MESH)

        mine = [pltpu.make_async_copy(srcs[t], outs[t].at[slot(me)], local_sems.at[t]) for t in range(self.nt)]
        first = []
        for t in range(self.nt):
            first.append(copy(t, 0, me, sibling, own=True))
            first += [copy(t, 1 + j, me, (*chip, c), own=True) for j, chip in enumerate(chips)]
        return me, sibling, chips, c, copy, mine, first

    def start(self, srcs, outs, scr):
        _, _, _, _, _, mine, first = self._plan(srcs, outs, scr)
        for cp in mine + first:
            cp.start()

    def finish(self, srcs, outs, scr):
        me, sibling, chips, c, copy, mine, first = self._plan(srcs, outs, scr)
        passed = []
        for j, chip in enumerate(chips):
            for t in range(self.nt):
                copy(t, 1 + j, (*chip, c), me).wait_recv()
                fwd = copy(t, 4 + j, (*chip, c), sibling)
                fwd.start()
                passed.append(fwd)
        for t in range(self.nt):
            copy(t, 0, sibling, me).wait_recv()
            for j, chip in enumerate(chips):
                copy(t, 4 + j, (*chip, 1 - c), me).wait_recv()
        for cp in first + passed:
            cp.wait_send()
        for cp in mine:
            cp.wait()


class _ExchangeJob:
    def __init__(self, srcs, axis):
        self.arrays = list(srcs)
        self.axis = axis
        nt = self.nt = len(self.arrays)
        self.half = [s.shape[0] // 2 for s in self.arrays]
        self.in_specs = [ANY] * nt
        self.out_shape = [jax.ShapeDtypeStruct((h,) + s.shape[1:], s.dtype) for h, s in zip(self.half, self.arrays)]
        self.out_specs = [ANY] * nt
        self.scratch = [pltpu.SemaphoreType.DMA((nt,)), pltpu.SemaphoreType.DMA((nt,))]

    def _copies(self, srcs, outs, scr):
        send_sems, recv_sems = scr
        pos = list(_my_pos())
        me = pos[self.axis]
        pos[self.axis] = 1 - me
        return [pltpu.make_async_remote_copy(
            src_ref=srcs[t].at[pl.ds((1 - me) * self.half[t], self.half[t])], dst_ref=outs[t],
            send_sem=send_sems.at[t], recv_sem=recv_sems.at[t], device_id=tuple(pos), device_id_type=MESH)
            for t in range(self.nt)]

    def start(self, srcs, outs, scr):
        for cp in self._copies(srcs, outs, scr):
            cp.start()

    def finish(self, srcs, outs, scr):
        for cp in self._copies(srcs, outs, scr):
            cp.wait()


def _run_job(job, name):
    def body(*refs):
        ins, outs, scr = refs[:job.nt], refs[job.nt:2 * job.nt], refs[2 * job.nt:]
        job.start(ins, outs, scr)
        job.finish(ins, outs, scr)

    return pl.pallas_call(body, name=name, out_shape=job.out_shape, in_specs=job.in_specs, out_specs=job.out_specs,
                          scratch_shapes=job.scratch)(*job.arrays)


def _hosted_call(body, *, name, grid, in_specs, out_specs, out_shape, scratch_shapes, args, job):
    n_in, n_out, n_scr = len(in_specs), len(out_specs), len(scratch_shapes)
    if job is None:
        outs = pl.pallas_call(body, name=name, grid=grid, in_specs=in_specs, out_specs=out_specs, out_shape=out_shape,
                              scratch_shapes=scratch_shapes, compiler_params=_cparams(len(grid)))(*args)
        return list(outs), None
    nt = job.nt

    def full_body(*refs):
        own_in, job_in = refs[:n_in], refs[n_in:n_in + nt]
        base = n_in + nt
        own_out, job_out = refs[base:base + n_out], refs[base + n_out:base + n_out + nt]
        base += n_out + nt
        own_scr, job_scr = refs[base:base + n_scr], refs[base + n_scr:]
        ids = [pl.program_id(a) for a in range(len(grid))]
        is_first = functools.reduce(jnp.logical_and, [i == 0 for i in ids])
        is_last = functools.reduce(jnp.logical_and, [i == g - 1 for i, g in zip(ids, grid)])

        @pl.when(is_first)
        def _():
            job.start(job_in, job_out, job_scr)

        body(*own_in, *own_out, *own_scr)

        @pl.when(is_last)
        def _():
            job.finish(job_in, job_out, job_scr)

    outs = pl.pallas_call(
        full_body, name=name, grid=grid, in_specs=list(in_specs) + job.in_specs,
        out_specs=list(out_specs) + job.out_specs, out_shape=list(out_shape) + job.out_shape,
        scratch_shapes=list(scratch_shapes) + job.scratch, compiler_params=_cparams(len(grid)))(*args, *job.arrays)
    return list(outs[:n_out]), list(outs[n_out:])


def _cast_bf16(x, name):
    nl, r, c = x.shape
    tr = min(r, 512)

    def body(x_ref, o_ref):
        o_ref[...] = x_ref[...].astype(BF16)

    spec = pl.BlockSpec((None, tr, c), lambda l, i: (l, i, 0))
    return pl.pallas_call(
        body, name=name, grid=(nl, r // tr), in_specs=[spec], out_specs=spec,
        out_shape=jax.ShapeDtypeStruct(x.shape, BF16), compiler_params=_cparams(2))(x)


def _w_in_rows(w_all):
    def body(i_ref, o_ref):
        o_ref[...] = i_ref[...]

    return pl.pallas_call(
        body, name="w_in_rows", grid=(N_DEV,),
        in_specs=[pl.BlockSpec((None, D_MODEL, SHARD_IN), lambda j: (j, 0, 0))],
        out_specs=pl.BlockSpec((D_MODEL, SHARD_IN), lambda j: (0, j)),
        out_shape=jax.ShapeDtypeStruct((D_MODEL, D_IN), BF16), compiler_params=_cparams(1))(w_all)


def _in_proj(x, g_row, w_full, job):
    s = x.shape[0]
    tm = min(s, 1024)
    tn = 2 * SHARD_IN

    def body(x_ref, g_ref, w_ref, h_ref, u_ref):
        @pl.when(pl.program_id(1) == 0)
        def _():
            xf = x_ref[...]
            rstd = lax.rsqrt(jnp.mean(xf * xf, axis=-1, keepdims=True) + RMS_EPS)
            h_ref[...] = (xf * rstd * g_ref[...]).astype(BF16)

        u_ref[...] = jnp.dot(h_ref[...], w_ref[...], preferred_element_type=F32)

    own, extra = _hosted_call(
        body, name="in_proj", grid=(s // tm, D_IN // tn),
        in_specs=[pl.BlockSpec((tm, D_MODEL), lambda i, j: (i, 0)),
                  pl.BlockSpec((1, D_MODEL), lambda i, j: (0, 0)),
                  pl.BlockSpec((D_MODEL, tn), lambda i, j: (0, j))],
        out_specs=[pl.BlockSpec((tm, D_MODEL), lambda i, j: (i, 0)),
                   pl.BlockSpec((tm, tn), lambda i, j: (i, j))],
        out_shape=[jax.ShapeDtypeStruct((s, D_MODEL), BF16), jax.ShapeDtypeStruct((s, D_IN), F32)],
        scratch_shapes=[], args=(x, g_row, w_full), job=job)
    return own[0], own[1], extra


def _shift_copies(cs_ref, buf):
    n = buf.shape[0]
    cs_ref[0] = buf
    for sft in range(1, 8):
        cs_ref[sft] = pltpu.roll(buf, n - sft, 0)


def _spread_taps(wb_ref, w_ref):
    for k in range(KW):
        wb_ref[k] = jnp.broadcast_to(w_ref[k:k + 1, :], (SUBLANES, D_BR))


def _conv_taps(cs_ref, wb_ref, q_ref, t_rows, offs):
    groups = ROW_CHUNK // SUBLANES

    def chunk(r, carry):
        r0 = pl.multiple_of(r * ROW_CHUNK, ROW_CHUNK)
        accs = [jnp.zeros((SUBLANES, D_BR), F32) for _ in range(groups)]
        for k, off in enumerate(offs):
            wv = wb_ref[k]
            ahead = cs_ref[off % 8, pl.ds(r0 + (off // 8) * 8, ROW_CHUNK), :]
            accs = [acc + wv * ahead[SUBLANES * g:SUBLANES * (g + 1)] for g, acc in enumerate(accs)]
        q_ref[pl.ds(r0, ROW_CHUNK), :] = jnp.concatenate(accs, axis=0)
        return carry

    lax.fori_loop(0, t_rows // ROW_CHUNK, chunk, 0)


def _scan_fwd(a, b, h_in):
    t_rows = a.shape[0]
    row8 = lax.broadcasted_iota(jnp.int32, a.shape, 0) & (SUBLANES - 1)
    d = 1
    while d < SUBLANES:
        keep = row8 >= d
        a_s = jnp.where(keep, pltpu.roll(a, d, 0), 1.0)
        b_s = jnp.where(keep, pltpu.roll(b, d, 0), 0.0)
        b = a * b_s + b
        a = a * a_s
        d *= 2
    carry = h_in
    groups = []
    for grp in range(t_rows // SUBLANES):
        rows = slice(grp * SUBLANES, (grp + 1) * SUBLANES)
        h_g = b[rows] + a[rows] * carry
        groups.append(h_g)
        carry = h_g[SUBLANES - 1:SUBLANES]
    return jnp.concatenate(groups, axis=0)


def _scan_rev(a, b, g_in):
    t_rows = a.shape[0]
    row8 = lax.broadcasted_iota(jnp.int32, a.shape, 0) & (SUBLANES - 1)
    d = 1
    while d < SUBLANES:
        keep = row8 < SUBLANES - d
        a_s = jnp.where(keep, pltpu.roll(a, t_rows - d, 0), 1.0)
        b_s = jnp.where(keep, pltpu.roll(b, t_rows - d, 0), 0.0)
        b = a * b_s + b
        a = a * a_s
        d *= 2
    carry = g_in
    groups = []
    for grp in reversed(range(t_rows // SUBLANES)):
        rows = slice(grp * SUBLANES, (grp + 1) * SUBLANES)
        g_g = b[rows] + a[rows] * carry
        groups.append(g_g)
        carry = g_g[0:1]
    return jnp.concatenate(groups[::-1], axis=0)


def _heads_matmul(x_bf, w_ref):
    return jnp.concatenate(
        [jnp.dot(x_bf[:, h * HD:(h + 1) * HD], w_ref[h], preferred_element_type=F32) for h in range(HEADS)], axis=1)


def _heads_matmul_t(d_bf, w_ref):
    return jnp.concatenate(
        [lax.dot_general(d_bf[:, h * HD:(h + 1) * HD], w_ref[h], (((1,), (1,)), ((), ())), preferred_element_type=F32)
         for h in range(HEADS)], axis=1)


def _layer_norm_swish(q, ln_g, ln_b):
    mu = jnp.mean(q, axis=-1, keepdims=True)
    xc = q - mu
    var = jnp.mean(xc * xc, axis=-1, keepdims=True)
    rstd = lax.rsqrt(var + LN_EPS)
    n = xc * rstd
    p = n * ln_g + ln_b
    return n, rstd, p, _sig(p)


def _lru_gates(xl, xbuf_ref, w4_ref, b4, wa_ref, ba, wx_ref, bx, lam, t_rows):
    xbuf_ref[pl.ds(HALO4, t_rows), :] = xl
    xb = xbuf_ref[...]
    n = t_rows + HALO4
    xc = b4 + w4_ref[3:4, :] * xl
    for k in range(KW4 - 1):
        off = HALO4 - (KW4 - 1) + k
        xc = xc + w4_ref[k:k + 1, :] * pltpu.roll(xb, n - off, 0)[0:t_rows]
    xc_bf = xc.astype(BF16)
    r = _sig(_heads_matmul(xc_bf, wa_ref) + ba)
    ig = _sig(_heads_matmul(xc_bf, wx_ref) + bx)
    log_s = _log_sigmoid(lam)
    la = LRU_C * r * log_s
    a = jnp.exp(la)
    m = jnp.sqrt(-_expm1(2.0 * la))
    return xb, xc, xc_bf, r, ig, log_s, a, m


def _mixer_specs(layer):
    row1 = lambda i: (layer, 0, 0)
    heads = lambda i: (layer, 0, 0, 0)
    return [pl.BlockSpec((None, KW, D_BR), row1),
            pl.BlockSpec((None, 1, D_BR), row1), pl.BlockSpec((None, 1, D_BR), row1),
            pl.BlockSpec((None, 1, D_BR), row1),
            pl.BlockSpec((None, KW4, D_BR), row1), pl.BlockSpec((None, 1, D_BR), row1),
            pl.BlockSpec((None, HEADS, HD, HD), heads), pl.BlockSpec((None, 1, D_BR), row1),
            pl.BlockSpec((None, HEADS, HD, HD), heads), pl.BlockSpec((None, 1, D_BR), row1),
            pl.BlockSpec((None, 1, D_BR), row1)]


def _mixer_fwd(u, params, layer, t_rows):
    s = u.shape[0]
    nb = s // t_rows

    def body(u_ref, w31_ref, b31_ref, lng_ref, lnb_ref, w4_ref, b4_ref, wa_ref, ba_ref, wx_ref, bx_ref, lam_ref,
             y_ref, q_out_ref, h_out_ref, hb_ref, xc_out_ref, r_out_ref, ig_out_ref, a_out_ref, m_out_ref,
             cbuf_ref, cs_ref, xbuf_ref, hcar_ref, wb_ref):
        @pl.when(pl.program_id(0) == 0)
        def _():
            _spread_taps(wb_ref, w31_ref)
            cbuf_ref[pl.ds(0, HALO), :] = jnp.zeros((HALO, D_BR), F32)
            xbuf_ref[pl.ds(0, HALO4), :] = jnp.zeros((HALO4, D_BR), F32)
            hcar_ref[...] = jnp.zeros_like(hcar_ref)

        zc = u_ref[:, 2 * D_BR:3 * D_BR]
        c = u_ref[:, 0:D_BR] * _sig(u_ref[:, D_BR:2 * D_BR])
        cbuf_ref[pl.ds(HALO, t_rows), :] = c
        _shift_copies(cs_ref, cbuf_ref[...])
        _conv_taps(cs_ref, wb_ref, q_out_ref, t_rows, [HALO - (KW - 1) + k for k in range(KW)])
        cbuf_ref[pl.ds(0, HALO), :] = c[t_rows - HALO:t_rows]
        q = q_out_ref[...] + b31_ref[...]
        q_out_ref[...] = q
        _, _, p, sp = _layer_norm_swish(q, lng_ref[...], lnb_ref[...])
        y_ref[:, 0:D_BR] = (p * sp * (zc * _sig(zc))).astype(BF16)

        xl = u_ref[:, 3 * D_BR:4 * D_BR]
        zl = u_ref[:, 4 * D_BR:5 * D_BR]
        _, xc, _, r, ig, _, a, m = _lru_gates(xl, xbuf_ref, w4_ref, b4_ref[...], wa_ref, ba_ref[...], wx_ref,
                                              bx_ref[...], lam_ref[...], t_rows)
        xbuf_ref[pl.ds(0, HALO4), :] = xl[t_rows - HALO4:t_rows]
        xc_out_ref[...] = xc
        r_out_ref[...] = r
        ig_out_ref[...] = ig
        a_out_ref[...] = a
        m_out_ref[...] = m
        h_in = hcar_ref[...]
        hb_ref[...] = h_in
        h = _scan_fwd(a, m * (ig * xc), h_in)
        h_out_ref[...] = h
        hcar_ref[...] = h_out_ref[pl.ds(t_rows - 1, 1), :]
        y_ref[:, D_BR:2 * D_BR] = (h * (zl * _sig(zl))).astype(BF16)

    blk = pl.BlockSpec((t_rows, D_BR), lambda i: (i, 0))
    return pl.pallas_call(
        body, name="mixer_fwd", grid=(nb,),
        in_specs=[pl.BlockSpec((t_rows, D_IN), lambda i: (i, 0))] + _mixer_specs(layer),
        out_specs=[pl.BlockSpec((t_rows, 2 * D_BR), lambda i: (i, 0)), blk, blk,
                   pl.BlockSpec((None, 1, D_BR), lambda i: (i, 0, 0))] + [blk] * 5,
        out_shape=[jax.ShapeDtypeStruct((s, 2 * D_BR), BF16), jax.ShapeDtypeStruct((s, D_BR), F32),
                   jax.ShapeDtypeStruct((s, D_BR), F32), jax.ShapeDtypeStruct((nb, 1, D_BR), F32)]
        + [jax.ShapeDtypeStruct((s, D_BR), F32)] * 5,
        scratch_shapes=[pltpu.VMEM((t_rows + HALO, D_BR), F32), pltpu.VMEM((8, t_rows + HALO, D_BR), F32),
                        pltpu.VMEM((t_rows + HALO4, D_BR), F32), pltpu.VMEM((1, D_BR), F32),
                        pltpu.VMEM((KW, SUBLANES, D_BR), F32)],
        compiler_params=_cparams(1))(u, *params)


def _out_proj(x, y, wo):
    s = x.shape[0]
    tm = min(s, 512)

    def body(x_ref, y_ref, w_ref, o_ref):
        o_ref[...] = x_ref[...] + jnp.dot(y_ref[...], w_ref[...], preferred_element_type=F32)

    blk = pl.BlockSpec((tm, D_MODEL), lambda i: (i, 0))
    return pl.pallas_call(
        body, name="out_proj", grid=(s // tm,),
        in_specs=[blk, blk, _resident((D_MODEL, D_MODEL), lambda i: (0, 0))],
        out_specs=blk,
        out_shape=jax.ShapeDtypeStruct((s, D_MODEL), F32), compiler_params=_cparams(1))(x, y, wo)


def _loss_head(x, g_row, target):
    s = x.shape[0]
    tm = min(s, 512)

    def body(x_ref, g_ref, t_ref, loss_ref, dx_ref, dg_ref):
        @pl.when(pl.program_id(0) == 0)
        def _():
            loss_ref[...] = jnp.zeros_like(loss_ref)
            dg_ref[...] = jnp.zeros_like(dg_ref)

        xf = x_ref[...]
        g = g_ref[...]
        rstd = lax.rsqrt(jnp.mean(xf * xf, axis=-1, keepdims=True) + RMS_EPS)
        n = xf * rstd
        err = n * g - t_ref[...]
        loss_ref[...] += 0.5 * jnp.sum(jnp.mean(err * err, axis=-1, keepdims=True))
        dy = err * (1.0 / D_MODEL)
        dg_ref[...] += _colsum(dy * n)
        dn = dy * g
        dx_ref[...] = rstd * (dn - n * jnp.mean(dn * n, axis=-1, keepdims=True))

    return pl.pallas_call(
        body, name="loss_head", grid=(s // tm,),
        in_specs=[pl.BlockSpec((tm, D_MODEL), lambda i: (i, 0)), pl.BlockSpec((1, D_MODEL), lambda i: (0, 0)),
                  pl.BlockSpec((tm, D_MODEL), lambda i: (i, 0))],
        out_specs=[pl.BlockSpec((8, 128), lambda i: (0, 0)), pl.BlockSpec((tm, D_MODEL), lambda i: (i, 0)),
                   pl.BlockSpec((1, D_MODEL), lambda i: (0, 0))],
        out_shape=[jax.ShapeDtypeStruct((8, 128), F32), jax.ShapeDtypeStruct((s, D_MODEL), F32),
                   jax.ShapeDtypeStruct((1, D_MODEL), F32)],
        compiler_params=_cparams(1))(x, g_row, target)


def _out_proj_bwd_x(dx, wo, job):
    s = dx.shape[0]
    tm = min(s, 512)

    def body(dx_ref, w_ref, dy_ref, dxb_ref):
        dxb = dx_ref[...].astype(BF16)
        dxb_ref[...] = dxb
        dy_ref[...] = lax.dot_general(dxb, w_ref[...], (((1,), (1,)), ((), ())), preferred_element_type=F32)

    blk = pl.BlockSpec((tm, D_MODEL), lambda i: (i, 0))
    own, extra = _hosted_call(
        body, name="out_proj_bwd_x", grid=(s // tm,),
        in_specs=[blk, _resident((D_MODEL, D_MODEL), lambda i: (0, 0))],
        out_specs=[blk, blk],
        out_shape=[jax.ShapeDtypeStruct((s, D_MODEL), F32), jax.ShapeDtypeStruct((s, D_MODEL), BF16)],
        scratch_shapes=[], args=(dx, wo), job=job)
    return own[0], own[1], extra


def _w_in_grad(h, du, job):
    s = h.shape[0]
    tk = min(s, 1024)
    nk = s // tk

    def body(h_ref, du_ref, o_ref, acc_ref):
        k = pl.program_id(1)

        @pl.when(k == 0)
        def _():
            acc_ref[...] = jnp.zeros_like(acc_ref)

        acc_ref[...] += lax.dot_general(h_ref[...], du_ref[...], (((0,), (0,)), ((), ())), preferred_element_type=F32)

        @pl.when(k == nk - 1)
        def _():
            o_ref[0] = acc_ref[:, 0:SHARD_IN].astype(BF16)
            o_ref[1] = acc_ref[:, SHARD_IN:2 * SHARD_IN].astype(BF16)

    own, extra = _hosted_call(
        body, name="w_in_grad", grid=(N_DEV // 2, nk),
        in_specs=[pl.BlockSpec((tk, D_MODEL), lambda q, k: (k, 0)),
                  pl.BlockSpec((tk, 2 * SHARD_IN), lambda q, k: (k, q))],
        out_specs=[pl.BlockSpec((2, None, D_MODEL, SHARD_IN), lambda q, k: (0, q, 0, 0))],
        out_shape=[jax.ShapeDtypeStruct((2, N_DEV // 2, D_MODEL, SHARD_IN), BF16)],
        scratch_shapes=[pltpu.VMEM((D_MODEL, 2 * SHARD_IN), F32)], args=(h, du), job=job)
    return jnp.reshape(own[0], (N_DEV, D_MODEL, SHARD_IN)), extra


def _w_out_grad(y, dxb, job):
    s = y.shape[0]
    tk = min(s, 1024)
    nk = s // tk
    tn = 512

    def body(y_ref, dx_ref, o_ref, acc_ref):
        k = pl.program_id(1)

        @pl.when(k == 0)
        def _():
            acc_ref[...] = jnp.zeros_like(acc_ref)

        acc_ref[...] += lax.dot_general(y_ref[...], dx_ref[...], (((0,), (0,)), ((), ())), preferred_element_type=F32)

        @pl.when(k == nk - 1)
        def _():
            for j in range(N_DEV):
                slot = (j % 2) * 4 + j // 2
                o_ref[slot] = acc_ref[pl.ds(j * SHARD_OUT, SHARD_OUT), :].astype(BF16)

    own, extra = _hosted_call(
        body, name="w_out_grad", grid=(D_MODEL // tn, nk),
        in_specs=[pl.BlockSpec((tk, D_MODEL), lambda n, k: (k, 0)),
                  pl.BlockSpec((tk, tn), lambda n, k: (k, n))],
        out_specs=[pl.BlockSpec((N_DEV, SHARD_OUT, tn), lambda n, k: (0, 0, n))],
        out_shape=[jax.ShapeDtypeStruct((N_DEV, SHARD_OUT, D_MODEL), BF16)],
        scratch_shapes=[pltpu.VMEM((D_MODEL, tn), F32)], args=(y, dxb), job=job)
    return own[0], extra


def _in_proj_bwd_x(du, w_full, x, g_row, dx_next, job):
    s = x.shape[0]
    tm = min(s, 512)
    tn = 512
    nn = D_MODEL // tn

    def body(du_ref, w_ref, x_ref, g_ref, dxn_ref, dx_ref, dg_ref, dh_ref):
        i, j = pl.program_id(0), pl.program_id(1)

        @pl.when((i == 0) & (j == 0))
        def _():
            dg_ref[...] = jnp.zeros_like(dg_ref)

        dh_ref[j] = lax.dot_general(du_ref[...], w_ref[...], (((1,), (1,)), ((), ())), preferred_element_type=F32)

        @pl.when(j == nn - 1)
        def _():
            xf = x_ref[...]
            dh = jnp.concatenate([dh_ref[k] for k in range(nn)], axis=1)
            rstd = lax.rsqrt(jnp.mean(xf * xf, axis=-1, keepdims=True) + RMS_EPS)
            n = xf * rstd
            dg_ref[...] += _colsum(dh * n)
            dn = dh * g_ref[...]
            dx_ref[...] = dxn_ref[...] + rstd * (dn - n * jnp.mean(dn * n, axis=-1, keepdims=True))

    own, extra = _hosted_call(
        body, name="in_proj_bwd_x", grid=(s // tm, nn),
        in_specs=[pl.BlockSpec((tm, D_IN), lambda i, j: (i, 0)),
                  pl.BlockSpec((tn, D_IN), lambda i, j: (j, 0)),
                  pl.BlockSpec((tm, D_MODEL), lambda i, j: (i, 0), pipeline_mode=pl.Buffered(1)),
                  pl.BlockSpec((1, D_MODEL), lambda i, j: (0, 0)),
                  pl.BlockSpec((tm, D_MODEL), lambda i, j: (i, 0))],
        out_specs=[pl.BlockSpec((tm, D_MODEL), lambda i, j: (i, 0)), pl.BlockSpec((1, D_MODEL), lambda i, j: (0, 0))],
        out_shape=[jax.ShapeDtypeStruct((s, D_MODEL), F32), jax.ShapeDtypeStruct((1, D_MODEL), F32)],
        scratch_shapes=[pltpu.VMEM((nn, tm, tn), F32)],
        args=(du, w_full, x, g_row, dx_next), job=job)
    return own[0], own[1], extra


PG_B31, PG_LNG, PG_LNB, PG_B4, PG_BA, PG_BX, PG_LAM, PG_W4 = 0, 1, 2, 3, 4, 5, 6, 8
PG_ROWS = 16


def _mixer_bwd(u, kept, dy, hb, params, layer, t_rows):
    s = u.shape[0]
    nb = s // t_rows

    def body(u_ref, q_ref, h_ref, xc_ref, r_ref, ig_ref, a_ref, m_ref, dy_ref, hb_ref,
             w31_ref, b31_ref, lng_ref, lnb_ref, w4_ref, b4_ref, wa_ref, ba_ref, wx_ref, bx_ref, lam_ref,
             du_ref, pg_ref, dw31_ref, dwa_ref, dwx_ref,
             cbuf_ref, cs_ref, dc_ref, dqbuf_ref, dwacc_ref, dxcbuf_ref, acar_ref, gcar_ref, wb_ref):
        step = pl.program_id(0)

        @pl.when(step == 0)
        def _():
            _spread_taps(wb_ref, w31_ref)
            pg_ref[...] = jnp.zeros_like(pg_ref)
            dwa_ref[...] = jnp.zeros_like(dwa_ref)
            dwx_ref[...] = jnp.zeros_like(dwx_ref)
            dwacc_ref[...] = jnp.zeros_like(dwacc_ref)
            dqbuf_ref[pl.ds(t_rows, HALO), :] = jnp.zeros((HALO, D_BR), F32)
            dxcbuf_ref[pl.ds(t_rows, HALO4), :] = jnp.zeros((HALO4, D_BR), F32)
            acar_ref[...] = jnp.zeros_like(acar_ref)
            gcar_ref[...] = jnp.zeros_like(gcar_ref)

        def add_row(r, val):
            pg_ref[r:r + 1, :] += val

        v = u_ref[:, 0:D_BR]
        g = u_ref[:, D_BR:2 * D_BR]
        zc = u_ref[:, 2 * D_BR:3 * D_BR]
        dyc = dy_ref[:, 0:D_BR]
        sg = _sig(g)
        cbuf_ref[...] = v * sg
        ln_gv = lng_ref[...]
        n, rstd, p, sp = _layer_norm_swish(q_ref[...], ln_gv, lnb_ref[...])
        sz = _sig(zc)
        du_ref[:, 2 * D_BR:3 * D_BR] = (dyc * (p * sp) * _dsilu(zc, sz)).astype(BF16)
        dp = dyc * (zc * sz) * _dsilu(p, sp)
        add_row(PG_LNG, _colsum(dp * n))
        add_row(PG_LNB, _colsum(dp))
        dn = dp * ln_gv
        dq = rstd * (dn - jnp.mean(dn, axis=-1, keepdims=True) - n * jnp.mean(dn * n, axis=-1, keepdims=True))
        add_row(PG_B31, _colsum(dq))
        dqbuf_ref[pl.ds(0, t_rows), :] = dq

        _shift_copies(cs_ref, dqbuf_ref[...])

        groups = ROW_CHUNK // SUBLANES

        def conv_chunk(r, carry):
            r0 = pl.multiple_of(r * ROW_CHUNK, ROW_CHUNK)
            cc = cbuf_ref[pl.ds(r0, ROW_CHUNK), :]
            accs = [jnp.zeros((SUBLANES, D_BR), F32) for _ in range(groups)]
            for k in range(KW):
                off = KW - 1 - k
                wv = wb_ref[k]
                ahead = cs_ref[off % 8, pl.ds(r0 + (off // 8) * 8, ROW_CHUNK), :]
                accs = [acc + wv * ahead[SUBLANES * g:SUBLANES * (g + 1)] for g, acc in enumerate(accs)]
                prod = cc * ahead
                part = prod[0:SUBLANES]
                for g in range(1, groups):
                    part = part + prod[SUBLANES * g:SUBLANES * (g + 1)]
                dwacc_ref[k] += part
            dc_ref[pl.ds(r0, ROW_CHUNK), :] = jnp.concatenate(accs, axis=0)
            return carry

        lax.fori_loop(0, t_rows // ROW_CHUNK, conv_chunk, 0)
        dqbuf_ref[pl.ds(t_rows, HALO), :] = dq[0:HALO]
        dc = dc_ref[...]
        du_ref[:, 0:D_BR] = (dc * sg).astype(BF16)
        du_ref[:, D_BR:2 * D_BR] = (dc * v * sg * (1.0 - sg)).astype(BF16)

        xl = u_ref[:, 3 * D_BR:4 * D_BR]
        zl = u_ref[:, 4 * D_BR:5 * D_BR]
        dyl = dy_ref[:, D_BR:2 * D_BR]
        xc = xc_ref[...]
        xc_bf = xc.astype(BF16)
        r = r_ref[...]
        ig = ig_ref[...]
        a = a_ref[...]
        m = m_ref[...]
        log_s = _log_sigmoid(lam_ref[...])
        row = lax.broadcasted_iota(jnp.int32, (t_rows, D_BR), 0)
        h = h_ref[...]
        h_prev = jnp.where(row >= 1, pltpu.roll(h, 1, 0), hb_ref[...])
        szl = _sig(zl)
        du_ref[:, 4 * D_BR:5 * D_BR] = (dyl * h * _dsilu(zl, szl)).astype(BF16)
        a_next = jnp.where(row < t_rows - 1, pltpu.roll(a, t_rows - 1, 0), acar_ref[...])
        gs = _scan_rev(a_next, dyl * (zl * szl), gcar_ref[...])
        dc_ref[...] = gs
        gcar_ref[...] = dc_ref[pl.ds(0, 1), :]
        dc_ref[...] = a
        acar_ref[...] = dc_ref[pl.ds(0, 1), :]

        dm = gs * ig * xc
        di = gs * m * xc
        dla = gs * h_prev * a - dm * (a * a / m)
        add_row(PG_LAM, _colsum(dla * r) * LRU_C)
        dra = dla * (LRU_C * log_s) * r * (1.0 - r)
        dia = di * ig * (1.0 - ig)
        add_row(PG_BA, _colsum(dra))
        add_row(PG_BX, _colsum(dia))
        dra_bf = dra.astype(BF16)
        dia_bf = dia.astype(BF16)
        for hd in range(HEADS):
            sl = slice(hd * HD, (hd + 1) * HD)
            dwa_ref[hd] += lax.dot_general(xc_bf[:, sl], dra_bf[:, sl], (((0,), (0,)), ((), ())),
                                           preferred_element_type=F32)
            dwx_ref[hd] += lax.dot_general(xc_bf[:, sl], dia_bf[:, sl], (((0,), (0,)), ((), ())),
                                           preferred_element_type=F32)
        dxc = gs * m * ig + _heads_matmul_t(dra_bf, wa_ref) + _heads_matmul_t(dia_bf, wx_ref)
        add_row(PG_B4, _colsum(dxc))
        n4 = t_rows + HALO4
        add_row(PG_W4 + 3, _colsum(dxc * xl))
        dxcbuf_ref[pl.ds(0, t_rows), :] = dxc
        db = dxcbuf_ref[...]
        dxl = w4_ref[3:4, :] * dxc
        for k in range(KW4 - 1):
            ahead = pltpu.roll(db, n4 - (KW4 - 1 - k), 0)[0:t_rows]
            dxl = dxl + w4_ref[k:k + 1, :] * ahead
            add_row(PG_W4 + k, _colsum(xl * ahead))
        dxcbuf_ref[pl.ds(t_rows, HALO4), :] = dxc[0:HALO4]
        du_ref[:, 3 * D_BR:4 * D_BR] = dxl.astype(BF16)

        @pl.when(step == nb - 1)
        def _():
            pg_ref[PG_LAM:PG_LAM + 1, :] = pg_ref[PG_LAM:PG_LAM + 1, :] * _sig(-lam_ref[...])
            dw31_ref[...] = jnp.zeros_like(dw31_ref)
            for k in range(KW):
                dw31_ref[k:k + 1, :] = jnp.sum(dwacc_ref[k], axis=0, keepdims=True)

    const2 = lambda i: (0, 0)
    const3 = lambda i: (0, 0, 0)
    rev = lambda i: (nb - 1 - i, 0)
    return pl.pallas_call(
        body, name="mixer_bwd", grid=(nb,),
        in_specs=[pl.BlockSpec((t_rows, D_IN), rev)] + [pl.BlockSpec((t_rows, D_BR), rev)] * len(kept) + [
                  pl.BlockSpec((t_rows, 2 * D_BR), rev),
                  pl.BlockSpec((None, 1, D_BR), lambda i: (nb - 1 - i, 0, 0))] + _mixer_specs(layer),
        out_specs=[pl.BlockSpec((t_rows, D_IN), rev),
                   pl.BlockSpec((PG_ROWS, D_BR), const2), pl.BlockSpec((32, D_BR), const2),
                   pl.BlockSpec((HEADS, HD, HD), const3), pl.BlockSpec((HEADS, HD, HD), const3)],
        out_shape=[jax.ShapeDtypeStruct((s, D_IN), BF16), jax.ShapeDtypeStruct((PG_ROWS, D_BR), F32),
                   jax.ShapeDtypeStruct((32, D_BR), F32), jax.ShapeDtypeStruct((HEADS, HD, HD), F32),
                   jax.ShapeDtypeStruct((HEADS, HD, HD), F32)],
        scratch_shapes=[pltpu.VMEM((t_rows, D_BR), F32), pltpu.VMEM((8, t_rows + HALO, D_BR), F32),
                        pltpu.VMEM((t_rows, D_BR), F32), pltpu.VMEM((t_rows + HALO, D_BR), F32),
                        pltpu.VMEM((KW, 8, D_BR), F32),
                        pltpu.VMEM((t_rows + HALO4, D_BR), F32), pltpu.VMEM((1, D_BR), F32),
                        pltpu.VMEM((1, D_BR), F32), pltpu.VMEM((KW, SUBLANES, D_BR), F32)],
        compiler_params=_cparams(1))(u, *kept, dy, hb, *params)


def _add_kept_half(src, recv, keep, out_dtype, name):
    h, r, c = recv.shape
    tr = min(r, 1024)

    def body(keep_ref, s_ref, r_ref, o_ref):
        o_ref[...] = (s_ref[...].astype(F32) + r_ref[...].astype(F32)).astype(out_dtype)

    grid_spec = pltpu.PrefetchScalarGridSpec(
        num_scalar_prefetch=1, grid=(h, r // tr),
        in_specs=[pl.BlockSpec((None, tr, c), lambda b, i, kp: (kp[0] * h + b, i, 0)),
                  pl.BlockSpec((None, tr, c), lambda b, i, kp: (b, i, 0))],
        out_specs=pl.BlockSpec((None, tr, c), lambda b, i, kp: (b, i, 0)))
    return pl.pallas_call(
        body, name=name, grid_spec=grid_spec, out_shape=jax.ShapeDtypeStruct(recv.shape, out_dtype),
        compiler_params=_cparams(2))(keep, src, recv)


class _PendingReduce:
    STAGE_AXES = (2, 0, 1)

    def __init__(self, bufs):
        self.bufs = list(bufs)
        self.stage = 0

    def job(self):
        return _ExchangeJob(self.bufs, self.STAGE_AXES[self.stage])

    def absorb(self, recvs):
        me = _my_pos()[self.STAGE_AXES[self.stage]]
        keep = jnp.reshape(me, (1,)).astype(jnp.int32)
        last = self.stage == 2
        self.bufs = [_add_kept_half(b, r, keep, F32 if last else BF16, f"rs_add{self.stage}_{t}")
                     for t, (b, r) in enumerate(zip(self.bufs, recvs))]
        self.stage += 1

    def finish_alone(self):
        while self.stage < 3:
            job = self.job()
            self.absorb(_run_job(job, f"rs_exchange{self.stage}"))
        return [b[0] for b in self.bufs]


def _all_reduce_small(pa, pb):
    def body(pa_ref, pb_ref, oa_ref, ob_ref, ra0, ra1, ra2, sb0, sb1, sb2, rb0, rb1, rb2, send_sems, recv_sems):
        x, y, c = _my_pos()
        peers = [(x, y, 1 - c), (1 - x, y, c), (x, 1 - y, c)]
        oa_ref[...] = pa_ref[...]
        ob_ref[...] = pb_ref[...]
        for k, (peer, ra, sb, rb) in enumerate(zip(peers, (ra0, ra1, ra2), (sb0, sb1, sb2), (rb0, rb1, rb2))):
            sb[...] = ob_ref[...].astype(BF16)
            copies = [pltpu.make_async_remote_copy(
                src_ref=src, dst_ref=dst, send_sem=send_sems.at[t, k], recv_sem=recv_sems.at[t, k],
                device_id=peer, device_id_type=MESH) for t, (src, dst) in enumerate(((oa_ref, ra), (sb, rb)))]
            for cp in copies:
                cp.start()
            for cp in copies:
                cp.wait()
            oa_ref[...] = oa_ref[...] + ra[...]
            ob_ref[...] = sb[...].astype(F32) + rb[...].astype(F32)

    vm = pl.BlockSpec(memory_space=pltpu.VMEM)
    return pl.pallas_call(
        body, name="small_all_reduce",
        out_shape=[jax.ShapeDtypeStruct(pa.shape, F32), jax.ShapeDtypeStruct(pb.shape, F32)],
        in_specs=[vm, vm], out_specs=[vm, vm],
        scratch_shapes=[pltpu.VMEM(pa.shape, F32)] * 3 + [pltpu.VMEM(pb.shape, BF16)] * 6
        + [pltpu.SemaphoreType.DMA((2, 3)), pltpu.SemaphoreType.DMA((2, 3))],
        compiler_params=pltpu.CompilerParams(vmem_limit_bytes=VMEM_LIMIT))(pa, pb)


def _adamw(w, g, m, v, name):
    r, c = w.shape
    tr = r
    for cand in (512, 256, 128, 64, 32, 16, 8):
        if r % cand == 0 and cand * c * 4 <= (2 << 20):
            tr = cand
            break

    def body(w_ref, g_ref, m_ref, v_ref, d_ref, mo_ref, vo_ref):
        gv = g_ref[...]
        m_new = ADAM_B1 * m_ref[...] + (1.0 - ADAM_B1) * gv
        v_new = ADAM_B2 * v_ref[...] + (1.0 - ADAM_B2) * (gv * gv)
        m_hat = m_new / (1.0 - ADAM_B1 ** ADAM_STEP)
        v_hat = v_new / (1.0 - ADAM_B2 ** ADAM_STEP)
        d_ref[...] = -ADAM_LR * (m_hat / (jnp.sqrt(v_hat) + ADAM_EPS) + ADAM_WD * w_ref[...])
        mo_ref[...] = m_new
        vo_ref[...] = v_new

    spec = pl.BlockSpec((tr, c), lambda i: (i, 0))
    shape = jax.ShapeDtypeStruct((r, c), F32)
    return pl.pallas_call(
        body, name=name, grid=(r // tr,), in_specs=[spec] * 4, out_specs=[spec] * 3, out_shape=[shape] * 3,
        compiler_params=_cparams(1))(w, g, m, v)


def _pack_rows(parts):
    flat = jnp.concatenate([jnp.reshape(p, (-1, D_BR)) for p in parts], axis=0)
    pad = (-flat.shape[0]) % 64
    if pad:
        flat = jnp.concatenate([flat, jnp.zeros((pad, D_BR), F32)], axis=0)
    return flat


def _unpack_rows(flat, shapes):
    out, r0 = [], 0
    for shp in shapes:
        n = 1
        for d in shp:
            n *= d
        rows = n // D_BR
        out.append(jnp.reshape(flat[r0:r0 + rows], shp))
        r0 += rows
    return out


def kernel(x, norm_g, w_in, conv_dw_w, conv_dw_b, conv_ln_g, conv_ln_b, lru_conv_w, lru_conv_b, lru_wa, lru_ba, lru_wx, lru_bx, lru_lambda, w_out, final_g, loss_target, m_norm_g, m_w_in, m_conv_dw_w, m_conv_dw_b, m_conv_ln_g, m_conv_ln_b, m_lru_conv_w, m_lru_conv_b, m_lru_wa, m_lru_ba, m_lru_wx, m_lru_bx, m_lru_lambda, m_w_out, m_final_g, v_norm_g, v_w_in, v_conv_dw_w, v_conv_dw_b, v_conv_ln_g, v_conv_ln_b, v_lru_conv_w, v_lru_conv_b, v_lru_wa, v_lru_ba, v_lru_wx, v_lru_bx, v_lru_lambda, v_w_out, v_final_g):
    n_layers = norm_g.shape[0]
    s = x.shape[1]
    t_rows = min(s, 128)
    xs = jnp.reshape(x, (s, D_MODEL))
    target = jnp.reshape(loss_target, (s, D_MODEL))
    dev = 4 * lax.axis_index("x") + 2 * lax.axis_index("y") + lax.axis_index("c")

    w_in_bf = _cast_bf16(w_in, "cast_w_in")
    w_out_bf = _cast_bf16(w_out, "cast_w_out")
    w_in_l, w31_all, w4_all = _run_job(_GatherJob([w_in_bf[0], conv_dw_w, lru_conv_w]), "weight_all_gather0")
    w_out_l = None
    w31_full = jnp.reshape(jnp.transpose(w31_all, (1, 2, 0, 3)), (n_layers, KW, D_BR))
    w4_full = jnp.reshape(jnp.transpose(w4_all, (1, 2, 0, 3)), (n_layers, KW4, D_BR))
    row3 = lambda p: jnp.reshape(p, (n_layers, 1, -1))
    mixer_params = (w31_full, row3(conv_dw_b), row3(conv_ln_g), row3(conv_ln_b), w4_full, row3(lru_conv_b),
                    lru_wa.astype(BF16), row3(lru_ba), lru_wx.astype(BF16), row3(lru_bx), row3(lru_lambda))

    saved = []
    act = xs
    for l in range(n_layers):
        wanted = [w_out_bf[0]] if l == 0 else []
        if l + 1 < n_layers:
            wanted += [w_in_bf[l + 1], w_out_bf[l + 1]]
        w_full = _w_in_rows(w_in_l)
        h, u, gathered = _in_proj(act, norm_g[l:l + 1], w_full, _GatherJob(wanted) if wanted else None)
        if l == 0:
            w_out_l, gathered = gathered[0], gathered[1:]
        y, q_sv, h_sv, hb, *gates_sv = _mixer_fwd(u, mixer_params, l, t_rows)
        kept = [q_sv, h_sv, *gates_sv]
        wo = jnp.reshape(w_out_l, (D_MODEL, D_MODEL))
        saved.append((act, h, u, y, kept, hb, w_full, wo))
        act = _out_proj(act, y, wo)
        if gathered:
            w_in_l, w_out_l = gathered
    loss_part, dx, d_final_g = _loss_head(act, jnp.reshape(final_g, (1, D_MODEL)), target)
    loss = lax.psum(loss_part[0, 0], AXES)

    pending = None
    reduced_big = [None] * n_layers
    small = [None] * n_layers
    for l in reversed(range(n_layers)):
        x_l, h, u, y, kept, hb, w_full, wo = saved[l]
        dy, dxb, recvs = _out_proj_bwd_x(dx, wo, pending.job() if pending else None)
        if pending:
            pending.absorb(recvs)
            reduced_big[l + 1] = [b[0] for b in pending.bufs]
        g_out, _ = _w_out_grad(y, dxb, None)
        du, pg, dw31, dwa, dwx = _mixer_bwd(u, kept, dy, hb, mixer_params, l, t_rows)
        g_in, _ = _w_in_grad(h, du, None)
        pending = _PendingReduce([g_in, g_out])
        pending.absorb(_run_job(pending.job(), "rs_exchange_c"))
        dx, d_norm, recvs = _in_proj_bwd_x(du, w_full, x_l, norm_g[l:l + 1], dx, pending.job())
        pending.absorb(recvs)
        small[l] = (d_norm, pg, dw31, dwa, dwx)
    reduced_big[0] = pending.finish_alone()
    grad_x = jnp.reshape(dx, x.shape)
    grad_w_in = jnp.stack([r[0] for r in reduced_big])
    grad_w_out = jnp.stack([r[1] for r in reduced_big])

    stack = lambda f: jnp.stack([f(small[l]) for l in range(n_layers)])
    pg_all = stack(lambda t: t[1])
    rep_parts = [
        (stack(lambda t: t[0][0]), norm_g.shape), (pg_all[:, PG_B31], conv_dw_b.shape),
        (pg_all[:, PG_LNG], conv_ln_g.shape), (pg_all[:, PG_LNB], conv_ln_b.shape),
        (pg_all[:, PG_B4], lru_conv_b.shape), (stack(lambda t: t[3]), lru_wa.shape), (pg_all[:, PG_BA], lru_ba.shape),
        (stack(lambda t: t[4]), lru_wx.shape), (pg_all[:, PG_BX], lru_bx.shape), (pg_all[:, PG_LAM], lru_lambda.shape),
        (d_final_g, final_g.shape)]
    shard_parts = [(stack(lambda t: t[2][0:KW]), (n_layers, KW, D_BR)),
                   (pg_all[:, PG_W4:PG_W4 + KW4], (n_layers, KW4, D_BR))]
    gate_w = (5, 7)
    f32_parts = [p for i, p in enumerate(rep_parts) if i not in gate_w] + shard_parts
    bf16_parts = [rep_parts[i] for i in gate_w]
    red_a, red_b = _all_reduce_small(_pack_rows([p for p, _ in f32_parts]), _pack_rows([p for p, _ in bf16_parts]))
    red_a = _unpack_rows(red_a, [shp for _, shp in f32_parts])
    red_b = _unpack_rows(red_b, [shp for _, shp in bf16_parts])
    rep_grads = red_a[:len(rep_parts) - len(gate_w)]
    for i, g in zip(gate_w, red_b):
        rep_grads.insert(i, g)
    grad_dw = lax.dynamic_slice_in_dim(red_a[-2], dev * HD, HD, axis=2)
    grad_w4 = lax.dynamic_slice_in_dim(red_a[-1], dev * HD, HD, axis=2)

    def adam_nd(w, g, m, v, name):
        two_d = (-1, w.shape[-1])
        outs = _adamw(*(jnp.reshape(t, two_d) for t in (w, g, m, v)), name)
        return [jnp.reshape(o, w.shape) for o in outs]

    upd = {}
    upd["w_in"] = adam_nd(w_in, grad_w_in, m_w_in, v_w_in, "adamw_w_in")
    upd["w_out"] = adam_nd(w_out, grad_w_out, m_w_out, v_w_out, "adamw_w_out")
    upd["conv_dw_w"] = adam_nd(conv_dw_w, grad_dw, m_conv_dw_w, v_conv_dw_w, "adamw_conv_dw_w")
    upd["lru_conv_w"] = adam_nd(lru_conv_w, grad_w4, m_lru_conv_w, v_lru_conv_w, "adamw_lru_conv_w")
    rep_w = [norm_g, conv_dw_b, conv_ln_g, conv_ln_b, lru_conv_b, lru_wa, lru_ba, lru_wx, lru_bx, lru_lambda, final_g]
    rep_m = [m_norm_g, m_conv_dw_b, m_conv_ln_g, m_conv_ln_b, m_lru_conv_b, m_lru_wa, m_lru_ba, m_lru_wx, m_lru_bx,
             m_lru_lambda, m_final_g]
    rep_v = [v_norm_g, v_conv_dw_b, v_conv_ln_g, v_conv_ln_b, v_lru_conv_b, v_lru_wa, v_lru_ba, v_lru_wx, v_lru_bx,
             v_lru_lambda, v_final_g]
    rep_shapes = [w.shape for w in rep_w]
    packed = _adamw(_pack_rows(rep_w), _pack_rows(rep_grads), _pack_rows(rep_m), _pack_rows(rep_v), "adamw_small")
    rep_out = [_unpack_rows(o, rep_shapes) for o in packed]
    rep_keys = ["norm_g", "conv_dw_b", "conv_ln_g", "conv_ln_b", "lru_conv_b", "lru_wa", "lru_ba", "lru_wx", "lru_bx",
                "lru_lambda", "final_g"]
    grads = {"w_in": grad_w_in, "w_out": grad_w_out, "conv_dw_w": grad_dw, "lru_conv_w": grad_w4}
    for i, key in enumerate(rep_keys):
        grads[key] = rep_grads[i]
        upd[key] = [rep_out[0][i], rep_out[1][i], rep_out[2][i]]

    order = ["norm_g", "w_in", "conv_dw_w", "conv_dw_b", "conv_ln_g", "conv_ln_b", "lru_conv_w", "lru_conv_b", "lru_wa",
             "lru_ba", "lru_wx", "lru_bx", "lru_lambda", "w_out", "final_g"]
    return (loss, grad_x, *[grads[k] for k in order], *[upd[k][0] for k in order], *[upd[k][1] for k in order],
            *[upd[k][2] for k in order])
```

```python
import functools

import jax
import jax.numpy as jnp
from jax import lax
from jax.experimental import pallas as pl
from jax.experimental.pallas import tpu as pltpu

F32 = jnp.float32
BF16 = jnp.bfloat16
MESH = pl.DeviceIdType.MESH
AXES = ("x", "y", "c")
N_DEV = 8

D_MODEL = 2048
D_BR = 1024
D_IN = 5 * D_BR
SHARD_IN = D_IN // N_DEV
SHARD_OUT = D_MODEL // N_DEV
KW = 31
KW4 = 4
HEADS = 8
HD = 128
LRU_C = 8.0
RMS_EPS = 1e-6
LN_EPS = 1e-5
SUBLANES = 8
HALO = 32
HALO4 = 8
ROW_CHUNK = 16

ADAM_LR = 0.001
ADAM_B1 = 0.9
ADAM_B2 = 0.999
ADAM_EPS = 1e-08
ADAM_WD = 0.01
ADAM_STEP = 10

VMEM_LIMIT = 60 * 1024 * 1024

ANY = pl.BlockSpec(memory_space=pl.ANY)


def _cparams(n_grid):
    return pltpu.CompilerParams(dimension_semantics=("arbitrary",) * n_grid, vmem_limit_bytes=VMEM_LIMIT)


def _resident(block_shape, index_map):
    return pl.BlockSpec(block_shape, index_map, pipeline_mode=pl.Buffered(1))


def _sig(x):
    return 0.5 * jnp.tanh(0.5 * x) + 0.5


def _dsilu(z, sz):
    return sz * (1.0 + z * (1.0 - sz))


def _expm1(x):
    small = jnp.abs(x) < 0.01
    series = x * (1.0 + x * (0.5 + x * (1.0 / 6.0 + x * (1.0 / 24.0))))
    return jnp.where(small, series, jnp.exp(x) - 1.0)


def _log_sigmoid(x):
    e = jnp.exp(-jnp.abs(x))
    l1p = jnp.where(e < 0.01, e * (1.0 - e * (0.5 - e * (1.0 / 3.0))), jnp.log(1.0 + e))
    return jnp.minimum(x, 0.0) - l1p


def _colsum(x):
    return jnp.sum(x, axis=0, keepdims=True)


def _my_pos():
    return lax.axis_index("x"), lax.axis_index("y"), lax.axis_index("c")


class _GatherJob:
    def __init__(self, shards):
        self.arrays = list(shards)
        nt = self.nt = len(self.arrays)
        self.in_specs = [ANY] * nt
        self.out_shape = [jax.ShapeDtypeStruct((N_DEV,) + s.shape, s.dtype) for s in self.arrays]
        self.out_specs = [ANY] * nt
        self.scratch = [pltpu.SemaphoreType.DMA((nt, 7)), pltpu.SemaphoreType.DMA((nt, 7)),
                        pltpu.SemaphoreType.DMA((nt,))]

    def _plan(self, srcs, outs, scr):
        send_sems, recv_sems, local_sems = scr
        x, y, c = _my_pos()
        me, sibling = (x, y, c), (x, y, 1 - c)
        chips = [(1 - x, y), (x, 1 - y), (1 - x, 1 - y)]

        def slot(p):
            return 4 * p[0] + 2 * p[1] + p[2]

        def copy(t, k, block, to, own=False):
            dst = outs[t].at[slot(block)]
            return pltpu.make_async_remote_copy(
                src_ref=srcs[t] if own else dst, dst_ref=dst, send_sem=send_sems.at[t, k], recv_sem=recv_sems.at[t, k],
                device_id=to, device_id_type=MESH)

        mine = [pltpu.make_async_copy(srcs[t], outs[t].at[slot(me)], local_sems.at[t]) for t in range(self.nt)]
        first = []
        for t in range(self.nt):
            first.append(copy(t, 0, me, sibling, own=True))
            first += [copy(t, 1 + j, me, (*chip, c), own=True) for j, chip in enumerate(chips)]
        return me, sibling, chips, c, copy, mine, first

    def start(self, srcs, outs, scr):
        _, _, _, _, _, mine, first = self._plan(srcs, outs, scr)
        for cp in mine + first:
            cp.start()

    def finish(self, srcs, outs, scr):
        me, sibling, chips, c, copy, mine, first = self._plan(srcs, outs, scr)
        passed = []
        for j, chip in enumerate(chips):
            for t in range(self.nt):
                copy(t, 1 + j, (*chip, c), me).wait_recv()
                fwd = copy(t, 4 + j, (*chip, c), sibling)
                fwd.start()
                passed.append(fwd)
        for t in range(self.nt):
            copy(t, 0, sibling, me).wait_recv()
            for j, chip in enumerate(chips):
                copy(t, 4 + j, (*chip, 1 - c), me).wait_recv()
        for cp in first + passed:
            cp.wait_send()
        for cp in mine:
            cp.wait()


class _ExchangeJob:
    def __init__(self, srcs, axis):
        self.arrays = list(srcs)
        self.axis = axis
        nt = self.nt = len(self.arrays)
        self.half = [s.shape[0] // 2 for s in self.arrays]
        self.in_specs = [ANY] * nt
        self.out_shape = [jax.ShapeDtypeStruct((h,) + s.shape[1:], s.dtype) for h, s in zip(self.half, self.arrays)]
        self.out_specs = [ANY] * nt
        self.scratch = [pltpu.SemaphoreType.DMA((nt,)), pltpu.SemaphoreType.DMA((nt,))]

    def _copies(self, srcs, outs, scr):
        send_sems, recv_sems = scr
        pos = list(_my_pos())
        me = pos[self.axis]
        pos[self.axis] = 1 - me
        return [pltpu.make_async_remote_copy(
            src_ref=srcs[t].at[pl.ds((1 - me) * self.half[t], self.half[t])], dst_ref=outs[t],
            send_sem=send_sems.at[t], recv_sem=recv_sems.at[t], device_id=tuple(pos), device_id_type=MESH)
            for t in range(self.nt)]

    def start(self, srcs, outs, scr):
        for cp in self._copies(srcs, outs, scr):
            cp.start()

    def finish(self, srcs, outs, scr):
        for cp in self._copies(srcs, outs, scr):
            cp.wait()


def _run_job(job, name):
    def body(*refs):
        ins, outs, scr = refs[:job.nt], refs[job.nt:2 * job.nt], refs[2 * job.nt:]
        job.start(ins, outs, scr)
        job.finish(ins, outs, scr)

    return pl.pallas_call(body, name=name, out_shape=job.out_shape, in_specs=job.in_specs, out_specs=job.out_specs,
                          scratch_shapes=job.scratch)(*job.arrays)


def _hosted_call(body, *, name, grid, in_specs, out_specs, out_shape, scratch_shapes, args, job):
    n_in, n_out, n_scr = len(in_specs), len(out_specs), len(scratch_shapes)
    if job is None:
        outs = pl.pallas_call(body, name=name, grid=grid, in_specs=in_specs, out_specs=out_specs, out_shape=out_shape,
                              scratch_shapes=scratch_shapes, compiler_params=_cparams(len(grid)))(*args)
        return list(outs), None
    nt = job.nt

    def full_body(*refs):
        own_in, job_in = refs[:n_in], refs[n_in:n_in + nt]
        base = n_in + nt
        own_out, job_out = refs[base:base + n_out], refs[base + n_out:base + n_out + nt]
        base += n_out + nt
        own_scr, job_scr = refs[base:base + n_scr], refs[base + n_scr:]
        ids = [pl.program_id(a) for a in range(len(grid))]
        is_first = functools.reduce(jnp.logical_and, [i == 0 for i in ids])
        is_last = functools.reduce(jnp.logical_and, [i == g - 1 for i, g in zip(ids, grid)])

        @pl.when(is_first)
        def _():
            job.start(job_in, job_out, job_scr)

        body(*own_in, *own_out, *own_scr)

        @pl.when(is_last)
        def _():
            job.finish(job_in, job_out, job_scr)

    outs = pl.pallas_call(
        full_body, name=name, grid=grid, in_specs=list(in_specs) + job.in_specs,
        out_specs=list(out_specs) + job.out_specs, out_shape=list(out_shape) + job.out_shape,
        scratch_shapes=list(scratch_shapes) + job.scratch, compiler_params=_cparams(len(grid)))(*args, *job.arrays)
    return list(outs[:n_out]), list(outs[n_out:])


def _cast_bf16(x, name):
    nl, r, c = x.shape
    tr = min(r, 512)

    def body(x_ref, o_ref):
        o_ref[...] = x_ref[...].astype(BF16)

    spec = pl.BlockSpec((None, tr, c), lambda l, i: (l, i, 0))
    return pl.pallas_call(
        body, name=name, grid=(nl, r // tr), in_specs=[spec], out_specs=spec,
        out_shape=jax.ShapeDtypeStruct(x.shape, BF16), compiler_params=_cparams(2))(x)


def _w_in_rows(w_all):
    def body(i_ref, o_ref):
        o_ref[...] = i_ref[...]

    return pl.pallas_call(
        body, name="w_in_rows", grid=(N_DEV,),
        in_specs=[pl.BlockSpec((None, D_MODEL, SHARD_IN), lambda j: (j, 0, 0))],
        out_specs=pl.BlockSpec((D_MODEL, SHARD_IN), lambda j: (0, j)),
        out_shape=jax.ShapeDtypeStruct((D_MODEL, D_IN), BF16), compiler_params=_cparams(1))(w_all)


def _in_proj(x, g_row, w_full, job):
    s = x.shape[0]
    tm = min(s, 1024)
    tn = 2 * SHARD_IN

    def body(x_ref, g_ref, w_ref, h_ref, u_ref):
        @pl.when(pl.program_id(1) == 0)
        def _():
            xf = x_ref[...]
            rstd = lax.rsqrt(jnp.mean(xf * xf, axis=-1, keepdims=True) + RMS_EPS)
            h_ref[...] = (xf * rstd * g_ref[...]).astype(BF16)

        u_ref[...] = jnp.dot(h_ref[...], w_ref[...], preferred_element_type=F32)

    own, extra = _hosted_call(
        body, name="in_proj", grid=(s // tm, D_IN // tn),
        in_specs=[pl.BlockSpec((tm, D_MODEL), lambda i, j: (i, 0)),
                  pl.BlockSpec((1, D_MODEL), lambda i, j: (0, 0)),
                  pl.BlockSpec((D_MODEL, tn), lambda i, j: (0, j))],
        out_specs=[pl.BlockSpec((tm, D_MODEL), lambda i, j: (i, 0)),
                   pl.BlockSpec((tm, tn), lambda i, j: (i, j))],
        out_shape=[jax.ShapeDtypeStruct((s, D_MODEL), BF16), jax.ShapeDtypeStruct((s, D_IN), F32)],
        scratch_shapes=[], args=(x, g_row, w_full), job=job)
    return own[0], own[1], extra


def _shift_copies(cs_ref, buf):
    n = buf.shape[0]
    cs_ref[0] = buf
    for sft in range(1, 8):
        cs_ref[sft] = pltpu.roll(buf, n - sft, 0)


def _spread_taps(wb_ref, w_ref):
    for k in range(KW):
        wb_ref[k] = jnp.broadcast_to(w_ref[k:k + 1, :], (SUBLANES, D_BR))


def _conv_taps(cs_ref, wb_ref, q_ref, t_rows, offs):
    groups = ROW_CHUNK // SUBLANES

    def chunk(r, carry):
        r0 = pl.multiple_of(r * ROW_CHUNK, ROW_CHUNK)
        accs = [jnp.zeros((SUBLANES, D_BR), F32) for _ in range(groups)]
        for k, off in enumerate(offs):
            wv = wb_ref[k]
            ahead = cs_ref[off % 8, pl.ds(r0 + (off // 8) * 8, ROW_CHUNK), :]
            accs = [acc + wv * ahead[SUBLANES * g:SUBLANES * (g + 1)] for g, acc in enumerate(accs)]
        q_ref[pl.ds(r0, ROW_CHUNK), :] = jnp.concatenate(accs, axis=0)
        return carry

    lax.fori_loop(0, t_rows // ROW_CHUNK, chunk, 0)


def _scan_fwd(a, b, h_in):
    t_rows = a.shape[0]
    row8 = lax.broadcasted_iota(jnp.int32, a.shape, 0) & (SUBLANES - 1)
    d = 1
    while d < SUBLANES:
        keep = row8 >= d
        a_s = jnp.where(keep, pltpu.roll(a, d, 0), 1.0)
        b_s = jnp.where(keep, pltpu.roll(b, d, 0), 0.0)
        b = a * b_s + b
        a = a * a_s
        d *= 2
    carry = h_in
    groups = []
    for grp in range(t_rows // SUBLANES):
        rows = slice(grp * SUBLANES, (grp + 1) * SUBLANES)
        h_g = b[rows] + a[rows] * carry
        groups.append(h_g)
        carry = h_g[SUBLANES - 1:SUBLANES]
    return jnp.concatenate(groups, axis=0)


def _scan_rev(a, b, g_in):
    t_rows = a.shape[0]
    row8 = lax.broadcasted_iota(jnp.int32, a.shape, 0) & (SUBLANES - 1)
    d = 1
    while d < SUBLANES:
        keep = row8 < SUBLANES - d
        a_s = jnp.where(keep, pltpu.roll(a, t_rows - d, 0), 1.0)
        b_s = jnp.where(keep, pltpu.roll(b, t_rows - d, 0), 0.0)
        b = a * b_s + b
        a = a * a_s
        d *= 2
    carry = g_in
    groups = []
    for grp in reversed(range(t_rows // SUBLANES)):
        rows = slice(grp * SUBLANES, (grp + 1) * SUBLANES)
        g_g = b[rows] + a[rows] * carry
        groups.append(g_g)
        carry = g_g[0:1]
    return jnp.concatenate(groups[::-1], axis=0)


def _heads_matmul(x_bf, w_ref):
    return jnp.concatenate(
        [jnp.dot(x_bf[:, h * HD:(h + 1) * HD], w_ref[h], preferred_element_type=F32) for h in range(HEADS)], axis=1)


def _heads_matmul_t(d_bf, w_ref):
    return jnp.concatenate(
        [lax.dot_general(d_bf[:, h * HD:(h + 1) * HD], w_ref[h], (((1,), (1,)), ((), ())), preferred_element_type=F32)
         for h in range(HEADS)], axis=1)


def _layer_norm_swish(q, ln_g, ln_b):
    mu = jnp.mean(q, axis=-1, keepdims=True)
    xc = q - mu
    var = jnp.mean(xc * xc, axis=-1, keepdims=True)
    rstd = lax.rsqrt(var + LN_EPS)
    n = xc * rstd
    p = n * ln_g + ln_b
    return n, rstd, p, _sig(p)


def _lru_gates(xl, xbuf_ref, w4_ref, b4, wa_ref, ba, wx_ref, bx, lam, t_rows):
    xbuf_ref[pl.ds(HALO4, t_rows), :] = xl
    xb = xbuf_ref[...]
    n = t_rows + HALO4
    xc = b4 + w4_ref[3:4, :] * xl
    for k in range(KW4 - 1):
        off = HALO4 - (KW4 - 1) + k
        xc = xc + w4_ref[k:k + 1, :] * pltpu.roll(xb, n - off, 0)[0:t_rows]
    xc_bf = xc.astype(BF16)
    r = _sig(_heads_matmul(xc_bf, wa_ref) + ba)
    ig = _sig(_heads_matmul(xc_bf, wx_ref) + bx)
    log_s = _log_sigmoid(lam)
    la = LRU_C * r * log_s
    a = jnp.exp(la)
    m = jnp.sqrt(-_expm1(2.0 * la))
    return xb, xc, xc_bf, r, ig, log_s, a, m


def _mixer_specs(layer):
    row1 = lambda i: (layer, 0, 0)
    heads = lambda i: (layer, 0, 0, 0)
    return [pl.BlockSpec((None, KW, D_BR), row1),
            pl.BlockSpec((None, 1, D_BR), row1), pl.BlockSpec((None, 1, D_BR), row1),
            pl.BlockSpec((None, 1, D_BR), row1),
            pl.BlockSpec((None, KW4, D_BR), row1), pl.BlockSpec((None, 1, D_BR), row1),
            pl.BlockSpec((None, HEADS, HD, HD), heads), pl.BlockSpec((None, 1, D_BR), row1),
            pl.BlockSpec((None, HEADS, HD, HD), heads), pl.BlockSpec((None, 1, D_BR), row1),
            pl.BlockSpec((None, 1, D_BR), row1)]


def _mixer_fwd(u, params, layer, t_rows):
    s = u.shape[0]
    nb = s // t_rows

    def body(u_ref, w31_ref, b31_ref, lng_ref, lnb_ref, w4_ref, b4_ref, wa_ref, ba_ref, wx_ref, bx_ref, lam_ref,
             y_ref, q_out_ref, h_out_ref, hb_ref, xc_out_ref, r_out_ref, ig_out_ref, a_out_ref, m_out_ref,
             cbuf_ref, cs_ref, xbuf_ref, hcar_ref, wb_ref):
        @pl.when(pl.program_id(0) == 0)
        def _():
            _spread_taps(wb_ref, w31_ref)
            cbuf_ref[pl.ds(0, HALO), :] = jnp.zeros((HALO, D_BR), F32)
            xbuf_ref[pl.ds(0, HALO4), :] = jnp.zeros((HALO4, D_BR), F32)
            hcar_ref[...] = jnp.zeros_like(hcar_ref)

        zc = u_ref[:, 2 * D_BR:3 * D_BR]
        c = u_ref[:, 0:D_BR] * _sig(u_ref[:, D_BR:2 * D_BR])
        cbuf_ref[pl.ds(HALO, t_rows), :] = c
        _shift_copies(cs_ref, cbuf_ref[...])
        _conv_taps(cs_ref, wb_ref, q_out_ref, t_rows, [HALO - (KW - 1) + k for k in range(KW)])
        cbuf_ref[pl.ds(0, HALO), :] = c[t_rows - HALO:t_rows]
        q = q_out_ref[...] + b31_ref[...]
        q_out_ref[...] = q
        _, _, p, sp = _layer_norm_swish(q, lng_ref[...], lnb_ref[...])
        y_ref[:, 0:D_BR] = (p * sp * (zc * _sig(zc))).astype(BF16)

        xl = u_ref[:, 3 * D_BR:4 * D_BR]
        zl = u_ref[:, 4 * D_BR:5 * D_BR]
        _, xc, _, r, ig, _, a, m = _lru_gates(xl, xbuf_ref, w4_ref, b4_ref[...], wa_ref, ba_ref[...], wx_ref,
                                              bx_ref[...], lam_ref[...], t_rows)
        xbuf_ref[pl.ds(0, HALO4), :] = xl[t_rows - HALO4:t_rows]
        xc_out_ref[...] = xc
        r_out_ref[...] = r
        ig_out_ref[...] = ig
        a_out_ref[...] = a
        m_out_ref[...] = m
        h_in = hcar_ref[...]
        hb_ref[...] = h_in
        h = _scan_fwd(a, m * (ig * xc), h_in)
        h_out_ref[...] = h
        hcar_ref[...] = h_out_ref[pl.ds(t_rows - 1, 1), :]
        y_ref[:, D_BR:2 * D_BR] = (h * (zl * _sig(zl))).astype(BF16)

    blk = pl.BlockSpec((t_rows, D_BR), lambda i: (i, 0))
    return pl.pallas_call(
        body, name="mixer_fwd", grid=(nb,),
        in_specs=[pl.BlockSpec((t_rows, D_IN), lambda i: (i, 0))] + _mixer_specs(layer),
        out_specs=[pl.BlockSpec((t_rows, 2 * D_BR), lambda i: (i, 0)), blk, blk,
                   pl.BlockSpec((None, 1, D_BR), lambda i: (i, 0, 0))] + [blk] * 5,
        out_shape=[jax.ShapeDtypeStruct((s, 2 * D_BR), BF16), jax.ShapeDtypeStruct((s, D_BR), F32),
                   jax.ShapeDtypeStruct((s, D_BR), F32), jax.ShapeDtypeStruct((nb, 1, D_BR), F32)]
        + [jax.ShapeDtypeStruct((s, D_BR), F32)] * 5,
        scratch_shapes=[pltpu.VMEM((t_rows + HALO, D_BR), F32), pltpu.VMEM((8, t_rows + HALO, D_BR), F32),
                        pltpu.VMEM((t_rows + HALO4, D_BR), F32), pltpu.VMEM((1, D_BR), F32),
                        pltpu.VMEM((KW, SUBLANES, D_BR), F32)],
        compiler_params=_cparams(1))(u, *params)


def _out_proj(x, y, wo):
    s = x.shape[0]
    tm = min(s, 512)

    def body(x_ref, y_ref, w_ref, o_ref):
        o_ref[...] = x_ref[...] + jnp.dot(y_ref[...], w_ref[...], preferred_element_type=F32)

    blk = pl.BlockSpec((tm, D_MODEL), lambda i: (i, 0))
    return pl.pallas_call(
        body, name="out_proj", grid=(s // tm,),
        in_specs=[blk, blk, _resident((D_MODEL, D_MODEL), lambda i: (0, 0))],
        out_specs=blk,
        out_shape=jax.ShapeDtypeStruct((s, D_MODEL), F32), compiler_params=_cparams(1))(x, y, wo)


def _loss_head(x, g_row, target):
    s = x.shape[0]
    tm = min(s, 512)

    def body(x_ref, g_ref, t_ref, loss_ref, dx_ref, dg_ref):
        @pl.when(pl.program_id(0) == 0)
        def _():
            loss_ref[...] = jnp.zeros_like(loss_ref)
            dg_ref[...] = jnp.zeros_like(dg_ref)

        xf = x_ref[...]
        g = g_ref[...]
        rstd = lax.rsqrt(jnp.mean(xf * xf, axis=-1, keepdims=True) + RMS_EPS)
        n = xf * rstd
        err = n * g - t_ref[...]
        loss_ref[...] += 0.5 * jnp.sum(jnp.mean(err * err, axis=-1, keepdims=True))
        dy = err * (1.0 / D_MODEL)
        dg_ref[...] += _colsum(dy * n)
        dn = dy * g
        dx_ref[...] = rstd * (dn - n * jnp.mean(dn * n, axis=-1, keepdims=True))

    return pl.pallas_call(
        body, name="loss_head", grid=(s // tm,),
        in_specs=[pl.BlockSpec((tm, D_MODEL), lambda i: (i, 0)), pl.BlockSpec((1, D_MODEL), lambda i: (0, 0)),
                  pl.BlockSpec((tm, D_MODEL), lambda i: (i, 0))],
        out_specs=[pl.BlockSpec((8, 128), lambda i: (0, 0)), pl.BlockSpec((tm, D_MODEL), lambda i: (i, 0)),
                   pl.BlockSpec((1, D_MODEL), lambda i: (0, 0))],
        out_shape=[jax.ShapeDtypeStruct((8, 128), F32), jax.ShapeDtypeStruct((s, D_MODEL), F32),
                   jax.ShapeDtypeStruct((1, D_MODEL), F32)],
        compiler_params=_cparams(1))(x, g_row, target)


def _out_proj_bwd_x(dx, wo, job):
    s = dx.shape[0]
    tm = min(s, 512)

    def body(dx_ref, w_ref, dy_ref, dxb_ref):
        dxb = dx_ref[...].astype(BF16)
        dxb_ref[...] = dxb
        dy_ref[...] = lax.dot_general(dxb, w_ref[...], (((1,), (1,)), ((), ())), preferred_element_type=F32)

    blk = pl.BlockSpec((tm, D_MODEL), lambda i: (i, 0))
    own, extra = _hosted_call(
        body, name="out_proj_bwd_x", grid=(s // tm,),
        in_specs=[blk, _resident((D_MODEL, D_MODEL), lambda i: (0, 0))],
        out_specs=[blk, blk],
        out_shape=[jax.ShapeDtypeStruct((s, D_MODEL), F32), jax.ShapeDtypeStruct((s, D_MODEL), BF16)],
        scratch_shapes=[], args=(dx, wo), job=job)
    return own[0], own[1], extra


def _w_in_grad(h, du, job):
    s = h.shape[0]
    tk = min(s, 1024)
    nk = s // tk

    def body(h_ref, du_ref, o_ref, acc_ref):
        k = pl.program_id(1)

        @pl.when(k == 0)
        def _():
            acc_ref[...] = jnp.zeros_like(acc_ref)

        acc_ref[...] += lax.dot_general(h_ref[...], du_ref[...], (((0,), (0,)), ((), ())), preferred_element_type=F32)

        @pl.when(k == nk - 1)
        def _():
            o_ref[0] = acc_ref[:, 0:SHARD_IN].astype(BF16)
            o_ref[1] = acc_ref[:, SHARD_IN:2 * SHARD_IN].astype(BF16)

    own, extra = _hosted_call(
        body, name="w_in_grad", grid=(N_DEV // 2, nk),
        in_specs=[pl.BlockSpec((tk, D_MODEL), lambda q, k: (k, 0)),
                  pl.BlockSpec((tk, 2 * SHARD_IN), lambda q, k: (k, q))],
        out_specs=[pl.BlockSpec((2, None, D_MODEL, SHARD_IN), lambda q, k: (0, q, 0, 0))],
        out_shape=[jax.ShapeDtypeStruct((2, N_DEV // 2, D_MODEL, SHARD_IN), BF16)],
        scratch_shapes=[pltpu.VMEM((D_MODEL, 2 * SHARD_IN), F32)], args=(h, du), job=job)
    return jnp.reshape(own[0], (N_DEV, D_MODEL, SHARD_IN)), extra


def _w_out_grad(y, dxb, job):
    s = y.shape[0]
    tk = min(s, 1024)
    nk = s // tk
    tn = 512

    def body(y_ref, dx_ref, o_ref, acc_ref):
        k = pl.program_id(1)

        @pl.when(k == 0)
        def _():
            acc_ref[...] = jnp.zeros_like(acc_ref)

        acc_ref[...] += lax.dot_general(y_ref[...], dx_ref[...], (((0,), (0,)), ((), ())), preferred_element_type=F32)

        @pl.when(k == nk - 1)
        def _():
            for j in range(N_DEV):
                slot = (j % 2) * 4 + j // 2
                o_ref[slot] = acc_ref[pl.ds(j * SHARD_OUT, SHARD_OUT), :].astype(BF16)

    own, extra = _hosted_call(
        body, name="w_out_grad", grid=(D_MODEL // tn, nk),
        in_specs=[pl.BlockSpec((tk, D_MODEL), lambda n, k: (k, 0)),
                  pl.BlockSpec((tk, tn), lambda n, k: (k, n))],
        out_specs=[pl.BlockSpec((N_DEV, SHARD_OUT, tn), lambda n, k: (0, 0, n))],
        out_shape=[jax.ShapeDtypeStruct((N_DEV, SHARD_OUT, D_MODEL), BF16)],
        scratch_shapes=[pltpu.VMEM((D_MODEL, tn), F32)], args=(y, dxb), job=job)
    return own[0], extra


def _in_proj_bwd_x(du, w_full, x, g_row, dx_next, job):
    s = x.shape[0]
    tm = min(s, 256)

    def body(du_ref, w_ref, x_ref, g_ref, dxn_ref, dx_ref, dg_ref):
        @pl.when(pl.program_id(0) == 0)
        def _():
            dg_ref[...] = jnp.zeros_like(dg_ref)

        dh = lax.dot_general(w_ref[...], du_ref[...], (((1,), (1,)), ((), ())), preferred_element_type=F32).T
        xf = x_ref[...]
        rstd = lax.rsqrt(jnp.mean(xf * xf, axis=-1, keepdims=True) + RMS_EPS)
        n = xf * rstd
        dg_ref[...] += _colsum(dh * n)
        dn = dh * g_ref[...]
        dx_ref[...] = dxn_ref[...] + rstd * (dn - n * jnp.mean(dn * n, axis=-1, keepdims=True))

    blk = pl.BlockSpec((tm, D_MODEL), lambda i: (i, 0))
    own, extra = _hosted_call(
        body, name="in_proj_bwd_x", grid=(s // tm,),
        in_specs=[pl.BlockSpec((tm, D_IN), lambda i: (i, 0)), _resident((D_MODEL, D_IN), lambda i: (0, 0)),
                  blk, pl.BlockSpec((1, D_MODEL), lambda i: (0, 0)), blk],
        out_specs=[blk, pl.BlockSpec((1, D_MODEL), lambda i: (0, 0))],
        out_shape=[jax.ShapeDtypeStruct((s, D_MODEL), F32), jax.ShapeDtypeStruct((1, D_MODEL), F32)],
        scratch_shapes=[], args=(du, w_full, x, g_row, dx_next), job=job)
    return own[0], own[1], extra


PG_B31, PG_LNG, PG_LNB, PG_B4, PG_BA, PG_BX, PG_LAM, PG_W4 = 0, 1, 2, 3, 4, 5, 6, 8
PG_ROWS = 16


def _mixer_bwd(u, kept, dy, hb, params, layer, t_rows):
    s = u.shape[0]
    nb = s // t_rows

    def body(u_ref, q_ref, h_ref, xc_ref, r_ref, ig_ref, a_ref, m_ref, dy_ref, hb_ref,
             w31_ref, b31_ref, lng_ref, lnb_ref, w4_ref, b4_ref, wa_ref, ba_ref, wx_ref, bx_ref, lam_ref,
             du_ref, pg_ref, dw31_ref, dwa_ref, dwx_ref,
             cbuf_ref, cs_ref, dc_ref, dqbuf_ref, dwacc_ref, dxcbuf_ref, acar_ref, gcar_ref, wb_ref):
        step = pl.program_id(0)

        @pl.when(step == 0)
        def _():
            _spread_taps(wb_ref, w31_ref)
            pg_ref[...] = jnp.zeros_like(pg_ref)
            dwa_ref[...] = jnp.zeros_like(dwa_ref)
            dwx_ref[...] = jnp.zeros_like(dwx_ref)
            dwacc_ref[...] = jnp.zeros_like(dwacc_ref)
            dqbuf_ref[pl.ds(t_rows, HALO), :] = jnp.zeros((HALO, D_BR), F32)
            dxcbuf_ref[pl.ds(t_rows, HALO4), :] = jnp.zeros((HALO4, D_BR), F32)
            acar_ref[...] = jnp.zeros_like(acar_ref)
            gcar_ref[...] = jnp.zeros_like(gcar_ref)

        def add_row(r, val):
            pg_ref[r:r + 1, :] += val

        v = u_ref[:, 0:D_BR]
        g = u_ref[:, D_BR:2 * D_BR]
        zc = u_ref[:, 2 * D_BR:3 * D_BR]
        dyc = dy_ref[:, 0:D_BR]
        sg = _sig(g)
        cbuf_ref[...] = v * sg
        ln_gv = lng_ref[...]
        n, rstd, p, sp = _layer_norm_swish(q_ref[...], ln_gv, lnb_ref[...])
        sz = _sig(zc)
        du_ref[:, 2 * D_BR:3 * D_BR] = (dyc * (p * sp) * _dsilu(zc, sz)).astype(BF16)
        dp = dyc * (zc * sz) * _dsilu(p, sp)
        add_row(PG_LNG, _colsum(dp * n))
        add_row(PG_LNB, _colsum(dp))
        dn = dp * ln_gv
        dq = rstd * (dn - jnp.mean(dn, axis=-1, keepdims=True) - n * jnp.mean(dn * n, axis=-1, keepdims=True))
        add_row(PG_B31, _colsum(dq))
        dqbuf_ref[pl.ds(0, t_rows), :] = dq

        _shift_copies(cs_ref, dqbuf_ref[...])

        groups = ROW_CHUNK // SUBLANES

        def conv_chunk(r, carry):
            r0 = pl.multiple_of(r * ROW_CHUNK, ROW_CHUNK)
            cc = cbuf_ref[pl.ds(r0, ROW_CHUNK), :]
            accs = [jnp.zeros((SUBLANES, D_BR), F32) for _ in range(groups)]
            for k in range(KW):
                off = KW - 1 - k
                wv = wb_ref[k]
                ahead = cs_ref[off % 8, pl.ds(r0 + (off // 8) * 8, ROW_CHUNK), :]
                accs = [acc + wv * ahead[SUBLANES * g:SUBLANES * (g + 1)] for g, acc in enumerate(accs)]
                prod = cc * ahead
                part = prod[0:SUBLANES]
                for g in range(1, groups):
                    part = part + prod[SUBLANES * g:SUBLANES * (g + 1)]
                dwacc_ref[k] += part
            dc_ref[pl.ds(r0, ROW_CHUNK), :] = jnp.concatenate(accs, axis=0)
            return carry

        lax.fori_loop(0, t_rows // ROW_CHUNK, conv_chunk, 0)
        dqbuf_ref[pl.ds(t_rows, HALO), :] = dq[0:HALO]
        dc = dc_ref[...]
        du_ref[:, 0:D_BR] = (dc * sg).astype(BF16)
        du_ref[:, D_BR:2 * D_BR] = (dc * v * sg * (1.0 - sg)).astype(BF16)

        xl = u_ref[:, 3 * D_BR:4 * D_BR]
        zl = u_ref[:, 4 * D_BR:5 * D_BR]
        dyl = dy_ref[:, D_BR:2 * D_BR]
        xc = xc_ref[...]
        xc_bf = xc.astype(BF16)
        r = r_ref[...]
        ig = ig_ref[...]
        a = a_ref[...]
        m = m_ref[...]
        log_s = _log_sigmoid(lam_ref[...])
        row = lax.broadcasted_iota(jnp.int32, (t_rows, D_BR), 0)
        h = h_ref[...]
        h_prev = jnp.where(row >= 1, pltpu.roll(h, 1, 0), hb_ref[...])
        szl = _sig(zl)
        du_ref[:, 4 * D_BR:5 * D_BR] = (dyl * h * _dsilu(zl, szl)).astype(BF16)
        a_next = jnp.where(row < t_rows - 1, pltpu.roll(a, t_rows - 1, 0), acar_ref[...])
        gs = _scan_rev(a_next, dyl * (zl * szl), gcar_ref[...])
        dc_ref[...] = gs
        gcar_ref[...] = dc_ref[pl.ds(0, 1), :]
        dc_ref[...] = a
        acar_ref[...] = dc_ref[pl.ds(0, 1), :]

        dm = gs * ig * xc
        di = gs * m * xc
        dla = gs * h_prev * a - dm * (a * a / m)
        add_row(PG_LAM, _colsum(dla * r) * LRU_C)
        dra = dla * (LRU_C * log_s) * r * (1.0 - r)
        dia = di * ig * (1.0 - ig)
        add_row(PG_BA, _colsum(dra))
        add_row(PG_BX, _colsum(dia))
        dra_bf = dra.astype(BF16)
        dia_bf = dia.astype(BF16)
        for hd in range(HEADS):
            sl = slice(hd * HD, (hd + 1) * HD)
            dwa_ref[hd] += lax.dot_general(xc_bf[:, sl], dra_bf[:, sl], (((0,), (0,)), ((), ())),
                                           preferred_element_type=F32)
            dwx_ref[hd] += lax.dot_general(xc_bf[:, sl], dia_bf[:, sl], (((0,), (0,)), ((), ())),
                                           preferred_element_type=F32)
        dxc = gs * m * ig + _heads_matmul_t(dra_bf, wa_ref) + _heads_matmul_t(dia_bf, wx_ref)
        add_row(PG_B4, _colsum(dxc))
        n4 = t_rows + HALO4
        add_row(PG_W4 + 3, _colsum(dxc * xl))
        dxcbuf_ref[pl.ds(0, t_rows), :] = dxc
        db = dxcbuf_ref[...]
        dxl = w4_ref[3:4, :] * dxc
        for k in range(KW4 - 1):
            ahead = pltpu.roll(db, n4 - (KW4 - 1 - k), 0)[0:t_rows]
            dxl = dxl + w4_ref[k:k + 1, :] * ahead
            add_row(PG_W4 + k, _colsum(xl * ahead))
        dxcbuf_ref[pl.ds(t_rows, HALO4), :] = dxc[0:HALO4]
        du_ref[:, 3 * D_BR:4 * D_BR] = dxl.astype(BF16)

        @pl.when(step == nb - 1)
        def _():
            pg_ref[PG_LAM:PG_LAM + 1, :] = pg_ref[PG_LAM:PG_LAM + 1, :] * _sig(-lam_ref[...])
            dw31_ref[...] = jnp.zeros_like(dw31_ref)
            for k in range(KW):
                dw31_ref[k:k + 1, :] = jnp.sum(dwacc_ref[k], axis=0, keepdims=True)

    const2 = lambda i: (0, 0)
    const3 = lambda i: (0, 0, 0)
    rev = lambda i: (nb - 1 - i, 0)
    return pl.pallas_call(
        body, name="mixer_bwd", grid=(nb,),
        in_specs=[pl.BlockSpec((t_rows, D_IN), rev)] + [pl.BlockSpec((t_rows, D_BR), rev)] * len(kept) + [
                  pl.BlockSpec((t_rows, 2 * D_BR), rev),
                  pl.BlockSpec((None, 1, D_BR), lambda i: (nb - 1 - i, 0, 0))] + _mixer_specs(layer),
        out_specs=[pl.BlockSpec((t_rows, D_IN), rev),
                   pl.BlockSpec((PG_ROWS, D_BR), const2), pl.BlockSpec((32, D_BR), const2),
                   pl.BlockSpec((HEADS, HD, HD), const3), pl.BlockSpec((HEADS, HD, HD), const3)],
        out_shape=[jax.ShapeDtypeStruct((s, D_IN), BF16), jax.ShapeDtypeStruct((PG_ROWS, D_BR), F32),
                   jax.ShapeDtypeStruct((32, D_BR), F32), jax.ShapeDtypeStruct((HEADS, HD, HD), F32),
                   jax.ShapeDtypeStruct((HEADS, HD, HD), F32)],
        scratch_shapes=[pltpu.VMEM((t_rows, D_BR), F32), pltpu.VMEM((8, t_rows + HALO, D_BR), F32),
                        pltpu.VMEM((t_rows, D_BR), F32), pltpu.VMEM((t_rows + HALO, D_BR), F32),
                        pltpu.VMEM((KW, 8, D_BR), F32),
                        pltpu.VMEM((t_rows + HALO4, D_BR), F32), pltpu.VMEM((1, D_BR), F32),
                        pltpu.VMEM((1, D_BR), F32), pltpu.VMEM((KW, SUBLANES, D_BR), F32)],
        compiler_params=_cparams(1))(u, *kept, dy, hb, *params)


def _add_kept_half(src, recv, keep, out_dtype, name):
    h, r, c = recv.shape
    tr = min(r, 1024)

    def body(keep_ref, s_ref, r_ref, o_ref):
        o_ref[...] = (s_ref[...].astype(F32) + r_ref[...].astype(F32)).astype(out_dtype)

    grid_spec = pltpu.PrefetchScalarGridSpec(
        num_scalar_prefetch=1, grid=(h, r // tr),
        in_specs=[pl.BlockSpec((None, tr, c), lambda b, i, kp: (kp[0] * h + b, i, 0)),
                  pl.BlockSpec((None, tr, c), lambda b, i, kp: (b, i, 0))],
        out_specs=pl.BlockSpec((None, tr, c), lambda b, i, kp: (b, i, 0)))
    return pl.pallas_call(
        body, name=name, grid_spec=grid_spec, out_shape=jax.ShapeDtypeStruct(recv.shape, out_dtype),
        compiler_params=_cparams(2))(keep, src, recv)


class _PendingReduce:
    STAGE_AXES = (2, 0, 1)

    def __init__(self, bufs):
        self.bufs = list(bufs)
        self.stage = 0

    def job(self):
        return _ExchangeJob(self.bufs, self.STAGE_AXES[self.stage])

    def absorb(self, recvs):
        me = _my_pos()[self.STAGE_AXES[self.stage]]
        keep = jnp.reshape(me, (1,)).astype(jnp.int32)
        last = self.stage == 2
        self.bufs = [_add_kept_half(b, r, keep, F32 if last else BF16, f"rs_add{self.stage}_{t}")
                     for t, (b, r) in enumerate(zip(self.bufs, recvs))]
        self.stage += 1

    def finish_alone(self):
        while self.stage < 3:
            job = self.job()
            self.absorb(_run_job(job, f"rs_exchange{self.stage}"))
        return [b[0] for b in self.bufs]


def _all_reduce_small(pa, pb):
    def body(pa_ref, pb_ref, oa_ref, ob_ref, ra0, ra1, ra2, sb0, sb1, sb2, rb0, rb1, rb2, send_sems, recv_sems):
        x, y, c = _my_pos()
        peers = [(x, y, 1 - c), (1 - x, y, c), (x, 1 - y, c)]
        oa_ref[...] = pa_ref[...]
        ob_ref[...] = pb_ref[...]
        for k, (peer, ra, sb, rb) in enumerate(zip(peers, (ra0, ra1, ra2), (sb0, sb1, sb2), (rb0, rb1, rb2))):
            sb[...] = ob_ref[...].astype(BF16)
            copies = [pltpu.make_async_remote_copy(
                src_ref=src, dst_ref=dst, send_sem=send_sems.at[t, k], recv_sem=recv_sems.at[t, k],
                device_id=peer, device_id_type=MESH) for t, (src, dst) in enumerate(((oa_ref, ra), (sb, rb)))]
            for cp in copies:
                cp.start()
            for cp in copies:
                cp.wait()
            oa_ref[...] = oa_ref[...] + ra[...]
            ob_ref[...] = sb[...].astype(F32) + rb[...].astype(F32)

    vm = pl.BlockSpec(memory_space=pltpu.VMEM)
    return pl.pallas_call(
        body, name="small_all_reduce",
        out_shape=[jax.ShapeDtypeStruct(pa.shape, F32), jax.ShapeDtypeStruct(pb.shape, F32)],
        in_specs=[vm, vm], out_specs=[vm, vm],
        scratch_shapes=[pltpu.VMEM(pa.shape, F32)] * 3 + [pltpu.VMEM(pb.shape, BF16)] * 6
        + [pltpu.SemaphoreType.DMA((2, 3)), pltpu.SemaphoreType.DMA((2, 3))],
        compiler_params=pltpu.CompilerParams(vmem_limit_bytes=VMEM_LIMIT))(pa, pb)


def _adamw(w, g, m, v, name):
    r, c = w.shape
    tr = r
    for cand in (512, 256, 128, 64, 32, 16, 8):
        if r % cand == 0 and cand * c * 4 <= (2 << 20):
            tr = cand
            break

    def body(w_ref, g_ref, m_ref, v_ref, d_ref, mo_ref, vo_ref):
        gv = g_ref[...]
        m_new = ADAM_B1 * m_ref[...] + (1.0 - ADAM_B1) * gv
        v_new = ADAM_B2 * v_ref[...] + (1.0 - ADAM_B2) * (gv * gv)
        m_hat = m_new / (1.0 - ADAM_B1 ** ADAM_STEP)
        v_hat = v_new / (1.0 - ADAM_B2 ** ADAM_STEP)
        d_ref[...] = -ADAM_LR * (m_hat / (jnp.sqrt(v_hat) + ADAM_EPS) + ADAM_WD * w_ref[...])
        mo_ref[...] = m_new
        vo_ref[...] = v_new

    spec = pl.BlockSpec((tr, c), lambda i: (i, 0))
    shape = jax.ShapeDtypeStruct((r, c), F32)
    return pl.pallas_call(
        body, name=name, grid=(r // tr,), in_specs=[spec] * 4, out_specs=[spec] * 3, out_shape=[shape] * 3,
        compiler_params=_cparams(1))(w, g, m, v)


def _pack_rows(parts):
    flat = jnp.concatenate([jnp.reshape(p, (-1, D_BR)) for p in parts], axis=0)
    pad = (-flat.shape[0]) % 64
    if pad:
        flat = jnp.concatenate([flat, jnp.zeros((pad, D_BR), F32)], axis=0)
    return flat


def _unpack_rows(flat, shapes):
    out, r0 = [], 0
    for shp in shapes:
        n = 1
        for d in shp:
            n *= d
        rows = n // D_BR
        out.append(jnp.reshape(flat[r0:r0 + rows], shp))
        r0 += rows
    return out


def kernel(x, norm_g, w_in, conv_dw_w, conv_dw_b, conv_ln_g, conv_ln_b, lru_conv_w, lru_conv_b, lru_wa, lru_ba, lru_wx, lru_bx, lru_lambda, w_out, final_g, loss_target, m_norm_g, m_w_in, m_conv_dw_w, m_conv_dw_b, m_conv_ln_g, m_conv_ln_b, m_lru_conv_w, m_lru_conv_b, m_lru_wa, m_lru_ba, m_lru_wx, m_lru_bx, m_lru_lambda, m_w_out, m_final_g, v_norm_g, v_w_in, v_conv_dw_w, v_conv_dw_b, v_conv_ln_g, v_conv_ln_b, v_lru_conv_w, v_lru_conv_b, v_lru_wa, v_lru_ba, v_lru_wx, v_lru_bx, v_lru_lambda, v_w_out, v_final_g):
    n_layers = norm_g.shape[0]
    s = x.shape[1]
    t_rows = min(s, 128)
    xs = jnp.reshape(x, (s, D_MODEL))
    target = jnp.reshape(loss_target, (s, D_MODEL))
    dev = 4 * lax.axis_index("x") + 2 * lax.axis_index("y") + lax.axis_index("c")

    w_in_bf = _cast_bf16(w_in, "cast_w_in")
    w_out_bf = _cast_bf16(w_out, "cast_w_out")
    w_in_l, w31_all, w4_all = _run_job(_GatherJob([w_in_bf[0], conv_dw_w, lru_conv_w]), "weight_all_gather0")
    w_out_l = None
    w31_full = jnp.reshape(jnp.transpose(w31_all, (1, 2, 0, 3)), (n_layers, KW, D_BR))
    w4_full = jnp.reshape(jnp.transpose(w4_all, (1, 2, 0, 3)), (n_layers, KW4, D_BR))
    row3 = lambda p: jnp.reshape(p, (n_layers, 1, -1))
    mixer_params = (w31_full, row3(conv_dw_b), row3(conv_ln_g), row3(conv_ln_b), w4_full, row3(lru_conv_b),
                    lru_wa.astype(BF16), row3(lru_ba), lru_wx.astype(BF16), row3(lru_bx), row3(lru_lambda))

    saved = []
    act = xs
    for l in range(n_layers):
        wanted = [w_out_bf[0]] if l == 0 else []
        if l + 1 < n_layers:
            wanted += [w_in_bf[l + 1], w_out_bf[l + 1]]
        w_full = _w_in_rows(w_in_l)
        h, u, gathered = _in_proj(act, norm_g[l:l + 1], w_full, _GatherJob(wanted) if wanted else None)
        if l == 0:
            w_out_l, gathered = gathered[0], gathered[1:]
        y, q_sv, h_sv, hb, *gates_sv = _mixer_fwd(u, mixer_params, l, t_rows)
        kept = [q_sv, h_sv, *gates_sv]
        wo = jnp.reshape(w_out_l, (D_MODEL, D_MODEL))
        saved.append((act, h, u, y, kept, hb, w_full, wo))
        act = _out_proj(act, y, wo)
        if gathered:
            w_in_l, w_out_l = gathered
    loss_part, dx, d_final_g = _loss_head(act, jnp.reshape(final_g, (1, D_MODEL)), target)
    loss = lax.psum(loss_part[0, 0], AXES)

    pending = None
    reduced_big = [None] * n_layers
    small = [None] * n_layers
    for l in reversed(range(n_layers)):
        x_l, h, u, y, kept, hb, w_full, wo = saved[l]
        dy, dxb, recvs = _out_proj_bwd_x(dx, wo, pending.job() if pending else None)
        if pending:
            pending.absorb(recvs)
            reduced_big[l + 1] = [b[0] for b in pending.bufs]
        g_out, _ = _w_out_grad(y, dxb, None)
        du, pg, dw31, dwa, dwx = _mixer_bwd(u, kept, dy, hb, mixer_params, l, t_rows)
        g_in, _ = _w_in_grad(h, du, None)
        pending = _PendingReduce([g_in, g_out])
        pending.absorb(_run_job(pending.job(), "rs_exchange_c"))
        dx, d_norm, recvs = _in_proj_bwd_x(du, w_full, x_l, norm_g[l:l + 1], dx, pending.job())
        pending.absorb(recvs)
        small[l] = (d_norm, pg, dw31, dwa, dwx)
    reduced_big[0] = pending.finish_alone()
    grad_x = jnp.reshape(dx, x.shape)
    grad_w_in = jnp.stack([r[0] for r in reduced_big])
    grad_w_out = jnp.stack([r[1] for r in reduced_big])

    stack = lambda f: jnp.stack([f(small[l]) for l in range(n_layers)])
    pg_all = stack(lambda t: t[1])
    rep_parts = [
        (stack(lambda t: t[0][0]), norm_g.shape), (pg_all[:, PG_B31], conv_dw_b.shape),
        (pg_all[:, PG_LNG], conv_ln_g.shape), (pg_all[:, PG_LNB], conv_ln_b.shape),
        (pg_all[:, PG_B4], lru_conv_b.shape), (stack(lambda t: t[3]), lru_wa.shape), (pg_all[:, PG_BA], lru_ba.shape),
        (stack(lambda t: t[4]), lru_wx.shape), (pg_all[:, PG_BX], lru_bx.shape), (pg_all[:, PG_LAM], lru_lambda.shape),
        (d_final_g, final_g.shape)]
    shard_parts = [(stack(lambda t: t[2][0:KW]), (n_layers, KW, D_BR)),
                   (pg_all[:, PG_W4:PG_W4 + KW4], (n_layers, KW4, D_BR))]
    gate_w = (5, 7)
    f32_parts = [p for i, p in enumerate(rep_parts) if i not in gate_w] + shard_parts
    bf16_parts = [rep_parts[i] for i in gate_w]
    red_a, red_b = _all_reduce_small(_pack_rows([p for p, _ in f32_parts]), _pack_rows([p for p, _ in bf16_parts]))
    red_a = _unpack_rows(red_a, [shp for _, shp in f32_parts])
    red_b = _unpack_rows(red_b, [shp for _, shp in bf16_parts])
    rep_grads = red_a[:len(rep_parts) - len(gate_w)]
    for i, g in zip(gate_w, red_b):
        rep_grads.insert(i, g)
    grad_dw = lax.dynamic_slice_in_dim(red_a[-2], dev * HD, HD, axis=2)
    grad_w4 = lax.dynamic_slice_in_dim(red_a[-1], dev * HD, HD, axis=2)

    def adam_nd(w, g, m, v, name):
        two_d = (-1, w.shape[-1])
        outs = _adamw(*(jnp.reshape(t, two_d) for t in (w, g, m, v)), name)
        return [jnp.reshape(o, w.shape) for o in outs]

    upd = {}
    upd["w_in"] = adam_nd(w_in, grad_w_in, m_w_in, v_w_in, "adamw_w_in")
    upd["w_out"] = adam_nd(w_out, grad_w_out, m_w_out, v_w_out, "adamw_w_out")
    upd["conv_dw_w"] = adam_nd(conv_dw_w, grad_dw, m_conv_dw_w, v_conv_dw_w, "adamw_conv_dw_w")
    upd["lru_conv_w"] = adam_nd(lru_conv_w, grad_w4, m_lru_conv_w, v_lru_conv_w, "adamw_lru_conv_w")
    rep_w = [norm_g, conv_dw_b, conv_ln_g, conv_ln_b, lru_conv_b, lru_wa, lru_ba, lru_wx, lru_bx, lru_lambda, final_g]
    rep_m = [m_norm_g, m_conv_dw_b, m_conv_ln_g, m_conv_ln_b, m_lru_conv_b, m_lru_wa, m_lru_ba, m_lru_wx, m_lru_bx,
             m_lru_lambda, m_final_g]
    rep_v = [v_norm_g, v_conv_dw_b, v_conv_ln_g, v_conv_ln_b, v_lru_conv_b, v_lru_wa, v_lru_ba, v_lru_wx, v_lru_bx,
             v_lru_lambda, v_final_g]
    rep_shapes = [w.shape for w in rep_w]
    packed = _adamw(_pack_rows(rep_w), _pack_rows(rep_grads), _pack_rows(rep_m), _pack_rows(rep_v), "adamw_small")
    rep_out = [_unpack_rows(o, rep_shapes) for o in packed]
    rep_keys = ["norm_g", "conv_dw_b", "conv_ln_g", "conv_ln_b", "lru_conv_b", "lru_wa", "lru_ba", "lru_wx", "lru_bx",
                "lru_lambda", "final_g"]
    grads = {"w_in": grad_w_in, "w_out": grad_w_out, "conv_dw_w": grad_dw, "lru_conv_w": grad_w4}
    for i, key in enumerate(rep_keys):
        grads[key] = rep_grads[i]
        upd[key] = [rep_out[0][i], rep_out[1][i], rep_out[2][i]]

    order = ["norm_g", "w_in", "conv_dw_w", "conv_dw_b", "conv_ln_g", "conv_ln_b", "lru_conv_w", "lru_conv_b", "lru_wa",
             "lru_ba", "lru_wx", "lru_bx", "lru_lambda", "w_out", "final_g"]
    return (loss, grad_x, *[grads[k] for k in order], *[upd[k][0] for k in order], *[upd[k][1] for k in order],
            *[upd[k][2] for k in order])
```

```python
import functools

import jax
import jax.numpy as jnp
from jax import lax
from jax.experimental import pallas as pl
from jax.experimental.pallas import tpu as pltpu

F32 = jnp.float32
BF16 = jnp.bfloat16
MESH = pl.DeviceIdType.MESH
AXES = ("x", "y", "c")
N_DEV = 8

D_MODEL = 2048
D_BR = 1024
D_IN = 5 * D_BR
SHARD_IN = D_IN // N_DEV
SHARD_OUT = D_MODEL // N_DEV
KW = 31
KW4 = 4
HEADS = 8
HD = 128
LRU_C = 8.0
RMS_EPS = 1e-6
LN_EPS = 1e-5
SUBLANES = 8
HALO = 32
HALO4 = 8
ROW_CHUNK = 16

ADAM_LR = 0.001
ADAM_B1 = 0.9
ADAM_B2 = 0.999
ADAM_EPS = 1e-08
ADAM_WD = 0.01
ADAM_STEP = 10

VMEM_LIMIT = 60 * 1024 * 1024

ANY = pl.BlockSpec(memory_space=pl.ANY)


def _cparams(n_grid):
    return pltpu.CompilerParams(dimension_semantics=("arbitrary",) * n_grid, vmem_limit_bytes=VMEM_LIMIT)


def _resident(block_shape, index_map):
    return pl.BlockSpec(block_shape, index_map, pipeline_mode=pl.Buffered(1))


def _sig(x):
    return 0.5 * jnp.tanh(0.5 * x) + 0.5


def _dsilu(z, sz):
    return sz * (1.0 + z * (1.0 - sz))


def _expm1(x):
    small = jnp.abs(x) < 0.01
    series = x * (1.0 + x * (0.5 + x * (1.0 / 6.0 + x * (1.0 / 24.0))))
    return jnp.where(small, series, jnp.exp(x) - 1.0)


def _log_sigmoid(x):
    e = jnp.exp(-jnp.abs(x))
    l1p = jnp.where(e < 0.01, e * (1.0 - e * (0.5 - e * (1.0 / 3.0))), jnp.log(1.0 + e))
    return jnp.minimum(x, 0.0) - l1p


def _colsum(x):
    return jnp.sum(x, axis=0, keepdims=True)


def _my_pos():
    return lax.axis_index("x"), lax.axis_index("y"), lax.axis_index("c")


class _GatherJob:
    def __init__(self, shards):
        self.arrays = list(shards)
        nt = self.nt = len(self.arrays)
        self.in_specs = [ANY] * nt
        self.out_shape = [jax.ShapeDtypeStruct((N_DEV,) + s.shape, s.dtype) for s in self.arrays]
        self.out_specs = [ANY] * nt
        self.scratch = [pltpu.SemaphoreType.DMA((nt, 7)), pltpu.SemaphoreType.DMA((nt, 7)),
                        pltpu.SemaphoreType.DMA((nt,))]

    def _plan(self, srcs, outs, scr):
        send_sems, recv_sems, local_sems = scr
        x, y, c = _my_pos()
        me, sibling = (x, y, c), (x, y, 1 - c)
        chips = [(1 - x, y), (x, 1 - y), (1 - x, 1 - y)]

        def slot(p):
            return 4 * p[0] + 2 * p[1] + p[2]

        def copy(t, k, block, to, own=False):
            dst = outs[t].at[slot(block)]
            return pltpu.make_async_remote_copy(
                src_ref=srcs[t] if own else dst, dst_ref=dst, send_sem=send_sems.at[t, k], recv_sem=recv_sems.at[t, k],
                device_id=to, device_id_type=MESH)

        mine = [pltpu.make_async_copy(srcs[t], outs[t].at[slot(me)], local_sems.at[t]) for t in range(self.nt)]
        first = []
        for t in range(self.nt):
            first.append(copy(t, 0, me, sibling, own=True))
            first += [copy(t, 1 + j, me, (*chip, c), own=True) for j, chip in enumerate(chips)]
        return me, sibling, chips, c, copy, mine, first

    def start(self, srcs, outs, scr):
        _, _, _, _, _, mine, first = self._plan(srcs, outs, scr)
        for cp in mine + first:
            cp.start()

    def finish(self, srcs, outs, scr):
        me, sibling, chips, c, copy, mine, first = self._plan(srcs, outs, scr)
        passed = []
        for j, chip in enumerate(chips):
            for t in range(self.nt):
                copy(t, 1 + j, (*chip, c), me).wait_recv()
                fwd = copy(t, 4 + j, (*chip, c), sibling)
                fwd.start()
                passed.append(fwd)
        for t in range(self.nt):
            copy(t, 0, sibling, me).wait_recv()
            for j, chip in enumerate(chips):
                copy(t, 4 + j, (*chip, 1 - c), me).wait_recv()
        for cp in first + passed:
            cp.wait_send()
        for cp in mine:
            cp.wait()


class _ExchangeJob:
    def __init__(self, srcs, axis):
        self.arrays = list(srcs)
        self.axis = axis
        nt = self.nt = len(self.arrays)
        self.half = [s.shape[0] // 2 for s in self.arrays]
        self.in_specs = [ANY] * nt
        self.out_shape = [jax.ShapeDtypeStruct((h,) + s.shape[1:], s.dtype) for h, s in zip(self.half, self.arrays)]
        self.out_specs = [ANY] * nt
        self.scratch = [pltpu.SemaphoreType.DMA((nt,)), pltpu.SemaphoreType.DMA((nt,))]

    def _copies(self, srcs, outs, scr):
        send_sems, recv_sems = scr
        pos = list(_my_pos())
        me = pos[self.axis]
        pos[self.axis] = 1 - me
        return [pltpu.make_async_remote_copy(
            src_ref=srcs[t].at[pl.ds((1 - me) * self.half[t], self.half[t])], dst_ref=outs[t],
            send_sem=send_sems.at[t], recv_sem=recv_sems.at[t], device_id=tuple(pos), device_id_type=MESH)
            for t in range(self.nt)]

    def start(self, srcs, outs, scr):
        for cp in self._copies(srcs, outs, scr):
            cp.start()

    def finish(self, srcs, outs, scr):
        for cp in self._copies(srcs, outs, scr):
            cp.wait()


def _run_job(job, name):
    def body(*refs):
        ins, outs, scr = refs[:job.nt], refs[job.nt:2 * job.nt], refs[2 * job.nt:]
        job.start(ins, outs, scr)
        job.finish(ins, outs, scr)

    return pl.pallas_call(body, name=name, out_shape=job.out_shape, in_specs=job.in_specs, out_specs=job.out_specs,
                          scratch_shapes=job.scratch)(*job.arrays)


def _hosted_call(body, *, name, grid, in_specs, out_specs, out_shape, scratch_shapes, args, job):
    n_in, n_out, n_scr = len(in_specs), len(out_specs), len(scratch_shapes)
    if job is None:
        outs = pl.pallas_call(body, name=name, grid=grid, in_specs=in_specs, out_specs=out_specs, out_shape=out_shape,
                              scratch_shapes=scratch_shapes, compiler_params=_cparams(len(grid)))(*args)
        return list(outs), None
    nt = job.nt

    def full_body(*refs):
        own_in, job_in = refs[:n_in], refs[n_in:n_in + nt]
        base = n_in + nt
        own_out, job_out = refs[base:base + n_out], refs[base + n_out:base + n_out + nt]
        base += n_out + nt
        own_scr, job_scr = refs[base:base + n_scr], refs[base + n_scr:]
        ids = [pl.program_id(a) for a in range(len(grid))]
        is_first = functools.reduce(jnp.logical_and, [i == 0 for i in ids])
        is_last = functools.reduce(jnp.logical_and, [i == g - 1 for i, g in zip(ids, grid)])

        @pl.when(is_first)
        def _():
            job.start(job_in, job_out, job_scr)

        body(*own_in, *own_out, *own_scr)

        @pl.when(is_last)
        def _():
            job.finish(job_in, job_out, job_scr)

    outs = pl.pallas_call(
        full_body, name=name, grid=grid, in_specs=list(in_specs) + job.in_specs,
        out_specs=list(out_specs) + job.out_specs, out_shape=list(out_shape) + job.out_shape,
        scratch_shapes=list(scratch_shapes) + job.scratch, compiler_params=_cparams(len(grid)))(*args, *job.arrays)
    return list(outs[:n_out]), list(outs[n_out:])


def _cast_bf16(x, name):
    nl, r, c = x.shape
    tr = min(r, 512)

    def body(x_ref, o_ref):
        o_ref[...] = x_ref[...].astype(BF16)

    spec = pl.BlockSpec((None, tr, c), lambda l, i: (l, i, 0))
    return pl.pallas_call(
        body, name=name, grid=(nl, r // tr), in_specs=[spec], out_specs=spec,
        out_shape=jax.ShapeDtypeStruct(x.shape, BF16), compiler_params=_cparams(2))(x)


def _w_in_rows(w_all):
    def body(i_ref, o_ref):
        o_ref[...] = i_ref[...]

    return pl.pallas_call(
        body, name="w_in_rows", grid=(N_DEV,),
        in_specs=[pl.BlockSpec((None, D_MODEL, SHARD_IN), lambda j: (j, 0, 0))],
        out_specs=pl.BlockSpec((D_MODEL, SHARD_IN), lambda j: (0, j)),
        out_shape=jax.ShapeDtypeStruct((D_MODEL, D_IN), BF16), compiler_params=_cparams(1))(w_all)


def _in_proj(x, g_row, w_full, job):
    s = x.shape[0]
    tm = min(s, 1024)
    tn = 2 * SHARD_IN

    def body(x_ref, g_ref, w_ref, h_ref, u_ref):
        @pl.when(pl.program_id(1) == 0)
        def _():
            xf = x_ref[...]
            rstd = lax.rsqrt(jnp.mean(xf * xf, axis=-1, keepdims=True) + RMS_EPS)
            h_ref[...] = (xf * rstd * g_ref[...]).astype(BF16)

        u_ref[...] = jnp.dot(h_ref[...], w_ref[...], preferred_element_type=F32)

    own, extra = _hosted_call(
        body, name="in_proj", grid=(s // tm, D_IN // tn),
        in_specs=[pl.BlockSpec((tm, D_MODEL), lambda i, j: (i, 0)),
                  pl.BlockSpec((1, D_MODEL), lambda i, j: (0, 0)),
                  pl.BlockSpec((D_MODEL, tn), lambda i, j: (0, j))],
        out_specs=[pl.BlockSpec((tm, D_MODEL), lambda i, j: (i, 0)),
                   pl.BlockSpec((tm, tn), lambda i, j: (i, j))],
        out_shape=[jax.ShapeDtypeStruct((s, D_MODEL), BF16), jax.ShapeDtypeStruct((s, D_IN), F32)],
        scratch_shapes=[], args=(x, g_row, w_full), job=job)
    return own[0], own[1], extra


def _shift_copies(cs_ref, buf):
    n = buf.shape[0]
    cs_ref[0] = buf
    for sft in range(1, 8):
        cs_ref[sft] = pltpu.roll(buf, n - sft, 0)


def _spread_taps(wb_ref, w_ref):
    for k in range(KW):
        wb_ref[k] = jnp.broadcast_to(w_ref[k:k + 1, :], (SUBLANES, D_BR))


def _conv_taps(cs_ref, wb_ref, q_ref, t_rows, offs):
    groups = ROW_CHUNK // SUBLANES

    def chunk(r, carry):
        r0 = pl.multiple_of(r * ROW_CHUNK, ROW_CHUNK)
        accs = [jnp.zeros((SUBLANES, D_BR), F32) for _ in range(groups)]
        for k, off in enumerate(offs):
            wv = wb_ref[k]
            ahead = cs_ref[off % 8, pl.ds(r0 + (off // 8) * 8, ROW_CHUNK), :]
            accs = [acc + wv * ahead[SUBLANES * g:SUBLANES * (g + 1)] for g, acc in enumerate(accs)]
        q_ref[pl.ds(r0, ROW_CHUNK), :] = jnp.concatenate(accs, axis=0)
        return carry

    lax.fori_loop(0, t_rows // ROW_CHUNK, chunk, 0)


def _scan_fwd(a, b, h_in):
    t_rows = a.shape[0]
    row8 = lax.broadcasted_iota(jnp.int32, a.shape, 0) & (SUBLANES - 1)
    d = 1
    while d < SUBLANES:
        keep = row8 >= d
        a_s = jnp.where(keep, pltpu.roll(a, d, 0), 1.0)
        b_s = jnp.where(keep, pltpu.roll(b, d, 0), 0.0)
        b = a * b_s + b
        a = a * a_s
        d *= 2
    carry = h_in
    groups = []
    for grp in range(t_rows // SUBLANES):
        rows = slice(grp * SUBLANES, (grp + 1) * SUBLANES)
        h_g = b[rows] + a[rows] * carry
        groups.append(h_g)
        carry = h_g[SUBLANES - 1:SUBLANES]
    return jnp.concatenate(groups, axis=0)


def _scan_rev(a, b, g_in):
    t_rows = a.shape[0]
    row8 = lax.broadcasted_iota(jnp.int32, a.shape, 0) & (SUBLANES - 1)
    d = 1
    while d < SUBLANES:
        keep = row8 < SUBLANES - d
        a_s = jnp.where(keep, pltpu.roll(a, t_rows - d, 0), 1.0)
        b_s = jnp.where(keep, pltpu.roll(b, t_rows - d, 0), 0.0)
        b = a * b_s + b
        a = a * a_s
        d *= 2
    carry = g_in
    groups = []
    for grp in reversed(range(t_rows // SUBLANES)):
        rows = slice(grp * SUBLANES, (grp + 1) * SUBLANES)
        g_g = b[rows] + a[rows] * carry
        groups.append(g_g)
        carry = g_g[0:1]
    return jnp.concatenate(groups[::-1], axis=0)


def _heads_matmul(x_bf, w_ref):
    return jnp.concatenate(
        [jnp.dot(x_bf[:, h * HD:(h + 1) * HD], w_ref[h], preferred_element_type=F32) for h in range(HEADS)], axis=1)


def _heads_matmul_t(d_bf, w_ref):
    return jnp.concatenate(
        [lax.dot_general(d_bf[:, h * HD:(h + 1) * HD], w_ref[h], (((1,), (1,)), ((), ())), preferred_element_type=F32)
         for h in range(HEADS)], axis=1)


def _layer_norm_swish(q, ln_g, ln_b):
    mu = jnp.mean(q, axis=-1, keepdims=True)
    xc = q - mu
    var = jnp.mean(xc * xc, axis=-1, keepdims=True)
    rstd = lax.rsqrt(var + LN_EPS)
    n = xc * rstd
    p = n * ln_g + ln_b
    return n, rstd, p, _sig(p)


def _lru_gates(xl, xbuf_ref, w4_ref, b4, wa_ref, ba, wx_ref, bx, lam, t_rows):
    xbuf_ref[pl.ds(HALO4, t_rows), :] = xl
    xb = xbuf_ref[...]
    n = t_rows + HALO4
    xc = b4 + w4_ref[3:4, :] * xl
    for k in range(KW4 - 1):
        off = HALO4 - (KW4 - 1) + k
        xc = xc + w4_ref[k:k + 1, :] * pltpu.roll(xb, n - off, 0)[0:t_rows]
    xc_bf = xc.astype(BF16)
    r = _sig(_heads_matmul(xc_bf, wa_ref) + ba)
    ig = _sig(_heads_matmul(xc_bf, wx_ref) + bx)
    log_s = _log_sigmoid(lam)
    la = LRU_C * r * log_s
    a = jnp.exp(la)
    m = jnp.sqrt(-_expm1(2.0 * la))
    return xb, xc, xc_bf, r, ig, log_s, a, m


def _mixer_specs(layer):
    row1 = lambda i: (layer, 0, 0)
    heads = lambda i: (layer, 0, 0, 0)
    return [pl.BlockSpec((None, KW, D_BR), row1),
            pl.BlockSpec((None, 1, D_BR), row1), pl.BlockSpec((None, 1, D_BR), row1),
            pl.BlockSpec((None, 1, D_BR), row1),
            pl.BlockSpec((None, KW4, D_BR), row1), pl.BlockSpec((None, 1, D_BR), row1),
            pl.BlockSpec((None, HEADS, HD, HD), heads), pl.BlockSpec((None, 1, D_BR), row1),
            pl.BlockSpec((None, HEADS, HD, HD), heads), pl.BlockSpec((None, 1, D_BR), row1),
            pl.BlockSpec((None, 1, D_BR), row1)]


def _mixer_fwd(u, params, layer, t_rows):
    s = u.shape[0]
    nb = s // t_rows

    def body(u_ref, w31_ref, b31_ref, lng_ref, lnb_ref, w4_ref, b4_ref, wa_ref, ba_ref, wx_ref, bx_ref, lam_ref,
             y_ref, q_out_ref, h_out_ref, hb_ref, xc_out_ref, r_out_ref, ig_out_ref, a_out_ref, m_out_ref,
             cbuf_ref, cs_ref, xbuf_ref, hcar_ref, wb_ref):
        @pl.when(pl.program_id(0) == 0)
        def _():
            _spread_taps(wb_ref, w31_ref)
            cbuf_ref[pl.ds(0, HALO), :] = jnp.zeros((HALO, D_BR), F32)
            xbuf_ref[pl.ds(0, HALO4), :] = jnp.zeros((HALO4, D_BR), F32)
            hcar_ref[...] = jnp.zeros_like(hcar_ref)

        zc = u_ref[:, 2 * D_BR:3 * D_BR]
        c = u_ref[:, 0:D_BR] * _sig(u_ref[:, D_BR:2 * D_BR])
        cbuf_ref[pl.ds(HALO, t_rows), :] = c
        _shift_copies(cs_ref, cbuf_ref[...])
        _conv_taps(cs_ref, wb_ref, q_out_ref, t_rows, [HALO - (KW - 1) + k for k in range(KW)])
        cbuf_ref[pl.ds(0, HALO), :] = c[t_rows - HALO:t_rows]
        q = q_out_ref[...] + b31_ref[...]
        q_out_ref[...] = q
        _, _, p, sp = _layer_norm_swish(q, lng_ref[...], lnb_ref[...])
        y_ref[:, 0:D_BR] = (p * sp * (zc * _sig(zc))).astype(BF16)

        xl = u_ref[:, 3 * D_BR:4 * D_BR]
        zl = u_ref[:, 4 * D_BR:5 * D_BR]
        _, xc, _, r, ig, _, a, m = _lru_gates(xl, xbuf_ref, w4_ref, b4_ref[...], wa_ref, ba_ref[...], wx_ref,
                                              bx_ref[...], lam_ref[...], t_rows)
        xbuf_ref[pl.ds(0, HALO4), :] = xl[t_rows - HALO4:t_rows]
        xc_out_ref[...] = xc
        r_out_ref[...] = r
        ig_out_ref[...] = ig
        a_out_ref[...] = a
        m_out_ref[...] = m
        h_in = hcar_ref[...]
        hb_ref[...] = h_in
        h = _scan_fwd(a, m * (ig * xc), h_in)
        h_out_ref[...] = h
        hcar_ref[...] = h_out_ref[pl.ds(t_rows - 1, 1), :]
        y_ref[:, D_BR:2 * D_BR] = (h * (zl * _sig(zl))).astype(BF16)

    blk = pl.BlockSpec((t_rows, D_BR), lambda i: (i, 0))
    return pl.pallas_call(
        body, name="mixer_fwd", grid=(nb,),
        in_specs=[pl.BlockSpec((t_rows, D_IN), lambda i: (i, 0))] + _mixer_specs(layer),
        out_specs=[pl.BlockSpec((t_rows, 2 * D_BR), lambda i: (i, 0)), blk, blk,
                   pl.BlockSpec((None, 1, D_BR), lambda i: (i, 0, 0))] + [blk] * 5,
        out_shape=[jax.ShapeDtypeStruct((s, 2 * D_BR), BF16), jax.ShapeDtypeStruct((s, D_BR), F32),
                   jax.ShapeDtypeStruct((s, D_BR), F32), jax.ShapeDtypeStruct((nb, 1, D_BR), F32)]
        + [jax.ShapeDtypeStruct((s, D_BR), F32)] * 5,
        scratch_shapes=[pltpu.VMEM((t_rows + HALO, D_BR), F32), pltpu.VMEM((8, t_rows + HALO, D_BR), F32),
                        pltpu.VMEM((t_rows + HALO4, D_BR), F32), pltpu.VMEM((1, D_BR), F32),
                        pltpu.VMEM((KW, SUBLANES, D_BR), F32)],
        compiler_params=_cparams(1))(u, *params)


def _out_proj(x, y, wo):
    s = x.shape[0]
    tm = min(s, 512)

    def body(x_ref, y_ref, w_ref, o_ref):
        o_ref[...] = x_ref[...] + jnp.dot(y_ref[...], w_ref[...], preferred_element_type=F32)

    blk = pl.BlockSpec((tm, D_MODEL), lambda i: (i, 0))
    return pl.pallas_call(
        body, name="out_proj", grid=(s // tm,),
        in_specs=[blk, blk, _resident((D_MODEL, D_MODEL), lambda i: (0, 0))],
        out_specs=blk,
        out_shape=jax.ShapeDtypeStruct((s, D_MODEL), F32), compiler_params=_cparams(1))(x, y, wo)


def _out_proj_loss_head(x, y, wo, g_row, target):
    s = x.shape[0]
    tm = min(s, 512)

    def body(x_ref, y_ref, w_ref, g_ref, t_ref, loss_ref, dx_ref, dg_ref):
        @pl.when(pl.program_id(0) == 0)
        def _():
            loss_ref[...] = jnp.zeros_like(loss_ref)
            dg_ref[...] = jnp.zeros_like(dg_ref)

        xf = x_ref[...] + jnp.dot(y_ref[...], w_ref[...], preferred_element_type=F32)
        g = g_ref[...]
        rstd = lax.rsqrt(jnp.mean(xf * xf, axis=-1, keepdims=True) + RMS_EPS)
        n = xf * rstd
        err = n * g - t_ref[...]
        loss_ref[...] += 0.5 * jnp.sum(jnp.mean(err * err, axis=-1, keepdims=True))
        dy = err * (1.0 / D_MODEL)
        dg_ref[...] += _colsum(dy * n)
        dn = dy * g
        dx_ref[...] = rstd * (dn - n * jnp.mean(dn * n, axis=-1, keepdims=True))

    blk = pl.BlockSpec((tm, D_MODEL), lambda i: (i, 0))
    return pl.pallas_call(
        body, name="out_proj_loss_head", grid=(s // tm,),
        in_specs=[blk, blk, _resident((D_MODEL, D_MODEL), lambda i: (0, 0)),
                  pl.BlockSpec((1, D_MODEL), lambda i: (0, 0)), blk],
        out_specs=[pl.BlockSpec((8, 128), lambda i: (0, 0)), blk, pl.BlockSpec((1, D_MODEL), lambda i: (0, 0))],
        out_shape=[jax.ShapeDtypeStruct((8, 128), F32), jax.ShapeDtypeStruct((s, D_MODEL), F32),
                   jax.ShapeDtypeStruct((1, D_MODEL), F32)],
        compiler_params=_cparams(1))(x, y, wo, g_row, target)


def _out_proj_bwd_x(dx, wo, job):
    s = dx.shape[0]
    tm = min(s, 512)

    def body(dx_ref, w_ref, dy_ref, dxb_ref):
        dxb = dx_ref[...].astype(BF16)
        dxb_ref[...] = dxb
        dy_ref[...] = lax.dot_general(dxb, w_ref[...], (((1,), (1,)), ((), ())), preferred_element_type=F32)

    blk = pl.BlockSpec((tm, D_MODEL), lambda i: (i, 0))
    own, extra = _hosted_call(
        body, name="out_proj_bwd_x", grid=(s // tm,),
        in_specs=[blk, _resident((D_MODEL, D_MODEL), lambda i: (0, 0))],
        out_specs=[blk, blk],
        out_shape=[jax.ShapeDtypeStruct((s, D_MODEL), F32), jax.ShapeDtypeStruct((s, D_MODEL), BF16)],
        scratch_shapes=[], args=(dx, wo), job=job)
    return own[0], own[1], extra


def _w_in_grad(h, du, job):
    s = h.shape[0]
    tk = min(s, 1024)
    nk = s // tk

    def body(h_ref, du_ref, o_ref, acc_ref):
        k = pl.program_id(1)

        @pl.when(k == 0)
        def _():
            acc_ref[...] = jnp.zeros_like(acc_ref)

        acc_ref[...] += lax.dot_general(h_ref[...], du_ref[...], (((0,), (0,)), ((), ())), preferred_element_type=F32)

        @pl.when(k == nk - 1)
        def _():
            o_ref[0] = acc_ref[:, 0:SHARD_IN].astype(BF16)
            o_ref[1] = acc_ref[:, SHARD_IN:2 * SHARD_IN].astype(BF16)

    own, extra = _hosted_call(
        body, name="w_in_grad", grid=(N_DEV // 2, nk),
        in_specs=[pl.BlockSpec((tk, D_MODEL), lambda q, k: (k, 0)),
                  pl.BlockSpec((tk, 2 * SHARD_IN), lambda q, k: (k, q))],
        out_specs=[pl.BlockSpec((2, None, D_MODEL, SHARD_IN), lambda q, k: (0, q, 0, 0))],
        out_shape=[jax.ShapeDtypeStruct((2, N_DEV // 2, D_MODEL, SHARD_IN), BF16)],
        scratch_shapes=[pltpu.VMEM((D_MODEL, 2 * SHARD_IN), F32)], args=(h, du), job=job)
    return jnp.reshape(own[0], (N_DEV, D_MODEL, SHARD_IN)), extra


def _w_out_grad(y, dxb, job):
    s = y.shape[0]
    tk = min(s, 1024)
    nk = s // tk
    tn = 512

    def body(y_ref, dx_ref, o_ref, acc_ref):
        k = pl.program_id(1)

        @pl.when(k == 0)
        def _():
            acc_ref[...] = jnp.zeros_like(acc_ref)

        acc_ref[...] += lax.dot_general(y_ref[...], dx_ref[...], (((0,), (0,)), ((), ())), preferred_element_type=F32)

        @pl.when(k == nk - 1)
        def _():
            for j in range(N_DEV):
                slot = (j % 2) * 4 + j // 2
                o_ref[slot] = acc_ref[pl.ds(j * SHARD_OUT, SHARD_OUT), :].astype(BF16)

    own, extra = _hosted_call(
        body, name="w_out_grad", grid=(D_MODEL // tn, nk),
        in_specs=[pl.BlockSpec((tk, D_MODEL), lambda n, k: (k, 0)),
                  pl.BlockSpec((tk, tn), lambda n, k: (k, n))],
        out_specs=[pl.BlockSpec((N_DEV, SHARD_OUT, tn), lambda n, k: (0, 0, n))],
        out_shape=[jax.ShapeDtypeStruct((N_DEV, SHARD_OUT, D_MODEL), BF16)],
        scratch_shapes=[pltpu.VMEM((D_MODEL, tn), F32)], args=(y, dxb), job=job)
    return own[0], extra


def _in_proj_bwd_x(du, w_full, x, g_row, dx_next, job):
    s = x.shape[0]
    tm = min(s, 256)

    def body(du_ref, w_ref, x_ref, g_ref, dxn_ref, dx_ref, dg_ref):
        @pl.when(pl.program_id(0) == 0)
        def _():
            dg_ref[...] = jnp.zeros_like(dg_ref)

        dh = lax.dot_general(w_ref[...], du_ref[...], (((1,), (1,)), ((), ())), preferred_element_type=F32).T
        xf = x_ref[...]
        rstd = lax.rsqrt(jnp.mean(xf * xf, axis=-1, keepdims=True) + RMS_EPS)
        n = xf * rstd
        dg_ref[...] += _colsum(dh * n)
        dn = dh * g_ref[...]
        dx_ref[...] = dxn_ref[...] + rstd * (dn - n * jnp.mean(dn * n, axis=-1, keepdims=True))

    blk = pl.BlockSpec((tm, D_MODEL), lambda i: (i, 0))
    own, extra = _hosted_call(
        body, name="in_proj_bwd_x", grid=(s // tm,),
        in_specs=[pl.BlockSpec((tm, D_IN), lambda i: (i, 0)), _resident((D_MODEL, D_IN), lambda i: (0, 0)),
                  blk, pl.BlockSpec((1, D_MODEL), lambda i: (0, 0)), blk],
        out_specs=[blk, pl.BlockSpec((1, D_MODEL), lambda i: (0, 0))],
        out_shape=[jax.ShapeDtypeStruct((s, D_MODEL), F32), jax.ShapeDtypeStruct((1, D_MODEL), F32)],
        scratch_shapes=[], args=(du, w_full, x, g_row, dx_next), job=job)
    return own[0], own[1], extra


PG_B31, PG_LNG, PG_LNB, PG_B4, PG_BA, PG_BX, PG_LAM, PG_W4 = 0, 1, 2, 3, 4, 5, 6, 8
PG_ROWS = 16


def _mixer_bwd(u, kept, dy, hb, params, layer, t_rows):
    s = u.shape[0]
    nb = s // t_rows

    def body(u_ref, q_ref, h_ref, xc_ref, r_ref, ig_ref, a_ref, m_ref, dy_ref, hb_ref,
             w31_ref, b31_ref, lng_ref, lnb_ref, w4_ref, b4_ref, wa_ref, ba_ref, wx_ref, bx_ref, lam_ref,
             du_ref, pg_ref, dw31_ref, dwa_ref, dwx_ref,
             cbuf_ref, cs_ref, dc_ref, dqbuf_ref, dwacc_ref, dxcbuf_ref, acar_ref, gcar_ref, wb_ref):
        step = pl.program_id(0)

        @pl.when(step == 0)
        def _():
            _spread_taps(wb_ref, w31_ref)
            pg_ref[...] = jnp.zeros_like(pg_ref)
            dwa_ref[...] = jnp.zeros_like(dwa_ref)
            dwx_ref[...] = jnp.zeros_like(dwx_ref)
            dwacc_ref[...] = jnp.zeros_like(dwacc_ref)
            dqbuf_ref[pl.ds(t_rows, HALO), :] = jnp.zeros((HALO, D_BR), F32)
            dxcbuf_ref[pl.ds(t_rows, HALO4), :] = jnp.zeros((HALO4, D_BR), F32)
            acar_ref[...] = jnp.zeros_like(acar_ref)
            gcar_ref[...] = jnp.zeros_like(gcar_ref)

        def add_row(r, val):
            pg_ref[r:r + 1, :] += val

        v = u_ref[:, 0:D_BR]
        g = u_ref[:, D_BR:2 * D_BR]
        zc = u_ref[:, 2 * D_BR:3 * D_BR]
        dyc = dy_ref[:, 0:D_BR]
        sg = _sig(g)
        cbuf_ref[...] = v * sg
        ln_gv = lng_ref[...]
        n, rstd, p, sp = _layer_norm_swish(q_ref[...], ln_gv, lnb_ref[...])
        sz = _sig(zc)
        du_ref[:, 2 * D_BR:3 * D_BR] = (dyc * (p * sp) * _dsilu(zc, sz)).astype(BF16)
        dp = dyc * (zc * sz) * _dsilu(p, sp)
        add_row(PG_LNG, _colsum(dp * n))
        add_row(PG_LNB, _colsum(dp))
        dn = dp * ln_gv
        dq = rstd * (dn - jnp.mean(dn, axis=-1, keepdims=True) - n * jnp.mean(dn * n, axis=-1, keepdims=True))
        add_row(PG_B31, _colsum(dq))
        dqbuf_ref[pl.ds(0, t_rows), :] = dq

        _shift_copies(cs_ref, dqbuf_ref[...])

        groups = ROW_CHUNK // SUBLANES

        def conv_chunk(r, carry):
            r0 = pl.multiple_of(r * ROW_CHUNK, ROW_CHUNK)
            cc = cbuf_ref[pl.ds(r0, ROW_CHUNK), :]
            accs = [jnp.zeros((SUBLANES, D_BR), F32) for _ in range(groups)]
            for k in range(KW):
                off = KW - 1 - k
                wv = wb_ref[k]
                ahead = cs_ref[off % 8, pl.ds(r0 + (off // 8) * 8, ROW_CHUNK), :]
                accs = [acc + wv * ahead[SUBLANES * g:SUBLANES * (g + 1)] for g, acc in enumerate(accs)]
                prod = cc * ahead
                part = prod[0:SUBLANES]
                for g in range(1, groups):
                    part = part + prod[SUBLANES * g:SUBLANES * (g + 1)]
                dwacc_ref[k] += part
            dc_ref[pl.ds(r0, ROW_CHUNK), :] = jnp.concatenate(accs, axis=0)
            return carry

        lax.fori_loop(0, t_rows // ROW_CHUNK, conv_chunk, 0)
        dqbuf_ref[pl.ds(t_rows, HALO), :] = dq[0:HALO]
        dc = dc_ref[...]
        du_ref[:, 0:D_BR] = (dc * sg).astype(BF16)
        du_ref[:, D_BR:2 * D_BR] = (dc * v * sg * (1.0 - sg)).astype(BF16)

        xl = u_ref[:, 3 * D_BR:4 * D_BR]
        zl = u_ref[:, 4 * D_BR:5 * D_BR]
        dyl = dy_ref[:, D_BR:2 * D_BR]
        xc = xc_ref[...]
        xc_bf = xc.astype(BF16)
        r = r_ref[...]
        ig = ig_ref[...]
        a = a_ref[...]
        m = m_ref[...]
        log_s = _log_sigmoid(lam_ref[...])
        row = lax.broadcasted_iota(jnp.int32, (t_rows, D_BR), 0)
        h = h_ref[...]
        h_prev = jnp.where(row >= 1, pltpu.roll(h, 1, 0), hb_ref[...])
        szl = _sig(zl)
        du_ref[:, 4 * D_BR:5 * D_BR] = (dyl * h * _dsilu(zl, szl)).astype(BF16)
        a_next = jnp.where(row < t_rows - 1, pltpu.roll(a, t_rows - 1, 0), acar_ref[...])
        gs = _scan_rev(a_next, dyl * (zl * szl), gcar_ref[...])
        dc_ref[...] = gs
        gcar_ref[...] = dc_ref[pl.ds(0, 1), :]
        dc_ref[...] = a
        acar_ref[...] = dc_ref[pl.ds(0, 1), :]

        dm = gs * ig * xc
        di = gs * m * xc
        dla = gs * h_prev * a - dm * (a * a / m)
        add_row(PG_LAM, _colsum(dla * r) * LRU_C)
        dra = dla * (LRU_C * log_s) * r * (1.0 - r)
        dia = di * ig * (1.0 - ig)
        add_row(PG_BA, _colsum(dra))
        add_row(PG_BX, _colsum(dia))
        dra_bf = dra.astype(BF16)
        dia_bf = dia.astype(BF16)
        for hd in range(HEADS):
            sl = slice(hd * HD, (hd + 1) * HD)
            dwa_ref[hd] += lax.dot_general(xc_bf[:, sl], dra_bf[:, sl], (((0,), (0,)), ((), ())),
                                           preferred_element_type=F32)
            dwx_ref[hd] += lax.dot_general(xc_bf[:, sl], dia_bf[:, sl], (((0,), (0,)), ((), ())),
                                           preferred_element_type=F32)
        dxc = gs * m * ig + _heads_matmul_t(dra_bf, wa_ref) + _heads_matmul_t(dia_bf, wx_ref)
        add_row(PG_B4, _colsum(dxc))
        n4 = t_rows + HALO4
        add_row(PG_W4 + 3, _colsum(dxc * xl))
        dxcbuf_ref[pl.ds(0, t_rows), :] = dxc
        db = dxcbuf_ref[...]
        dxl = w4_ref[3:4, :] * dxc
        for k in range(KW4 - 1):
            ahead = pltpu.roll(db, n4 - (KW4 - 1 - k), 0)[0:t_rows]
            dxl = dxl + w4_ref[k:k + 1, :] * ahead
            add_row(PG_W4 + k, _colsum(xl * ahead))
        dxcbuf_ref[pl.ds(t_rows, HALO4), :] = dxc[0:HALO4]
        du_ref[:, 3 * D_BR:4 * D_BR] = dxl.astype(BF16)

        @pl.when(step == nb - 1)
        def _():
            pg_ref[PG_LAM:PG_LAM + 1, :] = pg_ref[PG_LAM:PG_LAM + 1, :] * _sig(-lam_ref[...])
            dw31_ref[...] = jnp.zeros_like(dw31_ref)
            for k in range(KW):
                dw31_ref[k:k + 1, :] = jnp.sum(dwacc_ref[k], axis=0, keepdims=True)

    const2 = lambda i: (0, 0)
    const3 = lambda i: (0, 0, 0)
    rev = lambda i: (nb - 1 - i, 0)
    return pl.pallas_call(
        body, name="mixer_bwd", grid=(nb,),
        in_specs=[pl.BlockSpec((t_rows, D_IN), rev)] + [pl.BlockSpec((t_rows, D_BR), rev)] * len(kept) + [
                  pl.BlockSpec((t_rows, 2 * D_BR), rev),
                  pl.BlockSpec((None, 1, D_BR), lambda i: (nb - 1 - i, 0, 0))] + _mixer_specs(layer),
        out_specs=[pl.BlockSpec((t_rows, D_IN), rev),
                   pl.BlockSpec((PG_ROWS, D_BR), const2), pl.BlockSpec((32, D_BR), const2),
                   pl.BlockSpec((HEADS, HD, HD), const3), pl.BlockSpec((HEADS, HD, HD), const3)],
        out_shape=[jax.ShapeDtypeStruct((s, D_IN), BF16), jax.ShapeDtypeStruct((PG_ROWS, D_BR), F32),
                   jax.ShapeDtypeStruct((32, D_BR), F32), jax.ShapeDtypeStruct((HEADS, HD, HD), F32),
                   jax.ShapeDtypeStruct((HEADS, HD, HD), F32)],
        scratch_shapes=[pltpu.VMEM((t_rows, D_BR), F32), pltpu.VMEM((8, t_rows + HALO, D_BR), F32),
                        pltpu.VMEM((t_rows, D_BR), F32), pltpu.VMEM((t_rows + HALO, D_BR), F32),
                        pltpu.VMEM((KW, 8, D_BR), F32),
                        pltpu.VMEM((t_rows + HALO4, D_BR), F32), pltpu.VMEM((1, D_BR), F32),
                        pltpu.VMEM((1, D_BR), F32), pltpu.VMEM((KW, SUBLANES, D_BR), F32)],
        compiler_params=_cparams(1))(u, *kept, dy, hb, *params)


def _add_kept_half(src, recv, keep, out_dtype, name):
    h, r, c = recv.shape
    tr = min(r, 1024)

    def body(keep_ref, s_ref, r_ref, o_ref):
        o_ref[...] = (s_ref[...].astype(F32) + r_ref[...].astype(F32)).astype(out_dtype)

    grid_spec = pltpu.PrefetchScalarGridSpec(
        num_scalar_prefetch=1, grid=(h, r // tr),
        in_specs=[pl.BlockSpec((None, tr, c), lambda b, i, kp: (kp[0] * h + b, i, 0)),
                  pl.BlockSpec((None, tr, c), lambda b, i, kp: (b, i, 0))],
        out_specs=pl.BlockSpec((None, tr, c), lambda b, i, kp: (b, i, 0)))
    return pl.pallas_call(
        body, name=name, grid_spec=grid_spec, out_shape=jax.ShapeDtypeStruct(recv.shape, out_dtype),
        compiler_params=_cparams(2))(keep, src, recv)


class _PendingReduce:
    STAGE_AXES = (2, 0, 1)

    def __init__(self, bufs):
        self.bufs = list(bufs)
        self.stage = 0

    def job(self):
        return _ExchangeJob(self.bufs, self.STAGE_AXES[self.stage])

    def absorb(self, recvs):
        me = _my_pos()[self.STAGE_AXES[self.stage]]
        keep = jnp.reshape(me, (1,)).astype(jnp.int32)
        last = self.stage == 2
        self.bufs = [_add_kept_half(b, r, keep, F32 if last else BF16, f"rs_add{self.stage}_{t}")
                     for t, (b, r) in enumerate(zip(self.bufs, recvs))]
        self.stage += 1

    def finish_alone(self):
        while self.stage < 3:
            job = self.job()
            self.absorb(_run_job(job, f"rs_exchange{self.stage}"))
        return [b[0] for b in self.bufs]


def _all_reduce_small(pa, pb, job):
    nt = job.nt

    def body(*refs):
        pa_ref, pb_ref = refs[:2]
        job_in = refs[2:2 + nt]
        oa_ref, ob_ref = refs[2 + nt:4 + nt]
        job_out = refs[4 + nt:4 + 2 * nt]
        ra0, ra1, ra2, sb0, sb1, sb2, rb0, rb1, rb2, send_sems, recv_sems = refs[4 + 2 * nt:15 + 2 * nt]
        job_scr = refs[15 + 2 * nt:]
        job.start(job_in, job_out, job_scr)
        x, y, c = _my_pos()
        peers = [(x, y, 1 - c), (1 - x, y, c), (x, 1 - y, c)]
        oa_ref[...] = pa_ref[...]
        ob_ref[...] = pb_ref[...]
        for k, (peer, ra, sb, rb) in enumerate(zip(peers, (ra0, ra1, ra2), (sb0, sb1, sb2), (rb0, rb1, rb2))):
            sb[...] = ob_ref[...].astype(BF16)
            copies = [pltpu.make_async_remote_copy(
                src_ref=src, dst_ref=dst, send_sem=send_sems.at[t, k], recv_sem=recv_sems.at[t, k],
                device_id=peer, device_id_type=MESH) for t, (src, dst) in enumerate(((oa_ref, ra), (sb, rb)))]
            for cp in copies:
                cp.start()
            for cp in copies:
                cp.wait()
            oa_ref[...] = oa_ref[...] + ra[...]
            ob_ref[...] = sb[...].astype(F32) + rb[...].astype(F32)
        job.finish(job_in, job_out, job_scr)

    vm = pl.BlockSpec(memory_space=pltpu.VMEM)
    outs = pl.pallas_call(
        body, name="small_all_reduce",
        out_shape=[jax.ShapeDtypeStruct(pa.shape, F32), jax.ShapeDtypeStruct(pb.shape, F32)] + job.out_shape,
        in_specs=[vm, vm] + job.in_specs, out_specs=[vm, vm] + job.out_specs,
        scratch_shapes=[pltpu.VMEM(pa.shape, F32)] * 3 + [pltpu.VMEM(pb.shape, BF16)] * 6
        + [pltpu.SemaphoreType.DMA((2, 3)), pltpu.SemaphoreType.DMA((2, 3))] + job.scratch,
        compiler_params=pltpu.CompilerParams(vmem_limit_bytes=VMEM_LIMIT))(pa, pb, *job.arrays)
    return outs[0], outs[1], list(outs[2:])


def _adamw(w, g, m, v, name):
    r, c = w.shape
    tr = r
    for cand in (512, 256, 128, 64, 32, 16, 8):
        if r % cand == 0 and cand * c * 4 <= (2 << 20):
            tr = cand
            break

    def body(w_ref, g_ref, m_ref, v_ref, d_ref, mo_ref, vo_ref):
        gv = g_ref[...]
        m_new = ADAM_B1 * m_ref[...] + (1.0 - ADAM_B1) * gv
        v_new = ADAM_B2 * v_ref[...] + (1.0 - ADAM_B2) * (gv * gv)
        m_hat = m_new / (1.0 - ADAM_B1 ** ADAM_STEP)
        v_hat = v_new / (1.0 - ADAM_B2 ** ADAM_STEP)
        d_ref[...] = -ADAM_LR * (m_hat / (jnp.sqrt(v_hat) + ADAM_EPS) + ADAM_WD * w_ref[...])
        mo_ref[...] = m_new
        vo_ref[...] = v_new

    spec = pl.BlockSpec((tr, c), lambda i: (i, 0))
    shape = jax.ShapeDtypeStruct((r, c), F32)
    return pl.pallas_call(
        body, name=name, grid=(r // tr,), in_specs=[spec] * 4, out_specs=[spec] * 3, out_shape=[shape] * 3,
        compiler_params=_cparams(1))(w, g, m, v)


def _pack_rows(parts):
    flat = jnp.concatenate([jnp.reshape(p, (-1, D_BR)) for p in parts], axis=0)
    pad = (-flat.shape[0]) % 64
    if pad:
        flat = jnp.concatenate([flat, jnp.zeros((pad, D_BR), F32)], axis=0)
    return flat


def _unpack_rows(flat, shapes):
    out, r0 = [], 0
    for shp in shapes:
        n = 1
        for d in shp:
            n *= d
        rows = n // D_BR
        out.append(jnp.reshape(flat[r0:r0 + rows], shp))
        r0 += rows
    return out


def kernel(x, norm_g, w_in, conv_dw_w, conv_dw_b, conv_ln_g, conv_ln_b, lru_conv_w, lru_conv_b, lru_wa, lru_ba, lru_wx, lru_bx, lru_lambda, w_out, final_g, loss_target, m_norm_g, m_w_in, m_conv_dw_w, m_conv_dw_b, m_conv_ln_g, m_conv_ln_b, m_lru_conv_w, m_lru_conv_b, m_lru_wa, m_lru_ba, m_lru_wx, m_lru_bx, m_lru_lambda, m_w_out, m_final_g, v_norm_g, v_w_in, v_conv_dw_w, v_conv_dw_b, v_conv_ln_g, v_conv_ln_b, v_lru_conv_w, v_lru_conv_b, v_lru_wa, v_lru_ba, v_lru_wx, v_lru_bx, v_lru_lambda, v_w_out, v_final_g):
    n_layers = norm_g.shape[0]
    s = x.shape[1]
    t_rows = min(s, 128)
    xs = jnp.reshape(x, (s, D_MODEL))
    target = jnp.reshape(loss_target, (s, D_MODEL))
    dev = 4 * lax.axis_index("x") + 2 * lax.axis_index("y") + lax.axis_index("c")

    w_in_bf = _cast_bf16(w_in, "cast_w_in")
    w_out_bf = _cast_bf16(w_out, "cast_w_out")
    w_in_l, w31_all, w4_all = _run_job(_GatherJob([w_in_bf[0], conv_dw_w, lru_conv_w]), "weight_all_gather0")
    w_out_l = None
    w31_full = jnp.reshape(jnp.transpose(w31_all, (1, 2, 0, 3)), (n_layers, KW, D_BR))
    w4_full = jnp.reshape(jnp.transpose(w4_all, (1, 2, 0, 3)), (n_layers, KW4, D_BR))
    row3 = lambda p: jnp.reshape(p, (n_layers, 1, -1))
    mixer_params = (w31_full, row3(conv_dw_b), row3(conv_ln_g), row3(conv_ln_b), w4_full, row3(lru_conv_b),
                    lru_wa.astype(BF16), row3(lru_ba), lru_wx.astype(BF16), row3(lru_bx), row3(lru_lambda))

    saved = []
    act = xs
    for l in range(n_layers):
        wanted = [w_out_bf[0]] if l == 0 else []
        if l + 1 < n_layers:
            wanted += [w_in_bf[l + 1], w_out_bf[l + 1]]
        w_full = _w_in_rows(w_in_l)
        h, u, gathered = _in_proj(act, norm_g[l:l + 1], w_full, _GatherJob(wanted) if wanted else None)
        if l == 0:
            w_out_l, gathered = gathered[0], gathered[1:]
        y, q_sv, h_sv, hb, *gates_sv = _mixer_fwd(u, mixer_params, l, t_rows)
        kept = [q_sv, h_sv, *gates_sv]
        wo = jnp.reshape(w_out_l, (D_MODEL, D_MODEL))
        saved.append((act, h, u, y, kept, hb, w_full, wo))
        if l + 1 < n_layers:
            act = _out_proj(act, y, wo)
            w_in_l, w_out_l = gathered
    loss_part, dx, d_final_g = _out_proj_loss_head(act, y, wo, jnp.reshape(final_g, (1, D_MODEL)), target)
    loss = lax.psum(loss_part[0, 0], AXES)

    pending = None
    reduced_big = [None] * n_layers
    small = [None] * n_layers
    for l in reversed(range(n_layers)):
        x_l, h, u, y, kept, hb, w_full, wo = saved[l]
        dy, dxb, recvs = _out_proj_bwd_x(dx, wo, pending.job() if pending else None)
        if pending:
            pending.absorb(recvs)
            reduced_big[l + 1] = [b[0] for b in pending.bufs]
        g_out, _ = _w_out_grad(y, dxb, None)
        du, pg, dw31, dwa, dwx = _mixer_bwd(u, kept, dy, hb, mixer_params, l, t_rows)
        g_in, _ = _w_in_grad(h, du, None)
        pending = _PendingReduce([g_in, g_out])
        pending.absorb(_run_job(pending.job(), "rs_exchange_c"))
        dx, d_norm, recvs = _in_proj_bwd_x(du, w_full, x_l, norm_g[l:l + 1], dx, pending.job())
        pending.absorb(recvs)
        small[l] = (d_norm, pg, dw31, dwa, dwx)
    grad_x = jnp.reshape(dx, x.shape)

    stack = lambda f: jnp.stack([f(small[l]) for l in range(n_layers)])
    pg_all = stack(lambda t: t[1])
    rep_parts = [
        (stack(lambda t: t[0][0]), norm_g.shape), (pg_all[:, PG_B31], conv_dw_b.shape),
        (pg_all[:, PG_LNG], conv_ln_g.shape), (pg_all[:, PG_LNB], conv_ln_b.shape),
        (pg_all[:, PG_B4], lru_conv_b.shape), (stack(lambda t: t[3]), lru_wa.shape), (pg_all[:, PG_BA], lru_ba.shape),
        (stack(lambda t: t[4]), lru_wx.shape), (pg_all[:, PG_BX], lru_bx.shape), (pg_all[:, PG_LAM], lru_lambda.shape),
        (d_final_g, final_g.shape)]
    shard_parts = [(stack(lambda t: t[2][0:KW]), (n_layers, KW, D_BR)),
                   (pg_all[:, PG_W4:PG_W4 + KW4], (n_layers, KW4, D_BR))]
    gate_w = (5, 7)
    f32_parts = [p for i, p in enumerate(rep_parts) if i not in gate_w] + shard_parts
    bf16_parts = [rep_parts[i] for i in gate_w]
    red_a, red_b, recvs = _all_reduce_small(
        _pack_rows([p for p, _ in f32_parts]), _pack_rows([p for p, _ in bf16_parts]), pending.job())
    pending.absorb(recvs)
    reduced_big[0] = [b[0] for b in pending.bufs]
    grad_w_in = jnp.stack([r[0] for r in reduced_big])
    grad_w_out = jnp.stack([r[1] for r in reduced_big])
    red_a = _unpack_rows(red_a, [shp for _, shp in f32_parts])
    red_b = _unpack_rows(red_b, [shp for _, shp in bf16_parts])
    rep_grads = red_a[:len(rep_parts) - len(gate_w)]
    for i, g in zip(gate_w, red_b):
        rep_grads.insert(i, g)
    grad_dw = lax.dynamic_slice_in_dim(red_a[-2], dev * HD, HD, axis=2)
    grad_w4 = lax.dynamic_slice_in_dim(red_a[-1], dev * HD, HD, axis=2)

    def adam_nd(w, g, m, v, name):
        two_d = (-1, w.shape[-1])
        outs = _adamw(*(jnp.reshape(t, two_d) for t in (w, g, m, v)), name)
        return [jnp.reshape(o, w.shape) for o in outs]

    upd = {}
    upd["w_in"] = adam_nd(w_in, grad_w_in, m_w_in, v_w_in, "adamw_w_in")
    upd["w_out"] = adam_nd(w_out, grad_w_out, m_w_out, v_w_out, "adamw_w_out")
    upd["conv_dw_w"] = adam_nd(conv_dw_w, grad_dw, m_conv_dw_w, v_conv_dw_w, "adamw_conv_dw_w")
    upd["lru_conv_w"] = adam_nd(lru_conv_w, grad_w4, m_lru_conv_w, v_lru_conv_w, "adamw_lru_conv_w")
    rep_w = [norm_g, conv_dw_b, conv_ln_g, conv_ln_b, lru_conv_b, lru_wa, lru_ba, lru_wx, lru_bx, lru_lambda, final_g]
    rep_m = [m_norm_g, m_conv_dw_b, m_conv_ln_g, m_conv_ln_b, m_lru_conv_b, m_lru_wa, m_lru_ba, m_lru_wx, m_lru_bx,
             m_lru_lambda, m_final_g]
    rep_v = [v_norm_g, v_conv_dw_b, v_conv_ln_g, v_conv_ln_b, v_lru_conv_b, v_lru_wa, v_lru_ba, v_lru_wx, v_lru_bx,
             v_lru_lambda, v_final_g]
    rep_keys = ["norm_g", "conv_dw_b", "conv_ln_g", "conv_ln_b", "lru_conv_b", "lru_wa", "lru_ba", "lru_wx", "lru_bx",
                "lru_lambda", "final_g"]
    grads = {"w_in": grad_w_in, "w_out": grad_w_out, "conv_dw_w": grad_dw, "lru_conv_w": grad_w4}
    for i, key in enumerate(rep_keys):
        grads[key] = rep_grads[i]
        upd[key] = adam_nd(rep_w[i], rep_grads[i], rep_m[i], rep_v[i], "adamw_" + key)

    order = ["norm_g", "w_in", "conv_dw_w", "conv_dw_b", "conv_ln_g", "conv_ln_b", "lru_conv_w", "lru_conv_b", "lru_wa",
             "lru_ba", "lru_wx", "lru_bx", "lru_lambda", "w_out", "final_g"]
    return (loss, grad_x, *[grads[k] for k in order], *[upd[k][0] for k in order], *[upd[k][1] for k in order],
            *[upd[k][2] for k in order])
```

```python
import functools

import jax
import jax.numpy as jnp
from jax import lax
from jax.experimental import pallas as pl
from jax.experimental.pallas import tpu as pltpu

F32 = jnp.float32
BF16 = jnp.bfloat16
MESH = pl.DeviceIdType.MESH
AXES = ("x", "y", "c")
N_DEV = 8

D_MODEL = 2048
D_BR = 1024
D_IN = 5 * D_BR
SHARD_IN = D_IN // N_DEV
SHARD_OUT = D_MODEL // N_DEV
KW = 31
KW4 = 4
HEADS = 8
HD = 128
LRU_C = 8.0
RMS_EPS = 1e-6
LN_EPS = 1e-5
SUBLANES = 8
HALO = 32
HALO4 = 8
ROW_CHUNK = 16

ADAM_LR = 0.001
ADAM_B1 = 0.9
ADAM_B2 = 0.999
ADAM_EPS = 1e-08
ADAM_WD = 0.01
ADAM_STEP = 10

VMEM_LIMIT = 60 * 1024 * 1024

ANY = pl.BlockSpec(memory_space=pl.ANY)


def _cparams(n_grid):
    return pltpu.CompilerParams(dimension_semantics=("arbitrary",) * n_grid, vmem_limit_bytes=VMEM_LIMIT)


def _resident(block_shape, index_map):
    return pl.BlockSpec(block_shape, index_map, pipeline_mode=pl.Buffered(1))


def _sig(x):
    return 0.5 * jnp.tanh(0.5 * x) + 0.5


def _dsilu(z, sz):
    return sz * (1.0 + z * (1.0 - sz))


def _expm1(x):
    small = jnp.abs(x) < 0.01
    series = x * (1.0 + x * (0.5 + x * (1.0 / 6.0 + x * (1.0 / 24.0))))
    return jnp.where(small, series, jnp.exp(x) - 1.0)


def _log_sigmoid(x):
    e = jnp.exp(-jnp.abs(x))
    l1p = jnp.where(e < 0.01, e * (1.0 - e * (0.5 - e * (1.0 / 3.0))), jnp.log(1.0 + e))
    return jnp.minimum(x, 0.0) - l1p


def _colsum(x):
    return jnp.sum(x, axis=0, keepdims=True)


def _my_pos():
    return lax.axis_index("x"), lax.axis_index("y"), lax.axis_index("c")


class _GatherJob:
    def __init__(self, shards):
        self.arrays = list(shards)
        nt = self.nt = len(self.arrays)
        self.in_specs = [ANY] * nt
        self.out_shape = [jax.ShapeDtypeStruct((N_DEV,) + s.shape, s.dtype) for s in self.arrays]
        self.out_specs = [ANY] * nt
        self.scratch = [pltpu.SemaphoreType.DMA((nt, 7)), pltpu.SemaphoreType.DMA((nt, 7)),
                        pltpu.SemaphoreType.DMA((nt,))]

    def _plan(self, srcs, outs, scr):
        send_sems, recv_sems, local_sems = scr
        x, y, c = _my_pos()
        me, sibling = (x, y, c), (x, y, 1 - c)
        chips = [(1 - x, y), (x, 1 - y), (1 - x, 1 - y)]

        def slot(p):
            return 4 * p[0] + 2 * p[1] + p[2]

        def copy(t, k, block, to, own=False):
            dst = outs[t].at[slot(block)]
            return pltpu.make_async_remote_copy(
                src_ref=srcs[t] if own else dst, dst_ref=dst, send_sem=send_sems.at[t, k], recv_sem=recv_sems.at[t, k],
                device_id=to, device_id_type=MESH)

        mine = [pltpu.make_async_copy(srcs[t], outs[t].at[slot(me)], local_sems.at[t]) for t in range(self.nt)]
        first = []
        for t in range(self.nt):
            first.append(copy(t, 0, me, sibling, own=True))
            first += [copy(t, 1 + j, me, (*chip, c), own=True) for j, chip in enumerate(chips)]
        return me, sibling, chips, c, copy, mine, first

    def start(self, srcs, outs, scr):
        _, _, _, _, _, mine, first = self._plan(srcs, outs, scr)
        for cp in mine + first:
            cp.start()

    def finish(self, srcs, outs, scr):
        me, sibling, chips, c, copy, mine, first = self._plan(srcs, outs, scr)
        passed = []
        for j, chip in enumerate(chips):
            for t in range(self.nt):
                copy(t, 1 + j, (*chip, c), me).wait_recv()
                fwd = copy(t, 4 + j, (*chip, c), sibling)
                fwd.start()
                passed.append(fwd)
        for t in range(self.nt):
            copy(t, 0, sibling, me).wait_recv()
            for j, chip in enumerate(chips):
                copy(t, 4 + j, (*chip, 1 - c), me).wait_recv()
        for cp in first + passed:
            cp.wait_send()
        for cp in mine:
            cp.wait()


class _ExchangeJob:
    def __init__(self, srcs, axis):
        self.arrays = list(srcs)
        self.axis = axis
        nt = self.nt = len(self.arrays)
        self.half = [s.shape[0] // 2 for s in self.arrays]
        self.in_specs = [ANY] * nt
        self.out_shape = [jax.ShapeDtypeStruct((h,) + s.shape[1:], s.dtype) for h, s in zip(self.half, self.arrays)]
        self.out_specs = [ANY] * nt
        self.scratch = [pltpu.SemaphoreType.DMA((nt,)), pltpu.SemaphoreType.DMA((nt,))]

    def _copies(self, srcs, outs, scr):
        send_sems, recv_sems = scr
        pos = list(_my_pos())
        me = pos[self.axis]
        pos[self.axis] = 1 - me
        return [pltpu.make_async_remote_copy(
            src_ref=srcs[t].at[pl.ds((1 - me) * self.half[t], self.half[t])], dst_ref=outs[t],
            send_sem=send_sems.at[t], recv_sem=recv_sems.at[t], device_id=tuple(pos), device_id_type=MESH)
            for t in range(self.nt)]

    def start(self, srcs, outs, scr):
        for cp in self._copies(srcs, outs, scr):
            cp.start()

    def finish(self, srcs, outs, scr):
        for cp in self._copies(srcs, outs, scr):
            cp.wait()


def _run_job(job, name):
    def body(*refs):
        ins, outs, scr = refs[:job.nt], refs[job.nt:2 * job.nt], refs[2 * job.nt:]
        job.start(ins, outs, scr)
        job.finish(ins, outs, scr)

    return pl.pallas_call(body, name=name, out_shape=job.out_shape, in_specs=job.in_specs, out_specs=job.out_specs,
                          scratch_shapes=job.scratch)(*job.arrays)


def _hosted_call(body, *, name, grid, in_specs, out_specs, out_shape, scratch_shapes, args, job):
    n_in, n_out, n_scr = len(in_specs), len(out_specs), len(scratch_shapes)
    if job is None:
        outs = pl.pallas_call(body, name=name, grid=grid, in_specs=in_specs, out_specs=out_specs, out_shape=out_shape,
                              scratch_shapes=scratch_shapes, compiler_params=_cparams(len(grid)))(*args)
        return list(outs), None
    nt = job.nt

    def full_body(*refs):
        own_in, job_in = refs[:n_in], refs[n_in:n_in + nt]
        base = n_in + nt
        own_out, job_out = refs[base:base + n_out], refs[base + n_out:base + n_out + nt]
        base += n_out + nt
        own_scr, job_scr = refs[base:base + n_scr], refs[base + n_scr:]
        ids = [pl.program_id(a) for a in range(len(grid))]
        is_first = functools.reduce(jnp.logical_and, [i == 0 for i in ids])
        is_last = functools.reduce(jnp.logical_and, [i == g - 1 for i, g in zip(ids, grid)])

        @pl.when(is_first)
        def _():
            job.start(job_in, job_out, job_scr)

        body(*own_in, *own_out, *own_scr)

        @pl.when(is_last)
        def _():
            job.finish(job_in, job_out, job_scr)

    outs = pl.pallas_call(
        full_body, name=name, grid=grid, in_specs=list(in_specs) + job.in_specs,
        out_specs=list(out_specs) + job.out_specs, out_shape=list(out_shape) + job.out_shape,
        scratch_shapes=list(scratch_shapes) + job.scratch, compiler_params=_cparams(len(grid)))(*args, *job.arrays)
    return list(outs[:n_out]), list(outs[n_out:])


def _cast_bf16(x, name):
    nl, r, c = x.shape
    tr = min(r, 512)

    def body(x_ref, o_ref):
        o_ref[...] = x_ref[...].astype(BF16)

    spec = pl.BlockSpec((None, tr, c), lambda l, i: (l, i, 0))
    return pl.pallas_call(
        body, name=name, grid=(nl, r // tr), in_specs=[spec], out_specs=spec,
        out_shape=jax.ShapeDtypeStruct(x.shape, BF16), compiler_params=_cparams(2))(x)


def _w_in_rows(w_all):
    def body(i_ref, o_ref):
        o_ref[...] = i_ref[...]

    return pl.pallas_call(
        body, name="w_in_rows", grid=(N_DEV,),
        in_specs=[pl.BlockSpec((None, D_MODEL, SHARD_IN), lambda j: (j, 0, 0))],
        out_specs=pl.BlockSpec((D_MODEL, SHARD_IN), lambda j: (0, j)),
        out_shape=jax.ShapeDtypeStruct((D_MODEL, D_IN), BF16), compiler_params=_cparams(1))(w_all)


def _in_proj(x, g_row, w_full, job):
    s = x.shape[0]
    tm = min(s, 1024)
    tn = 2 * SHARD_IN

    def body(x_ref, g_ref, w_ref, h_ref, u_ref):
        @pl.when(pl.program_id(1) == 0)
        def _():
            xf = x_ref[...]
            rstd = lax.rsqrt(jnp.mean(xf * xf, axis=-1, keepdims=True) + RMS_EPS)
            h_ref[...] = (xf * rstd * g_ref[...]).astype(BF16)

        u_ref[...] = jnp.dot(h_ref[...], w_ref[...], preferred_element_type=F32)

    own, extra = _hosted_call(
        body, name="in_proj", grid=(s // tm, D_IN // tn),
        in_specs=[pl.BlockSpec((tm, D_MODEL), lambda i, j: (i, 0)),
                  pl.BlockSpec((1, D_MODEL), lambda i, j: (0, 0)),
                  pl.BlockSpec((D_MODEL, tn), lambda i, j: (0, j))],
        out_specs=[pl.BlockSpec((tm, D_MODEL), lambda i, j: (i, 0)),
                   pl.BlockSpec((tm, tn), lambda i, j: (i, j))],
        out_shape=[jax.ShapeDtypeStruct((s, D_MODEL), BF16), jax.ShapeDtypeStruct((s, D_IN), F32)],
        scratch_shapes=[], args=(x, g_row, w_full), job=job)
    return own[0], own[1], extra


def _shift_copies(cs_ref, buf):
    n = buf.shape[0]
    cs_ref[0] = buf
    for sft in range(1, 8):
        cs_ref[sft] = pltpu.roll(buf, n - sft, 0)


def _spread_taps(wb_ref, w_ref):
    for k in range(KW):
        wb_ref[k] = jnp.broadcast_to(w_ref[k:k + 1, :], (SUBLANES, D_BR))


def _conv_taps(cs_ref, wb_ref, q_ref, t_rows, offs):
    groups = ROW_CHUNK // SUBLANES

    def chunk(r, carry):
        r0 = pl.multiple_of(r * ROW_CHUNK, ROW_CHUNK)
        accs = [jnp.zeros((SUBLANES, D_BR), F32) for _ in range(groups)]
        for k, off in enumerate(offs):
            wv = wb_ref[k]
            ahead = cs_ref[off % 8, pl.ds(r0 + (off // 8) * 8, ROW_CHUNK), :]
            accs = [acc + wv * ahead[SUBLANES * g:SUBLANES * (g + 1)] for g, acc in enumerate(accs)]
        q_ref[pl.ds(r0, ROW_CHUNK), :] = jnp.concatenate(accs, axis=0)
        return carry

    lax.fori_loop(0, t_rows // ROW_CHUNK, chunk, 0)


def _scan_fwd(a, b, h_in):
    t_rows = a.shape[0]
    row8 = lax.broadcasted_iota(jnp.int32, a.shape, 0) & (SUBLANES - 1)
    d = 1
    while d < SUBLANES:
        keep = row8 >= d
        a_s = jnp.where(keep, pltpu.roll(a, d, 0), 1.0)
        b_s = jnp.where(keep, pltpu.roll(b, d, 0), 0.0)
        b = a * b_s + b
        a = a * a_s
        d *= 2
    carry = h_in
    groups = []
    for grp in range(t_rows // SUBLANES):
        rows = slice(grp * SUBLANES, (grp + 1) * SUBLANES)
        h_g = b[rows] + a[rows] * carry
        groups.append(h_g)
        carry = h_g[SUBLANES - 1:SUBLANES]
    return jnp.concatenate(groups, axis=0)


def _scan_rev(a, b, g_in):
    t_rows = a.shape[0]
    row8 = lax.broadcasted_iota(jnp.int32, a.shape, 0) & (SUBLANES - 1)
    d = 1
    while d < SUBLANES:
        keep = row8 < SUBLANES - d
        a_s = jnp.where(keep, pltpu.roll(a, t_rows - d, 0), 1.0)
        b_s = jnp.where(keep, pltpu.roll(b, t_rows - d, 0), 0.0)
        b = a * b_s + b
        a = a * a_s
        d *= 2
    carry = g_in
    groups = []
    for grp in reversed(range(t_rows // SUBLANES)):
        rows = slice(grp * SUBLANES, (grp + 1) * SUBLANES)
        g_g = b[rows] + a[rows] * carry
        groups.append(g_g)
        carry = g_g[0:1]
    return jnp.concatenate(groups[::-1], axis=0)


def _heads_matmul(x_bf, w_ref):
    return jnp.concatenate(
        [jnp.dot(x_bf[:, h * HD:(h + 1) * HD], w_ref[h], preferred_element_type=F32) for h in range(HEADS)], axis=1)


def _heads_matmul_t(d_bf, w_ref):
    return jnp.concatenate(
        [lax.dot_general(d_bf[:, h * HD:(h + 1) * HD], w_ref[h], (((1,), (1,)), ((), ())), preferred_element_type=F32)
         for h in range(HEADS)], axis=1)


def _layer_norm_swish(q, ln_g, ln_b, with_swish=True):
    mu = jnp.mean(q, axis=-1, keepdims=True)
    xc = q - mu
    var = jnp.mean(xc * xc, axis=-1, keepdims=True)
    rstd = lax.rsqrt(var + LN_EPS)
    n = xc * rstd
    p = n * ln_g + ln_b
    return n, rstd, p, (_sig(p) if with_swish else None)


def _lru_gates(xl, xbuf_ref, w4_ref, b4, wa_ref, ba, wx_ref, bx, lam, t_rows):
    xbuf_ref[pl.ds(HALO4, t_rows), :] = xl
    xb = xbuf_ref[...]
    n = t_rows + HALO4
    xc = b4 + w4_ref[3:4, :] * xl
    for k in range(KW4 - 1):
        off = HALO4 - (KW4 - 1) + k
        xc = xc + w4_ref[k:k + 1, :] * pltpu.roll(xb, n - off, 0)[0:t_rows]
    xc_bf = xc.astype(BF16)
    r = _sig(_heads_matmul(xc_bf, wa_ref) + ba)
    ig = _sig(_heads_matmul(xc_bf, wx_ref) + bx)
    log_s = _log_sigmoid(lam)
    la = LRU_C * r * log_s
    a = jnp.exp(la)
    m = jnp.sqrt(-_expm1(2.0 * la))
    return xb, xc, xc_bf, r, ig, log_s, a, m


def _mixer_specs(layer):
    row1 = lambda i: (layer, 0, 0)
    heads = lambda i: (layer, 0, 0, 0)
    return [pl.BlockSpec((None, KW, D_BR), row1),
            pl.BlockSpec((None, 1, D_BR), row1), pl.BlockSpec((None, 1, D_BR), row1),
            pl.BlockSpec((None, 1, D_BR), row1),
            pl.BlockSpec((None, KW4, D_BR), row1), pl.BlockSpec((None, 1, D_BR), row1),
            pl.BlockSpec((None, HEADS, HD, HD), heads), pl.BlockSpec((None, 1, D_BR), row1),
            pl.BlockSpec((None, HEADS, HD, HD), heads), pl.BlockSpec((None, 1, D_BR), row1),
            pl.BlockSpec((None, 1, D_BR), row1)]


def _mixer_fwd(u, params, layer, t_rows):
    s = u.shape[0]
    nb = s // t_rows

    def body(u_ref, w31_ref, b31_ref, lng_ref, lnb_ref, w4_ref, b4_ref, wa_ref, ba_ref, wx_ref, bx_ref, lam_ref,
             y_ref, q_out_ref, h_out_ref, hb_ref, xc_out_ref, r_out_ref, ig_out_ref, a_out_ref, m_out_ref,
             sg_out_ref, sp_out_ref, szc_out_ref, szl_out_ref,
             cbuf_ref, cs_ref, xbuf_ref, hcar_ref, wb_ref):
        @pl.when(pl.program_id(0) == 0)
        def _():
            _spread_taps(wb_ref, w31_ref)
            cbuf_ref[pl.ds(0, HALO), :] = jnp.zeros((HALO, D_BR), F32)
            xbuf_ref[pl.ds(0, HALO4), :] = jnp.zeros((HALO4, D_BR), F32)
            hcar_ref[...] = jnp.zeros_like(hcar_ref)

        zc = u_ref[:, 2 * D_BR:3 * D_BR]
        sg = _sig(u_ref[:, D_BR:2 * D_BR])
        sg_out_ref[...] = sg
        c = u_ref[:, 0:D_BR] * sg
        cbuf_ref[pl.ds(HALO, t_rows), :] = c
        _shift_copies(cs_ref, cbuf_ref[...])
        _conv_taps(cs_ref, wb_ref, q_out_ref, t_rows, [HALO - (KW - 1) + k for k in range(KW)])
        cbuf_ref[pl.ds(0, HALO), :] = c[t_rows - HALO:t_rows]
        q = q_out_ref[...] + b31_ref[...]
        q_out_ref[...] = q
        _, _, p, sp = _layer_norm_swish(q, lng_ref[...], lnb_ref[...])
        sp_out_ref[...] = sp
        szc = _sig(zc)
        szc_out_ref[...] = szc
        y_ref[:, 0:D_BR] = (p * sp * (zc * szc)).astype(BF16)

        xl = u_ref[:, 3 * D_BR:4 * D_BR]
        zl = u_ref[:, 4 * D_BR:5 * D_BR]
        _, xc, _, r, ig, _, a, m = _lru_gates(xl, xbuf_ref, w4_ref, b4_ref[...], wa_ref, ba_ref[...], wx_ref,
                                              bx_ref[...], lam_ref[...], t_rows)
        xbuf_ref[pl.ds(0, HALO4), :] = xl[t_rows - HALO4:t_rows]
        xc_out_ref[...] = xc
        r_out_ref[...] = r
        ig_out_ref[...] = ig
        a_out_ref[...] = a
        m_out_ref[...] = m
        h_in = hcar_ref[...]
        hb_ref[...] = h_in
        h = _scan_fwd(a, m * (ig * xc), h_in)
        h_out_ref[...] = h
        hcar_ref[...] = h_out_ref[pl.ds(t_rows - 1, 1), :]
        szl = _sig(zl)
        szl_out_ref[...] = szl
        y_ref[:, D_BR:2 * D_BR] = (h * (zl * szl)).astype(BF16)

    blk = pl.BlockSpec((t_rows, D_BR), lambda i: (i, 0))
    return pl.pallas_call(
        body, name="mixer_fwd", grid=(nb,),
        in_specs=[pl.BlockSpec((t_rows, D_IN), lambda i: (i, 0))] + _mixer_specs(layer),
        out_specs=[pl.BlockSpec((t_rows, 2 * D_BR), lambda i: (i, 0)), blk, blk,
                   pl.BlockSpec((None, 1, D_BR), lambda i: (i, 0, 0))] + [blk] * 9,
        out_shape=[jax.ShapeDtypeStruct((s, 2 * D_BR), BF16), jax.ShapeDtypeStruct((s, D_BR), F32),
                   jax.ShapeDtypeStruct((s, D_BR), F32), jax.ShapeDtypeStruct((nb, 1, D_BR), F32)]
        + [jax.ShapeDtypeStruct((s, D_BR), F32)] * 9,
        scratch_shapes=[pltpu.VMEM((t_rows + HALO, D_BR), F32), pltpu.VMEM((8, t_rows + HALO, D_BR), F32),
                        pltpu.VMEM((t_rows + HALO4, D_BR), F32), pltpu.VMEM((1, D_BR), F32),
                        pltpu.VMEM((KW, SUBLANES, D_BR), F32)],
        compiler_params=_cparams(1))(u, *params)


def _out_proj(x, y, wo):
    s = x.shape[0]
    tm = min(s, 512)

    def body(x_ref, y_ref, w_ref, o_ref):
        o_ref[...] = x_ref[...] + jnp.dot(y_ref[...], w_ref[...], preferred_element_type=F32)

    blk = pl.BlockSpec((tm, D_MODEL), lambda i: (i, 0))
    return pl.pallas_call(
        body, name="out_proj", grid=(s // tm,),
        in_specs=[blk, blk, _resident((D_MODEL, D_MODEL), lambda i: (0, 0))],
        out_specs=blk,
        out_shape=jax.ShapeDtypeStruct((s, D_MODEL), F32), compiler_params=_cparams(1))(x, y, wo)


def _out_proj_loss_head(x, y, wo, g_row, target):
    s = x.shape[0]
    tm = min(s, 512)

    def body(x_ref, y_ref, w_ref, g_ref, t_ref, loss_ref, dx_ref, dg_ref):
        @pl.when(pl.program_id(0) == 0)
        def _():
            loss_ref[...] = jnp.zeros_like(loss_ref)
            dg_ref[...] = jnp.zeros_like(dg_ref)

        xf = x_ref[...] + jnp.dot(y_ref[...], w_ref[...], preferred_element_type=F32)
        g = g_ref[...]
        rstd = lax.rsqrt(jnp.mean(xf * xf, axis=-1, keepdims=True) + RMS_EPS)
        n = xf * rstd
        err = n * g - t_ref[...]
        loss_ref[...] += 0.5 * jnp.sum(jnp.mean(err * err, axis=-1, keepdims=True))
        dy = err * (1.0 / D_MODEL)
        dg_ref[...] += _colsum(dy * n)
        dn = dy * g
        dx_ref[...] = rstd * (dn - n * jnp.mean(dn * n, axis=-1, keepdims=True))

    blk = pl.BlockSpec((tm, D_MODEL), lambda i: (i, 0))
    return pl.pallas_call(
        body, name="out_proj_loss_head", grid=(s // tm,),
        in_specs=[blk, blk, _resident((D_MODEL, D_MODEL), lambda i: (0, 0)),
                  pl.BlockSpec((1, D_MODEL), lambda i: (0, 0)), blk],
        out_specs=[pl.BlockSpec((8, 128), lambda i: (0, 0)), blk, pl.BlockSpec((1, D_MODEL), lambda i: (0, 0))],
        out_shape=[jax.ShapeDtypeStruct((8, 128), F32), jax.ShapeDtypeStruct((s, D_MODEL), F32),
                   jax.ShapeDtypeStruct((1, D_MODEL), F32)],
        compiler_params=_cparams(1))(x, y, wo, g_row, target)


def _out_proj_bwd_x(dx, wo, job):
    s = dx.shape[0]
    tm = min(s, 512)

    def body(dx_ref, w_ref, dy_ref, dxb_ref):
        dxb = dx_ref[...].astype(BF16)
        dxb_ref[...] = dxb
        dy_ref[...] = lax.dot_general(dxb, w_ref[...], (((1,), (1,)), ((), ())), preferred_element_type=F32)

    blk = pl.BlockSpec((tm, D_MODEL), lambda i: (i, 0))
    own, extra = _hosted_call(
        body, name="out_proj_bwd_x", grid=(s // tm,),
        in_specs=[blk, _resident((D_MODEL, D_MODEL), lambda i: (0, 0))],
        out_specs=[blk, blk],
        out_shape=[jax.ShapeDtypeStruct((s, D_MODEL), F32), jax.ShapeDtypeStruct((s, D_MODEL), BF16)],
        scratch_shapes=[], args=(dx, wo), job=job)
    return own[0], own[1], extra


def _w_in_grad(h, du, job):
    s = h.shape[0]
    tk = min(s, 1024)
    nk = s // tk

    def body(h_ref, du_ref, o_ref, acc_ref):
        k = pl.program_id(1)

        @pl.when(k == 0)
        def _():
            acc_ref[...] = jnp.zeros_like(acc_ref)

        acc_ref[...] += lax.dot_general(h_ref[...], du_ref[...], (((0,), (0,)), ((), ())), preferred_element_type=F32)

        @pl.when(k == nk - 1)
        def _():
            o_ref[0] = acc_ref[:, 0:SHARD_IN].astype(BF16)
            o_ref[1] = acc_ref[:, SHARD_IN:2 * SHARD_IN].astype(BF16)

    own, extra = _hosted_call(
        body, name="w_in_grad", grid=(N_DEV // 2, nk),
        in_specs=[pl.BlockSpec((tk, D_MODEL), lambda q, k: (k, 0)),
                  pl.BlockSpec((tk, 2 * SHARD_IN), lambda q, k: (k, q))],
        out_specs=[pl.BlockSpec((2, None, D_MODEL, SHARD_IN), lambda q, k: (0, q, 0, 0))],
        out_shape=[jax.ShapeDtypeStruct((2, N_DEV // 2, D_MODEL, SHARD_IN), BF16)],
        scratch_shapes=[pltpu.VMEM((D_MODEL, 2 * SHARD_IN), F32)], args=(h, du), job=job)
    return jnp.reshape(own[0], (N_DEV, D_MODEL, SHARD_IN)), extra


def _w_out_grad(y, dxb, job):
    s = y.shape[0]
    tk = min(s, 1024)
    nk = s // tk
    tn = 512

    def body(y_ref, dx_ref, o_ref, acc_ref):
        k = pl.program_id(1)

        @pl.when(k == 0)
        def _():
            acc_ref[...] = jnp.zeros_like(acc_ref)

        acc_ref[...] += lax.dot_general(y_ref[...], dx_ref[...], (((0,), (0,)), ((), ())), preferred_element_type=F32)

        @pl.when(k == nk - 1)
        def _():
            for j in range(N_DEV):
                slot = (j % 2) * 4 + j // 2
                o_ref[slot] = acc_ref[pl.ds(j * SHARD_OUT, SHARD_OUT), :].astype(BF16)

    own, extra = _hosted_call(
        body, name="w_out_grad", grid=(D_MODEL // tn, nk),
        in_specs=[pl.BlockSpec((tk, D_MODEL), lambda n, k: (k, 0)),
                  pl.BlockSpec((tk, tn), lambda n, k: (k, n))],
        out_specs=[pl.BlockSpec((N_DEV, SHARD_OUT, tn), lambda n, k: (0, 0, n))],
        out_shape=[jax.ShapeDtypeStruct((N_DEV, SHARD_OUT, D_MODEL), BF16)],
        scratch_shapes=[pltpu.VMEM((D_MODEL, tn), F32)], args=(y, dxb), job=job)
    return own[0], extra


def _in_proj_bwd_x(du, w_full, x, g_row, dx_next, job):
    s = x.shape[0]
    tm = min(s, 256)

    def body(du_ref, w_ref, x_ref, g_ref, dxn_ref, dx_ref, dg_ref):
        @pl.when(pl.program_id(0) == 0)
        def _():
            dg_ref[...] = jnp.zeros_like(dg_ref)

        dh = lax.dot_general(w_ref[...], du_ref[...], (((1,), (1,)), ((), ())), preferred_element_type=F32).T
        xf = x_ref[...]
        rstd = lax.rsqrt(jnp.mean(xf * xf, axis=-1, keepdims=True) + RMS_EPS)
        n = xf * rstd
        dg_ref[...] += _colsum(dh * n)
        dn = dh * g_ref[...]
        dx_ref[...] = dxn_ref[...] + rstd * (dn - n * jnp.mean(dn * n, axis=-1, keepdims=True))

    blk = pl.BlockSpec((tm, D_MODEL), lambda i: (i, 0))
    own, extra = _hosted_call(
        body, name="in_proj_bwd_x", grid=(s // tm,),
        in_specs=[pl.BlockSpec((tm, D_IN), lambda i: (i, 0)), _resident((D_MODEL, D_IN), lambda i: (0, 0)),
                  blk, pl.BlockSpec((1, D_MODEL), lambda i: (0, 0)), blk],
        out_specs=[blk, pl.BlockSpec((1, D_MODEL), lambda i: (0, 0))],
        out_shape=[jax.ShapeDtypeStruct((s, D_MODEL), F32), jax.ShapeDtypeStruct((1, D_MODEL), F32)],
        scratch_shapes=[], args=(du, w_full, x, g_row, dx_next), job=job)
    return own[0], own[1], extra


PG_B31, PG_LNG, PG_LNB, PG_B4, PG_BA, PG_BX, PG_LAM, PG_W4 = 0, 1, 2, 3, 4, 5, 6, 8
PG_ROWS = 16


def _mixer_bwd(u, kept, dy, hb, params, layer, t_rows):
    s = u.shape[0]
    nb = s // t_rows

    def body(u_ref, q_ref, h_ref, xc_ref, r_ref, ig_ref, a_ref, m_ref, sg_ref, sp_ref, szc_ref, szl_ref, dy_ref, hb_ref,
             w31_ref, b31_ref, lng_ref, lnb_ref, w4_ref, b4_ref, wa_ref, ba_ref, wx_ref, bx_ref, lam_ref,
             du_ref, pg_ref, dw31_ref, dwa_ref, dwx_ref,
             cbuf_ref, cs_ref, dc_ref, dqbuf_ref, dwacc_ref, dxcbuf_ref, acar_ref, gcar_ref, wb_ref):
        step = pl.program_id(0)

        @pl.when(step == 0)
        def _():
            _spread_taps(wb_ref, w31_ref)
            pg_ref[...] = jnp.zeros_like(pg_ref)
            dwa_ref[...] = jnp.zeros_like(dwa_ref)
            dwx_ref[...] = jnp.zeros_like(dwx_ref)
            dwacc_ref[...] = jnp.zeros_like(dwacc_ref)
            dqbuf_ref[pl.ds(t_rows, HALO), :] = jnp.zeros((HALO, D_BR), F32)
            dxcbuf_ref[pl.ds(t_rows, HALO4), :] = jnp.zeros((HALO4, D_BR), F32)
            acar_ref[...] = jnp.zeros_like(acar_ref)
            gcar_ref[...] = jnp.zeros_like(gcar_ref)

        def add_row(r, val):
            pg_ref[r:r + 1, :] += val

        v = u_ref[:, 0:D_BR]
        g = u_ref[:, D_BR:2 * D_BR]
        zc = u_ref[:, 2 * D_BR:3 * D_BR]
        dyc = dy_ref[:, 0:D_BR]
        sg = sg_ref[...]
        cbuf_ref[...] = v * sg
        ln_gv = lng_ref[...]
        n, rstd, p, _ = _layer_norm_swish(q_ref[...], ln_gv, lnb_ref[...], with_swish=False)
        sp = sp_ref[...]
        sz = szc_ref[...]
        du_ref[:, 2 * D_BR:3 * D_BR] = (dyc * (p * sp) * _dsilu(zc, sz)).astype(BF16)
        dp = dyc * (zc * sz) * _dsilu(p, sp)
        add_row(PG_LNG, _colsum(dp * n))
        add_row(PG_LNB, _colsum(dp))
        dn = dp * ln_gv
        dq = rstd * (dn - jnp.mean(dn, axis=-1, keepdims=True) - n * jnp.mean(dn * n, axis=-1, keepdims=True))
        add_row(PG_B31, _colsum(dq))
        dqbuf_ref[pl.ds(0, t_rows), :] = dq

        _shift_copies(cs_ref, dqbuf_ref[...])

        groups = ROW_CHUNK // SUBLANES

        def conv_chunk(r, carry):
            r0 = pl.multiple_of(r * ROW_CHUNK, ROW_CHUNK)
            cc = cbuf_ref[pl.ds(r0, ROW_CHUNK), :]
            accs = [jnp.zeros((SUBLANES, D_BR), F32) for _ in range(groups)]
            for k in range(KW):
                off = KW - 1 - k
                wv = wb_ref[k]
                ahead = cs_ref[off % 8, pl.ds(r0 + (off // 8) * 8, ROW_CHUNK), :]
                accs = [acc + wv * ahead[SUBLANES * g:SUBLANES * (g + 1)] for g, acc in enumerate(accs)]
                prod = cc * ahead
                part = prod[0:SUBLANES]
                for g in range(1, groups):
                    part = part + prod[SUBLANES * g:SUBLANES * (g + 1)]
                dwacc_ref[k] += part
            dc_ref[pl.ds(r0, ROW_CHUNK), :] = jnp.concatenate(accs, axis=0)
            return carry

        lax.fori_loop(0, t_rows // ROW_CHUNK, conv_chunk, 0)
        dqbuf_ref[pl.ds(t_rows, HALO), :] = dq[0:HALO]
        dc = dc_ref[...]
        du_ref[:, 0:D_BR] = (dc * sg).astype(BF16)
        du_ref[:, D_BR:2 * D_BR] = (dc * v * sg * (1.0 - sg)).astype(BF16)

        xl = u_ref[:, 3 * D_BR:4 * D_BR]
        zl = u_ref[:, 4 * D_BR:5 * D_BR]
        dyl = dy_ref[:, D_BR:2 * D_BR]
        xc = xc_ref[...]
        xc_bf = xc.astype(BF16)
        r = r_ref[...]
        ig = ig_ref[...]
        a = a_ref[...]
        m = m_ref[...]
        log_s = _log_sigmoid(lam_ref[...])
        row = lax.broadcasted_iota(jnp.int32, (t_rows, D_BR), 0)
        h = h_ref[...]
        h_prev = jnp.where(row >= 1, pltpu.roll(h, 1, 0), hb_ref[...])
        szl = szl_ref[...]
        du_ref[:, 4 * D_BR:5 * D_BR] = (dyl * h * _dsilu(zl, szl)).astype(BF16)
        a_next = jnp.where(row < t_rows - 1, pltpu.roll(a, t_rows - 1, 0), acar_ref[...])
        gs = _scan_rev(a_next, dyl * (zl * szl), gcar_ref[...])
        dc_ref[...] = gs
        gcar_ref[...] = dc_ref[pl.ds(0, 1), :]
        dc_ref[...] = a
        acar_ref[...] = dc_ref[pl.ds(0, 1), :]

        dm = gs * ig * xc
        di = gs * m * xc
        dla = gs * h_prev * a - dm * (a * a / m)
        add_row(PG_LAM, _colsum(dla * r) * LRU_C)
        dra = dla * (LRU_C * log_s) * r * (1.0 - r)
        dia = di * ig * (1.0 - ig)
        add_row(PG_BA, _colsum(dra))
        add_row(PG_BX, _colsum(dia))
        dra_bf = dra.astype(BF16)
        dia_bf = dia.astype(BF16)
        for hd in range(HEADS):
            sl = slice(hd * HD, (hd + 1) * HD)
            dwa_ref[hd] += lax.dot_general(xc_bf[:, sl], dra_bf[:, sl], (((0,), (0,)), ((), ())),
                                           preferred_element_type=F32)
            dwx_ref[hd] += lax.dot_general(xc_bf[:, sl], dia_bf[:, sl], (((0,), (0,)), ((), ())),
                                           preferred_element_type=F32)
        dxc = gs * m * ig + _heads_matmul_t(dra_bf, wa_ref) + _heads_matmul_t(dia_bf, wx_ref)
        add_row(PG_B4, _colsum(dxc))
        n4 = t_rows + HALO4
        add_row(PG_W4 + 3, _colsum(dxc * xl))
        dxcbuf_ref[pl.ds(0, t_rows), :] = dxc
        db = dxcbuf_ref[...]
        dxl = w4_ref[3:4, :] * dxc
        for k in range(KW4 - 1):
            ahead = pltpu.roll(db, n4 - (KW4 - 1 - k), 0)[0:t_rows]
            dxl = dxl + w4_ref[k:k + 1, :] * ahead
            add_row(PG_W4 + k, _colsum(xl * ahead))
        dxcbuf_ref[pl.ds(t_rows, HALO4), :] = dxc[0:HALO4]
        du_ref[:, 3 * D_BR:4 * D_BR] = dxl.astype(BF16)

        @pl.when(step == nb - 1)
        def _():
            pg_ref[PG_LAM:PG_LAM + 1, :] = pg_ref[PG_LAM:PG_LAM + 1, :] * _sig(-lam_ref[...])
            dw31_ref[...] = jnp.zeros_like(dw31_ref)
            for k in range(KW):
                dw31_ref[k:k + 1, :] = jnp.sum(dwacc_ref[k], axis=0, keepdims=True)

    const2 = lambda i: (0, 0)
    const3 = lambda i: (0, 0, 0)
    rev = lambda i: (nb - 1 - i, 0)
    return pl.pallas_call(
        body, name="mixer_bwd", grid=(nb,),
        in_specs=[pl.BlockSpec((t_rows, D_IN), rev)] + [pl.BlockSpec((t_rows, D_BR), rev)] * len(kept) + [
                  pl.BlockSpec((t_rows, 2 * D_BR), rev),
                  pl.BlockSpec((None, 1, D_BR), lambda i: (nb - 1 - i, 0, 0))] + _mixer_specs(layer),
        out_specs=[pl.BlockSpec((t_rows, D_IN), rev),
                   pl.BlockSpec((PG_ROWS, D_BR), const2), pl.BlockSpec((32, D_BR), const2),
                   pl.BlockSpec((HEADS, HD, HD), const3), pl.BlockSpec((HEADS, HD, HD), const3)],
        out_shape=[jax.ShapeDtypeStruct((s, D_IN), BF16), jax.ShapeDtypeStruct((PG_ROWS, D_BR), F32),
                   jax.ShapeDtypeStruct((32, D_BR), F32), jax.ShapeDtypeStruct((HEADS, HD, HD), F32),
                   jax.ShapeDtypeStruct((HEADS, HD, HD), F32)],
        scratch_shapes=[pltpu.VMEM((t_rows, D_BR), F32), pltpu.VMEM((8, t_rows + HALO, D_BR), F32),
                        pltpu.VMEM((t_rows, D_BR), F32), pltpu.VMEM((t_rows + HALO, D_BR), F32),
                        pltpu.VMEM((KW, 8, D_BR), F32),
                        pltpu.VMEM((t_rows + HALO4, D_BR), F32), pltpu.VMEM((1, D_BR), F32),
                        pltpu.VMEM((1, D_BR), F32), pltpu.VMEM((KW, SUBLANES, D_BR), F32)],
        compiler_params=_cparams(1))(u, *kept, dy, hb, *params)


def _add_kept_half(src, recv, keep, out_dtype, name):
    h, r, c = recv.shape
    tr = min(r, 1024)

    def body(keep_ref, s_ref, r_ref, o_ref):
        o_ref[...] = (s_ref[...].astype(F32) + r_ref[...].astype(F32)).astype(out_dtype)

    grid_spec = pltpu.PrefetchScalarGridSpec(
        num_scalar_prefetch=1, grid=(h, r // tr),
        in_specs=[pl.BlockSpec((None, tr, c), lambda b, i, kp: (kp[0] * h + b, i, 0)),
                  pl.BlockSpec((None, tr, c), lambda b, i, kp: (b, i, 0))],
        out_specs=pl.BlockSpec((None, tr, c), lambda b, i, kp: (b, i, 0)))
    return pl.pallas_call(
        body, name=name, grid_spec=grid_spec, out_shape=jax.ShapeDtypeStruct(recv.shape, out_dtype),
        compiler_params=_cparams(2))(keep, src, recv)


class _PendingReduce:
    STAGE_AXES = (2, 0, 1)

    def __init__(self, bufs):
        self.bufs = list(bufs)
        self.stage = 0

    def job(self):
        return _ExchangeJob(self.bufs, self.STAGE_AXES[self.stage])

    def absorb(self, recvs):
        me = _my_pos()[self.STAGE_AXES[self.stage]]
        keep = jnp.reshape(me, (1,)).astype(jnp.int32)
        last = self.stage == 2
        self.bufs = [_add_kept_half(b, r, keep, F32 if last else BF16, f"rs_add{self.stage}_{t}")
                     for t, (b, r) in enumerate(zip(self.bufs, recvs))]
        self.stage += 1

    def finish_alone(self):
        while self.stage < 3:
            job = self.job()
            self.absorb(_run_job(job, f"rs_exchange{self.stage}"))
        return [b[0] for b in self.bufs]


def _all_reduce_small(pa, pb, job):
    nt = job.nt

    def body(*refs):
        pa_ref, pb_ref = refs[:2]
        job_in = refs[2:2 + nt]
        oa_ref, ob_ref = refs[2 + nt:4 + nt]
        job_out = refs[4 + nt:4 + 2 * nt]
        ra0, ra1, ra2, sb0, sb1, sb2, rb0, rb1, rb2, send_sems, recv_sems = refs[4 + 2 * nt:15 + 2 * nt]
        job_scr = refs[15 + 2 * nt:]
        job.start(job_in, job_out, job_scr)
        x, y, c = _my_pos()
        peers = [(x, y, 1 - c), (1 - x, y, c), (x, 1 - y, c)]
        oa_ref[...] = pa_ref[...]
        ob_ref[...] = pb_ref[...]
        for k, (peer, ra, sb, rb) in enumerate(zip(peers, (ra0, ra1, ra2), (sb0, sb1, sb2), (rb0, rb1, rb2))):
            sb[...] = ob_ref[...].astype(BF16)
            copies = [pltpu.make_async_remote_copy(
                src_ref=src, dst_ref=dst, send_sem=send_sems.at[t, k], recv_sem=recv_sems.at[t, k],
                device_id=peer, device_id_type=MESH) for t, (src, dst) in enumerate(((oa_ref, ra), (sb, rb)))]
            for cp in copies:
                cp.start()
            for cp in copies:
                cp.wait()
            oa_ref[...] = oa_ref[...] + ra[...]
            ob_ref[...] = sb[...].astype(F32) + rb[...].astype(F32)
        job.finish(job_in, job_out, job_scr)

    vm = pl.BlockSpec(memory_space=pltpu.VMEM)
    outs = pl.pallas_call(
        body, name="small_all_reduce",
        out_shape=[jax.ShapeDtypeStruct(pa.shape, F32), jax.ShapeDtypeStruct(pb.shape, F32)] + job.out_shape,
        in_specs=[vm, vm] + job.in_specs, out_specs=[vm, vm] + job.out_specs,
        scratch_shapes=[pltpu.VMEM(pa.shape, F32)] * 3 + [pltpu.VMEM(pb.shape, BF16)] * 6
        + [pltpu.SemaphoreType.DMA((2, 3)), pltpu.SemaphoreType.DMA((2, 3))] + job.scratch,
        compiler_params=pltpu.CompilerParams(vmem_limit_bytes=VMEM_LIMIT))(pa, pb, *job.arrays)
    return outs[0], outs[1], list(outs[2:])


def _adamw(w, g, m, v, name):
    r, c = w.shape
    tr = r
    for cand in (512, 256, 128, 64, 32, 16, 8):
        if r % cand == 0 and cand * c * 4 <= (2 << 20):
            tr = cand
            break

    def body(w_ref, g_ref, m_ref, v_ref, d_ref, mo_ref, vo_ref):
        gv = g_ref[...]
        m_new = ADAM_B1 * m_ref[...] + (1.0 - ADAM_B1) * gv
        v_new = ADAM_B2 * v_ref[...] + (1.0 - ADAM_B2) * (gv * gv)
        m_hat = m_new / (1.0 - ADAM_B1 ** ADAM_STEP)
        v_hat = v_new / (1.0 - ADAM_B2 ** ADAM_STEP)
        d_ref[...] = -ADAM_LR * (m_hat / (jnp.sqrt(v_hat) + ADAM_EPS) + ADAM_WD * w_ref[...])
        mo_ref[...] = m_new
        vo_ref[...] = v_new

    spec = pl.BlockSpec((tr, c), lambda i: (i, 0))
    shape = jax.ShapeDtypeStruct((r, c), F32)
    return pl.pallas_call(
        body, name=name, grid=(r // tr,), in_specs=[spec] * 4, out_specs=[spec] * 3, out_shape=[shape] * 3,
        compiler_params=_cparams(1))(w, g, m, v)


def _pack_rows(parts):
    flat = jnp.concatenate([jnp.reshape(p, (-1, D_BR)) for p in parts], axis=0)
    pad = (-flat.shape[0]) % 64
    if pad:
        flat = jnp.concatenate([flat, jnp.zeros((pad, D_BR), F32)], axis=0)
    return flat


def _unpack_rows(flat, shapes):
    out, r0 = [], 0
    for shp in shapes:
        n = 1
        for d in shp:
            n *= d
        rows = n // D_BR
        out.append(jnp.reshape(flat[r0:r0 + rows], shp))
        r0 += rows
    return out


def kernel(x, norm_g, w_in, conv_dw_w, conv_dw_b, conv_ln_g, conv_ln_b, lru_conv_w, lru_conv_b, lru_wa, lru_ba, lru_wx, lru_bx, lru_lambda, w_out, final_g, loss_target, m_norm_g, m_w_in, m_conv_dw_w, m_conv_dw_b, m_conv_ln_g, m_conv_ln_b, m_lru_conv_w, m_lru_conv_b, m_lru_wa, m_lru_ba, m_lru_wx, m_lru_bx, m_lru_lambda, m_w_out, m_final_g, v_norm_g, v_w_in, v_conv_dw_w, v_conv_dw_b, v_conv_ln_g, v_conv_ln_b, v_lru_conv_w, v_lru_conv_b, v_lru_wa, v_lru_ba, v_lru_wx, v_lru_bx, v_lru_lambda, v_w_out, v_final_g):
    n_layers = norm_g.shape[0]
    s = x.shape[1]
    t_rows = min(s, 128)
    xs = jnp.reshape(x, (s, D_MODEL))
    target = jnp.reshape(loss_target, (s, D_MODEL))
    dev = 4 * lax.axis_index("x") + 2 * lax.axis_index("y") + lax.axis_index("c")

    w_in_bf = _cast_bf16(w_in, "cast_w_in")
    w_out_bf = _cast_bf16(w_out, "cast_w_out")
    w_in_l, w31_all, w4_all = _run_job(_GatherJob([w_in_bf[0], conv_dw_w, lru_conv_w]), "weight_all_gather0")
    w_out_l = None
    w31_full = jnp.reshape(jnp.transpose(w31_all, (1, 2, 0, 3)), (n_layers, KW, D_BR))
    w4_full = jnp.reshape(jnp.transpose(w4_all, (1, 2, 0, 3)), (n_layers, KW4, D_BR))
    row3 = lambda p: jnp.reshape(p, (n_layers, 1, -1))
    mixer_params = (w31_full, row3(conv_dw_b), row3(conv_ln_g), row3(conv_ln_b), w4_full, row3(lru_conv_b),
                    lru_wa.astype(BF16), row3(lru_ba), lru_wx.astype(BF16), row3(lru_bx), row3(lru_lambda))

    saved = []
    act = xs
    for l in range(n_layers):
        wanted = [w_out_bf[0]] if l == 0 else []
        if l + 1 < n_layers:
            wanted += [w_in_bf[l + 1], w_out_bf[l + 1]]
        w_full = _w_in_rows(w_in_l)
        h, u, gathered = _in_proj(act, norm_g[l:l + 1], w_full, _GatherJob(wanted) if wanted else None)
        if l == 0:
            w_out_l, gathered = gathered[0], gathered[1:]
        y, q_sv, h_sv, hb, *gates_sv = _mixer_fwd(u, mixer_params, l, t_rows)
        kept = [q_sv, h_sv, *gates_sv]
        wo = jnp.reshape(w_out_l, (D_MODEL, D_MODEL))
        saved.append((act, h, u, y, kept, hb, w_full, wo))
        if l + 1 < n_layers:
            act = _out_proj(act, y, wo)
            w_in_l, w_out_l = gathered
    loss_part, dx, d_final_g = _out_proj_loss_head(act, y, wo, jnp.reshape(final_g, (1, D_MODEL)), target)
    loss = lax.psum(loss_part[0, 0], AXES)

    pending = None
    reduced_big = [None] * n_layers
    small = [None] * n_layers
    for l in reversed(range(n_layers)):
        x_l, h, u, y, kept, hb, w_full, wo = saved[l]
        dy, dxb, recvs = _out_proj_bwd_x(dx, wo, pending.job() if pending else None)
        if pending:
            pending.absorb(recvs)
            reduced_big[l + 1] = [b[0] for b in pending.bufs]
        g_out, _ = _w_out_grad(y, dxb, None)
        du, pg, dw31, dwa, dwx = _mixer_bwd(u, kept, dy, hb, mixer_params, l, t_rows)
        g_in, _ = _w_in_grad(h, du, None)
        pending = _PendingReduce([g_in, g_out])
        pending.absorb(_run_job(pending.job(), "rs_exchange_c"))
        dx, d_norm, recvs = _in_proj_bwd_x(du, w_full, x_l, norm_g[l:l + 1], dx, pending.job())
        pending.absorb(recvs)
        small[l] = (d_norm, pg, dw31, dwa, dwx)
    grad_x = jnp.reshape(dx, x.shape)

    stack = lambda f: jnp.stack([f(small[l]) for l in range(n_layers)])
    pg_all = stack(lambda t: t[1])
    rep_parts = [
        (stack(lambda t: t[0][0]), norm_g.shape), (pg_all[:, PG_B31], conv_dw_b.shape),
        (pg_all[:, PG_LNG], conv_ln_g.shape), (pg_all[:, PG_LNB], conv_ln_b.shape),
        (pg_all[:, PG_B4], lru_conv_b.shape), (stack(lambda t: t[3]), lru_wa.shape), (pg_all[:, PG_BA], lru_ba.shape),
        (stack(lambda t: t[4]), lru_wx.shape), (pg_all[:, PG_BX], lru_bx.shape), (pg_all[:, PG_LAM], lru_lambda.shape),
        (d_final_g, final_g.shape)]
    shard_parts = [(stack(lambda t: t[2][0:KW]), (n_layers, KW, D_BR)),
                   (pg_all[:, PG_W4:PG_W4 + KW4], (n_layers, KW4, D_BR))]
    gate_w = (5, 7)
    f32_parts = [p for i, p in enumerate(rep_parts) if i not in gate_w] + shard_parts
    bf16_parts = [rep_parts[i] for i in gate_w]
    red_a, red_b, recvs = _all_reduce_small(
        _pack_rows([p for p, _ in f32_parts]), _pack_rows([p for p, _ in bf16_parts]), pending.job())
    pending.absorb(recvs)
    reduced_big[0] = [b[0] for b in pending.bufs]
    grad_w_in = jnp.stack([r[0] for r in reduced_big])
    grad_w_out = jnp.stack([r[1] for r in reduced_big])
    red_a = _unpack_rows(red_a, [shp for _, shp in f32_parts])
    red_b = _unpack_rows(red_b, [shp for _, shp in bf16_parts])
    rep_grads = red_a[:len(rep_parts) - len(gate_w)]
    for i, g in zip(gate_w, red_b):
        rep_grads.insert(i, g)
    grad_dw = lax.dynamic_slice_in_dim(red_a[-2], dev * HD, HD, axis=2)
    grad_w4 = lax.dynamic_slice_in_dim(red_a[-1], dev * HD, HD, axis=2)

    def adam_nd(w, g, m, v, name):
        two_d = (-1, w.shape[-1])
        outs = _adamw(*(jnp.reshape(t, two_d) for t in (w, g, m, v)), name)
        return [jnp.reshape(o, w.shape) for o in outs]

    upd = {}
    upd["w_in"] = adam_nd(w_in, grad_w_in, m_w_in, v_w_in, "adamw_w_in")
    upd["w_out"] = adam_nd(w_out, grad_w_out, m_w_out, v_w_out, "adamw_w_out")
    upd["conv_dw_w"] = adam_nd(conv_dw_w, grad_dw, m_conv_dw_w, v_conv_dw_w, "adamw_conv_dw_w")
    upd["lru_conv_w"] = adam_nd(lru_conv_w, grad_w4, m_lru_conv_w, v_lru_conv_w, "adamw_lru_conv_w")
    rep_w = [norm_g, conv_dw_b, conv_ln_g, conv_ln_b, lru_conv_b, lru_wa, lru_ba, lru_wx, lru_bx, lru_lambda, final_g]
    rep_m = [m_norm_g, m_conv_dw_b, m_conv_ln_g, m_conv_ln_b, m_lru_conv_b, m_lru_wa, m_lru_ba, m_lru_wx, m_lru_bx,
             m_lru_lambda, m_final_g]
    rep_v = [v_norm_g, v_conv_dw_b, v_conv_ln_g, v_conv_ln_b, v_lru_conv_b, v_lru_wa, v_lru_ba, v_lru_wx, v_lru_bx,
             v_lru_lambda, v_final_g]
    rep_keys = ["norm_g", "conv_dw_b", "conv_ln_g", "conv_ln_b", "lru_conv_b", "lru_wa", "lru_ba", "lru_wx", "lru_bx",
                "lru_lambda", "final_g"]
    grads = {"w_in": grad_w_in, "w_out": grad_w_out, "conv_dw_w": grad_dw, "lru_conv_w": grad_w4}
    for i, key in enumerate(rep_keys):
        grads[key] = rep_grads[i]
        upd[key] = adam_nd(rep_w[i], rep_grads[i], rep_m[i], rep_v[i], "adamw_" + key)

    order = ["norm_g", "w_in", "conv_dw_w", "conv_dw_b", "conv_ln_g", "conv_ln_b", "lru_conv_w", "lru_conv_b", "lru_wa",
             "lru_ba", "lru_wx", "lru_bx", "lru_lambda", "w_out", "final_g"]
    return (loss, grad_x, *[grads[k] for k in order], *[upd[k][0] for k in order], *[upd[k][1] for k in order],
            *[upd[k][2] for k in order])
```

```python
import functools

import jax
import jax.numpy as jnp
from jax import lax
from jax.experimental import pallas as pl
from jax.experimental.pallas import tpu as pltpu

F32 = jnp.float32
BF16 = jnp.bfloat16
MESH = pl.DeviceIdType.MESH
AXES = ("x", "y", "c")
N_DEV = 8

D_MODEL = 2048
D_BR = 1024
D_IN = 5 * D_BR
SHARD_IN = D_IN // N_DEV
SHARD_OUT = D_MODEL // N_DEV
KW = 31
KW4 = 4
HEADS = 8
HD = 128
LRU_C = 8.0
RMS_EPS = 1e-6
LN_EPS = 1e-5
SUBLANES = 8
HALO = 32
HALO4 = 8
ROW_CHUNK = 16

ADAM_LR = 0.001
ADAM_B1 = 0.9
ADAM_B2 = 0.999
ADAM_EPS = 1e-08
ADAM_WD = 0.01
ADAM_STEP = 10

VMEM_LIMIT = 60 * 1024 * 1024

ANY = pl.BlockSpec(memory_space=pl.ANY)


def _cparams(n_grid):
    return pltpu.CompilerParams(dimension_semantics=("arbitrary",) * n_grid, vmem_limit_bytes=VMEM_LIMIT)


def _resident(block_shape, index_map):
    return pl.BlockSpec(block_shape, index_map, pipeline_mode=pl.Buffered(1))


def _sig(x):
    return 0.5 * jnp.tanh(0.5 * x) + 0.5


def _dsilu(z, sz):
    return sz * (1.0 + z * (1.0 - sz))


def _expm1(x):
    small = jnp.abs(x) < 0.01
    series = x * (1.0 + x * (0.5 + x * (1.0 / 6.0 + x * (1.0 / 24.0))))
    return jnp.where(small, series, jnp.exp(x) - 1.0)


def _log_sigmoid(x):
    e = jnp.exp(-jnp.abs(x))
    l1p = jnp.where(e < 0.01, e * (1.0 - e * (0.5 - e * (1.0 / 3.0))), jnp.log(1.0 + e))
    return jnp.minimum(x, 0.0) - l1p


def _colsum(x):
    return jnp.sum(x, axis=0, keepdims=True)


def _my_pos():
    return lax.axis_index("x"), lax.axis_index("y"), lax.axis_index("c")


class _GatherJob:
    def __init__(self, shards):
        self.arrays = list(shards)
        nt = self.nt = len(self.arrays)
        self.in_specs = [ANY] * nt
        self.out_shape = [jax.ShapeDtypeStruct((N_DEV,) + s.shape, s.dtype) for s in self.arrays]
        self.out_specs = [ANY] * nt
        self.scratch = [pltpu.SemaphoreType.DMA((nt, 7)), pltpu.SemaphoreType.DMA((nt, 7)),
                        pltpu.SemaphoreType.DMA((nt,))]

    def _plan(self, srcs, outs, scr):
        send_sems, recv_sems, local_sems = scr
        x, y, c = _my_pos()
        me, sibling = (x, y, c), (x, y, 1 - c)
        chips = [(1 - x, y), (x, 1 - y), (1 - x, 1 - y)]

        def slot(p):
            return 4 * p[0] + 2 * p[1] + p[2]

        def copy(t, k, block, to, own=False):
            dst = outs[t].at[slot(block)]
            return pltpu.make_async_remote_copy(
                src_ref=srcs[t] if own else dst, dst_ref=dst, send_sem=send_sems.at[t, k], recv_sem=recv_sems.at[t, k],
                device_id=to, device_id_type=MESH)

        mine = [pltpu.make_async_copy(srcs[t], outs[t].at[slot(me)], local_sems.at[t]) for t in range(self.nt)]
        first = []
        for t in range(self.nt):
            first.append(copy(t, 0, me, sibling, own=True))
            first += [copy(t, 1 + j, me, (*chip, c), own=True) for j, chip in enumerate(chips)]
        return me, sibling, chips, c, copy, mine, first

    def start(self, srcs, outs, scr):
        _, _, _, _, _, mine, first = self._plan(srcs, outs, scr)
        for cp in mine + first:
            cp.start()

    def finish(self, srcs, outs, scr):
        me, sibling, chips, c, copy, mine, first = self._plan(srcs, outs, scr)
        passed = []
        for j, chip in enumerate(chips):
            for t in range(self.nt):
                copy(t, 1 + j, (*chip, c), me).wait_recv()
                fwd = copy(t, 4 + j, (*chip, c), sibling)
                fwd.start()
                passed.append(fwd)
        for t in range(self.nt):
            copy(t, 0, sibling, me).wait_recv()
            for j, chip in enumerate(chips):
                copy(t, 4 + j, (*chip, 1 - c), me).wait_recv()
        for cp in first + passed:
            cp.wait_send()
        for cp in mine:
            cp.wait()


class _ExchangeJob:
    def __init__(self, srcs, axis):
        self.arrays = list(srcs)
        self.axis = axis
        nt = self.nt = len(self.arrays)
        self.half = [s.shape[0] // 2 for s in self.arrays]
        self.in_specs = [ANY] * nt
        self.out_shape = [jax.ShapeDtypeStruct((h,) + s.shape[1:], s.dtype) for h, s in zip(self.half, self.arrays)]
        self.out_specs = [ANY] * nt
        self.scratch = [pltpu.SemaphoreType.DMA((nt,)), pltpu.SemaphoreType.DMA((nt,))]

    def _copies(self, srcs, outs, scr):
        send_sems, recv_sems = scr
        pos = list(_my_pos())
        me = pos[self.axis]
        pos[self.axis] = 1 - me
        return [pltpu.make_async_remote_copy(
            src_ref=srcs[t].at[pl.ds((1 - me) * self.half[t], self.half[t])], dst_ref=outs[t],
            send_sem=send_sems.at[t], recv_sem=recv_sems.at[t], device_id=tuple(pos), device_id_type=MESH)
            for t in range(self.nt)]

    def start(self, srcs, outs, scr):
        for cp in self._copies(srcs, outs, scr):
            cp.start()

    def finish(self, srcs, outs, scr):
        for cp in self._copies(srcs, outs, scr):
            cp.wait()


def _run_job(job, name):
    def body(*refs):
        ins, outs, scr = refs[:job.nt], refs[job.nt:2 * job.nt], refs[2 * job.nt:]
        job.start(ins, outs, scr)
        job.finish(ins, outs, scr)

    return pl.pallas_call(body, name=name, out_shape=job.out_shape, in_specs=job.in_specs, out_specs=job.out_specs,
                          scratch_shapes=job.scratch)(*job.arrays)


def _hosted_call(body, *, name, grid, in_specs, out_specs, out_shape, scratch_shapes, args, job):
    n_in, n_out, n_scr = len(in_specs), len(out_specs), len(scratch_shapes)
    if job is None:
        outs = pl.pallas_call(body, name=name, grid=grid, in_specs=in_specs, out_specs=out_specs, out_shape=out_shape,
                              scratch_shapes=scratch_shapes, compiler_params=_cparams(len(grid)))(*args)
        return list(outs), None
    nt = job.nt

    def full_body(*refs):
        own_in, job_in = refs[:n_in], refs[n_in:n_in + nt]
        base = n_in + nt
        own_out, job_out = refs[base:base + n_out], refs[base + n_out:base + n_out + nt]
        base += n_out + nt
        own_scr, job_scr = refs[base:base + n_scr], refs[base + n_scr:]
        ids = [pl.program_id(a) for a in range(len(grid))]
        is_first = functools.reduce(jnp.logical_and, [i == 0 for i in ids])
        is_last = functools.reduce(jnp.logical_and, [i == g - 1 for i, g in zip(ids, grid)])

        @pl.when(is_first)
        def _():
            job.start(job_in, job_out, job_scr)

        body(*own_in, *own_out, *own_scr)

        @pl.when(is_last)
        def _():
            job.finish(job_in, job_out, job_scr)

    outs = pl.pallas_call(
        full_body, name=name, grid=grid, in_specs=list(in_specs) + job.in_specs,
        out_specs=list(out_specs) + job.out_specs, out_shape=list(out_shape) + job.out_shape,
        scratch_shapes=list(scratch_shapes) + job.scratch, compiler_params=_cparams(len(grid)))(*args, *job.arrays)
    return list(outs[:n_out]), list(outs[n_out:])


def _cast_bf16(x, name):
    nl, r, c = x.shape
    tr = min(r, 512)

    def body(x_ref, o_ref):
        o_ref[...] = x_ref[...].astype(BF16)

    spec = pl.BlockSpec((None, tr, c), lambda l, i: (l, i, 0))
    return pl.pallas_call(
        body, name=name, grid=(nl, r // tr), in_specs=[spec], out_specs=spec,
        out_shape=jax.ShapeDtypeStruct(x.shape, BF16), compiler_params=_cparams(2))(x)


def _w_in_rows(w_all):
    def body(i_ref, o_ref):
        o_ref[...] = i_ref[...]

    return pl.pallas_call(
        body, name="w_in_rows", grid=(N_DEV,),
        in_specs=[pl.BlockSpec((None, D_MODEL, SHARD_IN), lambda j: (j, 0, 0))],
        out_specs=pl.BlockSpec((D_MODEL, SHARD_IN), lambda j: (0, j)),
        out_shape=jax.ShapeDtypeStruct((D_MODEL, D_IN), BF16), compiler_params=_cparams(1))(w_all)


def _in_proj(h, w_full, job):
    s = h.shape[0]
    tm = min(s, 2048)
    tn = 2 * SHARD_IN

    def body(h_ref, w_ref, u_ref):
        u_ref[...] = jnp.dot(h_ref[...], w_ref[...], preferred_element_type=F32)

    own, extra = _hosted_call(
        body, name="in_proj", grid=(s // tm, D_IN // tn),
        in_specs=[pl.BlockSpec((tm, D_MODEL), lambda i, j: (i, 0)),
                  pl.BlockSpec((D_MODEL, tn), lambda i, j: (0, j))],
        out_specs=[pl.BlockSpec((tm, tn), lambda i, j: (i, j))],
        out_shape=[jax.ShapeDtypeStruct((s, D_IN), F32)],
        scratch_shapes=[], args=(h, w_full), job=job)
    return own[0], extra


def _shift_copies(cs_ref, buf):
    n = buf.shape[0]
    cs_ref[0] = buf
    for sft in range(1, 8):
        cs_ref[sft] = pltpu.roll(buf, n - sft, 0)


def _spread_taps(wb_ref, w_ref):
    for k in range(KW):
        wb_ref[k] = jnp.broadcast_to(w_ref[k:k + 1, :], (SUBLANES, D_BR))


def _conv_taps(cs_ref, wb_ref, q_ref, t_rows, offs):
    groups = ROW_CHUNK // SUBLANES

    def chunk(r, carry):
        r0 = pl.multiple_of(r * ROW_CHUNK, ROW_CHUNK)
        accs = [jnp.zeros((SUBLANES, D_BR), F32) for _ in range(groups)]
        for k, off in enumerate(offs):
            wv = wb_ref[k]
            ahead = cs_ref[off % 8, pl.ds(r0 + (off // 8) * 8, ROW_CHUNK), :]
            accs = [acc + wv * ahead[SUBLANES * g:SUBLANES * (g + 1)] for g, acc in enumerate(accs)]
        q_ref[pl.ds(r0, ROW_CHUNK), :] = jnp.concatenate(accs, axis=0)
        return carry

    lax.fori_loop(0, t_rows // ROW_CHUNK, chunk, 0)


def _scan_fwd(a, b, h_in):
    t_rows = a.shape[0]
    row8 = lax.broadcasted_iota(jnp.int32, a.shape, 0) & (SUBLANES - 1)
    d = 1
    while d < SUBLANES:
        keep = row8 >= d
        a_s = jnp.where(keep, pltpu.roll(a, d, 0), 1.0)
        b_s = jnp.where(keep, pltpu.roll(b, d, 0), 0.0)
        b = a * b_s + b
        a = a * a_s
        d *= 2
    carry = h_in
    groups = []
    for grp in range(t_rows // SUBLANES):
        rows = slice(grp * SUBLANES, (grp + 1) * SUBLANES)
        h_g = b[rows] + a[rows] * carry
        groups.append(h_g)
        carry = h_g[SUBLANES - 1:SUBLANES]
    return jnp.concatenate(groups, axis=0)


def _scan_rev(a, b, g_in):
    t_rows = a.shape[0]
    row8 = lax.broadcasted_iota(jnp.int32, a.shape, 0) & (SUBLANES - 1)
    d = 1
    while d < SUBLANES:
        keep = row8 < SUBLANES - d
        a_s = jnp.where(keep, pltpu.roll(a, t_rows - d, 0), 1.0)
        b_s = jnp.where(keep, pltpu.roll(b, t_rows - d, 0), 0.0)
        b = a * b_s + b
        a = a * a_s
        d *= 2
    carry = g_in
    groups = []
    for grp in reversed(range(t_rows // SUBLANES)):
        rows = slice(grp * SUBLANES, (grp + 1) * SUBLANES)
        g_g = b[rows] + a[rows] * carry
        groups.append(g_g)
        carry = g_g[0:1]
    return jnp.concatenate(groups[::-1], axis=0)


def _heads_matmul(x_bf, w_ref):
    return jnp.concatenate(
        [jnp.dot(x_bf[:, h * HD:(h + 1) * HD], w_ref[h], preferred_element_type=F32) for h in range(HEADS)], axis=1)


def _heads_matmul_t(d_bf, w_ref):
    return jnp.concatenate(
        [lax.dot_general(d_bf[:, h * HD:(h + 1) * HD], w_ref[h], (((1,), (1,)), ((), ())), preferred_element_type=F32)
         for h in range(HEADS)], axis=1)


def _layer_norm_swish(q, ln_g, ln_b, with_swish=True):
    mu = jnp.mean(q, axis=-1, keepdims=True)
    xc = q - mu
    var = jnp.mean(xc * xc, axis=-1, keepdims=True)
    rstd = lax.rsqrt(var + LN_EPS)
    n = xc * rstd
    p = n * ln_g + ln_b
    return n, rstd, p, (_sig(p) if with_swish else None)


def _lru_gates(xl, xbuf_ref, w4_ref, b4, wa_ref, ba, wx_ref, bx, lam, t_rows):
    xbuf_ref[pl.ds(HALO4, t_rows), :] = xl
    xb = xbuf_ref[...]
    n = t_rows + HALO4
    xc = b4 + w4_ref[3:4, :] * xl
    for k in range(KW4 - 1):
        off = HALO4 - (KW4 - 1) + k
        xc = xc + w4_ref[k:k + 1, :] * pltpu.roll(xb, n - off, 0)[0:t_rows]
    xc_bf = xc.astype(BF16)
    r = _sig(_heads_matmul(xc_bf, wa_ref) + ba)
    ig = _sig(_heads_matmul(xc_bf, wx_ref) + bx)
    log_s = _log_sigmoid(lam)
    la = LRU_C * r * log_s
    a = jnp.exp(la)
    m = jnp.sqrt(-_expm1(2.0 * la))
    return xb, xc, xc_bf, r, ig, log_s, a, m


def _mixer_specs(layer):
    row1 = lambda i: (layer, 0, 0)
    heads = lambda i: (layer, 0, 0, 0)
    return [pl.BlockSpec((None, KW, D_BR), row1),
            pl.BlockSpec((None, 1, D_BR), row1), pl.BlockSpec((None, 1, D_BR), row1),
            pl.BlockSpec((None, 1, D_BR), row1),
            pl.BlockSpec((None, KW4, D_BR), row1), pl.BlockSpec((None, 1, D_BR), row1),
            pl.BlockSpec((None, HEADS, HD, HD), heads), pl.BlockSpec((None, 1, D_BR), row1),
            pl.BlockSpec((None, HEADS, HD, HD), heads), pl.BlockSpec((None, 1, D_BR), row1),
            pl.BlockSpec((None, 1, D_BR), row1)]


def _mixer_fwd(u, params, layer, t_rows):
    s = u.shape[0]
    nb = s // t_rows

    def body(u_ref, w31_ref, b31_ref, lng_ref, lnb_ref, w4_ref, b4_ref, wa_ref, ba_ref, wx_ref, bx_ref, lam_ref,
             y_ref, q_out_ref, h_out_ref, hb_ref, xc_out_ref, r_out_ref, ig_out_ref, a_out_ref, m_out_ref,
             sg_out_ref, sp_out_ref, szc_out_ref, szl_out_ref,
             cbuf_ref, cs_ref, xbuf_ref, hcar_ref, wb_ref):
        @pl.when(pl.program_id(0) == 0)
        def _():
            _spread_taps(wb_ref, w31_ref)
            cbuf_ref[pl.ds(0, HALO), :] = jnp.zeros((HALO, D_BR), F32)
            xbuf_ref[pl.ds(0, HALO4), :] = jnp.zeros((HALO4, D_BR), F32)
            hcar_ref[...] = jnp.zeros_like(hcar_ref)

        zc = u_ref[:, 2 * D_BR:3 * D_BR]
        sg = _sig(u_ref[:, D_BR:2 * D_BR])
        sg_out_ref[...] = sg
        c = u_ref[:, 0:D_BR] * sg
        cbuf_ref[pl.ds(HALO, t_rows), :] = c
        _shift_copies(cs_ref, cbuf_ref[...])
        _conv_taps(cs_ref, wb_ref, q_out_ref, t_rows, [HALO - (KW - 1) + k for k in range(KW)])
        cbuf_ref[pl.ds(0, HALO), :] = c[t_rows - HALO:t_rows]
        q = q_out_ref[...] + b31_ref[...]
        q_out_ref[...] = q
        _, _, p, sp = _layer_norm_swish(q, lng_ref[...], lnb_ref[...])
        sp_out_ref[...] = sp
        szc = _sig(zc)
        szc_out_ref[...] = szc
        y_ref[:, 0:D_BR] = (p * sp * (zc * szc)).astype(BF16)

        xl = u_ref[:, 3 * D_BR:4 * D_BR]
        zl = u_ref[:, 4 * D_BR:5 * D_BR]
        _, xc, _, r, ig, _, a, m = _lru_gates(xl, xbuf_ref, w4_ref, b4_ref[...], wa_ref, ba_ref[...], wx_ref,
                                              bx_ref[...], lam_ref[...], t_rows)
        xbuf_ref[pl.ds(0, HALO4), :] = xl[t_rows - HALO4:t_rows]
        xc_out_ref[...] = xc
        r_out_ref[...] = r
        ig_out_ref[...] = ig
        a_out_ref[...] = a
        m_out_ref[...] = m
        h_in = hcar_ref[...]
        hb_ref[...] = h_in
        h = _scan_fwd(a, m * (ig * xc), h_in)
        h_out_ref[...] = h
        hcar_ref[...] = h_out_ref[pl.ds(t_rows - 1, 1), :]
        szl = _sig(zl)
        szl_out_ref[...] = szl
        y_ref[:, D_BR:2 * D_BR] = (h * (zl * szl)).astype(BF16)

    blk = pl.BlockSpec((t_rows, D_BR), lambda i: (i, 0))
    return pl.pallas_call(
        body, name="mixer_fwd", grid=(nb,),
        in_specs=[pl.BlockSpec((t_rows, D_IN), lambda i: (i, 0))] + _mixer_specs(layer),
        out_specs=[pl.BlockSpec((t_rows, 2 * D_BR), lambda i: (i, 0)), blk, blk,
                   pl.BlockSpec((None, 1, D_BR), lambda i: (i, 0, 0))] + [blk] * 9,
        out_shape=[jax.ShapeDtypeStruct((s, 2 * D_BR), BF16), jax.ShapeDtypeStruct((s, D_BR), F32),
                   jax.ShapeDtypeStruct((s, D_BR), F32), jax.ShapeDtypeStruct((nb, 1, D_BR), F32)]
        + [jax.ShapeDtypeStruct((s, D_BR), F32)] * 9,
        scratch_shapes=[pltpu.VMEM((t_rows + HALO, D_BR), F32), pltpu.VMEM((8, t_rows + HALO, D_BR), F32),
                        pltpu.VMEM((t_rows + HALO4, D_BR), F32), pltpu.VMEM((1, D_BR), F32),
                        pltpu.VMEM((KW, SUBLANES, D_BR), F32)],
        compiler_params=_cparams(1))(u, *params)


def _rms_bf16(xf, g):
    rstd = lax.rsqrt(jnp.mean(xf * xf, axis=-1, keepdims=True) + RMS_EPS)
    return (xf * rstd * g).astype(BF16)


def _first_norm(x, g_row, job):
    s = x.shape[0]
    tm = min(s, 512)

    def body(x_ref, g_ref, h_ref):
        h_ref[...] = _rms_bf16(x_ref[...], g_ref[...])

    blk = pl.BlockSpec((tm, D_MODEL), lambda i: (i, 0))
    own, extra = _hosted_call(
        body, name="first_norm", grid=(s // tm,), in_specs=[blk, pl.BlockSpec((1, D_MODEL), lambda i: (0, 0))],
        out_specs=[blk], out_shape=[jax.ShapeDtypeStruct((s, D_MODEL), BF16)], scratch_shapes=[], args=(x, g_row),
        job=job)
    return own[0], extra


def _out_proj(x, y, wo, g_next):
    s = x.shape[0]
    tm = min(s, 512)

    def body(x_ref, y_ref, w_ref, g_ref, o_ref, h_ref):
        x_next = x_ref[...] + jnp.dot(y_ref[...], w_ref[...], preferred_element_type=F32)
        o_ref[...] = x_next
        h_ref[...] = _rms_bf16(x_next, g_ref[...])

    blk = pl.BlockSpec((tm, D_MODEL), lambda i: (i, 0))
    return pl.pallas_call(
        body, name="out_proj", grid=(s // tm,),
        in_specs=[blk, blk, _resident((D_MODEL, D_MODEL), lambda i: (0, 0)),
                  pl.BlockSpec((1, D_MODEL), lambda i: (0, 0))],
        out_specs=[blk, blk],
        out_shape=[jax.ShapeDtypeStruct((s, D_MODEL), F32), jax.ShapeDtypeStruct((s, D_MODEL), BF16)],
        compiler_params=_cparams(1))(x, y, wo, g_next)


def _out_proj_loss_head(x, y, wo, g_row, target):
    s = x.shape[0]
    tm = min(s, 512)

    def body(x_ref, y_ref, w_ref, g_ref, t_ref, loss_ref, dx_ref, dg_ref):
        @pl.when(pl.program_id(0) == 0)
        def _():
            loss_ref[...] = jnp.zeros_like(loss_ref)
            dg_ref[...] = jnp.zeros_like(dg_ref)

        xf = x_ref[...] + jnp.dot(y_ref[...], w_ref[...], preferred_element_type=F32)
        g = g_ref[...]
        rstd = lax.rsqrt(jnp.mean(xf * xf, axis=-1, keepdims=True) + RMS_EPS)
        n = xf * rstd
        err = n * g - t_ref[...]
        loss_ref[...] += 0.5 * jnp.sum(jnp.mean(err * err, axis=-1, keepdims=True))
        dy = err * (1.0 / D_MODEL)
        dg_ref[...] += _colsum(dy * n)
        dn = dy * g
        dx_ref[...] = rstd * (dn - n * jnp.mean(dn * n, axis=-1, keepdims=True))

    blk = pl.BlockSpec((tm, D_MODEL), lambda i: (i, 0))
    return pl.pallas_call(
        body, name="out_proj_loss_head", grid=(s // tm,),
        in_specs=[blk, blk, _resident((D_MODEL, D_MODEL), lambda i: (0, 0)),
                  pl.BlockSpec((1, D_MODEL), lambda i: (0, 0)), blk],
        out_specs=[pl.BlockSpec((8, 128), lambda i: (0, 0)), blk, pl.BlockSpec((1, D_MODEL), lambda i: (0, 0))],
        out_shape=[jax.ShapeDtypeStruct((8, 128), F32), jax.ShapeDtypeStruct((s, D_MODEL), F32),
                   jax.ShapeDtypeStruct((1, D_MODEL), F32)],
        compiler_params=_cparams(1))(x, y, wo, g_row, target)


def _out_proj_bwd_x(dx, wo, job):
    s = dx.shape[0]
    tm = min(s, 512)

    def body(dx_ref, w_ref, dy_ref, dxb_ref):
        dxb = dx_ref[...].astype(BF16)
        dxb_ref[...] = dxb
        dy_ref[...] = lax.dot_general(dxb, w_ref[...], (((1,), (1,)), ((), ())), preferred_element_type=F32)

    blk = pl.BlockSpec((tm, D_MODEL), lambda i: (i, 0))
    own, extra = _hosted_call(
        body, name="out_proj_bwd_x", grid=(s // tm,),
        in_specs=[blk, _resident((D_MODEL, D_MODEL), lambda i: (0, 0))],
        out_specs=[blk, blk],
        out_shape=[jax.ShapeDtypeStruct((s, D_MODEL), F32), jax.ShapeDtypeStruct((s, D_MODEL), BF16)],
        scratch_shapes=[], args=(dx, wo), job=job)
    return own[0], own[1], extra


def _w_in_grad(h, du, job):
    s = h.shape[0]
    tk = min(s, 1024)
    nk = s // tk

    def body(h_ref, du_ref, o_ref, acc_ref):
        k = pl.program_id(1)

        @pl.when(k == 0)
        def _():
            acc_ref[...] = jnp.zeros_like(acc_ref)

        acc_ref[...] += lax.dot_general(h_ref[...], du_ref[...], (((0,), (0,)), ((), ())), preferred_element_type=F32)

        @pl.when(k == nk - 1)
        def _():
            o_ref[0] = acc_ref[:, 0:SHARD_IN].astype(BF16)
            o_ref[1] = acc_ref[:, SHARD_IN:2 * SHARD_IN].astype(BF16)

    own, extra = _hosted_call(
        body, name="w_in_grad", grid=(N_DEV // 2, nk),
        in_specs=[pl.BlockSpec((tk, D_MODEL), lambda q, k: (k, 0)),
                  pl.BlockSpec((tk, 2 * SHARD_IN), lambda q, k: (k, q))],
        out_specs=[pl.BlockSpec((2, None, D_MODEL, SHARD_IN), lambda q, k: (0, q, 0, 0))],
        out_shape=[jax.ShapeDtypeStruct((2, N_DEV // 2, D_MODEL, SHARD_IN), BF16)],
        scratch_shapes=[pltpu.VMEM((D_MODEL, 2 * SHARD_IN), F32)], args=(h, du), job=job)
    return jnp.reshape(own[0], (N_DEV, D_MODEL, SHARD_IN)), extra


def _w_out_grad(y, dxb, job):
    s = y.shape[0]
    tk = min(s, 1024)
    nk = s // tk
    tn = 512

    def body(y_ref, dx_ref, o_ref, acc_ref):
        k = pl.program_id(1)

        @pl.when(k == 0)
        def _():
            acc_ref[...] = jnp.zeros_like(acc_ref)

        acc_ref[...] += lax.dot_general(y_ref[...], dx_ref[...], (((0,), (0,)), ((), ())), preferred_element_type=F32)

        @pl.when(k == nk - 1)
        def _():
            for j in range(N_DEV):
                slot = (j % 2) * 4 + j // 2
                o_ref[slot] = acc_ref[pl.ds(j * SHARD_OUT, SHARD_OUT), :].astype(BF16)

    own, extra = _hosted_call(
        body, name="w_out_grad", grid=(D_MODEL // tn, nk),
        in_specs=[pl.BlockSpec((tk, D_MODEL), lambda n, k: (k, 0)),
                  pl.BlockSpec((tk, tn), lambda n, k: (k, n))],
        out_specs=[pl.BlockSpec((N_DEV, SHARD_OUT, tn), lambda n, k: (0, 0, n))],
        out_shape=[jax.ShapeDtypeStruct((N_DEV, SHARD_OUT, D_MODEL), BF16)],
        scratch_shapes=[pltpu.VMEM((D_MODEL, tn), F32)], args=(y, dxb), job=job)
    return own[0], extra


def _in_proj_bwd_x(du, w_full, x, g_row, dx_next, job):
    s = x.shape[0]
    tm = min(s, 256)

    def body(du_ref, w_ref, x_ref, g_ref, dxn_ref, dx_ref, dg_ref):
        @pl.when(pl.program_id(0) == 0)
        def _():
            dg_ref[...] = jnp.zeros_like(dg_ref)

        dh = lax.dot_general(w_ref[...], du_ref[...], (((1,), (1,)), ((), ())), preferred_element_type=F32).T
        xf = x_ref[...]
        rstd = lax.rsqrt(jnp.mean(xf * xf, axis=-1, keepdims=True) + RMS_EPS)
        n = xf * rstd
        dg_ref[...] += _colsum(dh * n)
        dn = dh * g_ref[...]
        dx_ref[...] = dxn_ref[...] + rstd * (dn - n * jnp.mean(dn * n, axis=-1, keepdims=True))

    blk = pl.BlockSpec((tm, D_MODEL), lambda i: (i, 0))
    own, extra = _hosted_call(
        body, name="in_proj_bwd_x", grid=(s // tm,),
        in_specs=[pl.BlockSpec((tm, D_IN), lambda i: (i, 0)), _resident((D_MODEL, D_IN), lambda i: (0, 0)),
                  blk, pl.BlockSpec((1, D_MODEL), lambda i: (0, 0)), blk],
        out_specs=[blk, pl.BlockSpec((1, D_MODEL), lambda i: (0, 0))],
        out_shape=[jax.ShapeDtypeStruct((s, D_MODEL), F32), jax.ShapeDtypeStruct((1, D_MODEL), F32)],
        scratch_shapes=[], args=(du, w_full, x, g_row, dx_next), job=job)
    return own[0], own[1], extra


PG_B31, PG_LNG, PG_LNB, PG_B4, PG_BA, PG_BX, PG_LAM, PG_W4 = 0, 1, 2, 3, 4, 5, 6, 8
PG_ROWS = 16


def _mixer_bwd(u, kept, dy, hb, params, layer, t_rows):
    s = u.shape[0]
    nb = s // t_rows

    def body(u_ref, q_ref, h_ref, xc_ref, r_ref, ig_ref, a_ref, m_ref, sg_ref, sp_ref, szc_ref, szl_ref, dy_ref, hb_ref,
             w31_ref, b31_ref, lng_ref, lnb_ref, w4_ref, b4_ref, wa_ref, ba_ref, wx_ref, bx_ref, lam_ref,
             du_ref, pg_ref, dw31_ref, dwa_ref, dwx_ref,
             cbuf_ref, cs_ref, dc_ref, dqbuf_ref, dwacc_ref, dxcbuf_ref, acar_ref, gcar_ref, wb_ref):
        step = pl.program_id(0)

        @pl.when(step == 0)
        def _():
            _spread_taps(wb_ref, w31_ref)
            pg_ref[...] = jnp.zeros_like(pg_ref)
            dwa_ref[...] = jnp.zeros_like(dwa_ref)
            dwx_ref[...] = jnp.zeros_like(dwx_ref)
            dwacc_ref[...] = jnp.zeros_like(dwacc_ref)
            dqbuf_ref[pl.ds(t_rows, HALO), :] = jnp.zeros((HALO, D_BR), F32)
            dxcbuf_ref[pl.ds(t_rows, HALO4), :] = jnp.zeros((HALO4, D_BR), F32)
            acar_ref[...] = jnp.zeros_like(acar_ref)
            gcar_ref[...] = jnp.zeros_like(gcar_ref)

        def add_row(r, val):
            pg_ref[r:r + 1, :] += val

        v = u_ref[:, 0:D_BR]
        g = u_ref[:, D_BR:2 * D_BR]
        zc = u_ref[:, 2 * D_BR:3 * D_BR]
        dyc = dy_ref[:, 0:D_BR]
        sg = sg_ref[...]
        cbuf_ref[...] = v * sg
        ln_gv = lng_ref[...]
        n, rstd, p, _ = _layer_norm_swish(q_ref[...], ln_gv, lnb_ref[...], with_swish=False)
        sp = sp_ref[...]
        sz = szc_ref[...]
        du_ref[:, 2 * D_BR:3 * D_BR] = (dyc * (p * sp) * _dsilu(zc, sz)).astype(BF16)
        dp = dyc * (zc * sz) * _dsilu(p, sp)
        add_row(PG_LNG, _colsum(dp * n))
        add_row(PG_LNB, _colsum(dp))
        dn = dp * ln_gv
        dq = rstd * (dn - jnp.mean(dn, axis=-1, keepdims=True) - n * jnp.mean(dn * n, axis=-1, keepdims=True))
        add_row(PG_B31, _colsum(dq))
        dqbuf_ref[pl.ds(0, t_rows), :] = dq

        _shift_copies(cs_ref, dqbuf_ref[...])

        groups = ROW_CHUNK // SUBLANES

        def conv_chunk(r, carry):
            r0 = pl.multiple_of(r * ROW_CHUNK, ROW_CHUNK)
            cc = cbuf_ref[pl.ds(r0, ROW_CHUNK), :]
            accs = [jnp.zeros((SUBLANES, D_BR), F32) for _ in range(groups)]
            for k in range(KW):
                off = KW - 1 - k
                wv = wb_ref[k]
                ahead = cs_ref[off % 8, pl.ds(r0 + (off // 8) * 8, ROW_CHUNK), :]
                accs = [acc + wv * ahead[SUBLANES * g:SUBLANES * (g + 1)] for g, acc in enumerate(accs)]
                prod = cc * ahead
                part = prod[0:SUBLANES]
                for g in range(1, groups):
                    part = part + prod[SUBLANES * g:SUBLANES * (g + 1)]
                dwacc_ref[k] += part
            dc_ref[pl.ds(r0, ROW_CHUNK), :] = jnp.concatenate(accs, axis=0)
            return carry

        lax.fori_loop(0, t_rows // ROW_CHUNK, conv_chunk, 0)
        dqbuf_ref[pl.ds(t_rows, HALO), :] = dq[0:HALO]
        dc = dc_ref[...]
        du_ref[:, 0:D_BR] = (dc * sg).astype(BF16)
        du_ref[:, D_BR:2 * D_BR] = (dc * v * sg * (1.0 - sg)).astype(BF16)

        xl = u_ref[:, 3 * D_BR:4 * D_BR]
        zl = u_ref[:, 4 * D_BR:5 * D_BR]
        dyl = dy_ref[:, D_BR:2 * D_BR]
        xc = xc_ref[...]
        xc_bf = xc.astype(BF16)
        r = r_ref[...]
        ig = ig_ref[...]
        a = a_ref[...]
        m = m_ref[...]
        log_s = _log_sigmoid(lam_ref[...])
        row = lax.broadcasted_iota(jnp.int32, (t_rows, D_BR), 0)
        h = h_ref[...]
        h_prev = jnp.where(row >= 1, pltpu.roll(h, 1, 0), hb_ref[...])
        szl = szl_ref[...]
        du_ref[:, 4 * D_BR:5 * D_BR] = (dyl * h * _dsilu(zl, szl)).astype(BF16)
        a_next = jnp.where(row < t_rows - 1, pltpu.roll(a, t_rows - 1, 0), acar_ref[...])
        gs = _scan_rev(a_next, dyl * (zl * szl), gcar_ref[...])
        dc_ref[...] = gs
        gcar_ref[...] = dc_ref[pl.ds(0, 1), :]
        dc_ref[...] = a
        acar_ref[...] = dc_ref[pl.ds(0, 1), :]

        dm = gs * ig * xc
        di = gs * m * xc
        dla = gs * h_prev * a - dm * (a * a / m)
        add_row(PG_LAM, _colsum(dla * r) * LRU_C)
        dra = dla * (LRU_C * log_s) * r * (1.0 - r)
        dia = di * ig * (1.0 - ig)
        add_row(PG_BA, _colsum(dra))
        add_row(PG_BX, _colsum(dia))
        dra_bf = dra.astype(BF16)
        dia_bf = dia.astype(BF16)
        for hd in range(HEADS):
            sl = slice(hd * HD, (hd + 1) * HD)
            dwa_ref[hd] += lax.dot_general(xc_bf[:, sl], dra_bf[:, sl], (((0,), (0,)), ((), ())),
                                           preferred_element_type=F32)
            dwx_ref[hd] += lax.dot_general(xc_bf[:, sl], dia_bf[:, sl], (((0,), (0,)), ((), ())),
                                           preferred_element_type=F32)
        dxc = gs * m * ig + _heads_matmul_t(dra_bf, wa_ref) + _heads_matmul_t(dia_bf, wx_ref)
        add_row(PG_B4, _colsum(dxc))
        n4 = t_rows + HALO4
        add_row(PG_W4 + 3, _colsum(dxc * xl))
        dxcbuf_ref[pl.ds(0, t_rows), :] = dxc
        db = dxcbuf_ref[...]
        dxl = w4_ref[3:4, :] * dxc
        for k in range(KW4 - 1):
            ahead = pltpu.roll(db, n4 - (KW4 - 1 - k), 0)[0:t_rows]
            dxl = dxl + w4_ref[k:k + 1, :] * ahead
            add_row(PG_W4 + k, _colsum(xl * ahead))
        dxcbuf_ref[pl.ds(t_rows, HALO4), :] = dxc[0:HALO4]
        du_ref[:, 3 * D_BR:4 * D_BR] = dxl.astype(BF16)

        @pl.when(step == nb - 1)
        def _():
            pg_ref[PG_LAM:PG_LAM + 1, :] = pg_ref[PG_LAM:PG_LAM + 1, :] * _sig(-lam_ref[...])
            dw31_ref[...] = jnp.zeros_like(dw31_ref)
            for k in range(KW):
                dw31_ref[k:k + 1, :] = jnp.sum(dwacc_ref[k], axis=0, keepdims=True)

    const2 = lambda i: (0, 0)
    const3 = lambda i: (0, 0, 0)
    rev = lambda i: (nb - 1 - i, 0)
    return pl.pallas_call(
        body, name="mixer_bwd", grid=(nb,),
        in_specs=[pl.BlockSpec((t_rows, D_IN), rev)] + [pl.BlockSpec((t_rows, D_BR), rev)] * len(kept) + [
                  pl.BlockSpec((t_rows, 2 * D_BR), rev),
                  pl.BlockSpec((None, 1, D_BR), lambda i: (nb - 1 - i, 0, 0))] + _mixer_specs(layer),
        out_specs=[pl.BlockSpec((t_rows, D_IN), rev),
                   pl.BlockSpec((PG_ROWS, D_BR), const2), pl.BlockSpec((32, D_BR), const2),
                   pl.BlockSpec((HEADS, HD, HD), const3), pl.BlockSpec((HEADS, HD, HD), const3)],
        out_shape=[jax.ShapeDtypeStruct((s, D_IN), BF16), jax.ShapeDtypeStruct((PG_ROWS, D_BR), F32),
                   jax.ShapeDtypeStruct((32, D_BR), F32), jax.ShapeDtypeStruct((HEADS, HD, HD), F32),
                   jax.ShapeDtypeStruct((HEADS, HD, HD), F32)],
        scratch_shapes=[pltpu.VMEM((t_rows, D_BR), F32), pltpu.VMEM((8, t_rows + HALO, D_BR), F32),
                        pltpu.VMEM((t_rows, D_BR), F32), pltpu.VMEM((t_rows + HALO, D_BR), F32),
                        pltpu.VMEM((KW, 8, D_BR), F32),
                        pltpu.VMEM((t_rows + HALO4, D_BR), F32), pltpu.VMEM((1, D_BR), F32),
                        pltpu.VMEM((1, D_BR), F32), pltpu.VMEM((KW, SUBLANES, D_BR), F32)],
        compiler_params=_cparams(1))(u, *kept, dy, hb, *params)


def _add_kept_half(src, recv, keep, out_dtype, name):
    h, r, c = recv.shape
    tr = min(r, 1024)

    def body(keep_ref, s_ref, r_ref, o_ref):
        o_ref[...] = (s_ref[...].astype(F32) + r_ref[...].astype(F32)).astype(out_dtype)

    grid_spec = pltpu.PrefetchScalarGridSpec(
        num_scalar_prefetch=1, grid=(h, r // tr),
        in_specs=[pl.BlockSpec((None, tr, c), lambda b, i, kp: (kp[0] * h + b, i, 0)),
                  pl.BlockSpec((None, tr, c), lambda b, i, kp: (b, i, 0))],
        out_specs=pl.BlockSpec((None, tr, c), lambda b, i, kp: (b, i, 0)))
    return pl.pallas_call(
        body, name=name, grid_spec=grid_spec, out_shape=jax.ShapeDtypeStruct(recv.shape, out_dtype),
        compiler_params=_cparams(2))(keep, src, recv)


class _PendingReduce:
    STAGE_AXES = (2, 0, 1)

    def __init__(self, bufs):
        self.bufs = list(bufs)
        self.stage = 0

    def job(self):
        return _ExchangeJob(self.bufs, self.STAGE_AXES[self.stage])

    def absorb(self, recvs):
        me = _my_pos()[self.STAGE_AXES[self.stage]]
        keep = jnp.reshape(me, (1,)).astype(jnp.int32)
        last = self.stage == 2
        self.bufs = [_add_kept_half(b, r, keep, F32 if last else BF16, f"rs_add{self.stage}_{t}")
                     for t, (b, r) in enumerate(zip(self.bufs, recvs))]
        self.stage += 1

    def finish_alone(self):
        while self.stage < 3:
            job = self.job()
            self.absorb(_run_job(job, f"rs_exchange{self.stage}"))
        return [b[0] for b in self.bufs]


def _all_reduce_small(pa, pb, job):
    nt = job.nt

    def body(*refs):
        pa_ref, pb_ref = refs[:2]
        job_in = refs[2:2 + nt]
        oa_ref, ob_ref = refs[2 + nt:4 + nt]
        job_out = refs[4 + nt:4 + 2 * nt]
        ra0, ra1, ra2, sb0, sb1, sb2, rb0, rb1, rb2, send_sems, recv_sems = refs[4 + 2 * nt:15 + 2 * nt]
        job_scr = refs[15 + 2 * nt:]
        job.start(job_in, job_out, job_scr)
        x, y, c = _my_pos()
        peers = [(x, y, 1 - c), (1 - x, y, c), (x, 1 - y, c)]
        oa_ref[...] = pa_ref[...]
        ob_ref[...] = pb_ref[...]
        for k, (peer, ra, sb, rb) in enumerate(zip(peers, (ra0, ra1, ra2), (sb0, sb1, sb2), (rb0, rb1, rb2))):
            sb[...] = ob_ref[...].astype(BF16)
            copies = [pltpu.make_async_remote_copy(
                src_ref=src, dst_ref=dst, send_sem=send_sems.at[t, k], recv_sem=recv_sems.at[t, k],
                device_id=peer, device_id_type=MESH) for t, (src, dst) in enumerate(((oa_ref, ra), (sb, rb)))]
            for cp in copies:
                cp.start()
            for cp in copies:
                cp.wait()
            oa_ref[...] = oa_ref[...] + ra[...]
            ob_ref[...] = sb[...].astype(F32) + rb[...].astype(F32)
        job.finish(job_in, job_out, job_scr)

    vm = pl.BlockSpec(memory_space=pltpu.VMEM)
    outs = pl.pallas_call(
        body, name="small_all_reduce",
        out_shape=[jax.ShapeDtypeStruct(pa.shape, F32), jax.ShapeDtypeStruct(pb.shape, F32)] + job.out_shape,
        in_specs=[vm, vm] + job.in_specs, out_specs=[vm, vm] + job.out_specs,
        scratch_shapes=[pltpu.VMEM(pa.shape, F32)] * 3 + [pltpu.VMEM(pb.shape, BF16)] * 6
        + [pltpu.SemaphoreType.DMA((2, 3)), pltpu.SemaphoreType.DMA((2, 3))] + job.scratch,
        compiler_params=pltpu.CompilerParams(vmem_limit_bytes=VMEM_LIMIT))(pa, pb, *job.arrays)
    return outs[0], outs[1], list(outs[2:])


def _adamw(w, g, m, v, name):
    r, c = w.shape
    tr = r
    for cand in (512, 256, 128, 64, 32, 16, 8):
        if r % cand == 0 and cand * c * 4 <= (2 << 20):
            tr = cand
            break

    def body(w_ref, g_ref, m_ref, v_ref, d_ref, mo_ref, vo_ref):
        gv = g_ref[...]
        m_new = ADAM_B1 * m_ref[...] + (1.0 - ADAM_B1) * gv
        v_new = ADAM_B2 * v_ref[...] + (1.0 - ADAM_B2) * (gv * gv)
        m_hat = m_new / (1.0 - ADAM_B1 ** ADAM_STEP)
        v_hat = v_new / (1.0 - ADAM_B2 ** ADAM_STEP)
        d_ref[...] = -ADAM_LR * (m_hat / (jnp.sqrt(v_hat) + ADAM_EPS) + ADAM_WD * w_ref[...])
        mo_ref[...] = m_new
        vo_ref[...] = v_new

    spec = pl.BlockSpec((tr, c), lambda i: (i, 0))
    shape = jax.ShapeDtypeStruct((r, c), F32)
    return pl.pallas_call(
        body, name=name, grid=(r // tr,), in_specs=[spec] * 4, out_specs=[spec] * 3, out_shape=[shape] * 3,
        compiler_params=_cparams(1))(w, g, m, v)


def _pack_rows(parts):
    flat = jnp.concatenate([jnp.reshape(p, (-1, D_BR)) for p in parts], axis=0)
    pad = (-flat.shape[0]) % 64
    if pad:
        flat = jnp.concatenate([flat, jnp.zeros((pad, D_BR), F32)], axis=0)
    return flat


def _unpack_rows(flat, shapes):
    out, r0 = [], 0
    for shp in shapes:
        n = 1
        for d in shp:
            n *= d
        rows = n // D_BR
        out.append(jnp.reshape(flat[r0:r0 + rows], shp))
        r0 += rows
    return out


def kernel(x, norm_g, w_in, conv_dw_w, conv_dw_b, conv_ln_g, conv_ln_b, lru_conv_w, lru_conv_b, lru_wa, lru_ba, lru_wx, lru_bx, lru_lambda, w_out, final_g, loss_target, m_norm_g, m_w_in, m_conv_dw_w, m_conv_dw_b, m_conv_ln_g, m_conv_ln_b, m_lru_conv_w, m_lru_conv_b, m_lru_wa, m_lru_ba, m_lru_wx, m_lru_bx, m_lru_lambda, m_w_out, m_final_g, v_norm_g, v_w_in, v_conv_dw_w, v_conv_dw_b, v_conv_ln_g, v_conv_ln_b, v_lru_conv_w, v_lru_conv_b, v_lru_wa, v_lru_ba, v_lru_wx, v_lru_bx, v_lru_lambda, v_w_out, v_final_g):
    n_layers = norm_g.shape[0]
    s = x.shape[1]
    t_rows = min(s, 128)
    xs = jnp.reshape(x, (s, D_MODEL))
    target = jnp.reshape(loss_target, (s, D_MODEL))
    dev = 4 * lax.axis_index("x") + 2 * lax.axis_index("y") + lax.axis_index("c")

    w_in_bf = _cast_bf16(w_in, "cast_w_in")
    w_out_bf = _cast_bf16(w_out, "cast_w_out")
    h, (w_in_l, w31_all, w4_all) = _first_norm(xs, norm_g[0:1], _GatherJob([w_in_bf[0], conv_dw_w, lru_conv_w]))
    w_out_l = None
    w31_full = jnp.reshape(jnp.transpose(w31_all, (1, 2, 0, 3)), (n_layers, KW, D_BR))
    w4_full = jnp.reshape(jnp.transpose(w4_all, (1, 2, 0, 3)), (n_layers, KW4, D_BR))
    row3 = lambda p: jnp.reshape(p, (n_layers, 1, -1))
    mixer_params = (w31_full, row3(conv_dw_b), row3(conv_ln_g), row3(conv_ln_b), w4_full, row3(lru_conv_b),
                    lru_wa.astype(BF16), row3(lru_ba), lru_wx.astype(BF16), row3(lru_bx), row3(lru_lambda))

    saved = []
    act = xs
    for l in range(n_layers):
        wanted = [w_out_bf[0]] if l == 0 else []
        if l + 1 < n_layers:
            wanted += [w_in_bf[l + 1], w_out_bf[l + 1]]
        w_full = _w_in_rows(w_in_l)
        u, gathered = _in_proj(h, w_full, _GatherJob(wanted) if wanted else None)
        if l == 0:
            w_out_l, gathered = gathered[0], gathered[1:]
        y, q_sv, h_sv, hb, *gates_sv = _mixer_fwd(u, mixer_params, l, t_rows)
        kept = [q_sv, h_sv, *gates_sv]
        wo = jnp.reshape(w_out_l, (D_MODEL, D_MODEL))
        saved.append((act, h, u, y, kept, hb, w_full, wo))
        if l + 1 < n_layers:
            act, h = _out_proj(act, y, wo, norm_g[l + 1:l + 2])
            w_in_l, w_out_l = gathered
    loss_part, dx, d_final_g = _out_proj_loss_head(act, y, wo, jnp.reshape(final_g, (1, D_MODEL)), target)
    loss = lax.psum(loss_part[0, 0], AXES)

    pending = None
    reduced_big = [None] * n_layers
    small = [None] * n_layers
    for l in reversed(range(n_layers)):
        x_l, h, u, y, kept, hb, w_full, wo = saved[l]
        dy, dxb, recvs = _out_proj_bwd_x(dx, wo, pending.job() if pending else None)
        if pending:
            pending.absorb(recvs)
            reduced_big[l + 1] = [b[0] for b in pending.bufs]
        g_out, _ = _w_out_grad(y, dxb, None)
        du, pg, dw31, dwa, dwx = _mixer_bwd(u, kept, dy, hb, mixer_params, l, t_rows)
        g_in, _ = _w_in_grad(h, du, None)
        pending = _PendingReduce([g_in, g_out])
        pending.absorb(_run_job(pending.job(), "rs_exchange_c"))
        dx, d_norm, recvs = _in_proj_bwd_x(du, w_full, x_l, norm_g[l:l + 1], dx, pending.job())
        pending.absorb(recvs)
        small[l] = (d_norm, pg, dw31, dwa, dwx)
    grad_x = jnp.reshape(dx, x.shape)

    stack = lambda f: jnp.stack([f(small[l]) for l in range(n_layers)])
    pg_all = stack(lambda t: t[1])
    rep_parts = [
        (stack(lambda t: t[0][0]), norm_g.shape), (pg_all[:, PG_B31], conv_dw_b.shape),
        (pg_all[:, PG_LNG], conv_ln_g.shape), (pg_all[:, PG_LNB], conv_ln_b.shape),
        (pg_all[:, PG_B4], lru_conv_b.shape), (stack(lambda t: t[3]), lru_wa.shape), (pg_all[:, PG_BA], lru_ba.shape),
        (stack(lambda t: t[4]), lru_wx.shape), (pg_all[:, PG_BX], lru_bx.shape), (pg_all[:, PG_LAM], lru_lambda.shape),
        (d_final_g, final_g.shape)]
    shard_parts = [(stack(lambda t: t[2][0:KW]), (n_layers, KW, D_BR)),
                   (pg_all[:, PG_W4:PG_W4 + KW4], (n_layers, KW4, D_BR))]
    gate_w = (5, 7)
    f32_parts = [p for i, p in enumerate(rep_parts) if i not in gate_w] + shard_parts
    bf16_parts = [rep_parts[i] for i in gate_w]
    red_a, red_b, recvs = _all_reduce_small(
        _pack_rows([p for p, _ in f32_parts]), _pack_rows([p for p, _ in bf16_parts]), pending.job())
    pending.absorb(recvs)
    reduced_big[0] = [b[0] for b in pending.bufs]
    grad_w_in = jnp.stack([r[0] for r in reduced_big])
    grad_w_out = jnp.stack([r[1] for r in reduced_big])
    red_a = _unpack_rows(red_a, [shp for _, shp in f32_parts])
    red_b = _unpack_rows(red_b, [shp for _, shp in bf16_parts])
    rep_grads = red_a[:len(rep_parts) - len(gate_w)]
    for i, g in zip(gate_w, red_b):
        rep_grads.insert(i, g)
    grad_dw = lax.dynamic_slice_in_dim(red_a[-2], dev * HD, HD, axis=2)
    grad_w4 = lax.dynamic_slice_in_dim(red_a[-1], dev * HD, HD, axis=2)

    def adam_nd(w, g, m, v, name):
        two_d = (-1, w.shape[-1])
        outs = _adamw(*(jnp.reshape(t, two_d) for t in (w, g, m, v)), name)
        return [jnp.reshape(o, w.shape) for o in outs]

    upd = {}
    upd["w_in"] = adam_nd(w_in, grad_w_in, m_w_in, v_w_in, "adamw_w_in")
    upd["w_out"] = adam_nd(w_out, grad_w_out, m_w_out, v_w_out, "adamw_w_out")
    upd["conv_dw_w"] = adam_nd(conv_dw_w, grad_dw, m_conv_dw_w, v_conv_dw_w, "adamw_conv_dw_w")
    upd["lru_conv_w"] = adam_nd(lru_conv_w, grad_w4, m_lru_conv_w, v_lru_conv_w, "adamw_lru_conv_w")
    rep_w = [norm_g, conv_dw_b, conv_ln_g, conv_ln_b, lru_conv_b, lru_wa, lru_ba, lru_wx, lru_bx, lru_lambda, final_g]
    rep_m = [m_norm_g, m_conv_dw_b, m_conv_ln_g, m_conv_ln_b, m_lru_conv_b, m_lru_wa, m_lru_ba, m_lru_wx, m_lru_bx,
             m_lru_lambda, m_final_g]
    rep_v = [v_norm_g, v_conv_dw_b, v_conv_ln_g, v_conv_ln_b, v_lru_conv_b, v_lru_wa, v_lru_ba, v_lru_wx, v_lru_bx,
             v_lru_lambda, v_final_g]
    rep_keys = ["norm_g", "conv_dw_b", "conv_ln_g", "conv_ln_b", "lru_conv_b", "lru_wa", "lru_ba", "lru_wx", "lru_bx",
                "lru_lambda", "final_g"]
    grads = {"w_in": grad_w_in, "w_out": grad_w_out, "conv_dw_w": grad_dw, "lru_conv_w": grad_w4}
    for i, key in enumerate(rep_keys):
        grads[key] = rep_grads[i]
        upd[key] = adam_nd(rep_w[i], rep_grads[i], rep_m[i], rep_v[i], "adamw_" + key)

    order = ["norm_g", "w_in", "conv_dw_w", "conv_dw_b", "conv_ln_g", "conv_ln_b", "lru_conv_w", "lru_conv_b", "lru_wa",
             "lru_ba", "lru_wx", "lru_bx", "lru_lambda", "w_out", "final_g"]
    return (loss, grad_x, *[grads[k] for k in order], *[upd[k][0] for k in order], *[upd[k][1] for k in order],
            *[upd[k][2] for k in order])
```

```python
import functools

import jax
import jax.numpy as jnp
from jax import lax
from jax.experimental import pallas as pl
from jax.experimental.pallas import tpu as pltpu

F32 = jnp.float32
BF16 = jnp.bfloat16
MESH = pl.DeviceIdType.MESH
AXES = ("x", "y", "c")
N_DEV = 8

D_MODEL = 2048
D_BR = 1024
D_IN = 5 * D_BR
SHARD_IN = D_IN // N_DEV
SHARD_OUT = D_MODEL // N_DEV
KW = 31
KW4 = 4
HEADS = 8
HD = 128
LRU_C = 8.0
RMS_EPS = 1e-6
LN_EPS = 1e-5
SUBLANES = 8
HALO = 32
HALO4 = 8
ROW_CHUNK = 16

ADAM_LR = 0.001
ADAM_B1 = 0.9
ADAM_B2 = 0.999
ADAM_EPS = 1e-08
ADAM_WD = 0.01
ADAM_STEP = 10

VMEM_LIMIT = 60 * 1024 * 1024

ANY = pl.BlockSpec(memory_space=pl.ANY)


def _cparams(n_grid):
    return pltpu.CompilerParams(dimension_semantics=("arbitrary",) * n_grid, vmem_limit_bytes=VMEM_LIMIT)


def _resident(block_shape, index_map):
    return pl.BlockSpec(block_shape, index_map, pipeline_mode=pl.Buffered(1))


def _sig(x):
    return 0.5 * jnp.tanh(0.5 * x) + 0.5


def _dsilu(z, sz):
    return sz * (1.0 + z * (1.0 - sz))


def _expm1(x):
    small = jnp.abs(x) < 0.01
    series = x * (1.0 + x * (0.5 + x * (1.0 / 6.0 + x * (1.0 / 24.0))))
    return jnp.where(small, series, jnp.exp(x) - 1.0)


def _log_sigmoid(x):
    e = jnp.exp(-jnp.abs(x))
    l1p = jnp.where(e < 0.01, e * (1.0 - e * (0.5 - e * (1.0 / 3.0))), jnp.log(1.0 + e))
    return jnp.minimum(x, 0.0) - l1p


def _colsum(x):
    return jnp.sum(x, axis=0, keepdims=True)


def _my_pos():
    return lax.axis_index("x"), lax.axis_index("y"), lax.axis_index("c")


class _GatherJob:
    def __init__(self, shards):
        self.arrays = list(shards)
        nt = self.nt = len(self.arrays)
        self.in_specs = [ANY] * nt
        self.out_shape = [jax.ShapeDtypeStruct((N_DEV,) + s.shape, s.dtype) for s in self.arrays]
        self.out_specs = [ANY] * nt
        self.scratch = [pltpu.SemaphoreType.DMA((nt, 7)), pltpu.SemaphoreType.DMA((nt, 7)),
                        pltpu.SemaphoreType.DMA((nt,))]

    def _plan(self, srcs, outs, scr):
        send_sems, recv_sems, local_sems = scr
        x, y, c = _my_pos()
        me, sibling = (x, y, c), (x, y, 1 - c)
        chips = [(1 - x, y), (x, 1 - y), (1 - x, 1 - y)]

        def slot(p):
            return 4 * p[0] + 2 * p[1] + p[2]

        def copy(t, k, block, to, own=False):
            dst = outs[t].at[slot(block)]
            return pltpu.make_async_remote_copy(
                src_ref=srcs[t] if own else dst, dst_ref=dst, send_sem=send_sems.at[t, k], recv_sem=recv_sems.at[t, k],
                device_id=to, device_id_type=MESH)

        mine = [pltpu.make_async_copy(srcs[t], outs[t].at[slot(me)], local_sems.at[t]) for t in range(self.nt)]
        first = []
        for t in range(self.nt):
            first.append(copy(t, 0, me, sibling, own=True))
            first += [copy(t, 1 + j, me, (*chip, c), own=True) for j, chip in enumerate(chips)]
        return me, sibling, chips, c, copy, mine, first

    def start(self, srcs, outs, scr):
        _, _, _, _, _, mine, first = self._plan(srcs, outs, scr)
        for cp in mine + first:
            cp.start()

    def finish(self, srcs, outs, scr):
        me, sibling, chips, c, copy, mine, first = self._plan(srcs, outs, scr)
        passed = []
        for j, chip in enumerate(chips):
            for t in range(self.nt):
                copy(t, 1 + j, (*chip, c), me).wait_recv()
                fwd = copy(t, 4 + j, (*chip, c), sibling)
                fwd.start()
                passed.append(fwd)
        for t in range(self.nt):
            copy(t, 0, sibling, me).wait_recv()
            for j, chip in enumerate(chips):
                copy(t, 4 + j, (*chip, 1 - c), me).wait_recv()
        for cp in first + passed:
            cp.wait_send()
        for cp in mine:
            cp.wait()


class _ExchangeJob:
    def __init__(self, srcs, axis):
        self.arrays = list(srcs)
        self.axis = axis
        nt = self.nt = len(self.arrays)
        self.half = [s.shape[0] // 2 for s in self.arrays]
        self.in_specs = [ANY] * nt
        self.out_shape = [jax.ShapeDtypeStruct((h,) + s.shape[1:], s.dtype) for h, s in zip(self.half, self.arrays)]
        self.out_specs = [ANY] * nt
        self.scratch = [pltpu.SemaphoreType.DMA((nt,)), pltpu.SemaphoreType.DMA((nt,))]

    def _copies(self, srcs, outs, scr):
        send_sems, recv_sems = scr
        pos = list(_my_pos())
        me = pos[self.axis]
        pos[self.axis] = 1 - me
        return [pltpu.make_async_remote_copy(
            src_ref=srcs[t].at[pl.ds((1 - me) * self.half[t], self.half[t])], dst_ref=outs[t],
            send_sem=send_sems.at[t], recv_sem=recv_sems.at[t], device_id=tuple(pos), device_id_type=MESH)
            for t in range(self.nt)]

    def start(self, srcs, outs, scr):
        for cp in self._copies(srcs, outs, scr):
            cp.start()

    def finish(self, srcs, outs, scr):
        for cp in self._copies(srcs, outs, scr):
            cp.wait()


def _run_job(job, name):
    def body(*refs):
        ins, outs, scr = refs[:job.nt], refs[job.nt:2 * job.nt], refs[2 * job.nt:]
        job.start(ins, outs, scr)
        job.finish(ins, outs, scr)

    return pl.pallas_call(body, name=name, out_shape=job.out_shape, in_specs=job.in_specs, out_specs=job.out_specs,
                          scratch_shapes=job.scratch)(*job.arrays)


def _hosted_call(body, *, name, grid, in_specs, out_specs, out_shape, scratch_shapes, args, job):
    n_in, n_out, n_scr = len(in_specs), len(out_specs), len(scratch_shapes)
    if job is None:
        outs = pl.pallas_call(body, name=name, grid=grid, in_specs=in_specs, out_specs=out_specs, out_shape=out_shape,
                              scratch_shapes=scratch_shapes, compiler_params=_cparams(len(grid)))(*args)
        return list(outs), None
    nt = job.nt

    def full_body(*refs):
        own_in, job_in = refs[:n_in], refs[n_in:n_in + nt]
        base = n_in + nt
        own_out, job_out = refs[base:base + n_out], refs[base + n_out:base + n_out + nt]
        base += n_out + nt
        own_scr, job_scr = refs[base:base + n_scr], refs[base + n_scr:]
        ids = [pl.program_id(a) for a in range(len(grid))]
        is_first = functools.reduce(jnp.logical_and, [i == 0 for i in ids])
        is_last = functools.reduce(jnp.logical_and, [i == g - 1 for i, g in zip(ids, grid)])

        @pl.when(is_first)
        def _():
            job.start(job_in, job_out, job_scr)

        body(*own_in, *own_out, *own_scr)

        @pl.when(is_last)
        def _():
            job.finish(job_in, job_out, job_scr)

    outs = pl.pallas_call(
        full_body, name=name, grid=grid, in_specs=list(in_specs) + job.in_specs,
        out_specs=list(out_specs) + job.out_specs, out_shape=list(out_shape) + job.out_shape,
        scratch_shapes=list(scratch_shapes) + job.scratch, compiler_params=_cparams(len(grid)))(*args, *job.arrays)
    return list(outs[:n_out]), list(outs[n_out:])


def _cast_bf16(x, name):
    nl, r, c = x.shape
    tr = min(r, 512)

    def body(x_ref, o_ref):
        o_ref[...] = x_ref[...].astype(BF16)

    spec = pl.BlockSpec((None, tr, c), lambda l, i: (l, i, 0))
    return pl.pallas_call(
        body, name=name, grid=(nl, r // tr), in_specs=[spec], out_specs=spec,
        out_shape=jax.ShapeDtypeStruct(x.shape, BF16), compiler_params=_cparams(2))(x)


def _w_in_rows(w_all):
    def body(i_ref, o_ref):
        o_ref[...] = i_ref[...]

    return pl.pallas_call(
        body, name="w_in_rows", grid=(N_DEV,),
        in_specs=[pl.BlockSpec((None, D_MODEL, SHARD_IN), lambda j: (j, 0, 0))],
        out_specs=pl.BlockSpec((D_MODEL, SHARD_IN), lambda j: (0, j)),
        out_shape=jax.ShapeDtypeStruct((D_MODEL, D_IN), BF16), compiler_params=_cparams(1))(w_all)


def _in_proj(h, w_full, job):
    s = h.shape[0]
    tm = min(s, 2048)
    tn = 2 * SHARD_IN

    def body(h_ref, w_ref, u_ref):
        u_ref[...] = jnp.dot(h_ref[...], w_ref[...], preferred_element_type=F32)

    own, extra = _hosted_call(
        body, name="in_proj", grid=(s // tm, D_IN // tn),
        in_specs=[pl.BlockSpec((tm, D_MODEL), lambda i, j: (i, 0)),
                  pl.BlockSpec((D_MODEL, tn), lambda i, j: (0, j))],
        out_specs=[pl.BlockSpec((tm, tn), lambda i, j: (i, j))],
        out_shape=[jax.ShapeDtypeStruct((s, D_IN), F32)],
        scratch_shapes=[], args=(h, w_full), job=job)
    return own[0], extra


def _shift_copies(cs_ref, buf):
    n = buf.shape[0]
    cs_ref[0] = buf
    for sft in range(1, 8):
        cs_ref[sft] = pltpu.roll(buf, n - sft, 0)


def _spread_taps(wb_ref, w_ref):
    for k in range(KW):
        wb_ref[k] = jnp.broadcast_to(w_ref[k:k + 1, :], (SUBLANES, D_BR))


def _conv_taps(cs_ref, wb_ref, q_ref, t_rows, offs):
    groups = ROW_CHUNK // SUBLANES

    def chunk(r, carry):
        r0 = pl.multiple_of(r * ROW_CHUNK, ROW_CHUNK)
        accs = [jnp.zeros((SUBLANES, D_BR), F32) for _ in range(groups)]
        for k, off in enumerate(offs):
            wv = wb_ref[k]
            ahead = cs_ref[off % 8, pl.ds(r0 + (off // 8) * 8, ROW_CHUNK), :]
            accs = [acc + wv * ahead[SUBLANES * g:SUBLANES * (g + 1)] for g, acc in enumerate(accs)]
        q_ref[pl.ds(r0, ROW_CHUNK), :] = jnp.concatenate(accs, axis=0)
        return carry

    lax.fori_loop(0, t_rows // ROW_CHUNK, chunk, 0)


def _scan_fwd(a, b, h_in):
    t_rows = a.shape[0]
    row8 = lax.broadcasted_iota(jnp.int32, a.shape, 0) & (SUBLANES - 1)
    d = 1
    while d < SUBLANES:
        keep = row8 >= d
        a_s = jnp.where(keep, pltpu.roll(a, d, 0), 1.0)
        b_s = jnp.where(keep, pltpu.roll(b, d, 0), 0.0)
        b = a * b_s + b
        a = a * a_s
        d *= 2
    carry = h_in
    groups = []
    for grp in range(t_rows // SUBLANES):
        rows = slice(grp * SUBLANES, (grp + 1) * SUBLANES)
        h_g = b[rows] + a[rows] * carry
        groups.append(h_g)
        carry = h_g[SUBLANES - 1:SUBLANES]
    return jnp.concatenate(groups, axis=0)


def _scan_rev(a, b, g_in):
    t_rows = a.shape[0]
    row8 = lax.broadcasted_iota(jnp.int32, a.shape, 0) & (SUBLANES - 1)
    d = 1
    while d < SUBLANES:
        keep = row8 < SUBLANES - d
        a_s = jnp.where(keep, pltpu.roll(a, t_rows - d, 0), 1.0)
        b_s = jnp.where(keep, pltpu.roll(b, t_rows - d, 0), 0.0)
        b = a * b_s + b
        a = a * a_s
        d *= 2
    carry = g_in
    groups = []
    for grp in reversed(range(t_rows // SUBLANES)):
        rows = slice(grp * SUBLANES, (grp + 1) * SUBLANES)
        g_g = b[rows] + a[rows] * carry
        groups.append(g_g)
        carry = g_g[0:1]
    return jnp.concatenate(groups[::-1], axis=0)


def _heads_matmul(x_bf, w_ref):
    return jnp.concatenate(
        [jnp.dot(x_bf[:, h * HD:(h + 1) * HD], w_ref[h], preferred_element_type=F32) for h in range(HEADS)], axis=1)


def _heads_matmul_t(d_bf, w_ref):
    return jnp.concatenate(
        [lax.dot_general(d_bf[:, h * HD:(h + 1) * HD], w_ref[h], (((1,), (1,)), ((), ())), preferred_element_type=F32)
         for h in range(HEADS)], axis=1)


def _layer_norm_swish(q, ln_g, ln_b, with_swish=True):
    mu = jnp.mean(q, axis=-1, keepdims=True)
    xc = q - mu
    var = jnp.mean(xc * xc, axis=-1, keepdims=True)
    rstd = lax.rsqrt(var + LN_EPS)
    n = xc * rstd
    p = n * ln_g + ln_b
    return n, rstd, p, (_sig(p) if with_swish else None)


def _lru_gates(xl, xbuf_ref, w4_ref, b4, wa_ref, ba, wx_ref, bx, lam, t_rows):
    xbuf_ref[pl.ds(HALO4, t_rows), :] = xl
    xb = xbuf_ref[...]
    n = t_rows + HALO4
    xc = b4 + w4_ref[3:4, :] * xl
    for k in range(KW4 - 1):
        off = HALO4 - (KW4 - 1) + k
        xc = xc + w4_ref[k:k + 1, :] * pltpu.roll(xb, n - off, 0)[0:t_rows]
    xc_bf = xc.astype(BF16)
    r = _sig(_heads_matmul(xc_bf, wa_ref) + ba)
    ig = _sig(_heads_matmul(xc_bf, wx_ref) + bx)
    log_s = _log_sigmoid(lam)
    la = LRU_C * r * log_s
    a = jnp.exp(la)
    m = jnp.sqrt(-_expm1(2.0 * la))
    return xb, xc, xc_bf, r, ig, log_s, a, m


def _mixer_specs(layer):
    row1 = lambda i: (layer, 0, 0)
    heads = lambda i: (layer, 0, 0, 0)
    return [pl.BlockSpec((None, KW, D_BR), row1),
            pl.BlockSpec((None, 1, D_BR), row1), pl.BlockSpec((None, 1, D_BR), row1),
            pl.BlockSpec((None, 1, D_BR), row1),
            pl.BlockSpec((None, KW4, D_BR), row1), pl.BlockSpec((None, 1, D_BR), row1),
            pl.BlockSpec((None, HEADS, HD, HD), heads), pl.BlockSpec((None, 1, D_BR), row1),
            pl.BlockSpec((None, HEADS, HD, HD), heads), pl.BlockSpec((None, 1, D_BR), row1),
            pl.BlockSpec((None, 1, D_BR), row1)]


def _mixer_fwd(u, params, layer, t_rows):
    s = u.shape[0]
    nb = s // t_rows

    def body(u_ref, w31_ref, b31_ref, lng_ref, lnb_ref, w4_ref, b4_ref, wa_ref, ba_ref, wx_ref, bx_ref, lam_ref,
             y_ref, q_out_ref, h_out_ref, hb_ref, xc_out_ref, r_out_ref, ig_out_ref, a_out_ref, m_out_ref,
             sg_out_ref, sp_out_ref, szc_out_ref, szl_out_ref,
             cbuf_ref, cs_ref, xbuf_ref, hcar_ref, wb_ref):
        @pl.when(pl.program_id(0) == 0)
        def _():
            _spread_taps(wb_ref, w31_ref)
            cbuf_ref[pl.ds(0, HALO), :] = jnp.zeros((HALO, D_BR), F32)
            xbuf_ref[pl.ds(0, HALO4), :] = jnp.zeros((HALO4, D_BR), F32)
            hcar_ref[...] = jnp.zeros_like(hcar_ref)

        zc = u_ref[:, 2 * D_BR:3 * D_BR]
        sg = _sig(u_ref[:, D_BR:2 * D_BR])
        sg_out_ref[...] = sg
        c = u_ref[:, 0:D_BR] * sg
        cbuf_ref[pl.ds(HALO, t_rows), :] = c
        _shift_copies(cs_ref, cbuf_ref[...])
        _conv_taps(cs_ref, wb_ref, q_out_ref, t_rows, [HALO - (KW - 1) + k for k in range(KW)])
        cbuf_ref[pl.ds(0, HALO), :] = c[t_rows - HALO:t_rows]
        q = q_out_ref[...] + b31_ref[...]
        q_out_ref[...] = q
        _, _, p, sp = _layer_norm_swish(q, lng_ref[...], lnb_ref[...])
        sp_out_ref[...] = sp
        szc = _sig(zc)
        szc_out_ref[...] = szc
        y_ref[:, 0:D_BR] = (p * sp * (zc * szc)).astype(BF16)

        xl = u_ref[:, 3 * D_BR:4 * D_BR]
        zl = u_ref[:, 4 * D_BR:5 * D_BR]
        _, xc, _, r, ig, _, a, m = _lru_gates(xl, xbuf_ref, w4_ref, b4_ref[...], wa_ref, ba_ref[...], wx_ref,
                                              bx_ref[...], lam_ref[...], t_rows)
        xbuf_ref[pl.ds(0, HALO4), :] = xl[t_rows - HALO4:t_rows]
        xc_out_ref[...] = xc
        r_out_ref[...] = r
        ig_out_ref[...] = ig
        a_out_ref[...] = a
        m_out_ref[...] = m
        h_in = hcar_ref[...]
        hb_ref[...] = h_in
        h = _scan_fwd(a, m * (ig * xc), h_in)
        h_out_ref[...] = h
        hcar_ref[...] = h_out_ref[pl.ds(t_rows - 1, 1), :]
        szl = _sig(zl)
        szl_out_ref[...] = szl
        y_ref[:, D_BR:2 * D_BR] = (h * (zl * szl)).astype(BF16)

    blk = pl.BlockSpec((t_rows, D_BR), lambda i: (i, 0))
    return pl.pallas_call(
        body, name="mixer_fwd", grid=(nb,),
        in_specs=[pl.BlockSpec((t_rows, D_IN), lambda i: (i, 0))] + _mixer_specs(layer),
        out_specs=[pl.BlockSpec((t_rows, 2 * D_BR), lambda i: (i, 0)), blk, blk,
                   pl.BlockSpec((None, 1, D_BR), lambda i: (i, 0, 0))] + [blk] * 9,
        out_shape=[jax.ShapeDtypeStruct((s, 2 * D_BR), BF16), jax.ShapeDtypeStruct((s, D_BR), F32),
                   jax.ShapeDtypeStruct((s, D_BR), F32), jax.ShapeDtypeStruct((nb, 1, D_BR), F32)]
        + [jax.ShapeDtypeStruct((s, D_BR), F32)] * 9,
        scratch_shapes=[pltpu.VMEM((t_rows + HALO, D_BR), F32), pltpu.VMEM((8, t_rows + HALO, D_BR), F32),
                        pltpu.VMEM((t_rows + HALO4, D_BR), F32), pltpu.VMEM((1, D_BR), F32),
                        pltpu.VMEM((KW, SUBLANES, D_BR), F32)],
        compiler_params=_cparams(1))(u, *params)


def _rows_times_transposed(a, w):
    return lax.dot_general(w, a, (((1,), (1,)), ((), ())), preferred_element_type=F32).T


def _rms_bf16(xf, g):
    rstd = lax.rsqrt(jnp.mean(xf * xf, axis=-1, keepdims=True) + RMS_EPS)
    return (xf * rstd * g).astype(BF16)


def _first_norm(x, g_row, job):
    s = x.shape[0]
    tm = min(s, 512)

    def body(x_ref, g_ref, h_ref):
        h_ref[...] = _rms_bf16(x_ref[...], g_ref[...])

    blk = pl.BlockSpec((tm, D_MODEL), lambda i: (i, 0))
    own, extra = _hosted_call(
        body, name="first_norm", grid=(s // tm,), in_specs=[blk, pl.BlockSpec((1, D_MODEL), lambda i: (0, 0))],
        out_specs=[blk], out_shape=[jax.ShapeDtypeStruct((s, D_MODEL), BF16)], scratch_shapes=[], args=(x, g_row),
        job=job)
    return own[0], extra


def _out_proj(x, y, wo, g_next):
    s = x.shape[0]
    tm = min(s, 512)

    def body(x_ref, y_ref, w_ref, g_ref, o_ref, h_ref):
        x_next = x_ref[...] + jnp.dot(y_ref[...], w_ref[...], preferred_element_type=F32)
        o_ref[...] = x_next
        h_ref[...] = _rms_bf16(x_next, g_ref[...])

    blk = pl.BlockSpec((tm, D_MODEL), lambda i: (i, 0))
    return pl.pallas_call(
        body, name="out_proj", grid=(s // tm,),
        in_specs=[blk, blk, _resident((D_MODEL, D_MODEL), lambda i: (0, 0)),
                  pl.BlockSpec((1, D_MODEL), lambda i: (0, 0))],
        out_specs=[blk, blk],
        out_shape=[jax.ShapeDtypeStruct((s, D_MODEL), F32), jax.ShapeDtypeStruct((s, D_MODEL), BF16)],
        compiler_params=_cparams(1))(x, y, wo, g_next)


def _out_proj_loss_head(x, y, wo, g_row, target):
    s = x.shape[0]
    tm = min(s, 512)

    def body(x_ref, y_ref, w_ref, g_ref, t_ref, loss_ref, dx_ref, dg_ref):
        @pl.when(pl.program_id(0) == 0)
        def _():
            loss_ref[...] = jnp.zeros_like(loss_ref)
            dg_ref[...] = jnp.zeros_like(dg_ref)

        xf = x_ref[...] + jnp.dot(y_ref[...], w_ref[...], preferred_element_type=F32)
        g = g_ref[...]
        rstd = lax.rsqrt(jnp.mean(xf * xf, axis=-1, keepdims=True) + RMS_EPS)
        n = xf * rstd
        err = n * g - t_ref[...]
        loss_ref[...] += 0.5 * jnp.sum(jnp.mean(err * err, axis=-1, keepdims=True))
        dy = err * (1.0 / D_MODEL)
        dg_ref[...] += _colsum(dy * n)
        dn = dy * g
        dx_ref[...] = rstd * (dn - n * jnp.mean(dn * n, axis=-1, keepdims=True))

    blk = pl.BlockSpec((tm, D_MODEL), lambda i: (i, 0))
    return pl.pallas_call(
        body, name="out_proj_loss_head", grid=(s // tm,),
        in_specs=[blk, blk, _resident((D_MODEL, D_MODEL), lambda i: (0, 0)),
                  pl.BlockSpec((1, D_MODEL), lambda i: (0, 0)), blk],
        out_specs=[pl.BlockSpec((8, 128), lambda i: (0, 0)), blk, pl.BlockSpec((1, D_MODEL), lambda i: (0, 0))],
        out_shape=[jax.ShapeDtypeStruct((8, 128), F32), jax.ShapeDtypeStruct((s, D_MODEL), F32),
                   jax.ShapeDtypeStruct((1, D_MODEL), F32)],
        compiler_params=_cparams(1))(x, y, wo, g_row, target)


def _out_proj_bwd_x(dx, wo, job):
    s = dx.shape[0]
    tm = min(s, 512)

    def body(dx_ref, w_ref, dy_ref, dxb_ref):
        dxb = dx_ref[...].astype(BF16)
        dxb_ref[...] = dxb
        dy_ref[...] = _rows_times_transposed(dxb, w_ref[...])

    blk = pl.BlockSpec((tm, D_MODEL), lambda i: (i, 0))
    own, extra = _hosted_call(
        body, name="out_proj_bwd_x", grid=(s // tm,),
        in_specs=[blk, _resident((D_MODEL, D_MODEL), lambda i: (0, 0))],
        out_specs=[blk, blk],
        out_shape=[jax.ShapeDtypeStruct((s, D_MODEL), F32), jax.ShapeDtypeStruct((s, D_MODEL), BF16)],
        scratch_shapes=[], args=(dx, wo), job=job)
    return own[0], own[1], extra


def _w_in_grad(h, du, job):
    s = h.shape[0]
    tk = min(s, 1024)
    nk = s // tk

    def body(h_ref, du_ref, o_ref, acc_ref):
        k = pl.program_id(1)

        @pl.when(k == 0)
        def _():
            acc_ref[...] = jnp.zeros_like(acc_ref)

        acc_ref[...] += lax.dot_general(h_ref[...], du_ref[...], (((0,), (0,)), ((), ())), preferred_element_type=F32)

        @pl.when(k == nk - 1)
        def _():
            o_ref[0] = acc_ref[:, 0:SHARD_IN].astype(BF16)
            o_ref[1] = acc_ref[:, SHARD_IN:2 * SHARD_IN].astype(BF16)

    own, extra = _hosted_call(
        body, name="w_in_grad", grid=(N_DEV // 2, nk),
        in_specs=[pl.BlockSpec((tk, D_MODEL), lambda q, k: (k, 0)),
                  pl.BlockSpec((tk, 2 * SHARD_IN), lambda q, k: (k, q))],
        out_specs=[pl.BlockSpec((2, None, D_MODEL, SHARD_IN), lambda q, k: (0, q, 0, 0))],
        out_shape=[jax.ShapeDtypeStruct((2, N_DEV // 2, D_MODEL, SHARD_IN), BF16)],
        scratch_shapes=[pltpu.VMEM((D_MODEL, 2 * SHARD_IN), F32)], args=(h, du), job=job)
    return jnp.reshape(own[0], (N_DEV, D_MODEL, SHARD_IN)), extra


def _w_out_grad(y, dxb, job):
    s = y.shape[0]
    tk = min(s, 1024)
    nk = s // tk
    tn = 512

    def body(y_ref, dx_ref, o_ref, acc_ref):
        k = pl.program_id(1)

        @pl.when(k == 0)
        def _():
            acc_ref[...] = jnp.zeros_like(acc_ref)

        acc_ref[...] += lax.dot_general(y_ref[...], dx_ref[...], (((0,), (0,)), ((), ())), preferred_element_type=F32)

        @pl.when(k == nk - 1)
        def _():
            for j in range(N_DEV):
                slot = (j % 2) * 4 + j // 2
                o_ref[slot] = acc_ref[pl.ds(j * SHARD_OUT, SHARD_OUT), :].astype(BF16)

    own, extra = _hosted_call(
        body, name="w_out_grad", grid=(D_MODEL // tn, nk),
        in_specs=[pl.BlockSpec((tk, D_MODEL), lambda n, k: (k, 0)),
                  pl.BlockSpec((tk, tn), lambda n, k: (k, n))],
        out_specs=[pl.BlockSpec((N_DEV, SHARD_OUT, tn), lambda n, k: (0, 0, n))],
        out_shape=[jax.ShapeDtypeStruct((N_DEV, SHARD_OUT, D_MODEL), BF16)],
        scratch_shapes=[pltpu.VMEM((D_MODEL, tn), F32)], args=(y, dxb), job=job)
    return own[0], extra


def _in_proj_bwd_x(du, w_full, x, g_row, dx_next, job):
    s = x.shape[0]
    tm = min(s, 256)

    def body(du_ref, w_ref, x_ref, g_ref, dxn_ref, dx_ref, dg_ref):
        @pl.when(pl.program_id(0) == 0)
        def _():
            dg_ref[...] = jnp.zeros_like(dg_ref)

        dh = _rows_times_transposed(du_ref[...], w_ref[...])
        xf = x_ref[...]
        rstd = lax.rsqrt(jnp.mean(xf * xf, axis=-1, keepdims=True) + RMS_EPS)
        n = xf * rstd
        dg_ref[...] += _colsum(dh * n)
        dn = dh * g_ref[...]
        dx_ref[...] = dxn_ref[...] + rstd * (dn - n * jnp.mean(dn * n, axis=-1, keepdims=True))

    blk = pl.BlockSpec((tm, D_MODEL), lambda i: (i, 0))
    own, extra = _hosted_call(
        body, name="in_proj_bwd_x", grid=(s // tm,),
        in_specs=[pl.BlockSpec((tm, D_IN), lambda i: (i, 0)), _resident((D_MODEL, D_IN), lambda i: (0, 0)),
                  blk, pl.BlockSpec((1, D_MODEL), lambda i: (0, 0)), blk],
        out_specs=[blk, pl.BlockSpec((1, D_MODEL), lambda i: (0, 0))],
        out_shape=[jax.ShapeDtypeStruct((s, D_MODEL), F32), jax.ShapeDtypeStruct((1, D_MODEL), F32)],
        scratch_shapes=[], args=(du, w_full, x, g_row, dx_next), job=job)
    return own[0], own[1], extra


PG_B31, PG_LNG, PG_LNB, PG_B4, PG_BA, PG_BX, PG_LAM, PG_W4 = 0, 1, 2, 3, 4, 5, 6, 8
PG_ROWS = 16


def _mixer_bwd(u, kept, dy, hb, params, layer, t_rows):
    s = u.shape[0]
    nb = s // t_rows

    def body(u_ref, q_ref, h_ref, xc_ref, r_ref, ig_ref, a_ref, m_ref, sg_ref, sp_ref, szc_ref, szl_ref, dy_ref, hb_ref,
             w31_ref, b31_ref, lng_ref, lnb_ref, w4_ref, b4_ref, wa_ref, ba_ref, wx_ref, bx_ref, lam_ref,
             du_ref, pg_ref, dw31_ref, dwa_ref, dwx_ref,
             cbuf_ref, cs_ref, dc_ref, dqbuf_ref, dwacc_ref, dxcbuf_ref, acar_ref, gcar_ref, wb_ref):
        step = pl.program_id(0)

        @pl.when(step == 0)
        def _():
            _spread_taps(wb_ref, w31_ref)
            pg_ref[...] = jnp.zeros_like(pg_ref)
            dwa_ref[...] = jnp.zeros_like(dwa_ref)
            dwx_ref[...] = jnp.zeros_like(dwx_ref)
            dwacc_ref[...] = jnp.zeros_like(dwacc_ref)
            dqbuf_ref[pl.ds(t_rows, HALO), :] = jnp.zeros((HALO, D_BR), F32)
            dxcbuf_ref[pl.ds(t_rows, HALO4), :] = jnp.zeros((HALO4, D_BR), F32)
            acar_ref[...] = jnp.zeros_like(acar_ref)
            gcar_ref[...] = jnp.zeros_like(gcar_ref)

        def add_row(r, val):
            pg_ref[r:r + 1, :] += val

        v = u_ref[:, 0:D_BR]
        g = u_ref[:, D_BR:2 * D_BR]
        zc = u_ref[:, 2 * D_BR:3 * D_BR]
        dyc = dy_ref[:, 0:D_BR]
        sg = sg_ref[...]
        cbuf_ref[...] = v * sg
        ln_gv = lng_ref[...]
        n, rstd, p, _ = _layer_norm_swish(q_ref[...], ln_gv, lnb_ref[...], with_swish=False)
        sp = sp_ref[...]
        sz = szc_ref[...]
        du_ref[:, 2 * D_BR:3 * D_BR] = (dyc * (p * sp) * _dsilu(zc, sz)).astype(BF16)
        dp = dyc * (zc * sz) * _dsilu(p, sp)
        add_row(PG_LNG, _colsum(dp * n))
        add_row(PG_LNB, _colsum(dp))
        dn = dp * ln_gv
        dq = rstd * (dn - jnp.mean(dn, axis=-1, keepdims=True) - n * jnp.mean(dn * n, axis=-1, keepdims=True))
        add_row(PG_B31, _colsum(dq))
        dqbuf_ref[pl.ds(0, t_rows), :] = dq

        _shift_copies(cs_ref, dqbuf_ref[...])

        groups = ROW_CHUNK // SUBLANES

        def conv_chunk(r, carry):
            r0 = pl.multiple_of(r * ROW_CHUNK, ROW_CHUNK)
            cc = cbuf_ref[pl.ds(r0, ROW_CHUNK), :]
            accs = [jnp.zeros((SUBLANES, D_BR), F32) for _ in range(groups)]
            for k in range(KW):
                off = KW - 1 - k
                wv = wb_ref[k]
                ahead = cs_ref[off % 8, pl.ds(r0 + (off // 8) * 8, ROW_CHUNK), :]
                accs = [acc + wv * ahead[SUBLANES * g:SUBLANES * (g + 1)] for g, acc in enumerate(accs)]
                prod = cc * ahead
                part = prod[0:SUBLANES]
                for g in range(1, groups):
                    part = part + prod[SUBLANES * g:SUBLANES * (g + 1)]
                dwacc_ref[k] += part
            dc_ref[pl.ds(r0, ROW_CHUNK), :] = jnp.concatenate(accs, axis=0)
            return carry

        lax.fori_loop(0, t_rows // ROW_CHUNK, conv_chunk, 0)
        dqbuf_ref[pl.ds(t_rows, HALO), :] = dq[0:HALO]
        dc = dc_ref[...]
        du_ref[:, 0:D_BR] = (dc * sg).astype(BF16)
        du_ref[:, D_BR:2 * D_BR] = (dc * v * sg * (1.0 - sg)).astype(BF16)

        xl = u_ref[:, 3 * D_BR:4 * D_BR]
        zl = u_ref[:, 4 * D_BR:5 * D_BR]
        dyl = dy_ref[:, D_BR:2 * D_BR]
        xc = xc_ref[...]
        xc_bf = xc.astype(BF16)
        r = r_ref[...]
        ig = ig_ref[...]
        a = a_ref[...]
        m = m_ref[...]
        log_s = _log_sigmoid(lam_ref[...])
        row = lax.broadcasted_iota(jnp.int32, (t_rows, D_BR), 0)
        h = h_ref[...]
        h_prev = jnp.where(row >= 1, pltpu.roll(h, 1, 0), hb_ref[...])
        szl = szl_ref[...]
        du_ref[:, 4 * D_BR:5 * D_BR] = (dyl * h * _dsilu(zl, szl)).astype(BF16)
        a_next = jnp.where(row < t_rows - 1, pltpu.roll(a, t_rows - 1, 0), acar_ref[...])
        gs = _scan_rev(a_next, dyl * (zl * szl), gcar_ref[...])
        dc_ref[...] = gs
        gcar_ref[...] = dc_ref[pl.ds(0, 1), :]
        dc_ref[...] = a
        acar_ref[...] = dc_ref[pl.ds(0, 1), :]

        dm = gs * ig * xc
        di = gs * m * xc
        dla = gs * h_prev * a - dm * (a * a / m)
        add_row(PG_LAM, _colsum(dla * r) * LRU_C)
        dra = dla * (LRU_C * log_s) * r * (1.0 - r)
        dia = di * ig * (1.0 - ig)
        add_row(PG_BA, _colsum(dra))
        add_row(PG_BX, _colsum(dia))
        dra_bf = dra.astype(BF16)
        dia_bf = dia.astype(BF16)
        for hd in range(HEADS):
            sl = slice(hd * HD, (hd + 1) * HD)
            dwa_ref[hd] += lax.dot_general(xc_bf[:, sl], dra_bf[:, sl], (((0,), (0,)), ((), ())),
                                           preferred_element_type=F32)
            dwx_ref[hd] += lax.dot_general(xc_bf[:, sl], dia_bf[:, sl], (((0,), (0,)), ((), ())),
                                           preferred_element_type=F32)
        dxc = gs * m * ig + _heads_matmul_t(dra_bf, wa_ref) + _heads_matmul_t(dia_bf, wx_ref)
        add_row(PG_B4, _colsum(dxc))
        n4 = t_rows + HALO4
        add_row(PG_W4 + 3, _colsum(dxc * xl))
        dxcbuf_ref[pl.ds(0, t_rows), :] = dxc
        db = dxcbuf_ref[...]
        dxl = w4_ref[3:4, :] * dxc
        for k in range(KW4 - 1):
            ahead = pltpu.roll(db, n4 - (KW4 - 1 - k), 0)[0:t_rows]
            dxl = dxl + w4_ref[k:k + 1, :] * ahead
            add_row(PG_W4 + k, _colsum(xl * ahead))
        dxcbuf_ref[pl.ds(t_rows, HALO4), :] = dxc[0:HALO4]
        du_ref[:, 3 * D_BR:4 * D_BR] = dxl.astype(BF16)

        @pl.when(step == nb - 1)
        def _():
            pg_ref[PG_LAM:PG_LAM + 1, :] = pg_ref[PG_LAM:PG_LAM + 1, :] * _sig(-lam_ref[...])
            dw31_ref[...] = jnp.zeros_like(dw31_ref)
            for k in range(KW):
                dw31_ref[k:k + 1, :] = jnp.sum(dwacc_ref[k], axis=0, keepdims=True)

    const2 = lambda i: (0, 0)
    const3 = lambda i: (0, 0, 0)
    rev = lambda i: (nb - 1 - i, 0)
    return pl.pallas_call(
        body, name="mixer_bwd", grid=(nb,),
        in_specs=[pl.BlockSpec((t_rows, D_IN), rev)] + [pl.BlockSpec((t_rows, D_BR), rev)] * len(kept) + [
                  pl.BlockSpec((t_rows, 2 * D_BR), rev),
                  pl.BlockSpec((None, 1, D_BR), lambda i: (nb - 1 - i, 0, 0))] + _mixer_specs(layer),
        out_specs=[pl.BlockSpec((t_rows, D_IN), rev),
                   pl.BlockSpec((PG_ROWS, D_BR), const2), pl.BlockSpec((32, D_BR), const2),
                   pl.BlockSpec((HEADS, HD, HD), const3), pl.BlockSpec((HEADS, HD, HD), const3)],
        out_shape=[jax.ShapeDtypeStruct((s, D_IN), BF16), jax.ShapeDtypeStruct((PG_ROWS, D_BR), F32),
                   jax.ShapeDtypeStruct((32, D_BR), F32), jax.ShapeDtypeStruct((HEADS, HD, HD), F32),
                   jax.ShapeDtypeStruct((HEADS, HD, HD), F32)],
        scratch_shapes=[pltpu.VMEM((t_rows, D_BR), F32), pltpu.VMEM((8, t_rows + HALO, D_BR), F32),
                        pltpu.VMEM((t_rows, D_BR), F32), pltpu.VMEM((t_rows + HALO, D_BR), F32),
                        pltpu.VMEM((KW, 8, D_BR), F32),
                        pltpu.VMEM((t_rows + HALO4, D_BR), F32), pltpu.VMEM((1, D_BR), F32),
                        pltpu.VMEM((1, D_BR), F32), pltpu.VMEM((KW, SUBLANES, D_BR), F32)],
        compiler_params=_cparams(1))(u, *kept, dy, hb, *params)


def _add_kept_half(src, recv, keep, out_dtype, name):
    h, r, c = recv.shape
    tr = min(r, 1024)

    def body(keep_ref, s_ref, r_ref, o_ref):
        o_ref[...] = (s_ref[...].astype(F32) + r_ref[...].astype(F32)).astype(out_dtype)

    grid_spec = pltpu.PrefetchScalarGridSpec(
        num_scalar_prefetch=1, grid=(h, r // tr),
        in_specs=[pl.BlockSpec((None, tr, c), lambda b, i, kp: (kp[0] * h + b, i, 0)),
                  pl.BlockSpec((None, tr, c), lambda b, i, kp: (b, i, 0))],
        out_specs=pl.BlockSpec((None, tr, c), lambda b, i, kp: (b, i, 0)))
    return pl.pallas_call(
        body, name=name, grid_spec=grid_spec, out_shape=jax.ShapeDtypeStruct(recv.shape, out_dtype),
        compiler_params=_cparams(2))(keep, src, recv)


class _PendingReduce:
    STAGE_AXES = (2, 0, 1)

    def __init__(self, bufs):
        self.bufs = list(bufs)
        self.stage = 0

    def job(self):
        return _ExchangeJob(self.bufs, self.STAGE_AXES[self.stage])

    def absorb(self, recvs):
        me = _my_pos()[self.STAGE_AXES[self.stage]]
        keep = jnp.reshape(me, (1,)).astype(jnp.int32)
        last = self.stage == 2
        self.bufs = [_add_kept_half(b, r, keep, F32 if last else BF16, f"rs_add{self.stage}_{t}")
                     for t, (b, r) in enumerate(zip(self.bufs, recvs))]
        self.stage += 1

    def finish_alone(self):
        while self.stage < 3:
            job = self.job()
            self.absorb(_run_job(job, f"rs_exchange{self.stage}"))
        return [b[0] for b in self.bufs]


def _all_reduce_small(pa, pb, job):
    nt = job.nt

    def body(*refs):
        pa_ref, pb_ref = refs[:2]
        job_in = refs[2:2 + nt]
        oa_ref, ob_ref = refs[2 + nt:4 + nt]
        job_out = refs[4 + nt:4 + 2 * nt]
        ra0, ra1, ra2, sb0, sb1, sb2, rb0, rb1, rb2, send_sems, recv_sems = refs[4 + 2 * nt:15 + 2 * nt]
        job_scr = refs[15 + 2 * nt:]
        job.start(job_in, job_out, job_scr)
        x, y, c = _my_pos()
        peers = [(x, y, 1 - c), (1 - x, y, c), (x, 1 - y, c)]
        oa_ref[...] = pa_ref[...]
        ob_ref[...] = pb_ref[...]
        for k, (peer, ra, sb, rb) in enumerate(zip(peers, (ra0, ra1, ra2), (sb0, sb1, sb2), (rb0, rb1, rb2))):
            sb[...] = ob_ref[...].astype(BF16)
            copies = [pltpu.make_async_remote_copy(
                src_ref=src, dst_ref=dst, send_sem=send_sems.at[t, k], recv_sem=recv_sems.at[t, k],
                device_id=peer, device_id_type=MESH) for t, (src, dst) in enumerate(((oa_ref, ra), (sb, rb)))]
            for cp in copies:
                cp.start()
            for cp in copies:
                cp.wait()
            oa_ref[...] = oa_ref[...] + ra[...]
            ob_ref[...] = sb[...].astype(F32) + rb[...].astype(F32)
        job.finish(job_in, job_out, job_scr)

    vm = pl.BlockSpec(memory_space=pltpu.VMEM)
    outs = pl.pallas_call(
        body, name="small_all_reduce",
        out_shape=[jax.ShapeDtypeStruct(pa.shape, F32), jax.ShapeDtypeStruct(pb.shape, F32)] + job.out_shape,
        in_specs=[vm, vm] + job.in_specs, out_specs=[vm, vm] + job.out_specs,
        scratch_shapes=[pltpu.VMEM(pa.shape, F32)] * 3 + [pltpu.VMEM(pb.shape, BF16)] * 6
        + [pltpu.SemaphoreType.DMA((2, 3)), pltpu.SemaphoreType.DMA((2, 3))] + job.scratch,
        compiler_params=pltpu.CompilerParams(vmem_limit_bytes=VMEM_LIMIT))(pa, pb, *job.arrays)
    return outs[0], outs[1], list(outs[2:])


def _adamw(w, g, m, v, name):
    r, c = w.shape
    tr = r
    for cand in (512, 256, 128, 64, 32, 16, 8):
        if r % cand == 0 and cand * c * 4 <= (2 << 20):
            tr = cand
            break

    def body(w_ref, g_ref, m_ref, v_ref, d_ref, mo_ref, vo_ref):
        gv = g_ref[...]
        m_new = ADAM_B1 * m_ref[...] + (1.0 - ADAM_B1) * gv
        v_new = ADAM_B2 * v_ref[...] + (1.0 - ADAM_B2) * (gv * gv)
        m_hat = m_new / (1.0 - ADAM_B1 ** ADAM_STEP)
        v_hat = v_new / (1.0 - ADAM_B2 ** ADAM_STEP)
        d_ref[...] = -ADAM_LR * (m_hat / (jnp.sqrt(v_hat) + ADAM_EPS) + ADAM_WD * w_ref[...])
        mo_ref[...] = m_new
        vo_ref[...] = v_new

    spec = pl.BlockSpec((tr, c), lambda i: (i, 0))
    shape = jax.ShapeDtypeStruct((r, c), F32)
    return pl.pallas_call(
        body, name=name, grid=(r // tr,), in_specs=[spec] * 4, out_specs=[spec] * 3, out_shape=[shape] * 3,
        compiler_params=_cparams(1))(w, g, m, v)


def _pack_rows(parts):
    flat = jnp.concatenate([jnp.reshape(p, (-1, D_BR)) for p in parts], axis=0)
    pad = (-flat.shape[0]) % 64
    if pad:
        flat = jnp.concatenate([flat, jnp.zeros((pad, D_BR), F32)], axis=0)
    return flat


def _unpack_rows(flat, shapes):
    out, r0 = [], 0
    for shp in shapes:
        n = 1
        for d in shp:
            n *= d
        rows = n // D_BR
        out.append(jnp.reshape(flat[r0:r0 + rows], shp))
        r0 += rows
    return out


def kernel(x, norm_g, w_in, conv_dw_w, conv_dw_b, conv_ln_g, conv_ln_b, lru_conv_w, lru_conv_b, lru_wa, lru_ba, lru_wx, lru_bx, lru_lambda, w_out, final_g, loss_target, m_norm_g, m_w_in, m_conv_dw_w, m_conv_dw_b, m_conv_ln_g, m_conv_ln_b, m_lru_conv_w, m_lru_conv_b, m_lru_wa, m_lru_ba, m_lru_wx, m_lru_bx, m_lru_lambda, m_w_out, m_final_g, v_norm_g, v_w_in, v_conv_dw_w, v_conv_dw_b, v_conv_ln_g, v_conv_ln_b, v_lru_conv_w, v_lru_conv_b, v_lru_wa, v_lru_ba, v_lru_wx, v_lru_bx, v_lru_lambda, v_w_out, v_final_g):
    n_layers = norm_g.shape[0]
    s = x.shape[1]
    t_rows = min(s, 128)
    xs = jnp.reshape(x, (s, D_MODEL))
    target = jnp.reshape(loss_target, (s, D_MODEL))
    dev = 4 * lax.axis_index("x") + 2 * lax.axis_index("y") + lax.axis_index("c")

    w_in_bf = _cast_bf16(w_in, "cast_w_in")
    w_out_bf = _cast_bf16(w_out, "cast_w_out")
    h, (w_in_l, w31_all, w4_all) = _first_norm(xs, norm_g[0:1], _GatherJob([w_in_bf[0], conv_dw_w, lru_conv_w]))
    w_out_l = None
    w31_full = jnp.reshape(jnp.transpose(w31_all, (1, 2, 0, 3)), (n_layers, KW, D_BR))
    w4_full = jnp.reshape(jnp.transpose(w4_all, (1, 2, 0, 3)), (n_layers, KW4, D_BR))
    row3 = lambda p: jnp.reshape(p, (n_layers, 1, -1))
    mixer_params = (w31_full, row3(conv_dw_b), row3(conv_ln_g), row3(conv_ln_b), w4_full, row3(lru_conv_b),
                    lru_wa.astype(BF16), row3(lru_ba), lru_wx.astype(BF16), row3(lru_bx), row3(lru_lambda))

    saved = []
    act = xs
    for l in range(n_layers):
        wanted = [w_out_bf[0]] if l == 0 else []
        if l + 1 < n_layers:
            wanted += [w_in_bf[l + 1], w_out_bf[l + 1]]
        w_full = _w_in_rows(w_in_l)
        u, gathered = _in_proj(h, w_full, _GatherJob(wanted) if wanted else None)
        if l == 0:
            w_out_l, gathered = gathered[0], gathered[1:]
        y, q_sv, h_sv, hb, *gates_sv = _mixer_fwd(u, mixer_params, l, t_rows)
        kept = [q_sv, h_sv, *gates_sv]
        wo = jnp.reshape(w_out_l, (D_MODEL, D_MODEL))
        saved.append((act, h, u, y, kept, hb, w_full, wo))
        if l + 1 < n_layers:
            act, h = _out_proj(act, y, wo, norm_g[l + 1:l + 2])
            w_in_l, w_out_l = gathered
    loss_part, dx, d_final_g = _out_proj_loss_head(act, y, wo, jnp.reshape(final_g, (1, D_MODEL)), target)
    loss = lax.psum(loss_part[0, 0], AXES)

    pending = None
    reduced_big = [None] * n_layers
    small = [None] * n_layers
    for l in reversed(range(n_layers)):
        x_l, h, u, y, kept, hb, w_full, wo = saved[l]
        dy, dxb, recvs = _out_proj_bwd_x(dx, wo, pending.job() if pending else None)
        if pending:
            pending.absorb(recvs)
            reduced_big[l + 1] = [b[0] for b in pending.bufs]
        g_out, _ = _w_out_grad(y, dxb, None)
        du, pg, dw31, dwa, dwx = _mixer_bwd(u, kept, dy, hb, mixer_params, l, t_rows)
        g_in, _ = _w_in_grad(h, du, None)
        pending = _PendingReduce([g_in, g_out])
        pending.absorb(_run_job(pending.job(), "rs_exchange_c"))
        dx, d_norm, recvs = _in_proj_bwd_x(du, w_full, x_l, norm_g[l:l + 1], dx, pending.job())
        pending.absorb(recvs)
        small[l] = (d_norm, pg, dw31, dwa, dwx)
    grad_x = jnp.reshape(dx, x.shape)

    stack = lambda f: jnp.stack([f(small[l]) for l in range(n_layers)])
    pg_all = stack(lambda t: t[1])
    rep_parts = [
        (stack(lambda t: t[0][0]), norm_g.shape), (pg_all[:, PG_B31], conv_dw_b.shape),
        (pg_all[:, PG_LNG], conv_ln_g.shape), (pg_all[:, PG_LNB], conv_ln_b.shape),
        (pg_all[:, PG_B4], lru_conv_b.shape), (stack(lambda t: t[3]), lru_wa.shape), (pg_all[:, PG_BA], lru_ba.shape),
        (stack(lambda t: t[4]), lru_wx.shape), (pg_all[:, PG_BX], lru_bx.shape), (pg_all[:, PG_LAM], lru_lambda.shape),
        (d_final_g, final_g.shape)]
    shard_parts = [(stack(lambda t: t[2][0:KW]), (n_layers, KW, D_BR)),
                   (pg_all[:, PG_W4:PG_W4 + KW4], (n_layers, KW4, D_BR))]
    gate_w = (5, 7)
    f32_parts = [p for i, p in enumerate(rep_parts) if i not in gate_w] + shard_parts
    bf16_parts = [rep_parts[i] for i in gate_w]
    red_a, red_b, recvs = _all_reduce_small(
        _pack_rows([p for p, _ in f32_parts]), _pack_rows([p for p, _ in bf16_parts]), pending.job())
    pending.absorb(recvs)
    reduced_big[0] = [b[0] for b in pending.bufs]
    grad_w_in = jnp.stack([r[0] for r in reduced_big])
    grad_w_out = jnp.stack([r[1] for r in reduced_big])
    red_a = _unpack_rows(red_a, [shp for _, shp in f32_parts])
    red_b = _unpack_rows(red_b, [shp for _, shp in bf16_parts])
    rep_grads = red_a[:len(rep_parts) - len(gate_w)]
    for i, g in zip(gate_w, red_b):
        rep_grads.insert(i, g)
    grad_dw = lax.dynamic_slice_in_dim(red_a[-2], dev * HD, HD, axis=2)
    grad_w4 = lax.dynamic_slice_in_dim(red_a[-1], dev * HD, HD, axis=2)

    def adam_nd(w, g, m, v, name):
        two_d = (-1, w.shape[-1])
        outs = _adamw(*(jnp.reshape(t, two_d) for t in (w, g, m, v)), name)
        return [jnp.reshape(o, w.shape) for o in outs]

    upd = {}
    upd["w_in"] = adam_nd(w_in, grad_w_in, m_w_in, v_w_in, "adamw_w_in")
    upd["w_out"] = adam_nd(w_out, grad_w_out, m_w_out, v_w_out, "adamw_w_out")
    upd["conv_dw_w"] = adam_nd(conv_dw_w, grad_dw, m_conv_dw_w, v_conv_dw_w, "adamw_conv_dw_w")
    upd["lru_conv_w"] = adam_nd(lru_conv_w, grad_w4, m_lru_conv_w, v_lru_conv_w, "adamw_lru_conv_w")
    rep_w = [norm_g, conv_dw_b, conv_ln_g, conv_ln_b, lru_conv_b, lru_wa, lru_ba, lru_wx, lru_bx, lru_lambda, final_g]
    rep_m = [m_norm_g, m_conv_dw_b, m_conv_ln_g, m_conv_ln_b, m_lru_conv_b, m_lru_wa, m_lru_ba, m_lru_wx, m_lru_bx,
             m_lru_lambda, m_final_g]
    rep_v = [v_norm_g, v_conv_dw_b, v_conv_ln_g, v_conv_ln_b, v_lru_conv_b, v_lru_wa, v_lru_ba, v_lru_wx, v_lru_bx,
             v_lru_lambda, v_final_g]
    rep_keys = ["norm_g", "conv_dw_b", "conv_ln_g", "conv_ln_b", "lru_conv_b", "lru_wa", "lru_ba", "lru_wx", "lru_bx",
                "lru_lambda", "final_g"]
    grads = {"w_in": grad_w_in, "w_out": grad_w_out, "conv_dw_w": grad_dw, "lru_conv_w": grad_w4}
    for i, key in enumerate(rep_keys):
        grads[key] = rep_grads[i]
        upd[key] = adam_nd(rep_w[i], rep_grads[i], rep_m[i], rep_v[i], "adamw_" + key)

    order = ["norm_g", "w_in", "conv_dw_w", "conv_dw_b", "conv_ln_g", "conv_ln_b", "lru_conv_w", "lru_conv_b", "lru_wa",
             "lru_ba", "lru_wx", "lru_bx", "lru_lambda", "w_out", "final_g"]
    return (loss, grad_x, *[grads[k] for k in order], *[upd[k][0] for k in order], *[upd[k][1] for k in order],
            *[upd[k][2] for k in order])
```

```python
import functools

import jax
import jax.numpy as jnp
from jax import lax
from jax.experimental import pallas as pl
from jax.experimental.pallas import tpu as pltpu

F32 = jnp.float32
BF16 = jnp.bfloat16
MESH = pl.DeviceIdType.MESH
AXES = ("x", "y", "c")
N_DEV = 8

D_MODEL = 2048
D_BR = 1024
D_IN = 5 * D_BR
SHARD_IN = D_IN // N_DEV
SHARD_OUT = D_MODEL // N_DEV
KW = 31
KW4 = 4
HEADS = 8
HD = 128
LRU_C = 8.0
RMS_EPS = 1e-6
LN_EPS = 1e-5
SUBLANES = 8
HALO = 32
HALO4 = 8
ROW_CHUNK = 16

ADAM_LR = 0.001
ADAM_B1 = 0.9
ADAM_B2 = 0.999
ADAM_EPS = 1e-08
ADAM_WD = 0.01
ADAM_STEP = 10

VMEM_LIMIT = 60 * 1024 * 1024

ANY = pl.BlockSpec(memory_space=pl.ANY)


def _cparams(n_grid):
    return pltpu.CompilerParams(dimension_semantics=("arbitrary",) * n_grid, vmem_limit_bytes=VMEM_LIMIT)


def _resident(block_shape, index_map):
    return pl.BlockSpec(block_shape, index_map, pipeline_mode=pl.Buffered(1))


def _sig(x):
    return 0.5 * jnp.tanh(0.5 * x) + 0.5


def _dsilu(z, sz):
    return sz * (1.0 + z * (1.0 - sz))


def _expm1(x):
    small = jnp.abs(x) < 0.01
    series = x * (1.0 + x * (0.5 + x * (1.0 / 6.0 + x * (1.0 / 24.0))))
    return jnp.where(small, series, jnp.exp(x) - 1.0)


def _log_sigmoid(x):
    e = jnp.exp(-jnp.abs(x))
    l1p = jnp.where(e < 0.01, e * (1.0 - e * (0.5 - e * (1.0 / 3.0))), jnp.log(1.0 + e))
    return jnp.minimum(x, 0.0) - l1p


def _colsum(x):
    return jnp.sum(x, axis=0, keepdims=True)


def _my_pos():
    return lax.axis_index("x"), lax.axis_index("y"), lax.axis_index("c")


class _GatherJob:
    def __init__(self, shards):
        self.arrays = list(shards)
        nt = self.nt = len(self.arrays)
        self.in_specs = [ANY] * nt
        self.out_shape = [jax.ShapeDtypeStruct((N_DEV,) + s.shape, s.dtype) for s in self.arrays]
        self.out_specs = [ANY] * nt
        self.scratch = [pltpu.SemaphoreType.DMA((nt, 7)), pltpu.SemaphoreType.DMA((nt, 7)),
                        pltpu.SemaphoreType.DMA((nt,))]

    def _plan(self, srcs, outs, scr):
        send_sems, recv_sems, local_sems = scr
        x, y, c = _my_pos()
        me, sibling = (x, y, c), (x, y, 1 - c)
        chips = [(1 - x, y), (x, 1 - y), (1 - x, 1 - y)]

        def slot(p):
            return 4 * p[0] + 2 * p[1] + p[2]

        def copy(t, k, block, to, own=False):
            dst = outs[t].at[slot(block)]
            return pltpu.make_async_remote_copy(
                src_ref=srcs[t] if own else dst, dst_ref=dst, send_sem=send_sems.at[t, k], recv_sem=recv_sems.at[t, k],
                device_id=to, device_id_type=MESH)

        mine = [pltpu.make_async_copy(srcs[t], outs[t].at[slot(me)], local_sems.at[t]) for t in range(self.nt)]
        first = []
        for t in range(self.nt):
            first.append(copy(t, 0, me, sibling, own=True))
            first += [copy(t, 1 + j, me, (*chip, c), own=True) for j, chip in enumerate(chips)]
        return me, sibling, chips, c, copy, mine, first

    def start(self, srcs, outs, scr):
        _, _, _, _, _, mine, first = self._plan(srcs, outs, scr)
        for cp in mine + first:
            cp.start()

    def finish(self, srcs, outs, scr):
        me, sibling, chips, c, copy, mine, first = self._plan(srcs, outs, scr)
        passed = []
        for j, chip in enumerate(chips):
            for t in range(self.nt):
                copy(t, 1 + j, (*chip, c), me).wait_recv()
                fwd = copy(t, 4 + j, (*chip, c), sibling)
                fwd.start()
                passed.append(fwd)
        for t in range(self.nt):
            copy(t, 0, sibling, me).wait_recv()
            for j, chip in enumerate(chips):
                copy(t, 4 + j, (*chip, 1 - c), me).wait_recv()
        for cp in first + passed:
            cp.wait_send()
        for cp in mine:
            cp.wait()


class _ExchangeJob:
    def __init__(self, srcs, axis):
        self.arrays = list(srcs)
        self.axis = axis
        nt = self.nt = len(self.arrays)
        self.half = [s.shape[0] // 2 for s in self.arrays]
        self.in_specs = [ANY] * nt
        self.out_shape = [jax.ShapeDtypeStruct((h,) + s.shape[1:], s.dtype) for h, s in zip(self.half, self.arrays)]
        self.out_specs = [ANY] * nt
        self.scratch = [pltpu.SemaphoreType.DMA((nt,)), pltpu.SemaphoreType.DMA((nt,))]

    def _copies(self, srcs, outs, scr):
        send_sems, recv_sems = scr
        pos = list(_my_pos())
        me = pos[self.axis]
        pos[self.axis] = 1 - me
        return [pltpu.make_async_remote_copy(
            src_ref=srcs[t].at[pl.ds((1 - me) * self.half[t], self.half[t])], dst_ref=outs[t],
            send_sem=send_sems.at[t], recv_sem=recv_sems.at[t], device_id=tuple(pos), device_id_type=MESH)
            for t in range(self.nt)]

    def start(self, srcs, outs, scr):
        for cp in self._copies(srcs, outs, scr):
            cp.start()

    def finish(self, srcs, outs, scr):
        for cp in self._copies(srcs, outs, scr):
            cp.wait()


def _run_job(job, name):
    def body(*refs):
        ins, outs, scr = refs[:job.nt], refs[job.nt:2 * job.nt], refs[2 * job.nt:]
        job.start(ins, outs, scr)
        job.finish(ins, outs, scr)

    return pl.pallas_call(body, name=name, out_shape=job.out_shape, in_specs=job.in_specs, out_specs=job.out_specs,
                          scratch_shapes=job.scratch)(*job.arrays)


def _hosted_call(body, *, name, grid, in_specs, out_specs, out_shape, scratch_shapes, args, job):
    n_in, n_out, n_scr = len(in_specs), len(out_specs), len(scratch_shapes)
    if job is None:
        outs = pl.pallas_call(body, name=name, grid=grid, in_specs=in_specs, out_specs=out_specs, out_shape=out_shape,
                              scratch_shapes=scratch_shapes, compiler_params=_cparams(len(grid)))(*args)
        return list(outs), None
    nt = job.nt

    def full_body(*refs):
        own_in, job_in = refs[:n_in], refs[n_in:n_in + nt]
        base = n_in + nt
        own_out, job_out = refs[base:base + n_out], refs[base + n_out:base + n_out + nt]
        base += n_out + nt
        own_scr, job_scr = refs[base:base + n_scr], refs[base + n_scr:]
        ids = [pl.program_id(a) for a in range(len(grid))]
        is_first = functools.reduce(jnp.logical_and, [i == 0 for i in ids])
        is_last = functools.reduce(jnp.logical_and, [i == g - 1 for i, g in zip(ids, grid)])

        @pl.when(is_first)
        def _():
            job.start(job_in, job_out, job_scr)

        body(*own_in, *own_out, *own_scr)

        @pl.when(is_last)
        def _():
            job.finish(job_in, job_out, job_scr)

    outs = pl.pallas_call(
        full_body, name=name, grid=grid, in_specs=list(in_specs) + job.in_specs,
        out_specs=list(out_specs) + job.out_specs, out_shape=list(out_shape) + job.out_shape,
        scratch_shapes=list(scratch_shapes) + job.scratch, compiler_params=_cparams(len(grid)))(*args, *job.arrays)
    return list(outs[:n_out]), list(outs[n_out:])


def _cast_bf16(x, name):
    nl, r, c = x.shape
    tr = min(r, 512)

    def body(x_ref, o_ref):
        o_ref[...] = x_ref[...].astype(BF16)

    spec = pl.BlockSpec((None, tr, c), lambda l, i: (l, i, 0))
    return pl.pallas_call(
        body, name=name, grid=(nl, r // tr), in_specs=[spec], out_specs=spec,
        out_shape=jax.ShapeDtypeStruct(x.shape, BF16), compiler_params=_cparams(2))(x)


def _w_in_rows(w_all):
    def body(i_ref, o_ref):
        o_ref[...] = i_ref[...]

    return pl.pallas_call(
        body, name="w_in_rows", grid=(N_DEV,),
        in_specs=[pl.BlockSpec((None, D_MODEL, SHARD_IN), lambda j: (j, 0, 0))],
        out_specs=pl.BlockSpec((D_MODEL, SHARD_IN), lambda j: (0, j)),
        out_shape=jax.ShapeDtypeStruct((D_MODEL, D_IN), BF16), compiler_params=_cparams(1))(w_all)


def _in_proj(h, w_full, job):
    s = h.shape[0]
    tm = min(s, 2048)
    tn = 2 * SHARD_IN

    def body(h_ref, w_ref, u_ref):
        u_ref[...] = jnp.dot(h_ref[...], w_ref[...], preferred_element_type=F32)

    own, extra = _hosted_call(
        body, name="in_proj", grid=(s // tm, D_IN // tn),
        in_specs=[pl.BlockSpec((tm, D_MODEL), lambda i, j: (i, 0)),
                  pl.BlockSpec((D_MODEL, tn), lambda i, j: (0, j))],
        out_specs=[pl.BlockSpec((tm, tn), lambda i, j: (i, j))],
        out_shape=[jax.ShapeDtypeStruct((s, D_IN), F32)],
        scratch_shapes=[], args=(h, w_full), job=job)
    return own[0], extra


def _shift_copies(cs_ref, buf):
    n = buf.shape[0]
    cs_ref[0] = buf
    for sft in range(1, 8):
        cs_ref[sft] = pltpu.roll(buf, n - sft, 0)


def _spread_taps(wb_ref, w_ref):
    for k in range(KW):
        wb_ref[k] = jnp.broadcast_to(w_ref[k:k + 1, :], (SUBLANES, D_BR))


def _conv_taps(cs_ref, wb_ref, q_ref, t_rows, offs):
    groups = ROW_CHUNK // SUBLANES

    def chunk(r, carry):
        r0 = pl.multiple_of(r * ROW_CHUNK, ROW_CHUNK)
        accs = [jnp.zeros((SUBLANES, D_BR), F32) for _ in range(groups)]
        for k, off in enumerate(offs):
            wv = wb_ref[k]
            ahead = cs_ref[off % 8, pl.ds(r0 + (off // 8) * 8, ROW_CHUNK), :]
            accs = [acc + wv * ahead[SUBLANES * g:SUBLANES * (g + 1)] for g, acc in enumerate(accs)]
        q_ref[pl.ds(r0, ROW_CHUNK), :] = jnp.concatenate(accs, axis=0)
        return carry

    lax.fori_loop(0, t_rows // ROW_CHUNK, chunk, 0)


def _scan_fwd(a, b, h_in):
    t_rows = a.shape[0]
    row8 = lax.broadcasted_iota(jnp.int32, a.shape, 0) & (SUBLANES - 1)
    d = 1
    while d < SUBLANES:
        keep = row8 >= d
        a_s = jnp.where(keep, pltpu.roll(a, d, 0), 1.0)
        b_s = jnp.where(keep, pltpu.roll(b, d, 0), 0.0)
        b = a * b_s + b
        a = a * a_s
        d *= 2
    carry = h_in
    groups = []
    for grp in range(t_rows // SUBLANES):
        rows = slice(grp * SUBLANES, (grp + 1) * SUBLANES)
        h_g = b[rows] + a[rows] * carry
        groups.append(h_g)
        carry = h_g[SUBLANES - 1:SUBLANES]
    return jnp.concatenate(groups, axis=0)


def _scan_rev(a, b, g_in):
    t_rows = a.shape[0]
    row8 = lax.broadcasted_iota(jnp.int32, a.shape, 0) & (SUBLANES - 1)
    d = 1
    while d < SUBLANES:
        keep = row8 < SUBLANES - d
        a_s = jnp.where(keep, pltpu.roll(a, t_rows - d, 0), 1.0)
        b_s = jnp.where(keep, pltpu.roll(b, t_rows - d, 0), 0.0)
        b = a * b_s + b
        a = a * a_s
        d *= 2
    carry = g_in
    groups = []
    for grp in reversed(range(t_rows // SUBLANES)):
        rows = slice(grp * SUBLANES, (grp + 1) * SUBLANES)
        g_g = b[rows] + a[rows] * carry
        groups.append(g_g)
        carry = g_g[0:1]
    return jnp.concatenate(groups[::-1], axis=0)


def _heads_matmul(x_bf, w_ref):
    return jnp.concatenate(
        [jnp.dot(x_bf[:, h * HD:(h + 1) * HD], w_ref[h], preferred_element_type=F32) for h in range(HEADS)], axis=1)


def _heads_matmul_t(d_bf, w_ref):
    return jnp.concatenate(
        [lax.dot_general(d_bf[:, h * HD:(h + 1) * HD], w_ref[h], (((1,), (1,)), ((), ())), preferred_element_type=F32)
         for h in range(HEADS)], axis=1)


def _layer_norm_swish(q, ln_g, ln_b, with_swish=True):
    mu = jnp.mean(q, axis=-1, keepdims=True)
    xc = q - mu
    var = jnp.mean(xc * xc, axis=-1, keepdims=True)
    rstd = lax.rsqrt(var + LN_EPS)
    n = xc * rstd
    p = n * ln_g + ln_b
    return n, rstd, p, (_sig(p) if with_swish else None)


def _lru_gates(xl, xbuf_ref, w4_ref, b4, wa_ref, ba, wx_ref, bx, lam, t_rows):
    xbuf_ref[pl.ds(HALO4, t_rows), :] = xl
    xb = xbuf_ref[...]
    n = t_rows + HALO4
    xc = b4 + w4_ref[3:4, :] * xl
    for k in range(KW4 - 1):
        off = HALO4 - (KW4 - 1) + k
        xc = xc + w4_ref[k:k + 1, :] * pltpu.roll(xb, n - off, 0)[0:t_rows]
    xc_bf = xc.astype(BF16)
    r = _sig(_heads_matmul(xc_bf, wa_ref) + ba)
    ig = _sig(_heads_matmul(xc_bf, wx_ref) + bx)
    log_s = _log_sigmoid(lam)
    la = LRU_C * r * log_s
    a = jnp.exp(la)
    m = jnp.sqrt(-_expm1(2.0 * la))
    return xb, xc, xc_bf, r, ig, log_s, a, m


def _mixer_specs(layer):
    row1 = lambda i: (layer, 0, 0)
    heads = lambda i: (layer, 0, 0, 0)
    return [pl.BlockSpec((None, KW, D_BR), row1),
            pl.BlockSpec((None, 1, D_BR), row1), pl.BlockSpec((None, 1, D_BR), row1),
            pl.BlockSpec((None, 1, D_BR), row1),
            pl.BlockSpec((None, KW4, D_BR), row1), pl.BlockSpec((None, 1, D_BR), row1),
            pl.BlockSpec((None, HEADS, HD, HD), heads), pl.BlockSpec((None, 1, D_BR), row1),
            pl.BlockSpec((None, HEADS, HD, HD), heads), pl.BlockSpec((None, 1, D_BR), row1),
            pl.BlockSpec((None, 1, D_BR), row1)]


def _mixer_fwd(u, params, layer, t_rows):
    s = u.shape[0]
    nb = s // t_rows

    def body(u_ref, w31_ref, b31_ref, lng_ref, lnb_ref, w4_ref, b4_ref, wa_ref, ba_ref, wx_ref, bx_ref, lam_ref,
             y_ref, q_out_ref, h_out_ref, hb_ref, xc_out_ref, r_out_ref, ig_out_ref, a_out_ref, m_out_ref,
             sg_out_ref, sp_out_ref, szc_out_ref, szl_out_ref,
             cbuf_ref, cs_ref, xbuf_ref, hcar_ref, wb_ref):
        @pl.when(pl.program_id(0) == 0)
        def _():
            _spread_taps(wb_ref, w31_ref)
            cbuf_ref[pl.ds(0, HALO), :] = jnp.zeros((HALO, D_BR), F32)
            xbuf_ref[pl.ds(0, HALO4), :] = jnp.zeros((HALO4, D_BR), F32)
            hcar_ref[...] = jnp.zeros_like(hcar_ref)

        zc = u_ref[:, 2 * D_BR:3 * D_BR]
        sg = _sig(u_ref[:, D_BR:2 * D_BR])
        sg_out_ref[...] = sg
        c = u_ref[:, 0:D_BR] * sg
        cbuf_ref[pl.ds(HALO, t_rows), :] = c
        _shift_copies(cs_ref, cbuf_ref[...])
        _conv_taps(cs_ref, wb_ref, q_out_ref, t_rows, [HALO - (KW - 1) + k for k in range(KW)])
        cbuf_ref[pl.ds(0, HALO), :] = c[t_rows - HALO:t_rows]
        q = q_out_ref[...] + b31_ref[...]
        q_out_ref[...] = q
        _, _, p, sp = _layer_norm_swish(q, lng_ref[...], lnb_ref[...])
        sp_out_ref[...] = sp
        szc = _sig(zc)
        szc_out_ref[...] = szc
        y_ref[:, 0:D_BR] = (p * sp * (zc * szc)).astype(BF16)

        xl = u_ref[:, 3 * D_BR:4 * D_BR]
        zl = u_ref[:, 4 * D_BR:5 * D_BR]
        _, xc, _, r, ig, _, a, m = _lru_gates(xl, xbuf_ref, w4_ref, b4_ref[...], wa_ref, ba_ref[...], wx_ref,
                                              bx_ref[...], lam_ref[...], t_rows)
        xbuf_ref[pl.ds(0, HALO4), :] = xl[t_rows - HALO4:t_rows]
        xc_out_ref[...] = xc
        r_out_ref[...] = r
        ig_out_ref[...] = ig
        a_out_ref[...] = a
        m_out_ref[...] = m
        h_in = hcar_ref[...]
        hb_ref[...] = h_in
        h = _scan_fwd(a, m * (ig * xc), h_in)
        h_out_ref[...] = h
        hcar_ref[...] = h_out_ref[pl.ds(t_rows - 1, 1), :]
        szl = _sig(zl)
        szl_out_ref[...] = szl
        y_ref[:, D_BR:2 * D_BR] = (h * (zl * szl)).astype(BF16)

    blk = pl.BlockSpec((t_rows, D_BR), lambda i: (i, 0))
    return pl.pallas_call(
        body, name="mixer_fwd", grid=(nb,),
        in_specs=[pl.BlockSpec((t_rows, D_IN), lambda i: (i, 0))] + _mixer_specs(layer),
        out_specs=[pl.BlockSpec((t_rows, 2 * D_BR), lambda i: (i, 0)), blk, blk,
                   pl.BlockSpec((None, 1, D_BR), lambda i: (i, 0, 0))] + [blk] * 9,
        out_shape=[jax.ShapeDtypeStruct((s, 2 * D_BR), BF16), jax.ShapeDtypeStruct((s, D_BR), F32),
                   jax.ShapeDtypeStruct((s, D_BR), F32), jax.ShapeDtypeStruct((nb, 1, D_BR), F32)]
        + [jax.ShapeDtypeStruct((s, D_BR), F32)] * 9,
        scratch_shapes=[pltpu.VMEM((t_rows + HALO, D_BR), F32), pltpu.VMEM((8, t_rows + HALO, D_BR), F32),
                        pltpu.VMEM((t_rows + HALO4, D_BR), F32), pltpu.VMEM((1, D_BR), F32),
                        pltpu.VMEM((KW, SUBLANES, D_BR), F32)],
        compiler_params=_cparams(1))(u, *params)


def _rms_bf16(xf, g):
    rstd = lax.rsqrt(jnp.mean(xf * xf, axis=-1, keepdims=True) + RMS_EPS)
    return (xf * rstd * g).astype(BF16)


def _first_norm(x, g_row, job):
    s = x.shape[0]
    tm = min(s, 512)

    def body(x_ref, g_ref, h_ref):
        h_ref[...] = _rms_bf16(x_ref[...], g_ref[...])

    blk = pl.BlockSpec((tm, D_MODEL), lambda i: (i, 0))
    own, extra = _hosted_call(
        body, name="first_norm", grid=(s // tm,), in_specs=[blk, pl.BlockSpec((1, D_MODEL), lambda i: (0, 0))],
        out_specs=[blk], out_shape=[jax.ShapeDtypeStruct((s, D_MODEL), BF16)], scratch_shapes=[], args=(x, g_row),
        job=job)
    return own[0], extra


def _out_proj(x, y, wo, g_next):
    s = x.shape[0]
    tm = min(s, 512)

    def body(x_ref, y_ref, w_ref, g_ref, o_ref, h_ref):
        x_next = x_ref[...] + jnp.dot(y_ref[...], w_ref[...], preferred_element_type=F32)
        o_ref[...] = x_next
        h_ref[...] = _rms_bf16(x_next, g_ref[...])

    blk = pl.BlockSpec((tm, D_MODEL), lambda i: (i, 0))
    return pl.pallas_call(
        body, name="out_proj", grid=(s // tm,),
        in_specs=[blk, blk, _resident((D_MODEL, D_MODEL), lambda i: (0, 0)),
                  pl.BlockSpec((1, D_MODEL), lambda i: (0, 0))],
        out_specs=[blk, blk],
        out_shape=[jax.ShapeDtypeStruct((s, D_MODEL), F32), jax.ShapeDtypeStruct((s, D_MODEL), BF16)],
        compiler_params=_cparams(1))(x, y, wo, g_next)


def _out_proj_loss_head(x, y, wo, g_row, target):
    s = x.shape[0]
    tm = min(s, 512)

    def body(x_ref, y_ref, w_ref, g_ref, t_ref, loss_ref, dx_ref, dg_ref):
        @pl.when(pl.program_id(0) == 0)
        def _():
            loss_ref[...] = jnp.zeros_like(loss_ref)
            dg_ref[...] = jnp.zeros_like(dg_ref)

        xf = x_ref[...] + jnp.dot(y_ref[...], w_ref[...], preferred_element_type=F32)
        g = g_ref[...]
        rstd = lax.rsqrt(jnp.mean(xf * xf, axis=-1, keepdims=True) + RMS_EPS)
        n = xf * rstd
        err = n * g - t_ref[...]
        loss_ref[...] += 0.5 * jnp.sum(jnp.mean(err * err, axis=-1, keepdims=True))
        dy = err * (1.0 / D_MODEL)
        dg_ref[...] += _colsum(dy * n)
        dn = dy * g
        dx_ref[...] = rstd * (dn - n * jnp.mean(dn * n, axis=-1, keepdims=True))

    blk = pl.BlockSpec((tm, D_MODEL), lambda i: (i, 0))
    return pl.pallas_call(
        body, name="out_proj_loss_head", grid=(s // tm,),
        in_specs=[blk, blk, _resident((D_MODEL, D_MODEL), lambda i: (0, 0)),
                  pl.BlockSpec((1, D_MODEL), lambda i: (0, 0)), blk],
        out_specs=[pl.BlockSpec((8, 128), lambda i: (0, 0)), blk, pl.BlockSpec((1, D_MODEL), lambda i: (0, 0))],
        out_shape=[jax.ShapeDtypeStruct((8, 128), F32), jax.ShapeDtypeStruct((s, D_MODEL), F32),
                   jax.ShapeDtypeStruct((1, D_MODEL), F32)],
        compiler_params=_cparams(1))(x, y, wo, g_row, target)


def _out_proj_bwd_x(dx, wo, job):
    s = dx.shape[0]
    tm = min(s, 512)

    def body(dx_ref, w_ref, dy_ref, dxb_ref):
        dxb = dx_ref[...].astype(BF16)
        dxb_ref[...] = dxb
        dy_ref[...] = lax.dot_general(dxb, w_ref[...], (((1,), (1,)), ((), ())), preferred_element_type=F32)

    blk = pl.BlockSpec((tm, D_MODEL), lambda i: (i, 0))
    own, extra = _hosted_call(
        body, name="out_proj_bwd_x", grid=(s // tm,),
        in_specs=[blk, _resident((D_MODEL, D_MODEL), lambda i: (0, 0))],
        out_specs=[blk, blk],
        out_shape=[jax.ShapeDtypeStruct((s, D_MODEL), F32), jax.ShapeDtypeStruct((s, D_MODEL), BF16)],
        scratch_shapes=[], args=(dx, wo), job=job)
    return own[0], own[1], extra


def _w_in_grad(h, du, job):
    s = h.shape[0]
    tk = min(s, 2048)
    nk = s // tk

    def body(h_ref, du_ref, o_ref, acc_ref):
        k = pl.program_id(1)

        @pl.when(k == 0)
        def _():
            acc_ref[...] = jnp.zeros_like(acc_ref)

        acc_ref[...] += lax.dot_general(h_ref[...], du_ref[...], (((0,), (0,)), ((), ())), preferred_element_type=F32)

        @pl.when(k == nk - 1)
        def _():
            o_ref[0] = acc_ref[:, 0:SHARD_IN].astype(BF16)
            o_ref[1] = acc_ref[:, SHARD_IN:2 * SHARD_IN].astype(BF16)

    own, extra = _hosted_call(
        body, name="w_in_grad", grid=(N_DEV // 2, nk),
        in_specs=[pl.BlockSpec((tk, D_MODEL), lambda q, k: (k, 0)),
                  pl.BlockSpec((tk, 2 * SHARD_IN), lambda q, k: (k, q))],
        out_specs=[pl.BlockSpec((2, None, D_MODEL, SHARD_IN), lambda q, k: (0, q, 0, 0))],
        out_shape=[jax.ShapeDtypeStruct((2, N_DEV // 2, D_MODEL, SHARD_IN), BF16)],
        scratch_shapes=[pltpu.VMEM((D_MODEL, 2 * SHARD_IN), F32)], args=(h, du), job=job)
    return jnp.reshape(own[0], (N_DEV, D_MODEL, SHARD_IN)), extra


def _w_out_grad(y, dxb, job):
    s = y.shape[0]
    tk = min(s, 2048)
    nk = s // tk
    tn = 1024

    def body(y_ref, dx_ref, o_ref, acc_ref):
        k = pl.program_id(1)

        @pl.when(k == 0)
        def _():
            acc_ref[...] = jnp.zeros_like(acc_ref)

        acc_ref[...] += lax.dot_general(y_ref[...], dx_ref[...], (((0,), (0,)), ((), ())), preferred_element_type=F32)

        @pl.when(k == nk - 1)
        def _():
            for j in range(N_DEV):
                slot = (j % 2) * 4 + j // 2
                o_ref[slot] = acc_ref[pl.ds(j * SHARD_OUT, SHARD_OUT), :].astype(BF16)

    own, extra = _hosted_call(
        body, name="w_out_grad", grid=(D_MODEL // tn, nk),
        in_specs=[pl.BlockSpec((tk, D_MODEL), lambda n, k: (k, 0)),
                  pl.BlockSpec((tk, tn), lambda n, k: (k, n))],
        out_specs=[pl.BlockSpec((N_DEV, SHARD_OUT, tn), lambda n, k: (0, 0, n))],
        out_shape=[jax.ShapeDtypeStruct((N_DEV, SHARD_OUT, D_MODEL), BF16)],
        scratch_shapes=[pltpu.VMEM((D_MODEL, tn), F32)], args=(y, dxb), job=job)
    return own[0], extra


def _in_proj_bwd_x(du, w_full, x, g_row, dx_next, job):
    s = x.shape[0]
    tm = min(s, 256)

    def body(du_ref, w_ref, x_ref, g_ref, dxn_ref, dx_ref, dg_ref):
        @pl.when(pl.program_id(0) == 0)
        def _():
            dg_ref[...] = jnp.zeros_like(dg_ref)

        dh = lax.dot_general(w_ref[...], du_ref[...], (((1,), (1,)), ((), ())), preferred_element_type=F32).T
        xf = x_ref[...]
        rstd = lax.rsqrt(jnp.mean(xf * xf, axis=-1, keepdims=True) + RMS_EPS)
        n = xf * rstd
        dg_ref[...] += _colsum(dh * n)
        dn = dh * g_ref[...]
        dx_ref[...] = dxn_ref[...] + rstd * (dn - n * jnp.mean(dn * n, axis=-1, keepdims=True))

    blk = pl.BlockSpec((tm, D_MODEL), lambda i: (i, 0))
    own, extra = _hosted_call(
        body, name="in_proj_bwd_x", grid=(s // tm,),
        in_specs=[pl.BlockSpec((tm, D_IN), lambda i: (i, 0)), _resident((D_MODEL, D_IN), lambda i: (0, 0)),
                  blk, pl.BlockSpec((1, D_MODEL), lambda i: (0, 0)), blk],
        out_specs=[blk, pl.BlockSpec((1, D_MODEL), lambda i: (0, 0))],
        out_shape=[jax.ShapeDtypeStruct((s, D_MODEL), F32), jax.ShapeDtypeStruct((1, D_MODEL), F32)],
        scratch_shapes=[], args=(du, w_full, x, g_row, dx_next), job=job)
    return own[0], own[1], extra


PG_B31, PG_LNG, PG_LNB, PG_B4, PG_BA, PG_BX, PG_LAM, PG_W4 = 0, 1, 2, 3, 4, 5, 6, 8
PG_ROWS = 16


def _mixer_bwd(u, kept, dy, hb, params, layer, t_rows):
    s = u.shape[0]
    nb = s // t_rows

    def body(u_ref, q_ref, h_ref, xc_ref, r_ref, ig_ref, a_ref, m_ref, sg_ref, sp_ref, szc_ref, szl_ref, dy_ref, hb_ref,
             w31_ref, b31_ref, lng_ref, lnb_ref, w4_ref, b4_ref, wa_ref, ba_ref, wx_ref, bx_ref, lam_ref,
             du_ref, pg_ref, dw31_ref, dwa_ref, dwx_ref,
             cbuf_ref, cs_ref, dc_ref, dqbuf_ref, dwacc_ref, dxcbuf_ref, acar_ref, gcar_ref, wb_ref):
        step = pl.program_id(0)

        @pl.when(step == 0)
        def _():
            _spread_taps(wb_ref, w31_ref)
            pg_ref[...] = jnp.zeros_like(pg_ref)
            dwa_ref[...] = jnp.zeros_like(dwa_ref)
            dwx_ref[...] = jnp.zeros_like(dwx_ref)
            dwacc_ref[...] = jnp.zeros_like(dwacc_ref)
            dqbuf_ref[pl.ds(t_rows, HALO), :] = jnp.zeros((HALO, D_BR), F32)
            dxcbuf_ref[pl.ds(t_rows, HALO4), :] = jnp.zeros((HALO4, D_BR), F32)
            acar_ref[...] = jnp.zeros_like(acar_ref)
            gcar_ref[...] = jnp.zeros_like(gcar_ref)

        def add_row(r, val):
            pg_ref[r:r + 1, :] += val

        v = u_ref[:, 0:D_BR]
        g = u_ref[:, D_BR:2 * D_BR]
        zc = u_ref[:, 2 * D_BR:3 * D_BR]
        dyc = dy_ref[:, 0:D_BR]
        sg = sg_ref[...]
        cbuf_ref[...] = v * sg
        ln_gv = lng_ref[...]
        n, rstd, p, _ = _layer_norm_swish(q_ref[...], ln_gv, lnb_ref[...], with_swish=False)
        sp = sp_ref[...]
        sz = szc_ref[...]
        du_ref[:, 2 * D_BR:3 * D_BR] = (dyc * (p * sp) * _dsilu(zc, sz)).astype(BF16)
        dp = dyc * (zc * sz) * _dsilu(p, sp)
        add_row(PG_LNG, _colsum(dp * n))
        add_row(PG_LNB, _colsum(dp))
        dn = dp * ln_gv
        dq = rstd * (dn - jnp.mean(dn, axis=-1, keepdims=True) - n * jnp.mean(dn * n, axis=-1, keepdims=True))
        add_row(PG_B31, _colsum(dq))
        dqbuf_ref[pl.ds(0, t_rows), :] = dq

        _shift_copies(cs_ref, dqbuf_ref[...])

        groups = ROW_CHUNK // SUBLANES

        def conv_chunk(r, carry):
            r0 = pl.multiple_of(r * ROW_CHUNK, ROW_CHUNK)
            cc = cbuf_ref[pl.ds(r0, ROW_CHUNK), :]
            accs = [jnp.zeros((SUBLANES, D_BR), F32) for _ in range(groups)]
            for k in range(KW):
                off = KW - 1 - k
                wv = wb_ref[k]
                ahead = cs_ref[off % 8, pl.ds(r0 + (off // 8) * 8, ROW_CHUNK), :]
                accs = [acc + wv * ahead[SUBLANES * g:SUBLANES * (g + 1)] for g, acc in enumerate(accs)]
                prod = cc * ahead
                part = prod[0:SUBLANES]
                for g in range(1, groups):
                    part = part + prod[SUBLANES * g:SUBLANES * (g + 1)]
                dwacc_ref[k] += part
            dc_ref[pl.ds(r0, ROW_CHUNK), :] = jnp.concatenate(accs, axis=0)
            return carry

        lax.fori_loop(0, t_rows // ROW_CHUNK, conv_chunk, 0)
        dqbuf_ref[pl.ds(t_rows, HALO), :] = dq[0:HALO]
        dc = dc_ref[...]
        du_ref[:, 0:D_BR] = (dc * sg).astype(BF16)
        du_ref[:, D_BR:2 * D_BR] = (dc * v * sg * (1.0 - sg)).astype(BF16)

        xl = u_ref[:, 3 * D_BR:4 * D_BR]
        zl = u_ref[:, 4 * D_BR:5 * D_BR]
        dyl = dy_ref[:, D_BR:2 * D_BR]
        xc = xc_ref[...]
        xc_bf = xc.astype(BF16)
        r = r_ref[...]
        ig = ig_ref[...]
        a = a_ref[...]
        m = m_ref[...]
        log_s = _log_sigmoid(lam_ref[...])
        row = lax.broadcasted_iota(jnp.int32, (t_rows, D_BR), 0)
        h = h_ref[...]
        h_prev = jnp.where(row >= 1, pltpu.roll(h, 1, 0), hb_ref[...])
        szl = szl_ref[...]
        du_ref[:, 4 * D_BR:5 * D_BR] = (dyl * h * _dsilu(zl, szl)).astype(BF16)
        a_next = jnp.where(row < t_rows - 1, pltpu.roll(a, t_rows - 1, 0), acar_ref[...])
        gs = _scan_rev(a_next, dyl * (zl * szl), gcar_ref[...])
        dc_ref[...] = gs
        gcar_ref[...] = dc_ref[pl.ds(0, 1), :]
        dc_ref[...] = a
        acar_ref[...] = dc_ref[pl.ds(0, 1), :]

        dm = gs * ig * xc
        di = gs * m * xc
        dla = gs * h_prev * a - dm * (a * a / m)
        add_row(PG_LAM, _colsum(dla * r) * LRU_C)
        dra = dla * (LRU_C * log_s) * r * (1.0 - r)
        dia = di * ig * (1.0 - ig)
        add_row(PG_BA, _colsum(dra))
        add_row(PG_BX, _colsum(dia))
        dra_bf = dra.astype(BF16)
        dia_bf = dia.astype(BF16)
        for hd in range(HEADS):
            sl = slice(hd * HD, (hd + 1) * HD)
            dwa_ref[hd] += lax.dot_general(xc_bf[:, sl], dra_bf[:, sl], (((0,), (0,)), ((), ())),
                                           preferred_element_type=F32)
            dwx_ref[hd] += lax.dot_general(xc_bf[:, sl], dia_bf[:, sl], (((0,), (0,)), ((), ())),
                                           preferred_element_type=F32)
        dxc = gs * m * ig + _heads_matmul_t(dra_bf, wa_ref) + _heads_matmul_t(dia_bf, wx_ref)
        add_row(PG_B4, _colsum(dxc))
        n4 = t_rows + HALO4
        add_row(PG_W4 + 3, _colsum(dxc * xl))
        dxcbuf_ref[pl.ds(0, t_rows), :] = dxc
        db = dxcbuf_ref[...]
        dxl = w4_ref[3:4, :] * dxc
        for k in range(KW4 - 1):
            ahead = pltpu.roll(db, n4 - (KW4 - 1 - k), 0)[0:t_rows]
            dxl = dxl + w4_ref[k:k + 1, :] * ahead
            add_row(PG_W4 + k, _colsum(xl * ahead))
        dxcbuf_ref[pl.ds(t_rows, HALO4), :] = dxc[0:HALO4]
        du_ref[:, 3 * D_BR:4 * D_BR] = dxl.astype(BF16)

        @pl.when(step == nb - 1)
        def _():
            pg_ref[PG_LAM:PG_LAM + 1, :] = pg_ref[PG_LAM:PG_LAM + 1, :] * _sig(-lam_ref[...])
            dw31_ref[...] = jnp.zeros_like(dw31_ref)
            for k in range(KW):
                dw31_ref[k:k + 1, :] = jnp.sum(dwacc_ref[k], axis=0, keepdims=True)

    const2 = lambda i: (0, 0)
    const3 = lambda i: (0, 0, 0)
    rev = lambda i: (nb - 1 - i, 0)
    return pl.pallas_call(
        body, name="mixer_bwd", grid=(nb,),
        in_specs=[pl.BlockSpec((t_rows, D_IN), rev)] + [pl.BlockSpec((t_rows, D_BR), rev)] * len(kept) + [
                  pl.BlockSpec((t_rows, 2 * D_BR), rev),
                  pl.BlockSpec((None, 1, D_BR), lambda i: (nb - 1 - i, 0, 0))] + _mixer_specs(layer),
        out_specs=[pl.BlockSpec((t_rows, D_IN), rev),
                   pl.BlockSpec((PG_ROWS, D_BR), const2), pl.BlockSpec((32, D_BR), const2),
                   pl.BlockSpec((HEADS, HD, HD), const3), pl.BlockSpec((HEADS, HD, HD), const3)],
        out_shape=[jax.ShapeDtypeStruct((s, D_IN), BF16), jax.ShapeDtypeStruct((PG_ROWS, D_BR), F32),
                   jax.ShapeDtypeStruct((32, D_BR), F32), jax.ShapeDtypeStruct((HEADS, HD, HD), F32),
                   jax.ShapeDtypeStruct((HEADS, HD, HD), F32)],
        scratch_shapes=[pltpu.VMEM((t_rows, D_BR), F32), pltpu.VMEM((8, t_rows + HALO, D_BR), F32),
                        pltpu.VMEM((t_rows, D_BR), F32), pltpu.VMEM((t_rows + HALO, D_BR), F32),
                        pltpu.VMEM((KW, 8, D_BR), F32),
                        pltpu.VMEM((t_rows + HALO4, D_BR), F32), pltpu.VMEM((1, D_BR), F32),
                        pltpu.VMEM((1, D_BR), F32), pltpu.VMEM((KW, SUBLANES, D_BR), F32)],
        compiler_params=_cparams(1))(u, *kept, dy, hb, *params)


def _add_kept_half(src, recv, keep, out_dtype, name):
    h, r, c = recv.shape
    tr = min(r, 1024)

    def body(keep_ref, s_ref, r_ref, o_ref):
        o_ref[...] = (s_ref[...].astype(F32) + r_ref[...].astype(F32)).astype(out_dtype)

    grid_spec = pltpu.PrefetchScalarGridSpec(
        num_scalar_prefetch=1, grid=(h, r // tr),
        in_specs=[pl.BlockSpec((None, tr, c), lambda b, i, kp: (kp[0] * h + b, i, 0)),
                  pl.BlockSpec((None, tr, c), lambda b, i, kp: (b, i, 0))],
        out_specs=pl.BlockSpec((None, tr, c), lambda b, i, kp: (b, i, 0)))
    return pl.pallas_call(
        body, name=name, grid_spec=grid_spec, out_shape=jax.ShapeDtypeStruct(recv.shape, out_dtype),
        compiler_params=_cparams(2))(keep, src, recv)


class _PendingReduce:
    STAGE_AXES = (2, 0, 1)

    def __init__(self, bufs):
        self.bufs = list(bufs)
        self.stage = 0

    def job(self):
        return _ExchangeJob(self.bufs, self.STAGE_AXES[self.stage])

    def absorb(self, recvs):
        me = _my_pos()[self.STAGE_AXES[self.stage]]
        keep = jnp.reshape(me, (1,)).astype(jnp.int32)
        last = self.stage == 2
        self.bufs = [_add_kept_half(b, r, keep, F32 if last else BF16, f"rs_add{self.stage}_{t}")
                     for t, (b, r) in enumerate(zip(self.bufs, recvs))]
        self.stage += 1

    def finish_alone(self):
        while self.stage < 3:
            job = self.job()
            self.absorb(_run_job(job, f"rs_exchange{self.stage}"))
        return [b[0] for b in self.bufs]


def _all_reduce_small(pa, pb, job):
    nt = job.nt

    def body(*refs):
        pa_ref, pb_ref = refs[:2]
        job_in = refs[2:2 + nt]
        oa_ref, ob_ref = refs[2 + nt:4 + nt]
        job_out = refs[4 + nt:4 + 2 * nt]
        ra0, ra1, ra2, sb0, sb1, sb2, rb0, rb1, rb2, send_sems, recv_sems = refs[4 + 2 * nt:15 + 2 * nt]
        job_scr = refs[15 + 2 * nt:]
        job.start(job_in, job_out, job_scr)
        x, y, c = _my_pos()
        peers = [(x, y, 1 - c), (1 - x, y, c), (x, 1 - y, c)]
        oa_ref[...] = pa_ref[...]
        ob_ref[...] = pb_ref[...]
        for k, (peer, ra, sb, rb) in enumerate(zip(peers, (ra0, ra1, ra2), (sb0, sb1, sb2), (rb0, rb1, rb2))):
            sb[...] = ob_ref[...].astype(BF16)
            copies = [pltpu.make_async_remote_copy(
                src_ref=src, dst_ref=dst, send_sem=send_sems.at[t, k], recv_sem=recv_sems.at[t, k],
                device_id=peer, device_id_type=MESH) for t, (src, dst) in enumerate(((oa_ref, ra), (sb, rb)))]
            for cp in copies:
                cp.start()
            for cp in copies:
                cp.wait()
            oa_ref[...] = oa_ref[...] + ra[...]
            ob_ref[...] = sb[...].astype(F32) + rb[...].astype(F32)
        job.finish(job_in, job_out, job_scr)

    vm = pl.BlockSpec(memory_space=pltpu.VMEM)
    outs = pl.pallas_call(
        body, name="small_all_reduce",
        out_shape=[jax.ShapeDtypeStruct(pa.shape, F32), jax.ShapeDtypeStruct(pb.shape, F32)] + job.out_shape,
        in_specs=[vm, vm] + job.in_specs, out_specs=[vm, vm] + job.out_specs,
        scratch_shapes=[pltpu.VMEM(pa.shape, F32)] * 3 + [pltpu.VMEM(pb.shape, BF16)] * 6
        + [pltpu.SemaphoreType.DMA((2, 3)), pltpu.SemaphoreType.DMA((2, 3))] + job.scratch,
        compiler_params=pltpu.CompilerParams(vmem_limit_bytes=VMEM_LIMIT))(pa, pb, *job.arrays)
    return outs[0], outs[1], list(outs[2:])


def _adamw(w, g, m, v, name):
    r, c = w.shape
    tr = r
    for cand in (512, 256, 128, 64, 32, 16, 8):
        if r % cand == 0 and cand * c * 4 <= (2 << 20):
            tr = cand
            break

    def body(w_ref, g_ref, m_ref, v_ref, d_ref, mo_ref, vo_ref):
        gv = g_ref[...]
        m_new = ADAM_B1 * m_ref[...] + (1.0 - ADAM_B1) * gv
        v_new = ADAM_B2 * v_ref[...] + (1.0 - ADAM_B2) * (gv * gv)
        m_hat = m_new / (1.0 - ADAM_B1 ** ADAM_STEP)
        v_hat = v_new / (1.0 - ADAM_B2 ** ADAM_STEP)
        d_ref[...] = -ADAM_LR * (m_hat / (jnp.sqrt(v_hat) + ADAM_EPS) + ADAM_WD * w_ref[...])
        mo_ref[...] = m_new
        vo_ref[...] = v_new

    spec = pl.BlockSpec((tr, c), lambda i: (i, 0))
    shape = jax.ShapeDtypeStruct((r, c), F32)
    return pl.pallas_call(
        body, name=name, grid=(r // tr,), in_specs=[spec] * 4, out_specs=[spec] * 3, out_shape=[shape] * 3,
        compiler_params=_cparams(1))(w, g, m, v)


def _pack_rows(parts):
    flat = jnp.concatenate([jnp.reshape(p, (-1, D_BR)) for p in parts], axis=0)
    pad = (-flat.shape[0]) % 64
    if pad:
        flat = jnp.concatenate([flat, jnp.zeros((pad, D_BR), F32)], axis=0)
    return flat


def _unpack_rows(flat, shapes):
    out, r0 = [], 0
    for shp in shapes:
        n = 1
        for d in shp:
            n *= d
        rows = n // D_BR
        out.append(jnp.reshape(flat[r0:r0 + rows], shp))
        r0 += rows
    return out


def kernel(x, norm_g, w_in, conv_dw_w, conv_dw_b, conv_ln_g, conv_ln_b, lru_conv_w, lru_conv_b, lru_wa, lru_ba, lru_wx, lru_bx, lru_lambda, w_out, final_g, loss_target, m_norm_g, m_w_in, m_conv_dw_w, m_conv_dw_b, m_conv_ln_g, m_conv_ln_b, m_lru_conv_w, m_lru_conv_b, m_lru_wa, m_lru_ba, m_lru_wx, m_lru_bx, m_lru_lambda, m_w_out, m_final_g, v_norm_g, v_w_in, v_conv_dw_w, v_conv_dw_b, v_conv_ln_g, v_conv_ln_b, v_lru_conv_w, v_lru_conv_b, v_lru_wa, v_lru_ba, v_lru_wx, v_lru_bx, v_lru_lambda, v_w_out, v_final_g):
    n_layers = norm_g.shape[0]
    s = x.shape[1]
    t_rows = min(s, 128)
    xs = jnp.reshape(x, (s, D_MODEL))
    target = jnp.reshape(loss_target, (s, D_MODEL))
    dev = 4 * lax.axis_index("x") + 2 * lax.axis_index("y") + lax.axis_index("c")

    w_in_bf = _cast_bf16(w_in, "cast_w_in")
    w_out_bf = _cast_bf16(w_out, "cast_w_out")
    h, (w_in_l, w31_all, w4_all) = _first_norm(xs, norm_g[0:1], _GatherJob([w_in_bf[0], conv_dw_w, lru_conv_w]))
    w_out_l = None
    w31_full = jnp.reshape(jnp.transpose(w31_all, (1, 2, 0, 3)), (n_layers, KW, D_BR))
    w4_full = jnp.reshape(jnp.transpose(w4_all, (1, 2, 0, 3)), (n_layers, KW4, D_BR))
    row3 = lambda p: jnp.reshape(p, (n_layers, 1, -1))
    mixer_params = (w31_full, row3(conv_dw_b), row3(conv_ln_g), row3(conv_ln_b), w4_full, row3(lru_conv_b),
                    lru_wa.astype(BF16), row3(lru_ba), lru_wx.astype(BF16), row3(lru_bx), row3(lru_lambda))

    saved = []
    act = xs
    for l in range(n_layers):
        wanted = [w_out_bf[0]] if l == 0 else []
        if l + 1 < n_layers:
            wanted += [w_in_bf[l + 1], w_out_bf[l + 1]]
        w_full = _w_in_rows(w_in_l)
        u, gathered = _in_proj(h, w_full, _GatherJob(wanted) if wanted else None)
        if l == 0:
            w_out_l, gathered = gathered[0], gathered[1:]
        y, q_sv, h_sv, hb, *gates_sv = _mixer_fwd(u, mixer_params, l, t_rows)
        kept = [q_sv, h_sv, *gates_sv]
        wo = jnp.reshape(w_out_l, (D_MODEL, D_MODEL))
        saved.append((act, h, u, y, kept, hb, w_full, wo))
        if l + 1 < n_layers:
            act, h = _out_proj(act, y, wo, norm_g[l + 1:l + 2])
            w_in_l, w_out_l = gathered
    loss_part, dx, d_final_g = _out_proj_loss_head(act, y, wo, jnp.reshape(final_g, (1, D_MODEL)), target)
    loss = lax.psum(loss_part[0, 0], AXES)

    pending = None
    reduced_big = [None] * n_layers
    small = [None] * n_layers
    for l in reversed(range(n_layers)):
        x_l, h, u, y, kept, hb, w_full, wo = saved[l]
        dy, dxb, recvs = _out_proj_bwd_x(dx, wo, pending.job() if pending else None)
        if pending:
            pending.absorb(recvs)
            reduced_big[l + 1] = [b[0] for b in pending.bufs]
        g_out, _ = _w_out_grad(y, dxb, None)
        du, pg, dw31, dwa, dwx = _mixer_bwd(u, kept, dy, hb, mixer_params, l, t_rows)
        g_in, _ = _w_in_grad(h, du, None)
        pending = _PendingReduce([g_in, g_out])
        pending.absorb(_run_job(pending.job(), "rs_exchange_c"))
        dx, d_norm, recvs = _in_proj_bwd_x(du, w_full, x_l, norm_g[l:l + 1], dx, pending.job())
        pending.absorb(recvs)
        small[l] = (d_norm, pg, dw31, dwa, dwx)
    grad_x = jnp.reshape(dx, x.shape)

    stack = lambda f: jnp.stack([f(small[l]) for l in range(n_layers)])
    pg_all = stack(lambda t: t[1])
    rep_parts = [
        (stack(lambda t: t[0][0]), norm_g.shape), (pg_all[:, PG_B31], conv_dw_b.shape),
        (pg_all[:, PG_LNG], conv_ln_g.shape), (pg_all[:, PG_LNB], conv_ln_b.shape),
        (pg_all[:, PG_B4], lru_conv_b.shape), (stack(lambda t: t[3]), lru_wa.shape), (pg_all[:, PG_BA], lru_ba.shape),
        (stack(lambda t: t[4]), lru_wx.shape), (pg_all[:, PG_BX], lru_bx.shape), (pg_all[:, PG_LAM], lru_lambda.shape),
        (d_final_g, final_g.shape)]
    shard_parts = [(stack(lambda t: t[2][0:KW]), (n_layers, KW, D_BR)),
                   (pg_all[:, PG_W4:PG_W4 + KW4], (n_layers, KW4, D_BR))]
    gate_w = (5, 7)
    f32_parts = [p for i, p in enumerate(rep_parts) if i not in gate_w] + shard_parts
    bf16_parts = [rep_parts[i] for i in gate_w]
    red_a, red_b, recvs = _all_reduce_small(
        _pack_rows([p for p, _ in f32_parts]), _pack_rows([p for p, _ in bf16_parts]), pending.job())
    pending.absorb(recvs)
    reduced_big[0] = [b[0] for b in pending.bufs]
    grad_w_in = jnp.stack([r[0] for r in reduced_big])
    grad_w_out = jnp.stack([r[1] for r in reduced_big])
    red_a = _unpack_rows(red_a, [shp for _, shp in f32_parts])
    red_b = _unpack_rows(red_b, [shp for _, shp in bf16_parts])
    rep_grads = red_a[:len(rep_parts) - len(gate_w)]
    for i, g in zip(gate_w, red_b):
        rep_grads.insert(i, g)
    grad_dw = lax.dynamic_slice_in_dim(red_a[-2], dev * HD, HD, axis=2)
    grad_w4 = lax.dynamic_slice_in_dim(red_a[-1], dev * HD, HD, axis=2)

    def adam_nd(w, g, m, v, name):
        two_d = (-1, w.shape[-1])
        outs = _adamw(*(jnp.reshape(t, two_d) for t in (w, g, m, v)), name)
        return [jnp.reshape(o, w.shape) for o in outs]

    upd = {}
    upd["w_in"] = adam_nd(w_in, grad_w_in, m_w_in, v_w_in, "adamw_w_in")
    upd["w_out"] = adam_nd(w_out, grad_w_out, m_w_out, v_w_out, "adamw_w_out")
    upd["conv_dw_w"] = adam_nd(conv_dw_w, grad_dw, m_conv_dw_w, v_conv_dw_w, "adamw_conv_dw_w")
    upd["lru_conv_w"] = adam_nd(lru_conv_w, grad_w4, m_lru_conv_w, v_lru_conv_w, "adamw_lru_conv_w")
    rep_w = [norm_g, conv_dw_b, conv_ln_g, conv_ln_b, lru_conv_b, lru_wa, lru_ba, lru_wx, lru_bx, lru_lambda, final_g]
    rep_m = [m_norm_g, m_conv_dw_b, m_conv_ln_g, m_conv_ln_b, m_lru_conv_b, m_lru_wa, m_lru_ba, m_lru_wx, m_lru_bx,
             m_lru_lambda, m_final_g]
    rep_v = [v_norm_g, v_conv_dw_b, v_conv_ln_g, v_conv_ln_b, v_lru_conv_b, v_lru_wa, v_lru_ba, v_lru_wx, v_lru_bx,
             v_lru_lambda, v_final_g]
    rep_keys = ["norm_g", "conv_dw_b", "conv_ln_g", "conv_ln_b", "lru_conv_b", "lru_wa", "lru_ba", "lru_wx", "lru_bx",
                "lru_lambda", "final_g"]
    grads = {"w_in": grad_w_in, "w_out": grad_w_out, "conv_dw_w": grad_dw, "lru_conv_w": grad_w4}
    for i, key in enumerate(rep_keys):
        grads[key] = rep_grads[i]
        upd[key] = adam_nd(rep_w[i], rep_grads[i], rep_m[i], rep_v[i], "adamw_" + key)

    order = ["norm_g", "w_in", "conv_dw_w", "conv_dw_b", "conv_ln_g", "conv_ln_b", "lru_conv_w", "lru_conv_b", "lru_wa",
             "lru_ba", "lru_wx", "lru_bx", "lru_lambda", "w_out", "final_g"]
    return (loss, grad_x, *[grads[k] for k in order], *[upd[k][0] for k in order], *[upd[k][1] for k in order],
            *[upd[k][2] for k in order])
```

```python
import functools

import jax
import jax.numpy as jnp
from jax import lax
from jax.experimental import pallas as pl
from jax.experimental.pallas import tpu as pltpu

F32 = jnp.float32
BF16 = jnp.bfloat16
MESH = pl.DeviceIdType.MESH
AXES = ("x", "y", "c")
N_DEV = 8

D_MODEL = 2048
D_BR = 1024
D_IN = 5 * D_BR
SHARD_IN = D_IN // N_DEV
SHARD_OUT = D_MODEL // N_DEV
KW = 31
KW4 = 4
HEADS = 8
HD = 128
LRU_C = 8.0
RMS_EPS = 1e-6
LN_EPS = 1e-5
SUBLANES = 8
HALO = 32
HALO4 = 8
ROW_CHUNK = 16

ADAM_LR = 0.001
ADAM_B1 = 0.9
ADAM_B2 = 0.999
ADAM_EPS = 1e-08
ADAM_WD = 0.01
ADAM_STEP = 10

VMEM_LIMIT = 60 * 1024 * 1024

ANY = pl.BlockSpec(memory_space=pl.ANY)


def _cparams(n_grid):
    return pltpu.CompilerParams(dimension_semantics=("arbitrary",) * n_grid, vmem_limit_bytes=VMEM_LIMIT)


def _resident(block_shape, index_map):
    return pl.BlockSpec(block_shape, index_map, pipeline_mode=pl.Buffered(1))


def _sig(x):
    return 0.5 * jnp.tanh(0.5 * x) + 0.5


def _dsilu(z, sz):
    return sz * (1.0 + z * (1.0 - sz))


def _expm1(x):
    small = jnp.abs(x) < 0.01
    series = x * (1.0 + x * (0.5 + x * (1.0 / 6.0 + x * (1.0 / 24.0))))
    return jnp.where(small, series, jnp.exp(x) - 1.0)


def _log_sigmoid(x):
    e = jnp.exp(-jnp.abs(x))
    l1p = jnp.where(e < 0.01, e * (1.0 - e * (0.5 - e * (1.0 / 3.0))), jnp.log(1.0 + e))
    return jnp.minimum(x, 0.0) - l1p


def _colsum(x):
    return jnp.sum(x, axis=0, keepdims=True)


def _my_pos():
    return lax.axis_index("x"), lax.axis_index("y"), lax.axis_index("c")


class _GatherJob:
    def __init__(self, shards):
        self.arrays = list(shards)
        nt = self.nt = len(self.arrays)
        self.in_specs = [ANY] * nt
        self.out_shape = [jax.ShapeDtypeStruct((N_DEV,) + s.shape, s.dtype) for s in self.arrays]
        self.out_specs = [ANY] * nt
        self.scratch = [pltpu.SemaphoreType.DMA((nt, 7)), pltpu.SemaphoreType.DMA((nt, 7)),
                        pltpu.SemaphoreType.DMA((nt,))]

    def _plan(self, srcs, outs, scr):
        send_sems, recv_sems, local_sems = scr
        x, y, c = _my_pos()
        me, sibling = (x, y, c), (x, y, 1 - c)
        chips = [(1 - x, y), (x, 1 - y), (1 - x, 1 - y)]

        def slot(p):
            return 4 * p[0] + 2 * p[1] + p[2]

        def copy(t, k, block, to, own=False):
            dst = outs[t].at[slot(block)]
            return pltpu.make_async_remote_copy(
                src_ref=srcs[t] if own else dst, dst_ref=dst, send_sem=send_sems.at[t, k], recv_sem=recv_sems.at[t, k],
                device_id=to, device_id_type=MESH)

        mine = [pltpu.make_async_copy(srcs[t], outs[t].at[slot(me)], local_sems.at[t]) for t in range(self.nt)]
        first = []
        for t in range(self.nt):
            first.append(copy(t, 0, me, sibling, own=True))
            first += [copy(t, 1 + j, me, (*chip, c), own=True) for j, chip in enumerate(chips)]
        return me, sibling, chips, c, copy, mine, first

    def start(self, srcs, outs, scr):
        _, _, _, _, _, mine, first = self._plan(srcs, outs, scr)
        for cp in mine + first:
            cp.start()

    def finish(self, srcs, outs, scr):
        me, sibling, chips, c, copy, mine, first = self._plan(srcs, outs, scr)
        passed = []
        for j, chip in enumerate(chips):
            for t in range(self.nt):
                copy(t, 1 + j, (*chip, c), me).wait_recv()
                fwd = copy(t, 4 + j, (*chip, c), sibling)
                fwd.start()
                passed.append(fwd)
        for t in range(self.nt):
            copy(t, 0, sibling, me).wait_recv()
            for j, chip in enumerate(chips):
                copy(t, 4 + j, (*chip, 1 - c), me).wait_recv()
        for cp in first + passed:
            cp.wait_send()
        for cp in mine:
            cp.wait()


class _ExchangeJob:
    def __init__(self, srcs, axis):
        self.arrays = list(srcs)
        self.axis = axis
        nt = self.nt = len(self.arrays)
        self.half = [s.shape[0] // 2 for s in self.arrays]
        self.in_specs = [ANY] * nt
        self.out_shape = [jax.ShapeDtypeStruct((h,) + s.shape[1:], s.dtype) for h, s in zip(self.half, self.arrays)]
        self.out_specs = [ANY] * nt
        self.scratch = [pltpu.SemaphoreType.DMA((nt,)), pltpu.SemaphoreType.DMA((nt,))]

    def _copies(self, srcs, outs, scr):
        send_sems, recv_sems = scr
        pos = list(_my_pos())
        me = pos[self.axis]
        pos[self.axis] = 1 - me
        return [pltpu.make_async_remote_copy(
            src_ref=srcs[t].at[pl.ds((1 - me) * self.half[t], self.half[t])], dst_ref=outs[t],
            send_sem=send_sems.at[t], recv_sem=recv_sems.at[t], device_id=tuple(pos), device_id_type=MESH)
            for t in range(self.nt)]

    def start(self, srcs, outs, scr):
        for cp in self._copies(srcs, outs, scr):
            cp.start()

    def finish(self, srcs, outs, scr):
        for cp in self._copies(srcs, outs, scr):
            cp.wait()


def _run_job(job, name):
    def body(*refs):
        ins, outs, scr = refs[:job.nt], refs[job.nt:2 * job.nt], refs[2 * job.nt:]
        job.start(ins, outs, scr)
        job.finish(ins, outs, scr)

    return pl.pallas_call(body, name=name, out_shape=job.out_shape, in_specs=job.in_specs, out_specs=job.out_specs,
                          scratch_shapes=job.scratch)(*job.arrays)


def _hosted_call(body, *, name, grid, in_specs, out_specs, out_shape, scratch_shapes, args, job):
    n_in, n_out, n_scr = len(in_specs), len(out_specs), len(scratch_shapes)
    if job is None:
        outs = pl.pallas_call(body, name=name, grid=grid, in_specs=in_specs, out_specs=out_specs, out_shape=out_shape,
                              scratch_shapes=scratch_shapes, compiler_params=_cparams(len(grid)))(*args)
        return list(outs), None
    nt = job.nt

    def full_body(*refs):
        own_in, job_in = refs[:n_in], refs[n_in:n_in + nt]
        base = n_in + nt
        own_out, job_out = refs[base:base + n_out], refs[base + n_out:base + n_out + nt]
        base += n_out + nt
        own_scr, job_scr = refs[base:base + n_scr], refs[base + n_scr:]
        ids = [pl.program_id(a) for a in range(len(grid))]
        is_first = functools.reduce(jnp.logical_and, [i == 0 for i in ids])
        is_last = functools.reduce(jnp.logical_and, [i == g - 1 for i, g in zip(ids, grid)])

        @pl.when(is_first)
        def _():
            job.start(job_in, job_out, job_scr)

        body(*own_in, *own_out, *own_scr)

        @pl.when(is_last)
        def _():
            job.finish(job_in, job_out, job_scr)

    outs = pl.pallas_call(
        full_body, name=name, grid=grid, in_specs=list(in_specs) + job.in_specs,
        out_specs=list(out_specs) + job.out_specs, out_shape=list(out_shape) + job.out_shape,
        scratch_shapes=list(scratch_shapes) + job.scratch, compiler_params=_cparams(len(grid)))(*args, *job.arrays)
    return list(outs[:n_out]), list(outs[n_out:])


def _cast_bf16(x, name):
    nl, r, c = x.shape
    tr = min(r, 512)

    def body(x_ref, o_ref):
        o_ref[...] = x_ref[...].astype(BF16)

    spec = pl.BlockSpec((None, tr, c), lambda l, i: (l, i, 0))
    return pl.pallas_call(
        body, name=name, grid=(nl, r // tr), in_specs=[spec], out_specs=spec,
        out_shape=jax.ShapeDtypeStruct(x.shape, BF16), compiler_params=_cparams(2))(x)


def _w_in_rows(w_all):
    def body(i_ref, o_ref):
        o_ref[...] = i_ref[...]

    return pl.pallas_call(
        body, name="w_in_rows", grid=(N_DEV,),
        in_specs=[pl.BlockSpec((None, D_MODEL, SHARD_IN), lambda j: (j, 0, 0))],
        out_specs=pl.BlockSpec((D_MODEL, SHARD_IN), lambda j: (0, j)),
        out_shape=jax.ShapeDtypeStruct((D_MODEL, D_IN), BF16), compiler_params=_cparams(1))(w_all)


def _in_proj(h, w_full, job):
    s = h.shape[0]
    tm = min(s, 2048)
    tn = 2 * SHARD_IN

    def body(h_ref, w_ref, u_ref):
        u_ref[...] = jnp.dot(h_ref[...], w_ref[...], preferred_element_type=F32)

    own, extra = _hosted_call(
        body, name="in_proj", grid=(s // tm, D_IN // tn),
        in_specs=[pl.BlockSpec((tm, D_MODEL), lambda i, j: (i, 0)),
                  pl.BlockSpec((D_MODEL, tn), lambda i, j: (0, j))],
        out_specs=[pl.BlockSpec((tm, tn), lambda i, j: (i, j))],
        out_shape=[jax.ShapeDtypeStruct((s, D_IN), F32)],
        scratch_shapes=[], args=(h, w_full), job=job)
    return own[0], extra


def _shift_copies(cs_ref, buf):
    n = buf.shape[0]
    cs_ref[0] = buf
    for sft in range(1, 8):
        cs_ref[sft] = pltpu.roll(buf, n - sft, 0)


def _spread_taps(wb_ref, w_ref):
    for k in range(KW):
        wb_ref[k] = jnp.broadcast_to(w_ref[k:k + 1, :], (SUBLANES, D_BR))


def _conv_taps(cs_ref, wb_ref, q_ref, t_rows, offs):
    groups = ROW_CHUNK // SUBLANES

    def chunk(r, carry):
        r0 = pl.multiple_of(r * ROW_CHUNK, ROW_CHUNK)
        accs = [jnp.zeros((SUBLANES, D_BR), F32) for _ in range(groups)]
        for k, off in enumerate(offs):
            wv = wb_ref[k]
            ahead = cs_ref[off % 8, pl.ds(r0 + (off // 8) * 8, ROW_CHUNK), :]
            accs = [acc + wv * ahead[SUBLANES * g:SUBLANES * (g + 1)] for g, acc in enumerate(accs)]
        q_ref[pl.ds(r0, ROW_CHUNK), :] = jnp.concatenate(accs, axis=0)
        return carry

    lax.fori_loop(0, t_rows // ROW_CHUNK, chunk, 0)


def _scan_fwd(a, b, h_in):
    t_rows = a.shape[0]
    row8 = lax.broadcasted_iota(jnp.int32, a.shape, 0) & (SUBLANES - 1)
    d = 1
    while d < SUBLANES:
        keep = row8 >= d
        a_s = jnp.where(keep, pltpu.roll(a, d, 0), 1.0)
        b_s = jnp.where(keep, pltpu.roll(b, d, 0), 0.0)
        b = a * b_s + b
        a = a * a_s
        d *= 2
    carry = h_in
    groups = []
    for grp in range(t_rows // SUBLANES):
        rows = slice(grp * SUBLANES, (grp + 1) * SUBLANES)
        h_g = b[rows] + a[rows] * carry
        groups.append(h_g)
        carry = h_g[SUBLANES - 1:SUBLANES]
    return jnp.concatenate(groups, axis=0)


def _scan_rev(a, b, g_in):
    t_rows = a.shape[0]
    row8 = lax.broadcasted_iota(jnp.int32, a.shape, 0) & (SUBLANES - 1)
    d = 1
    while d < SUBLANES:
        keep = row8 < SUBLANES - d
        a_s = jnp.where(keep, pltpu.roll(a, t_rows - d, 0), 1.0)
        b_s = jnp.where(keep, pltpu.roll(b, t_rows - d, 0), 0.0)
        b = a * b_s + b
        a = a * a_s
        d *= 2
    carry = g_in
    groups = []
    for grp in reversed(range(t_rows // SUBLANES)):
        rows = slice(grp * SUBLANES, (grp + 1) * SUBLANES)
        g_g = b[rows] + a[rows] * carry
        groups.append(g_g)
        carry = g_g[0:1]
    return jnp.concatenate(groups[::-1], axis=0)


def _heads_matmul(x_bf, w_ref):
    return jnp.concatenate(
        [jnp.dot(x_bf[:, h * HD:(h + 1) * HD], w_ref[h], preferred_element_type=F32) for h in range(HEADS)], axis=1)


def _heads_matmul_t(d_bf, w_ref):
    return jnp.concatenate(
        [lax.dot_general(d_bf[:, h * HD:(h + 1) * HD], w_ref[h], (((1,), (1,)), ((), ())), preferred_element_type=F32)
         for h in range(HEADS)], axis=1)


def _layer_norm_swish(q, ln_g, ln_b, with_swish=True):
    mu = jnp.mean(q, axis=-1, keepdims=True)
    xc = q - mu
    var = jnp.mean(xc * xc, axis=-1, keepdims=True)
    rstd = lax.rsqrt(var + LN_EPS)
    n = xc * rstd
    p = n * ln_g + ln_b
    return n, rstd, p, (_sig(p) if with_swish else None)


def _lru_gates(xl, xbuf_ref, w4_ref, b4, wa_ref, ba, wx_ref, bx, lam, t_rows):
    xbuf_ref[pl.ds(HALO4, t_rows), :] = xl
    xb = xbuf_ref[...]
    n = t_rows + HALO4
    xc = b4 + w4_ref[3:4, :] * xl
    for k in range(KW4 - 1):
        off = HALO4 - (KW4 - 1) + k
        xc = xc + w4_ref[k:k + 1, :] * pltpu.roll(xb, n - off, 0)[0:t_rows]
    xc_bf = xc.astype(BF16)
    r = _sig(_heads_matmul(xc_bf, wa_ref) + ba)
    ig = _sig(_heads_matmul(xc_bf, wx_ref) + bx)
    log_s = _log_sigmoid(lam)
    la = LRU_C * r * log_s
    a = jnp.exp(la)
    m = jnp.sqrt(-_expm1(2.0 * la))
    return xb, xc, xc_bf, r, ig, log_s, a, m


def _mixer_specs(layer):
    row1 = lambda i: (layer, 0, 0)
    heads = lambda i: (layer, 0, 0, 0)
    return [pl.BlockSpec((None, KW, D_BR), row1),
            pl.BlockSpec((None, 1, D_BR), row1), pl.BlockSpec((None, 1, D_BR), row1),
            pl.BlockSpec((None, 1, D_BR), row1),
            pl.BlockSpec((None, KW4, D_BR), row1), pl.BlockSpec((None, 1, D_BR), row1),
            pl.BlockSpec((None, HEADS, HD, HD), heads), pl.BlockSpec((None, 1, D_BR), row1),
            pl.BlockSpec((None, HEADS, HD, HD), heads), pl.BlockSpec((None, 1, D_BR), row1),
            pl.BlockSpec((None, 1, D_BR), row1)]


def _mixer_fwd(u, params, layer, t_rows):
    s = u.shape[0]
    nb = s // t_rows

    def body(u_ref, w31_ref, b31_ref, lng_ref, lnb_ref, w4_ref, b4_ref, wa_ref, ba_ref, wx_ref, bx_ref, lam_ref,
             y_ref, q_out_ref, h_out_ref, hb_ref, xc_out_ref, r_out_ref, ig_out_ref, a_out_ref, m_out_ref,
             sg_out_ref, sp_out_ref, szc_out_ref, szl_out_ref,
             cbuf_ref, cs_ref, xbuf_ref, hcar_ref, wb_ref):
        @pl.when(pl.program_id(0) == 0)
        def _():
            _spread_taps(wb_ref, w31_ref)
            cbuf_ref[pl.ds(0, HALO), :] = jnp.zeros((HALO, D_BR), F32)
            xbuf_ref[pl.ds(0, HALO4), :] = jnp.zeros((HALO4, D_BR), F32)
            hcar_ref[...] = jnp.zeros_like(hcar_ref)

        zc = u_ref[:, 2 * D_BR:3 * D_BR]
        sg = _sig(u_ref[:, D_BR:2 * D_BR])
        sg_out_ref[...] = sg
        c = u_ref[:, 0:D_BR] * sg
        cbuf_ref[pl.ds(HALO, t_rows), :] = c
        _shift_copies(cs_ref, cbuf_ref[...])
        _conv_taps(cs_ref, wb_ref, q_out_ref, t_rows, [HALO - (KW - 1) + k for k in range(KW)])
        cbuf_ref[pl.ds(0, HALO), :] = c[t_rows - HALO:t_rows]
        q = q_out_ref[...] + b31_ref[...]
        q_out_ref[...] = q
        _, _, p, sp = _layer_norm_swish(q, lng_ref[...], lnb_ref[...])
        sp_out_ref[...] = sp
        szc = _sig(zc)
        szc_out_ref[...] = szc
        y_ref[:, 0:D_BR] = (p * sp * (zc * szc)).astype(BF16)

        xl = u_ref[:, 3 * D_BR:4 * D_BR]
        zl = u_ref[:, 4 * D_BR:5 * D_BR]
        _, xc, _, r, ig, _, a, m = _lru_gates(xl, xbuf_ref, w4_ref, b4_ref[...], wa_ref, ba_ref[...], wx_ref,
                                              bx_ref[...], lam_ref[...], t_rows)
        xbuf_ref[pl.ds(0, HALO4), :] = xl[t_rows - HALO4:t_rows]
        xc_out_ref[...] = xc
        r_out_ref[...] = r
        ig_out_ref[...] = ig
        a_out_ref[...] = a
        m_out_ref[...] = m
        h_in = hcar_ref[...]
        hb_ref[...] = h_in
        h = _scan_fwd(a, m * (ig * xc), h_in)
        h_out_ref[...] = h
        hcar_ref[...] = h_out_ref[pl.ds(t_rows - 1, 1), :]
        szl = _sig(zl)
        szl_out_ref[...] = szl
        y_ref[:, D_BR:2 * D_BR] = (h * (zl * szl)).astype(BF16)

    blk = pl.BlockSpec((t_rows, D_BR), lambda i: (i, 0))
    return pl.pallas_call(
        body, name="mixer_fwd", grid=(nb,),
        in_specs=[pl.BlockSpec((t_rows, D_IN), lambda i: (i, 0))] + _mixer_specs(layer),
        out_specs=[pl.BlockSpec((t_rows, 2 * D_BR), lambda i: (i, 0)), blk, blk,
                   pl.BlockSpec((None, 1, D_BR), lambda i: (i, 0, 0))] + [blk] * 9,
        out_shape=[jax.ShapeDtypeStruct((s, 2 * D_BR), BF16), jax.ShapeDtypeStruct((s, D_BR), F32),
                   jax.ShapeDtypeStruct((s, D_BR), F32), jax.ShapeDtypeStruct((nb, 1, D_BR), F32)]
        + [jax.ShapeDtypeStruct((s, D_BR), F32)] * 9,
        scratch_shapes=[pltpu.VMEM((t_rows + HALO, D_BR), F32), pltpu.VMEM((8, t_rows + HALO, D_BR), F32),
                        pltpu.VMEM((t_rows + HALO4, D_BR), F32), pltpu.VMEM((1, D_BR), F32),
                        pltpu.VMEM((KW, SUBLANES, D_BR), F32)],
        compiler_params=_cparams(1))(u, *params)


def _rms_bf16(xf, g):
    rstd = lax.rsqrt(jnp.mean(xf * xf, axis=-1, keepdims=True) + RMS_EPS)
    return (xf * rstd * g).astype(BF16)


def _first_norm(x, g_row, job):
    s = x.shape[0]
    tm = min(s, 512)

    def body(x_ref, g_ref, h_ref):
        h_ref[...] = _rms_bf16(x_ref[...], g_ref[...])

    blk = pl.BlockSpec((tm, D_MODEL), lambda i: (i, 0))
    own, extra = _hosted_call(
        body, name="first_norm", grid=(s // tm,), in_specs=[blk, pl.BlockSpec((1, D_MODEL), lambda i: (0, 0))],
        out_specs=[blk], out_shape=[jax.ShapeDtypeStruct((s, D_MODEL), BF16)], scratch_shapes=[], args=(x, g_row),
        job=job)
    return own[0], extra


def _out_proj(x, y, wo, g_next):
    s = x.shape[0]
    tm = min(s, 512)

    def body(x_ref, y_ref, w_ref, g_ref, o_ref, h_ref):
        x_next = x_ref[...] + jnp.dot(y_ref[...], w_ref[...], preferred_element_type=F32)
        o_ref[...] = x_next
        h_ref[...] = _rms_bf16(x_next, g_ref[...])

    blk = pl.BlockSpec((tm, D_MODEL), lambda i: (i, 0))
    return pl.pallas_call(
        body, name="out_proj", grid=(s // tm,),
        in_specs=[blk, blk, _resident((D_MODEL, D_MODEL), lambda i: (0, 0)),
                  pl.BlockSpec((1, D_MODEL), lambda i: (0, 0))],
        out_specs=[blk, blk],
        out_shape=[jax.ShapeDtypeStruct((s, D_MODEL), F32), jax.ShapeDtypeStruct((s, D_MODEL), BF16)],
        compiler_params=_cparams(1))(x, y, wo, g_next)


def _out_proj_loss_head(x, y, wo, g_row, target):
    s = x.shape[0]
    tm = min(s, 512)

    def body(x_ref, y_ref, w_ref, g_ref, t_ref, loss_ref, dx_ref, dg_ref):
        @pl.when(pl.program_id(0) == 0)
        def _():
            loss_ref[...] = jnp.zeros_like(loss_ref)
            dg_ref[...] = jnp.zeros_like(dg_ref)

        xf = x_ref[...] + jnp.dot(y_ref[...], w_ref[...], preferred_element_type=F32)
        g = g_ref[...]
        rstd = lax.rsqrt(jnp.mean(xf * xf, axis=-1, keepdims=True) + RMS_EPS)
        n = xf * rstd
        err = n * g - t_ref[...]
        loss_ref[...] += 0.5 * jnp.sum(jnp.mean(err * err, axis=-1, keepdims=True))
        dy = err * (1.0 / D_MODEL)
        dg_ref[...] += _colsum(dy * n)
        dn = dy * g
        dx_ref[...] = rstd * (dn - n * jnp.mean(dn * n, axis=-1, keepdims=True))

    blk = pl.BlockSpec((tm, D_MODEL), lambda i: (i, 0))
    return pl.pallas_call(
        body, name="out_proj_loss_head", grid=(s // tm,),
        in_specs=[blk, blk, _resident((D_MODEL, D_MODEL), lambda i: (0, 0)),
                  pl.BlockSpec((1, D_MODEL), lambda i: (0, 0)), blk],
        out_specs=[pl.BlockSpec((8, 128), lambda i: (0, 0)), blk, pl.BlockSpec((1, D_MODEL), lambda i: (0, 0))],
        out_shape=[jax.ShapeDtypeStruct((8, 128), F32), jax.ShapeDtypeStruct((s, D_MODEL), F32),
                   jax.ShapeDtypeStruct((1, D_MODEL), F32)],
        compiler_params=_cparams(1))(x, y, wo, g_row, target)


def _out_proj_bwd_x(dx, wo, job):
    s = dx.shape[0]
    tm = min(s, 1024)

    def body(dx_ref, w_ref, dy_ref, dxb_ref):
        dxb = dx_ref[...].astype(BF16)
        dxb_ref[...] = dxb
        dy_ref[...] = lax.dot_general(dxb, w_ref[...], (((1,), (1,)), ((), ())), preferred_element_type=F32)

    blk = pl.BlockSpec((tm, D_MODEL), lambda i: (i, 0))
    own, extra = _hosted_call(
        body, name="out_proj_bwd_x", grid=(s // tm,),
        in_specs=[blk, _resident((D_MODEL, D_MODEL), lambda i: (0, 0))],
        out_specs=[blk, blk],
        out_shape=[jax.ShapeDtypeStruct((s, D_MODEL), F32), jax.ShapeDtypeStruct((s, D_MODEL), BF16)],
        scratch_shapes=[], args=(dx, wo), job=job)
    return own[0], own[1], extra


def _w_in_grad(h, du, job):
    s = h.shape[0]
    tk = min(s, 2048)
    nk = s // tk

    def body(h_ref, du_ref, o_ref, acc_ref):
        k = pl.program_id(1)

        @pl.when(k == 0)
        def _():
            acc_ref[...] = jnp.zeros_like(acc_ref)

        acc_ref[...] += lax.dot_general(h_ref[...], du_ref[...], (((0,), (0,)), ((), ())), preferred_element_type=F32)

        @pl.when(k == nk - 1)
        def _():
            o_ref[0] = acc_ref[:, 0:SHARD_IN].astype(BF16)
            o_ref[1] = acc_ref[:, SHARD_IN:2 * SHARD_IN].astype(BF16)

    own, extra = _hosted_call(
        body, name="w_in_grad", grid=(N_DEV // 2, nk),
        in_specs=[pl.BlockSpec((tk, D_MODEL), lambda q, k: (k, 0)),
                  pl.BlockSpec((tk, 2 * SHARD_IN), lambda q, k: (k, q))],
        out_specs=[pl.BlockSpec((2, None, D_MODEL, SHARD_IN), lambda q, k: (0, q, 0, 0))],
        out_shape=[jax.ShapeDtypeStruct((2, N_DEV // 2, D_MODEL, SHARD_IN), BF16)],
        scratch_shapes=[pltpu.VMEM((D_MODEL, 2 * SHARD_IN), F32)], args=(h, du), job=job)
    return jnp.reshape(own[0], (N_DEV, D_MODEL, SHARD_IN)), extra


def _w_out_grad(y, dxb, job):
    s = y.shape[0]
    tk = min(s, 2048)
    nk = s // tk
    tn = 1024

    def body(y_ref, dx_ref, o_ref, acc_ref):
        k = pl.program_id(1)

        @pl.when(k == 0)
        def _():
            acc_ref[...] = jnp.zeros_like(acc_ref)

        acc_ref[...] += lax.dot_general(y_ref[...], dx_ref[...], (((0,), (0,)), ((), ())), preferred_element_type=F32)

        @pl.when(k == nk - 1)
        def _():
            for j in range(N_DEV):
                slot = (j % 2) * 4 + j // 2
                o_ref[slot] = acc_ref[pl.ds(j * SHARD_OUT, SHARD_OUT), :].astype(BF16)

    own, extra = _hosted_call(
        body, name="w_out_grad", grid=(D_MODEL // tn, nk),
        in_specs=[pl.BlockSpec((tk, D_MODEL), lambda n, k: (k, 0)),
                  pl.BlockSpec((tk, tn), lambda n, k: (k, n))],
        out_specs=[pl.BlockSpec((N_DEV, SHARD_OUT, tn), lambda n, k: (0, 0, n))],
        out_shape=[jax.ShapeDtypeStruct((N_DEV, SHARD_OUT, D_MODEL), BF16)],
        scratch_shapes=[pltpu.VMEM((D_MODEL, tn), F32)], args=(y, dxb), job=job)
    return own[0], extra


def _in_proj_bwd_x(du, w_full, x, g_row, dx_next, job):
    s = x.shape[0]
    tm = min(s, 256)

    def body(du_ref, w_ref, x_ref, g_ref, dxn_ref, dx_ref, dg_ref):
        @pl.when(pl.program_id(0) == 0)
        def _():
            dg_ref[...] = jnp.zeros_like(dg_ref)

        dh = lax.dot_general(w_ref[...], du_ref[...], (((1,), (1,)), ((), ())), preferred_element_type=F32).T
        xf = x_ref[...]
        rstd = lax.rsqrt(jnp.mean(xf * xf, axis=-1, keepdims=True) + RMS_EPS)
        n = xf * rstd
        dg_ref[...] += _colsum(dh * n)
        dn = dh * g_ref[...]
        dx_ref[...] = dxn_ref[...] + rstd * (dn - n * jnp.mean(dn * n, axis=-1, keepdims=True))

    blk = pl.BlockSpec((tm, D_MODEL), lambda i: (i, 0))
    own, extra = _hosted_call(
        body, name="in_proj_bwd_x", grid=(s // tm,),
        in_specs=[pl.BlockSpec((tm, D_IN), lambda i: (i, 0)), _resident((D_MODEL, D_IN), lambda i: (0, 0)),
                  blk, pl.BlockSpec((1, D_MODEL), lambda i: (0, 0)), blk],
        out_specs=[blk, pl.BlockSpec((1, D_MODEL), lambda i: (0, 0))],
        out_shape=[jax.ShapeDtypeStruct((s, D_MODEL), F32), jax.ShapeDtypeStruct((1, D_MODEL), F32)],
        scratch_shapes=[], args=(du, w_full, x, g_row, dx_next), job=job)
    return own[0], own[1], extra


PG_B31, PG_LNG, PG_LNB, PG_B4, PG_BA, PG_BX, PG_LAM, PG_W4 = 0, 1, 2, 3, 4, 5, 6, 8
PG_ROWS = 16


def _mixer_bwd(u, kept, dy, hb, params, layer, t_rows):
    s = u.shape[0]
    nb = s // t_rows

    def body(u_ref, q_ref, h_ref, xc_ref, r_ref, ig_ref, a_ref, m_ref, sg_ref, sp_ref, szc_ref, szl_ref, dy_ref, hb_ref,
             w31_ref, b31_ref, lng_ref, lnb_ref, w4_ref, b4_ref, wa_ref, ba_ref, wx_ref, bx_ref, lam_ref,
             du_ref, pg_ref, dw31_ref, dwa_ref, dwx_ref,
             cbuf_ref, cs_ref, dc_ref, dqbuf_ref, dwacc_ref, dxcbuf_ref, acar_ref, gcar_ref, wb_ref):
        step = pl.program_id(0)

        @pl.when(step == 0)
        def _():
            _spread_taps(wb_ref, w31_ref)
            pg_ref[...] = jnp.zeros_like(pg_ref)
            dwa_ref[...] = jnp.zeros_like(dwa_ref)
            dwx_ref[...] = jnp.zeros_like(dwx_ref)
            dwacc_ref[...] = jnp.zeros_like(dwacc_ref)
            dqbuf_ref[pl.ds(t_rows, HALO), :] = jnp.zeros((HALO, D_BR), F32)
            dxcbuf_ref[pl.ds(t_rows, HALO4), :] = jnp.zeros((HALO4, D_BR), F32)
            acar_ref[...] = jnp.zeros_like(acar_ref)
            gcar_ref[...] = jnp.zeros_like(gcar_ref)

        def add_row(r, val):
            pg_ref[r:r + 1, :] += val

        v = u_ref[:, 0:D_BR]
        g = u_ref[:, D_BR:2 * D_BR]
        zc = u_ref[:, 2 * D_BR:3 * D_BR]
        dyc = dy_ref[:, 0:D_BR]
        sg = sg_ref[...]
        cbuf_ref[...] = v * sg
        ln_gv = lng_ref[...]
        n, rstd, p, _ = _layer_norm_swish(q_ref[...], ln_gv, lnb_ref[...], with_swish=False)
        sp = sp_ref[...]
        sz = szc_ref[...]
        du_ref[:, 2 * D_BR:3 * D_BR] = (dyc * (p * sp) * _dsilu(zc, sz)).astype(BF16)
        dp = dyc * (zc * sz) * _dsilu(p, sp)
        add_row(PG_LNG, _colsum(dp * n))
        add_row(PG_LNB, _colsum(dp))
        dn = dp * ln_gv
        dq = rstd * (dn - jnp.mean(dn, axis=-1, keepdims=True) - n * jnp.mean(dn * n, axis=-1, keepdims=True))
        add_row(PG_B31, _colsum(dq))
        dqbuf_ref[pl.ds(0, t_rows), :] = dq

        _shift_copies(cs_ref, dqbuf_ref[...])

        groups = ROW_CHUNK // SUBLANES

        def conv_chunk(r, carry):
            r0 = pl.multiple_of(r * ROW_CHUNK, ROW_CHUNK)
            cc = cbuf_ref[pl.ds(r0, ROW_CHUNK), :]
            accs = [jnp.zeros((SUBLANES, D_BR), F32) for _ in range(groups)]
            for k in range(KW):
                off = KW - 1 - k
                wv = wb_ref[k]
                ahead = cs_ref[off % 8, pl.ds(r0 + (off // 8) * 8, ROW_CHUNK), :]
                accs = [acc + wv * ahead[SUBLANES * g:SUBLANES * (g + 1)] for g, acc in enumerate(accs)]
                prod = cc * ahead
                part = prod[0:SUBLANES]
                for g in range(1, groups):
                    part = part + prod[SUBLANES * g:SUBLANES * (g + 1)]
                dwacc_ref[k] += part
            dc_ref[pl.ds(r0, ROW_CHUNK), :] = jnp.concatenate(accs, axis=0)
            return carry

        lax.fori_loop(0, t_rows // ROW_CHUNK, conv_chunk, 0)
        dqbuf_ref[pl.ds(t_rows, HALO), :] = dq[0:HALO]
        dc = dc_ref[...]
        du_ref[:, 0:D_BR] = (dc * sg).astype(BF16)
        du_ref[:, D_BR:2 * D_BR] = (dc * v * sg * (1.0 - sg)).astype(BF16)

        xl = u_ref[:, 3 * D_BR:4 * D_BR]
        zl = u_ref[:, 4 * D_BR:5 * D_BR]
        dyl = dy_ref[:, D_BR:2 * D_BR]
        xc = xc_ref[...]
        xc_bf = xc.astype(BF16)
        r = r_ref[...]
        ig = ig_ref[...]
        a = a_ref[...]
        m = m_ref[...]
        log_s = _log_sigmoid(lam_ref[...])
        row = lax.broadcasted_iota(jnp.int32, (t_rows, D_BR), 0)
        h = h_ref[...]
        h_prev = jnp.where(row >= 1, pltpu.roll(h, 1, 0), hb_ref[...])
        szl = szl_ref[...]
        du_ref[:, 4 * D_BR:5 * D_BR] = (dyl * h * _dsilu(zl, szl)).astype(BF16)
        a_next = jnp.where(row < t_rows - 1, pltpu.roll(a, t_rows - 1, 0), acar_ref[...])
        gs = _scan_rev(a_next, dyl * (zl * szl), gcar_ref[...])
        dc_ref[...] = gs
        gcar_ref[...] = dc_ref[pl.ds(0, 1), :]
        dc_ref[...] = a
        acar_ref[...] = dc_ref[pl.ds(0, 1), :]

        dm = gs * ig * xc
        di = gs * m * xc
        dla = gs * h_prev * a - dm * (a * a / m)
        add_row(PG_LAM, _colsum(dla * r) * LRU_C)
        dra = dla * (LRU_C * log_s) * r * (1.0 - r)
        dia = di * ig * (1.0 - ig)
        add_row(PG_BA, _colsum(dra))
        add_row(PG_BX, _colsum(dia))
        dra_bf = dra.astype(BF16)
        dia_bf = dia.astype(BF16)
        for hd in range(HEADS):
            sl = slice(hd * HD, (hd + 1) * HD)
            dwa_ref[hd] += lax.dot_general(xc_bf[:, sl], dra_bf[:, sl], (((0,), (0,)), ((), ())),
                                           preferred_element_type=F32)
            dwx_ref[hd] += lax.dot_general(xc_bf[:, sl], dia_bf[:, sl], (((0,), (0,)), ((), ())),
                                           preferred_element_type=F32)
        dxc = gs * m * ig + _heads_matmul_t(dra_bf, wa_ref) + _heads_matmul_t(dia_bf, wx_ref)
        add_row(PG_B4, _colsum(dxc))
        n4 = t_rows + HALO4
        add_row(PG_W4 + 3, _colsum(dxc * xl))
        dxcbuf_ref[pl.ds(0, t_rows), :] = dxc
        db = dxcbuf_ref[...]
        dxl = w4_ref[3:4, :] * dxc
        for k in range(KW4 - 1):
            ahead = pltpu.roll(db, n4 - (KW4 - 1 - k), 0)[0:t_rows]
            dxl = dxl + w4_ref[k:k + 1, :] * ahead
            add_row(PG_W4 + k, _colsum(xl * ahead))
        dxcbuf_ref[pl.ds(t_rows, HALO4), :] = dxc[0:HALO4]
        du_ref[:, 3 * D_BR:4 * D_BR] = dxl.astype(BF16)

        @pl.when(step == nb - 1)
        def _():
            pg_ref[PG_LAM:PG_LAM + 1, :] = pg_ref[PG_LAM:PG_LAM + 1, :] * _sig(-lam_ref[...])
            dw31_ref[...] = jnp.zeros_like(dw31_ref)
            for k in range(KW):
                dw31_ref[k:k + 1, :] = jnp.sum(dwacc_ref[k], axis=0, keepdims=True)

    const2 = lambda i: (0, 0)
    const3 = lambda i: (0, 0, 0)
    rev = lambda i: (nb - 1 - i, 0)
    return pl.pallas_call(
        body, name="mixer_bwd", grid=(nb,),
        in_specs=[pl.BlockSpec((t_rows, D_IN), rev)] + [pl.BlockSpec((t_rows, D_BR), rev)] * len(kept) + [
                  pl.BlockSpec((t_rows, 2 * D_BR), rev),
                  pl.BlockSpec((None, 1, D_BR), lambda i: (nb - 1 - i, 0, 0))] + _mixer_specs(layer),
        out_specs=[pl.BlockSpec((t_rows, D_IN), rev),
                   pl.BlockSpec((PG_ROWS, D_BR), const2), pl.BlockSpec((32, D_BR), const2),
                   pl.BlockSpec((HEADS, HD, HD), const3), pl.BlockSpec((HEADS, HD, HD), const3)],
        out_shape=[jax.ShapeDtypeStruct((s, D_IN), BF16), jax.ShapeDtypeStruct((PG_ROWS, D_BR), F32),
                   jax.ShapeDtypeStruct((32, D_BR), F32), jax.ShapeDtypeStruct((HEADS, HD, HD), F32),
                   jax.ShapeDtypeStruct((HEADS, HD, HD), F32)],
        scratch_shapes=[pltpu.VMEM((t_rows, D_BR), F32), pltpu.VMEM((8, t_rows + HALO, D_BR), F32),
                        pltpu.VMEM((t_rows, D_BR), F32), pltpu.VMEM((t_rows + HALO, D_BR), F32),
                        pltpu.VMEM((KW, 8, D_BR), F32),
                        pltpu.VMEM((t_rows + HALO4, D_BR), F32), pltpu.VMEM((1, D_BR), F32),
                        pltpu.VMEM((1, D_BR), F32), pltpu.VMEM((KW, SUBLANES, D_BR), F32)],
        compiler_params=_cparams(1))(u, *kept, dy, hb, *params)


def _add_kept_half(src, recv, keep, out_dtype, name):
    h, r, c = recv.shape
    tr = min(r, 1024)

    def body(keep_ref, s_ref, r_ref, o_ref):
        o_ref[...] = (s_ref[...].astype(F32) + r_ref[...].astype(F32)).astype(out_dtype)

    grid_spec = pltpu.PrefetchScalarGridSpec(
        num_scalar_prefetch=1, grid=(h, r // tr),
        in_specs=[pl.BlockSpec((None, tr, c), lambda b, i, kp: (kp[0] * h + b, i, 0)),
                  pl.BlockSpec((None, tr, c), lambda b, i, kp: (b, i, 0))],
        out_specs=pl.BlockSpec((None, tr, c), lambda b, i, kp: (b, i, 0)))
    return pl.pallas_call(
        body, name=name, grid_spec=grid_spec, out_shape=jax.ShapeDtypeStruct(recv.shape, out_dtype),
        compiler_params=_cparams(2))(keep, src, recv)


class _PendingReduce:
    STAGE_AXES = (2, 0, 1)

    def __init__(self, bufs):
        self.bufs = list(bufs)
        self.stage = 0

    def job(self):
        return _ExchangeJob(self.bufs, self.STAGE_AXES[self.stage])

    def absorb(self, recvs):
        me = _my_pos()[self.STAGE_AXES[self.stage]]
        keep = jnp.reshape(me, (1,)).astype(jnp.int32)
        last = self.stage == 2
        self.bufs = [_add_kept_half(b, r, keep, F32 if last else BF16, f"rs_add{self.stage}_{t}")
                     for t, (b, r) in enumerate(zip(self.bufs, recvs))]
        self.stage += 1

    def finish_alone(self):
        while self.stage < 3:
            job = self.job()
            self.absorb(_run_job(job, f"rs_exchange{self.stage}"))
        return [b[0] for b in self.bufs]


def _all_reduce_small(pa, pb, job):
    nt = job.nt

    def body(*refs):
        pa_ref, pb_ref = refs[:2]
        job_in = refs[2:2 + nt]
        oa_ref, ob_ref = refs[2 + nt:4 + nt]
        job_out = refs[4 + nt:4 + 2 * nt]
        ra0, ra1, ra2, sb0, sb1, sb2, rb0, rb1, rb2, send_sems, recv_sems = refs[4 + 2 * nt:15 + 2 * nt]
        job_scr = refs[15 + 2 * nt:]
        job.start(job_in, job_out, job_scr)
        x, y, c = _my_pos()
        peers = [(x, y, 1 - c), (1 - x, y, c), (x, 1 - y, c)]
        oa_ref[...] = pa_ref[...]
        ob_ref[...] = pb_ref[...]
        for k, (peer, ra, sb, rb) in enumerate(zip(peers, (ra0, ra1, ra2), (sb0, sb1, sb2), (rb0, rb1, rb2))):
            sb[...] = ob_ref[...].astype(BF16)
            copies = [pltpu.make_async_remote_copy(
                src_ref=src, dst_ref=dst, send_sem=send_sems.at[t, k], recv_sem=recv_sems.at[t, k],
                device_id=peer, device_id_type=MESH) for t, (src, dst) in enumerate(((oa_ref, ra), (sb, rb)))]
            for cp in copies:
                cp.start()
            for cp in copies:
                cp.wait()
            oa_ref[...] = oa_ref[...] + ra[...]
            ob_ref[...] = sb[...].astype(F32) + rb[...].astype(F32)
        job.finish(job_in, job_out, job_scr)

    vm = pl.BlockSpec(memory_space=pltpu.VMEM)
    outs = pl.pallas_call(
        body, name="small_all_reduce",
        out_shape=[jax.ShapeDtypeStruct(pa.shape, F32), jax.ShapeDtypeStruct(pb.shape, F32)] + job.out_shape,
        in_specs=[vm, vm] + job.in_specs, out_specs=[vm, vm] + job.out_specs,
        scratch_shapes=[pltpu.VMEM(pa.shape, F32)] * 3 + [pltpu.VMEM(pb.shape, BF16)] * 6
        + [pltpu.SemaphoreType.DMA((2, 3)), pltpu.SemaphoreType.DMA((2, 3))] + job.scratch,
        compiler_params=pltpu.CompilerParams(vmem_limit_bytes=VMEM_LIMIT))(pa, pb, *job.arrays)
    return outs[0], outs[1], list(outs[2:])


def _adamw(w, g, m, v, name):
    r, c = w.shape
    tr = r
    for cand in (512, 256, 128, 64, 32, 16, 8):
        if r % cand == 0 and cand * c * 4 <= (2 << 20):
            tr = cand
            break

    def body(w_ref, g_ref, m_ref, v_ref, d_ref, mo_ref, vo_ref):
        gv = g_ref[...]
        m_new = ADAM_B1 * m_ref[...] + (1.0 - ADAM_B1) * gv
        v_new = ADAM_B2 * v_ref[...] + (1.0 - ADAM_B2) * (gv * gv)
        m_hat = m_new / (1.0 - ADAM_B1 ** ADAM_STEP)
        v_hat = v_new / (1.0 - ADAM_B2 ** ADAM_STEP)
        d_ref[...] = -ADAM_LR * (m_hat / (jnp.sqrt(v_hat) + ADAM_EPS) + ADAM_WD * w_ref[...])
        mo_ref[...] = m_new
        vo_ref[...] = v_new

    spec = pl.BlockSpec((tr, c), lambda i: (i, 0))
    shape = jax.ShapeDtypeStruct((r, c), F32)
    return pl.pallas_call(
        body, name=name, grid=(r // tr,), in_specs=[spec] * 4, out_specs=[spec] * 3, out_shape=[shape] * 3,
        compiler_params=_cparams(1))(w, g, m, v)


def _pack_rows(parts):
    flat = jnp.concatenate([jnp.reshape(p, (-1, D_BR)) for p in parts], axis=0)
    pad = (-flat.shape[0]) % 64
    if pad:
        flat = jnp.concatenate([flat, jnp.zeros((pad, D_BR), F32)], axis=0)
    return flat


def _unpack_rows(flat, shapes):
    out, r0 = [], 0
    for shp in shapes:
        n = 1
        for d in shp:
            n *= d
        rows = n // D_BR
        out.append(jnp.reshape(flat[r0:r0 + rows], shp))
        r0 += rows
    return out


def kernel(x, norm_g, w_in, conv_dw_w, conv_dw_b, conv_ln_g, conv_ln_b, lru_conv_w, lru_conv_b, lru_wa, lru_ba, lru_wx, lru_bx, lru_lambda, w_out, final_g, loss_target, m_norm_g, m_w_in, m_conv_dw_w, m_conv_dw_b, m_conv_ln_g, m_conv_ln_b, m_lru_conv_w, m_lru_conv_b, m_lru_wa, m_lru_ba, m_lru_wx, m_lru_bx, m_lru_lambda, m_w_out, m_final_g, v_norm_g, v_w_in, v_conv_dw_w, v_conv_dw_b, v_conv_ln_g, v_conv_ln_b, v_lru_conv_w, v_lru_conv_b, v_lru_wa, v_lru_ba, v_lru_wx, v_lru_bx, v_lru_lambda, v_w_out, v_final_g):
    n_layers = norm_g.shape[0]
    s = x.shape[1]
    t_rows = min(s, 128)
    xs = jnp.reshape(x, (s, D_MODEL))
    target = jnp.reshape(loss_target, (s, D_MODEL))
    dev = 4 * lax.axis_index("x") + 2 * lax.axis_index("y") + lax.axis_index("c")

    w_in_bf = _cast_bf16(w_in, "cast_w_in")
    w_out_bf = _cast_bf16(w_out, "cast_w_out")
    h, (w_in_l, w31_all, w4_all) = _first_norm(xs, norm_g[0:1], _GatherJob([w_in_bf[0], conv_dw_w, lru_conv_w]))
    w_out_l = None
    w31_full = jnp.reshape(jnp.transpose(w31_all, (1, 2, 0, 3)), (n_layers, KW, D_BR))
    w4_full = jnp.reshape(jnp.transpose(w4_all, (1, 2, 0, 3)), (n_layers, KW4, D_BR))
    row3 = lambda p: jnp.reshape(p, (n_layers, 1, -1))
    mixer_params = (w31_full, row3(conv_dw_b), row3(conv_ln_g), row3(conv_ln_b), w4_full, row3(lru_conv_b),
                    lru_wa.astype(BF16), row3(lru_ba), lru_wx.astype(BF16), row3(lru_bx), row3(lru_lambda))

    saved = []
    act = xs
    for l in range(n_layers):
        wanted = [w_out_bf[0]] if l == 0 else []
        if l + 1 < n_layers:
            wanted += [w_in_bf[l + 1], w_out_bf[l + 1]]
        w_full = _w_in_rows(w_in_l)
        u, gathered = _in_proj(h, w_full, _GatherJob(wanted) if wanted else None)
        if l == 0:
            w_out_l, gathered = gathered[0], gathered[1:]
        y, q_sv, h_sv, hb, *gates_sv = _mixer_fwd(u, mixer_params, l, t_rows)
        kept = [q_sv, h_sv, *gates_sv]
        wo = jnp.reshape(w_out_l, (D_MODEL, D_MODEL))
        saved.append((act, h, u, y, kept, hb, w_full, wo))
        if l + 1 < n_layers:
            act, h = _out_proj(act, y, wo, norm_g[l + 1:l + 2])
            w_in_l, w_out_l = gathered
    loss_part, dx, d_final_g = _out_proj_loss_head(act, y, wo, jnp.reshape(final_g, (1, D_MODEL)), target)
    loss = lax.psum(loss_part[0, 0], AXES)

    pending = None
    reduced_big = [None] * n_layers
    small = [None] * n_layers
    for l in reversed(range(n_layers)):
        x_l, h, u, y, kept, hb, w_full, wo = saved[l]
        dy, dxb, recvs = _out_proj_bwd_x(dx, wo, pending.job() if pending else None)
        if pending:
            pending.absorb(recvs)
            reduced_big[l + 1] = [b[0] for b in pending.bufs]
        g_out, _ = _w_out_grad(y, dxb, None)
        du, pg, dw31, dwa, dwx = _mixer_bwd(u, kept, dy, hb, mixer_params, l, t_rows)
        g_in, _ = _w_in_grad(h, du, None)
        pending = _PendingReduce([g_in, g_out])
        pending.absorb(_run_job(pending.job(), "rs_exchange_c"))
        dx, d_norm, recvs = _in_proj_bwd_x(du, w_full, x_l, norm_g[l:l + 1], dx, pending.job())
        pending.absorb(recvs)
        small[l] = (d_norm, pg, dw31, dwa, dwx)
    grad_x = jnp.reshape(dx, x.shape)

    stack = lambda f: jnp.stack([f(small[l]) for l in range(n_layers)])
    pg_all = stack(lambda t: t[1])
    rep_parts = [
        (stack(lambda t: t[0][0]), norm_g.shape), (pg_all[:, PG_B31], conv_dw_b.shape),
        (pg_all[:, PG_LNG], conv_ln_g.shape), (pg_all[:, PG_LNB], conv_ln_b.shape),
        (pg_all[:, PG_B4], lru_conv_b.shape), (stack(lambda t: t[3]), lru_wa.shape), (pg_all[:, PG_BA], lru_ba.shape),
        (stack(lambda t: t[4]), lru_wx.shape), (pg_all[:, PG_BX], lru_bx.shape), (pg_all[:, PG_LAM], lru_lambda.shape),
        (d_final_g, final_g.shape)]
    shard_parts = [(stack(lambda t: t[2][0:KW]), (n_layers, KW, D_BR)),
                   (pg_all[:, PG_W4:PG_W4 + KW4], (n_layers, KW4, D_BR))]
    gate_w = (5, 7)
    f32_parts = [p for i, p in enumerate(rep_parts) if i not in gate_w] + shard_parts
    bf16_parts = [rep_parts[i] for i in gate_w]
    red_a, red_b, recvs = _all_reduce_small(
        _pack_rows([p for p, _ in f32_parts]), _pack_rows([p for p, _ in bf16_parts]), pending.job())
    pending.absorb(recvs)
    reduced_big[0] = [b[0] for b in pending.bufs]
    grad_w_in = jnp.stack([r[0] for r in reduced_big])
    grad_w_out = jnp.stack([r[1] for r in reduced_big])
    red_a = _unpack_rows(red_a, [shp for _, shp in f32_parts])
    red_b = _unpack_rows(red_b, [shp for _, shp in bf16_parts])
    rep_grads = red_a[:len(rep_parts) - len(gate_w)]
    for i, g in zip(gate_w, red_b):
        rep_grads.insert(i, g)
    grad_dw = lax.dynamic_slice_in_dim(red_a[-2], dev * HD, HD, axis=2)
    grad_w4 = lax.dynamic_slice_in_dim(red_a[-1], dev * HD, HD, axis=2)

    def adam_nd(w, g, m, v, name):
        two_d = (-1, w.shape[-1])
        outs = _adamw(*(jnp.reshape(t, two_d) for t in (w, g, m, v)), name)
        return [jnp.reshape(o, w.shape) for o in outs]

    upd = {}
    upd["w_in"] = adam_nd(w_in, grad_w_in, m_w_in, v_w_in, "adamw_w_in")
    upd["w_out"] = adam_nd(w_out, grad_w_out, m_w_out, v_w_out, "adamw_w_out")
    upd["conv_dw_w"] = adam_nd(conv_dw_w, grad_dw, m_conv_dw_w, v_conv_dw_w, "adamw_conv_dw_w")
    upd["lru_conv_w"] = adam_nd(lru_conv_w, grad_w4, m_lru_conv_w, v_lru_conv_w, "adamw_lru_conv_w")
    rep_w = [norm_g, conv_dw_b, conv_ln_g, conv_ln_b, lru_conv_b, lru_wa, lru_ba, lru_wx, lru_bx, lru_lambda, final_g]
    rep_m = [m_norm_g, m_conv_dw_b, m_conv_ln_g, m_conv_ln_b, m_lru_conv_b, m_lru_wa, m_lru_ba, m_lru_wx, m_lru_bx,
             m_lru_lambda, m_final_g]
    rep_v = [v_norm_g, v_conv_dw_b, v_conv_ln_g, v_conv_ln_b, v_lru_conv_b, v_lru_wa, v_lru_ba, v_lru_wx, v_lru_bx,
             v_lru_lambda, v_final_g]
    rep_keys = ["norm_g", "conv_dw_b", "conv_ln_g", "conv_ln_b", "lru_conv_b", "lru_wa", "lru_ba", "lru_wx", "lru_bx",
                "lru_lambda", "final_g"]
    grads = {"w_in": grad_w_in, "w_out": grad_w_out, "conv_dw_w": grad_dw, "lru_conv_w": grad_w4}
    for i, key in enumerate(rep_keys):
        grads[key] = rep_grads[i]
        upd[key] = adam_nd(rep_w[i], rep_grads[i], rep_m[i], rep_v[i], "adamw_" + key)

    order = ["norm_g", "w_in", "conv_dw_w", "conv_dw_b", "conv_ln_g", "conv_ln_b", "lru_conv_w", "lru_conv_b", "lru_wa",
             "lru_ba", "lru_wx", "lru_bx", "lru_lambda", "w_out", "final_g"]
    return (loss, grad_x, *[grads[k] for k in order], *[upd[k][0] for k in order], *[upd[k][1] for k in order],
            *[upd[k][2] for k in order])
```

```python
import functools

import jax
import jax.numpy as jnp
from jax import lax
from jax.experimental import pallas as pl
from jax.experimental.pallas import tpu as pltpu

F32 = jnp.float32
BF16 = jnp.bfloat16
MESH = pl.DeviceIdType.MESH
AXES = ("x", "y", "c")
N_DEV = 8

D_MODEL = 2048
D_BR = 1024
D_IN = 5 * D_BR
SHARD_IN = D_IN // N_DEV
SHARD_OUT = D_MODEL // N_DEV
KW = 31
KW4 = 4
HEADS = 8
HD = 128
LRU_C = 8.0
RMS_EPS = 1e-6
LN_EPS = 1e-5
SUBLANES = 8
HALO = 32
HALO4 = 8
ROW_CHUNK = 16

ADAM_LR = 0.001
ADAM_B1 = 0.9
ADAM_B2 = 0.999
ADAM_EPS = 1e-08
ADAM_WD = 0.01
ADAM_STEP = 10

VMEM_LIMIT = 60 * 1024 * 1024

ANY = pl.BlockSpec(memory_space=pl.ANY)


def _cparams(n_grid):
    return pltpu.CompilerParams(dimension_semantics=("arbitrary",) * n_grid, vmem_limit_bytes=VMEM_LIMIT)


def _resident(block_shape, index_map):
    return pl.BlockSpec(block_shape, index_map, pipeline_mode=pl.Buffered(1))


def _sig(x):
    return 0.5 * jnp.tanh(0.5 * x) + 0.5


def _dsilu(z, sz):
    return sz * (1.0 + z * (1.0 - sz))


def _expm1(x):
    small = jnp.abs(x) < 0.01
    series = x * (1.0 + x * (0.5 + x * (1.0 / 6.0 + x * (1.0 / 24.0))))
    return jnp.where(small, series, jnp.exp(x) - 1.0)


def _log_sigmoid(x):
    e = jnp.exp(-jnp.abs(x))
    l1p = jnp.where(e < 0.01, e * (1.0 - e * (0.5 - e * (1.0 / 3.0))), jnp.log(1.0 + e))
    return jnp.minimum(x, 0.0) - l1p


def _colsum(x):
    return jnp.sum(x, axis=0, keepdims=True)


def _my_pos():
    return lax.axis_index("x"), lax.axis_index("y"), lax.axis_index("c")


class _GatherJob:
    def __init__(self, shards):
        self.arrays = list(shards)
        nt = self.nt = len(self.arrays)
        self.in_specs = [ANY] * nt
        self.out_shape = [jax.ShapeDtypeStruct((N_DEV,) + s.shape, s.dtype) for s in self.arrays]
        self.out_specs = [ANY] * nt
        self.scratch = [pltpu.SemaphoreType.DMA((nt, 7)), pltpu.SemaphoreType.DMA((nt, 7)),
                        pltpu.SemaphoreType.DMA((nt,))]

    def _plan(self, srcs, outs, scr):
        send_sems, recv_sems, local_sems = scr
        x, y, c = _my_pos()
        me, sibling = (x, y, c), (x, y, 1 - c)
        chips = [(1 - x, y), (x, 1 - y), (1 - x, 1 - y)]

        def slot(p):
            return 4 * p[0] + 2 * p[1] + p[2]

        def copy(t, k, block, to, own=False):
            dst = outs[t].at[slot(block)]
            return pltpu.make_async_remote_copy(
                src_ref=srcs[t] if own else dst, dst_ref=dst, send_sem=send_sems.at[t, k], recv_sem=recv_sems.at[t, k],
                device_id=to, device_id_type=MESH)

        mine = [pltpu.make_async_copy(srcs[t], outs[t].at[slot(me)], local_sems.at[t]) for t in range(self.nt)]
        first = []
        for t in range(self.nt):
            first.append(copy(t, 0, me, sibling, own=True))
            first += [copy(t, 1 + j, me, (*chip, c), own=True) for j, chip in enumerate(chips)]
        return me, sibling, chips, c, copy, mine, first

    def start(self, srcs, outs, scr):
        _, _, _, _, _, mine, first = self._plan(srcs, outs, scr)
        for cp in mine + first:
            cp.start()

    def finish(self, srcs, outs, scr):
        me, sibling, chips, c, copy, mine, first = self._plan(srcs, outs, scr)
        passed = []
        for j, chip in enumerate(chips):
            for t in range(self.nt):
                copy(t, 1 + j, (*chip, c), me).wait_recv()
                fwd = copy(t, 4 + j, (*chip, c), sibling)
                fwd.start()
                passed.append(fwd)
        for t in range(self.nt):
            copy(t, 0, sibling, me).wait_recv()
            for j, chip in enumerate(chips):
                copy(t, 4 + j, (*chip, 1 - c), me).wait_recv()
        for cp in first + passed:
            cp.wait_send()
        for cp in mine:
            cp.wait()


class _ExchangeJob:
    def __init__(self, srcs, axis):
        self.arrays = list(srcs)
        self.axis = axis
        nt = self.nt = len(self.arrays)
        self.half = [s.shape[0] // 2 for s in self.arrays]
        self.in_specs = [ANY] * nt
        self.out_shape = [jax.ShapeDtypeStruct((h,) + s.shape[1:], s.dtype) for h, s in zip(self.half, self.arrays)]
        self.out_specs = [ANY] * nt
        self.scratch = [pltpu.SemaphoreType.DMA((nt,)), pltpu.SemaphoreType.DMA((nt,))]

    def _copies(self, srcs, outs, scr):
        send_sems, recv_sems = scr
        pos = list(_my_pos())
        me = pos[self.axis]
        pos[self.axis] = 1 - me
        return [pltpu.make_async_remote_copy(
            src_ref=srcs[t].at[pl.ds((1 - me) * self.half[t], self.half[t])], dst_ref=outs[t],
            send_sem=send_sems.at[t], recv_sem=recv_sems.at[t], device_id=tuple(pos), device_id_type=MESH)
            for t in range(self.nt)]

    def start(self, srcs, outs, scr):
        for cp in self._copies(srcs, outs, scr):
            cp.start()

    def finish(self, srcs, outs, scr):
        for cp in self._copies(srcs, outs, scr):
            cp.wait()


def _run_job(job, name):
    def body(*refs):
        ins, outs, scr = refs[:job.nt], refs[job.nt:2 * job.nt], refs[2 * job.nt:]
        job.start(ins, outs, scr)
        job.finish(ins, outs, scr)

    return pl.pallas_call(body, name=name, out_shape=job.out_shape, in_specs=job.in_specs, out_specs=job.out_specs,
                          scratch_shapes=job.scratch)(*job.arrays)


def _hosted_call(body, *, name, grid, in_specs, out_specs, out_shape, scratch_shapes, args, job):
    n_in, n_out, n_scr = len(in_specs), len(out_specs), len(scratch_shapes)
    if job is None:
        outs = pl.pallas_call(body, name=name, grid=grid, in_specs=in_specs, out_specs=out_specs, out_shape=out_shape,
                              scratch_shapes=scratch_shapes, compiler_params=_cparams(len(grid)))(*args)
        return list(outs), None
    nt = job.nt

    def full_body(*refs):
        own_in, job_in = refs[:n_in], refs[n_in:n_in + nt]
        base = n_in + nt
        own_out, job_out = refs[base:base + n_out], refs[base + n_out:base + n_out + nt]
        base += n_out + nt
        own_scr, job_scr = refs[base:base + n_scr], refs[base + n_scr:]
        ids = [pl.program_id(a) for a in range(len(grid))]
        is_first = functools.reduce(jnp.logical_and, [i == 0 for i in ids])
        is_last = functools.reduce(jnp.logical_and, [i == g - 1 for i, g in zip(ids, grid)])

        @pl.when(is_first)
        def _():
            job.start(job_in, job_out, job_scr)

        body(*own_in, *own_out, *own_scr)

        @pl.when(is_last)
        def _():
            job.finish(job_in, job_out, job_scr)

    outs = pl.pallas_call(
        full_body, name=name, grid=grid, in_specs=list(in_specs) + job.in_specs,
        out_specs=list(out_specs) + job.out_specs, out_shape=list(out_shape) + job.out_shape,
        scratch_shapes=list(scratch_shapes) + job.scratch, compiler_params=_cparams(len(grid)))(*args, *job.arrays)
    return list(outs[:n_out]), list(outs[n_out:])


def _cast_bf16(x, name):
    nl, r, c = x.shape
    tr = min(r, 512)

    def body(x_ref, o_ref):
        o_ref[...] = x_ref[...].astype(BF16)

    spec = pl.BlockSpec((None, tr, c), lambda l, i: (l, i, 0))
    return pl.pallas_call(
        body, name=name, grid=(nl, r // tr), in_specs=[spec], out_specs=spec,
        out_shape=jax.ShapeDtypeStruct(x.shape, BF16), compiler_params=_cparams(2))(x)


def _w_in_rows(w_all):
    def body(i_ref, o_ref):
        o_ref[...] = i_ref[...]

    return pl.pallas_call(
        body, name="w_in_rows", grid=(N_DEV,),
        in_specs=[pl.BlockSpec((None, D_MODEL, SHARD_IN), lambda j: (j, 0, 0))],
        out_specs=pl.BlockSpec((D_MODEL, SHARD_IN), lambda j: (0, j)),
        out_shape=jax.ShapeDtypeStruct((D_MODEL, D_IN), BF16), compiler_params=_cparams(1))(w_all)


def _in_proj(h, w_full, job):
    s = h.shape[0]
    tm = min(s, 2048)
    tn = 2 * SHARD_IN

    def body(h_ref, w_ref, u_ref):
        u_ref[...] = jnp.dot(h_ref[...], w_ref[...], preferred_element_type=F32)

    own, extra = _hosted_call(
        body, name="in_proj", grid=(s // tm, D_IN // tn),
        in_specs=[pl.BlockSpec((tm, D_MODEL), lambda i, j: (i, 0)),
                  pl.BlockSpec((D_MODEL, tn), lambda i, j: (0, j))],
        out_specs=[pl.BlockSpec((tm, tn), lambda i, j: (i, j))],
        out_shape=[jax.ShapeDtypeStruct((s, D_IN), F32)],
        scratch_shapes=[], args=(h, w_full), job=job)
    return own[0], extra


def _shift_copies(cs_ref, buf):
    n = buf.shape[0]
    cs_ref[0] = buf
    for sft in range(1, 8):
        cs_ref[sft] = pltpu.roll(buf, n - sft, 0)


def _spread_taps(wb_ref, w_ref):
    for k in range(KW):
        wb_ref[k] = jnp.broadcast_to(w_ref[k:k + 1, :], (SUBLANES, D_BR))


def _conv_taps(cs_ref, wb_ref, q_ref, t_rows, offs):
    groups = ROW_CHUNK // SUBLANES

    def chunk(r, carry):
        r0 = pl.multiple_of(r * ROW_CHUNK, ROW_CHUNK)
        accs = [jnp.zeros((SUBLANES, D_BR), F32) for _ in range(groups)]
        for k, off in enumerate(offs):
            wv = wb_ref[k]
            ahead = cs_ref[off % 8, pl.ds(r0 + (off // 8) * 8, ROW_CHUNK), :]
            accs = [acc + wv * ahead[SUBLANES * g:SUBLANES * (g + 1)] for g, acc in enumerate(accs)]
        q_ref[pl.ds(r0, ROW_CHUNK), :] = jnp.concatenate(accs, axis=0)
        return carry

    lax.fori_loop(0, t_rows // ROW_CHUNK, chunk, 0)


def _scan_fwd(a, b, h_in):
    t_rows = a.shape[0]
    row8 = lax.broadcasted_iota(jnp.int32, a.shape, 0) & (SUBLANES - 1)
    d = 1
    while d < SUBLANES:
        keep = row8 >= d
        a_s = jnp.where(keep, pltpu.roll(a, d, 0), 1.0)
        b_s = jnp.where(keep, pltpu.roll(b, d, 0), 0.0)
        b = a * b_s + b
        a = a * a_s
        d *= 2
    carry = h_in
    groups = []
    for grp in range(t_rows // SUBLANES):
        rows = slice(grp * SUBLANES, (grp + 1) * SUBLANES)
        h_g = b[rows] + a[rows] * carry
        groups.append(h_g)
        carry = h_g[SUBLANES - 1:SUBLANES]
    return jnp.concatenate(groups, axis=0)


def _scan_rev(a, b, g_in):
    t_rows = a.shape[0]
    row8 = lax.broadcasted_iota(jnp.int32, a.shape, 0) & (SUBLANES - 1)
    d = 1
    while d < SUBLANES:
        keep = row8 < SUBLANES - d
        a_s = jnp.where(keep, pltpu.roll(a, t_rows - d, 0), 1.0)
        b_s = jnp.where(keep, pltpu.roll(b, t_rows - d, 0), 0.0)
        b = a * b_s + b
        a = a * a_s
        d *= 2
    carry = g_in
    groups = []
    for grp in reversed(range(t_rows // SUBLANES)):
        rows = slice(grp * SUBLANES, (grp + 1) * SUBLANES)
        g_g = b[rows] + a[rows] * carry
        groups.append(g_g)
        carry = g_g[0:1]
    return jnp.concatenate(groups[::-1], axis=0)


def _heads_matmul(x_bf, w_ref):
    return jnp.concatenate(
        [jnp.dot(x_bf[:, h * HD:(h + 1) * HD], w_ref[h], preferred_element_type=F32) for h in range(HEADS)], axis=1)


def _heads_matmul_t(d_bf, w_ref):
    return jnp.concatenate(
        [lax.dot_general(d_bf[:, h * HD:(h + 1) * HD], w_ref[h], (((1,), (1,)), ((), ())), preferred_element_type=F32)
         for h in range(HEADS)], axis=1)


def _layer_norm_swish(q, ln_g, ln_b, with_swish=True):
    mu = jnp.mean(q, axis=-1, keepdims=True)
    xc = q - mu
    var = jnp.mean(xc * xc, axis=-1, keepdims=True)
    rstd = lax.rsqrt(var + LN_EPS)
    n = xc * rstd
    p = n * ln_g + ln_b
    return n, rstd, p, (_sig(p) if with_swish else None)


def _lru_gates(xl, xbuf_ref, w4_ref, b4, wa_ref, ba, wx_ref, bx, lam, t_rows):
    xbuf_ref[pl.ds(HALO4, t_rows), :] = xl
    xb = xbuf_ref[...]
    n = t_rows + HALO4
    xc = b4 + w4_ref[3:4, :] * xl
    for k in range(KW4 - 1):
        off = HALO4 - (KW4 - 1) + k
        xc = xc + w4_ref[k:k + 1, :] * pltpu.roll(xb, n - off, 0)[0:t_rows]
    xc_bf = xc.astype(BF16)
    r = _sig(_heads_matmul(xc_bf, wa_ref) + ba)
    ig = _sig(_heads_matmul(xc_bf, wx_ref) + bx)
    log_s = _log_sigmoid(lam)
    la = LRU_C * r * log_s
    a = jnp.exp(la)
    m = jnp.sqrt(-_expm1(2.0 * la))
    return xb, xc, xc_bf, r, ig, log_s, a, m


def _mixer_specs(layer):
    row1 = lambda i: (layer, 0, 0)
    heads = lambda i: (layer, 0, 0, 0)
    return [pl.BlockSpec((None, KW, D_BR), row1),
            pl.BlockSpec((None, 1, D_BR), row1), pl.BlockSpec((None, 1, D_BR), row1),
            pl.BlockSpec((None, 1, D_BR), row1),
            pl.BlockSpec((None, KW4, D_BR), row1), pl.BlockSpec((None, 1, D_BR), row1),
            pl.BlockSpec((None, HEADS, HD, HD), heads), pl.BlockSpec((None, 1, D_BR), row1),
            pl.BlockSpec((None, HEADS, HD, HD), heads), pl.BlockSpec((None, 1, D_BR), row1),
            pl.BlockSpec((None, 1, D_BR), row1)]


def _mixer_fwd(u, params, layer, t_rows):
    s = u.shape[0]
    nb = s // t_rows

    def body(u_ref, w31_ref, b31_ref, lng_ref, lnb_ref, w4_ref, b4_ref, wa_ref, ba_ref, wx_ref, bx_ref, lam_ref,
             y_ref, q_out_ref, h_out_ref, hb_ref, xc_out_ref, r_out_ref, ig_out_ref, a_out_ref, m_out_ref,
             sg_out_ref, sp_out_ref, szc_out_ref, szl_out_ref,
             cbuf_ref, cs_ref, xbuf_ref, hcar_ref, wb_ref):
        @pl.when(pl.program_id(0) == 0)
        def _():
            _spread_taps(wb_ref, w31_ref)
            cbuf_ref[pl.ds(0, HALO), :] = jnp.zeros((HALO, D_BR), F32)
            xbuf_ref[pl.ds(0, HALO4), :] = jnp.zeros((HALO4, D_BR), F32)
            hcar_ref[...] = jnp.zeros_like(hcar_ref)

        zc = u_ref[:, 2 * D_BR:3 * D_BR]
        sg = _sig(u_ref[:, D_BR:2 * D_BR])
        sg_out_ref[...] = sg
        c = u_ref[:, 0:D_BR] * sg
        cbuf_ref[pl.ds(HALO, t_rows), :] = c
        _shift_copies(cs_ref, cbuf_ref[...])
        _conv_taps(cs_ref, wb_ref, q_out_ref, t_rows, [HALO - (KW - 1) + k for k in range(KW)])
        cbuf_ref[pl.ds(0, HALO), :] = c[t_rows - HALO:t_rows]
        q = q_out_ref[...] + b31_ref[...]
        q_out_ref[...] = q
        _, _, p, sp = _layer_norm_swish(q, lng_ref[...], lnb_ref[...])
        sp_out_ref[...] = sp
        szc = _sig(zc)
        szc_out_ref[...] = szc
        y_ref[:, 0:D_BR] = (p * sp * (zc * szc)).astype(BF16)

        xl = u_ref[:, 3 * D_BR:4 * D_BR]
        zl = u_ref[:, 4 * D_BR:5 * D_BR]
        _, xc, _, r, ig, _, a, m = _lru_gates(xl, xbuf_ref, w4_ref, b4_ref[...], wa_ref, ba_ref[...], wx_ref,
                                              bx_ref[...], lam_ref[...], t_rows)
        xbuf_ref[pl.ds(0, HALO4), :] = xl[t_rows - HALO4:t_rows]
        xc_out_ref[...] = xc
        r_out_ref[...] = r
        ig_out_ref[...] = ig
        a_out_ref[...] = a
        m_out_ref[...] = m
        h_in = hcar_ref[...]
        hb_ref[...] = h_in
        h = _scan_fwd(a, m * (ig * xc), h_in)
        h_out_ref[...] = h
        hcar_ref[...] = h_out_ref[pl.ds(t_rows - 1, 1), :]
        szl = _sig(zl)
        szl_out_ref[...] = szl
        y_ref[:, D_BR:2 * D_BR] = (h * (zl * szl)).astype(BF16)

    blk = pl.BlockSpec((t_rows, D_BR), lambda i: (i, 0))
    return pl.pallas_call(
        body, name="mixer_fwd", grid=(nb,),
        in_specs=[pl.BlockSpec((t_rows, D_IN), lambda i: (i, 0))] + _mixer_specs(layer),
        out_specs=[pl.BlockSpec((t_rows, 2 * D_BR), lambda i: (i, 0)), blk, blk,
                   pl.BlockSpec((None, 1, D_BR), lambda i: (i, 0, 0))] + [blk] * 9,
        out_shape=[jax.ShapeDtypeStruct((s, 2 * D_BR), BF16), jax.ShapeDtypeStruct((s, D_BR), F32),
                   jax.ShapeDtypeStruct((s, D_BR), F32), jax.ShapeDtypeStruct((nb, 1, D_BR), F32)]
        + [jax.ShapeDtypeStruct((s, D_BR), F32)] * 9,
        scratch_shapes=[pltpu.VMEM((t_rows + HALO, D_BR), F32), pltpu.VMEM((8, t_rows + HALO, D_BR), F32),
                        pltpu.VMEM((t_rows + HALO4, D_BR), F32), pltpu.VMEM((1, D_BR), F32),
                        pltpu.VMEM((KW, SUBLANES, D_BR), F32)],
        compiler_params=_cparams(1))(u, *params)


def _rms_bf16(xf, g):
    rstd = lax.rsqrt(jnp.mean(xf * xf, axis=-1, keepdims=True) + RMS_EPS)
    return (xf * rstd * g).astype(BF16)


def _first_norm(x, g_row, job):
    s = x.shape[0]
    tm = min(s, 512)

    def body(x_ref, g_ref, h_ref):
        h_ref[...] = _rms_bf16(x_ref[...], g_ref[...])

    blk = pl.BlockSpec((tm, D_MODEL), lambda i: (i, 0))
    own, extra = _hosted_call(
        body, name="first_norm", grid=(s // tm,), in_specs=[blk, pl.BlockSpec((1, D_MODEL), lambda i: (0, 0))],
        out_specs=[blk], out_shape=[jax.ShapeDtypeStruct((s, D_MODEL), BF16)], scratch_shapes=[], args=(x, g_row),
        job=job)
    return own[0], extra


def _out_proj(x, y, wo, g_next):
    s = x.shape[0]
    tm = min(s, 512)

    def body(x_ref, y_ref, w_ref, g_ref, o_ref, h_ref):
        x_next = x_ref[...] + jnp.dot(y_ref[...], w_ref[...], preferred_element_type=F32)
        o_ref[...] = x_next
        h_ref[...] = _rms_bf16(x_next, g_ref[...])

    blk = pl.BlockSpec((tm, D_MODEL), lambda i: (i, 0))
    return pl.pallas_call(
        body, name="out_proj", grid=(s // tm,),
        in_specs=[blk, blk, _resident((D_MODEL, D_MODEL), lambda i: (0, 0)),
                  pl.BlockSpec((1, D_MODEL), lambda i: (0, 0))],
        out_specs=[blk, blk],
        out_shape=[jax.ShapeDtypeStruct((s, D_MODEL), F32), jax.ShapeDtypeStruct((s, D_MODEL), BF16)],
        compiler_params=_cparams(1))(x, y, wo, g_next)


def _out_proj_loss_head(x, y, wo, g_row, target):
    s = x.shape[0]
    tm = min(s, 512)

    def body(x_ref, y_ref, w_ref, g_ref, t_ref, loss_ref, dx_ref, dg_ref):
        @pl.when(pl.program_id(0) == 0)
        def _():
            loss_ref[...] = jnp.zeros_like(loss_ref)
            dg_ref[...] = jnp.zeros_like(dg_ref)

        xf = x_ref[...] + jnp.dot(y_ref[...], w_ref[...], preferred_element_type=F32)
        g = g_ref[...]
        rstd = lax.rsqrt(jnp.mean(xf * xf, axis=-1, keepdims=True) + RMS_EPS)
        n = xf * rstd
        err = n * g - t_ref[...]
        loss_ref[...] += 0.5 * jnp.sum(jnp.mean(err * err, axis=-1, keepdims=True))
        dy = err * (1.0 / D_MODEL)
        dg_ref[...] += _colsum(dy * n)
        dn = dy * g
        dx_ref[...] = rstd * (dn - n * jnp.mean(dn * n, axis=-1, keepdims=True))

    blk = pl.BlockSpec((tm, D_MODEL), lambda i: (i, 0))
    return pl.pallas_call(
        body, name="out_proj_loss_head", grid=(s // tm,),
        in_specs=[blk, blk, _resident((D_MODEL, D_MODEL), lambda i: (0, 0)),
                  pl.BlockSpec((1, D_MODEL), lambda i: (0, 0)), blk],
        out_specs=[pl.BlockSpec((8, 128), lambda i: (0, 0)), blk, pl.BlockSpec((1, D_MODEL), lambda i: (0, 0))],
        out_shape=[jax.ShapeDtypeStruct((8, 128), F32), jax.ShapeDtypeStruct((s, D_MODEL), F32),
                   jax.ShapeDtypeStruct((1, D_MODEL), F32)],
        compiler_params=_cparams(1))(x, y, wo, g_row, target)


def _out_proj_bwd_x(dx, wo, job):
    s = dx.shape[0]
    tm = min(s, 512)

    def body(dx_ref, w_ref, dy_ref, dxb_ref):
        dxb = dx_ref[...].astype(BF16)
        dxb_ref[...] = dxb
        dy_ref[...] = lax.dot_general(dxb, w_ref[...], (((1,), (1,)), ((), ())), preferred_element_type=F32)

    blk = pl.BlockSpec((tm, D_MODEL), lambda i: (i, 0))
    own, extra = _hosted_call(
        body, name="out_proj_bwd_x", grid=(s // tm,),
        in_specs=[blk, _resident((D_MODEL, D_MODEL), lambda i: (0, 0))],
        out_specs=[blk, blk],
        out_shape=[jax.ShapeDtypeStruct((s, D_MODEL), F32), jax.ShapeDtypeStruct((s, D_MODEL), BF16)],
        scratch_shapes=[], args=(dx, wo), job=job)
    return own[0], own[1], extra


def _w_in_grad(h, du, job):
    s = h.shape[0]
    tk = min(s, 2048)
    nk = s // tk

    def body(h_ref, du_ref, o_ref, acc_ref):
        k = pl.program_id(1)

        @pl.when(k == 0)
        def _():
            acc_ref[...] = jnp.zeros_like(acc_ref)

        acc_ref[...] += lax.dot_general(h_ref[...], du_ref[...], (((0,), (0,)), ((), ())), preferred_element_type=F32)

        @pl.when(k == nk - 1)
        def _():
            o_ref[0] = acc_ref[:, 0:SHARD_IN].astype(BF16)
            o_ref[1] = acc_ref[:, SHARD_IN:2 * SHARD_IN].astype(BF16)

    own, extra = _hosted_call(
        body, name="w_in_grad", grid=(N_DEV // 2, nk),
        in_specs=[pl.BlockSpec((tk, D_MODEL), lambda q, k: (k, 0)),
                  pl.BlockSpec((tk, 2 * SHARD_IN), lambda q, k: (k, q))],
        out_specs=[pl.BlockSpec((2, None, D_MODEL, SHARD_IN), lambda q, k: (0, q, 0, 0))],
        out_shape=[jax.ShapeDtypeStruct((2, N_DEV // 2, D_MODEL, SHARD_IN), BF16)],
        scratch_shapes=[pltpu.VMEM((D_MODEL, 2 * SHARD_IN), F32)], args=(h, du), job=job)
    return jnp.reshape(own[0], (N_DEV, D_MODEL, SHARD_IN)), extra


def _w_out_grad(y, dxb, job):
    s = y.shape[0]
    tk = min(s, 2048)
    nk = s // tk
    tn = 1024

    def body(y_ref, dx_ref, o_ref, acc_ref):
        k = pl.program_id(1)

        @pl.when(k == 0)
        def _():
            acc_ref[...] = jnp.zeros_like(acc_ref)

        acc_ref[...] += lax.dot_general(y_ref[...], dx_ref[...], (((0,), (0,)), ((), ())), preferred_element_type=F32)

        @pl.when(k == nk - 1)
        def _():
            for j in range(N_DEV):
                slot = (j % 2) * 4 + j // 2
                o_ref[slot] = acc_ref[pl.ds(j * SHARD_OUT, SHARD_OUT), :].astype(BF16)

    own, extra = _hosted_call(
        body, name="w_out_grad", grid=(D_MODEL // tn, nk),
        in_specs=[pl.BlockSpec((tk, D_MODEL), lambda n, k: (k, 0)),
                  pl.BlockSpec((tk, tn), lambda n, k: (k, n))],
        out_specs=[pl.BlockSpec((N_DEV, SHARD_OUT, tn), lambda n, k: (0, 0, n))],
        out_shape=[jax.ShapeDtypeStruct((N_DEV, SHARD_OUT, D_MODEL), BF16)],
        scratch_shapes=[pltpu.VMEM((D_MODEL, tn), F32)], args=(y, dxb), job=job)
    return own[0], extra


def _in_proj_bwd_x(du, w_full, x, g_row, dx_next, job):
    s = x.shape[0]
    tm = min(s, 256)

    def body(du_ref, w_ref, x_ref, g_ref, dxn_ref, dx_ref, dg_ref):
        @pl.when(pl.program_id(0) == 0)
        def _():
            dg_ref[...] = jnp.zeros_like(dg_ref)

        dh = lax.dot_general(w_ref[...], du_ref[...], (((1,), (1,)), ((), ())), preferred_element_type=F32).T
        xf = x_ref[...]
        rstd = lax.rsqrt(jnp.mean(xf * xf, axis=-1, keepdims=True) + RMS_EPS)
        n = xf * rstd
        dg_ref[...] += _colsum(dh * n)
        dn = dh * g_ref[...]
        dx_ref[...] = dxn_ref[...] + rstd * (dn - n * jnp.mean(dn * n, axis=-1, keepdims=True))

    blk = pl.BlockSpec((tm, D_MODEL), lambda i: (i, 0))
    own, extra = _hosted_call(
        body, name="in_proj_bwd_x", grid=(s // tm,),
        in_specs=[pl.BlockSpec((tm, D_IN), lambda i: (i, 0)), _resident((D_MODEL, D_IN), lambda i: (0, 0)),
                  blk, pl.BlockSpec((1, D_MODEL), lambda i: (0, 0)), blk],
        out_specs=[blk, pl.BlockSpec((1, D_MODEL), lambda i: (0, 0))],
        out_shape=[jax.ShapeDtypeStruct((s, D_MODEL), F32), jax.ShapeDtypeStruct((1, D_MODEL), F32)],
        scratch_shapes=[], args=(du, w_full, x, g_row, dx_next), job=job)
    return own[0], own[1], extra


PG_B31, PG_LNG, PG_LNB, PG_B4, PG_BA, PG_BX, PG_LAM, PG_W4 = 0, 1, 2, 3, 4, 5, 6, 8
PG_ROWS = 16


def _mixer_bwd(u, kept, dy, hb, params, layer, t_rows):
    s = u.shape[0]
    nb = s // t_rows

    def body(u_ref, q_ref, h_ref, xc_ref, r_ref, ig_ref, a_ref, m_ref, sg_ref, sp_ref, szc_ref, szl_ref, dy_ref, hb_ref,
             w31_ref, b31_ref, lng_ref, lnb_ref, w4_ref, b4_ref, wa_ref, ba_ref, wx_ref, bx_ref, lam_ref,
             du_ref, pg_ref, dw31_ref, dwa_ref, dwx_ref,
             cbuf_ref, cs_ref, dc_ref, dqbuf_ref, dwacc_ref, dxcbuf_ref, acar_ref, gcar_ref, wb_ref):
        step = pl.program_id(0)

        @pl.when(step == 0)
        def _():
            _spread_taps(wb_ref, w31_ref)
            pg_ref[...] = jnp.zeros_like(pg_ref)
            dwa_ref[...] = jnp.zeros_like(dwa_ref)
            dwx_ref[...] = jnp.zeros_like(dwx_ref)
            dwacc_ref[...] = jnp.zeros_like(dwacc_ref)
            dqbuf_ref[pl.ds(t_rows, HALO), :] = jnp.zeros((HALO, D_BR), F32)
            dxcbuf_ref[pl.ds(t_rows, HALO4), :] = jnp.zeros((HALO4, D_BR), F32)
            acar_ref[...] = jnp.zeros_like(acar_ref)
            gcar_ref[...] = jnp.zeros_like(gcar_ref)

        def add_row(r, val):
            pg_ref[r:r + 1, :] += val

        v = u_ref[:, 0:D_BR]
        g = u_ref[:, D_BR:2 * D_BR]
        zc = u_ref[:, 2 * D_BR:3 * D_BR]
        dyc = dy_ref[:, 0:D_BR]
        sg = sg_ref[...]
        cbuf_ref[...] = v * sg
        ln_gv = lng_ref[...]
        n, rstd, p, _ = _layer_norm_swish(q_ref[...], ln_gv, lnb_ref[...], with_swish=False)
        sp = sp_ref[...]
        sz = szc_ref[...]
        du_ref[:, 2 * D_BR:3 * D_BR] = (dyc * (p * sp) * _dsilu(zc, sz)).astype(BF16)
        dp = dyc * (zc * sz) * _dsilu(p, sp)
        add_row(PG_LNG, _colsum(dp * n))
        add_row(PG_LNB, _colsum(dp))
        dn = dp * ln_gv
        dq = rstd * (dn - jnp.mean(dn, axis=-1, keepdims=True) - n * jnp.mean(dn * n, axis=-1, keepdims=True))
        add_row(PG_B31, _colsum(dq))
        dqbuf_ref[pl.ds(0, t_rows), :] = dq

        _shift_copies(cs_ref, dqbuf_ref[...])

        groups = ROW_CHUNK // SUBLANES

        def conv_chunk(r, carry):
            r0 = pl.multiple_of(r * ROW_CHUNK, ROW_CHUNK)
            cc = cbuf_ref[pl.ds(r0, ROW_CHUNK), :]
            accs = [jnp.zeros((SUBLANES, D_BR), F32) for _ in range(groups)]
            for k in range(KW):
                off = KW - 1 - k
                wv = wb_ref[k]
                ahead = cs_ref[off % 8, pl.ds(r0 + (off // 8) * 8, ROW_CHUNK), :]
                accs = [acc + wv * ahead[SUBLANES * g:SUBLANES * (g + 1)] for g, acc in enumerate(accs)]
                prod = cc * ahead
                part = prod[0:SUBLANES]
                for g in range(1, groups):
                    part = part + prod[SUBLANES * g:SUBLANES * (g + 1)]
                dwacc_ref[k] += part
            dc_ref[pl.ds(r0, ROW_CHUNK), :] = jnp.concatenate(accs, axis=0)
            return carry

        lax.fori_loop(0, t_rows // ROW_CHUNK, conv_chunk, 0)
        dqbuf_ref[pl.ds(t_rows, HALO), :] = dq[0:HALO]
        dc = dc_ref[...]
        du_ref[:, 0:D_BR] = (dc * sg).astype(BF16)
        du_ref[:, D_BR:2 * D_BR] = (dc * v * sg * (1.0 - sg)).astype(BF16)

        xl = u_ref[:, 3 * D_BR:4 * D_BR]
        zl = u_ref[:, 4 * D_BR:5 * D_BR]
        dyl = dy_ref[:, D_BR:2 * D_BR]
        xc = xc_ref[...]
        xc_bf = xc.astype(BF16)
        r = r_ref[...]
        ig = ig_ref[...]
        a = a_ref[...]
        m = m_ref[...]
        log_s = _log_sigmoid(lam_ref[...])
        row = lax.broadcasted_iota(jnp.int32, (t_rows, D_BR), 0)
        h = h_ref[...]
        h_prev = jnp.where(row >= 1, pltpu.roll(h, 1, 0), hb_ref[...])
        szl = szl_ref[...]
        du_ref[:, 4 * D_BR:5 * D_BR] = (dyl * h * _dsilu(zl, szl)).astype(BF16)
        a_next = jnp.where(row < t_rows - 1, pltpu.roll(a, t_rows - 1, 0), acar_ref[...])
        gs = _scan_rev(a_next, dyl * (zl * szl), gcar_ref[...])
        dc_ref[...] = gs
        gcar_ref[...] = dc_ref[pl.ds(0, 1), :]
        dc_ref[...] = a
        acar_ref[...] = dc_ref[pl.ds(0, 1), :]

        dm = gs * ig * xc
        di = gs * m * xc
        dla = gs * h_prev * a - dm * (a * a / m)
        add_row(PG_LAM, _colsum(dla * r) * LRU_C)
        dra = dla * (LRU_C * log_s) * r * (1.0 - r)
        dia = di * ig * (1.0 - ig)
        add_row(PG_BA, _colsum(dra))
        add_row(PG_BX, _colsum(dia))
        dra_bf = dra.astype(BF16)
        dia_bf = dia.astype(BF16)
        for hd in range(HEADS):
            sl = slice(hd * HD, (hd + 1) * HD)
            dwa_ref[hd] += lax.dot_general(xc_bf[:, sl], dra_bf[:, sl], (((0,), (0,)), ((), ())),
                                           preferred_element_type=F32)
            dwx_ref[hd] += lax.dot_general(xc_bf[:, sl], dia_bf[:, sl], (((0,), (0,)), ((), ())),
                                           preferred_element_type=F32)
        dxc = gs * m * ig + _heads_matmul_t(dra_bf, wa_ref) + _heads_matmul_t(dia_bf, wx_ref)
        add_row(PG_B4, _colsum(dxc))
        n4 = t_rows + HALO4
        add_row(PG_W4 + 3, _colsum(dxc * xl))
        dxcbuf_ref[pl.ds(0, t_rows), :] = dxc
        db = dxcbuf_ref[...]
        dxl = w4_ref[3:4, :] * dxc
        for k in range(KW4 - 1):
            ahead = pltpu.roll(db, n4 - (KW4 - 1 - k), 0)[0:t_rows]
            dxl = dxl + w4_ref[k:k + 1, :] * ahead
            add_row(PG_W4 + k, _colsum(xl * ahead))
        dxcbuf_ref[pl.ds(t_rows, HALO4), :] = dxc[0:HALO4]
        du_ref[:, 3 * D_BR:4 * D_BR] = dxl.astype(BF16)

        @pl.when(step == nb - 1)
        def _():
            pg_ref[PG_LAM:PG_LAM + 1, :] = pg_ref[PG_LAM:PG_LAM + 1, :] * _sig(-lam_ref[...])
            dw31_ref[...] = jnp.zeros_like(dw31_ref)
            for k in range(KW):
                dw31_ref[k:k + 1, :] = jnp.sum(dwacc_ref[k], axis=0, keepdims=True)

    const2 = lambda i: (0, 0)
    const3 = lambda i: (0, 0, 0)
    rev = lambda i: (nb - 1 - i, 0)
    return pl.pallas_call(
        body, name="mixer_bwd", grid=(nb,),
        in_specs=[pl.BlockSpec((t_rows, D_IN), rev)] + [pl.BlockSpec((t_rows, D_BR), rev)] * len(kept) + [
                  pl.BlockSpec((t_rows, 2 * D_BR), rev),
                  pl.BlockSpec((None, 1, D_BR), lambda i: (nb - 1 - i, 0, 0))] + _mixer_specs(layer),
        out_specs=[pl.BlockSpec((t_rows, D_IN), rev),
                   pl.BlockSpec((PG_ROWS, D_BR), const2), pl.BlockSpec((32, D_BR), const2),
                   pl.BlockSpec((HEADS, HD, HD), const3), pl.BlockSpec((HEADS, HD, HD), const3)],
        out_shape=[jax.ShapeDtypeStruct((s, D_IN), BF16), jax.ShapeDtypeStruct((PG_ROWS, D_BR), F32),
                   jax.ShapeDtypeStruct((32, D_BR), F32), jax.ShapeDtypeStruct((HEADS, HD, HD), F32),
                   jax.ShapeDtypeStruct((HEADS, HD, HD), F32)],
        scratch_shapes=[pltpu.VMEM((t_rows, D_BR), F32), pltpu.VMEM((8, t_rows + HALO, D_BR), F32),
                        pltpu.VMEM((t_rows, D_BR), F32), pltpu.VMEM((t_rows + HALO, D_BR), F32),
                        pltpu.VMEM((KW, 8, D_BR), F32),
                        pltpu.VMEM((t_rows + HALO4, D_BR), F32), pltpu.VMEM((1, D_BR), F32),
                        pltpu.VMEM((1, D_BR), F32), pltpu.VMEM((KW, SUBLANES, D_BR), F32)],
        compiler_params=_cparams(1))(u, *kept, dy, hb, *params)


def _add_kept_half(src, recv, keep, out_dtype, name):
    h, r, c = recv.shape
    tr = min(r, 1024)

    def body(keep_ref, s_ref, r_ref, o_ref):
        o_ref[...] = (s_ref[...].astype(F32) + r_ref[...].astype(F32)).astype(out_dtype)

    grid_spec = pltpu.PrefetchScalarGridSpec(
        num_scalar_prefetch=1, grid=(h, r // tr),
        in_specs=[pl.BlockSpec((None, tr, c), lambda b, i, kp: (kp[0] * h + b, i, 0)),
                  pl.BlockSpec((None, tr, c), lambda b, i, kp: (b, i, 0))],
        out_specs=pl.BlockSpec((None, tr, c), lambda b, i, kp: (b, i, 0)))
    return pl.pallas_call(
        body, name=name, grid_spec=grid_spec, out_shape=jax.ShapeDtypeStruct(recv.shape, out_dtype),
        compiler_params=_cparams(2))(keep, src, recv)


class _PendingReduce:
    STAGE_AXES = (2, 0, 1)

    def __init__(self, bufs):
        self.bufs = list(bufs)
        self.stage = 0

    def job(self):
        return _ExchangeJob(self.bufs, self.STAGE_AXES[self.stage])

    def absorb(self, recvs):
        me = _my_pos()[self.STAGE_AXES[self.stage]]
        keep = jnp.reshape(me, (1,)).astype(jnp.int32)
        last = self.stage == 2
        self.bufs = [_add_kept_half(b, r, keep, F32 if last else BF16, f"rs_add{self.stage}_{t}")
                     for t, (b, r) in enumerate(zip(self.bufs, recvs))]
        self.stage += 1


def _all_reduce_small(pa, pb, job):
    nt = job.nt

    def body(*refs):
        pa_ref, pb_ref = refs[:2]
        job_in = refs[2:2 + nt]
        oa_ref, ob_ref = refs[2 + nt:4 + nt]
        job_out = refs[4 + nt:4 + 2 * nt]
        ra0, ra1, ra2, sb0, sb1, sb2, rb0, rb1, rb2, send_sems, recv_sems = refs[4 + 2 * nt:15 + 2 * nt]
        job_scr = refs[15 + 2 * nt:]
        job.start(job_in, job_out, job_scr)
        x, y, c = _my_pos()
        peers = [(x, y, 1 - c), (1 - x, y, c), (x, 1 - y, c)]
        oa_ref[...] = pa_ref[...]
        ob_ref[...] = pb_ref[...]
        for k, (peer, ra, sb, rb) in enumerate(zip(peers, (ra0, ra1, ra2), (sb0, sb1, sb2), (rb0, rb1, rb2))):
            sb[...] = ob_ref[...].astype(BF16)
            copies = [pltpu.make_async_remote_copy(
                src_ref=src, dst_ref=dst, send_sem=send_sems.at[t, k], recv_sem=recv_sems.at[t, k],
                device_id=peer, device_id_type=MESH) for t, (src, dst) in enumerate(((oa_ref, ra), (sb, rb)))]
            for cp in copies:
                cp.start()
            for cp in copies:
                cp.wait()
            oa_ref[...] = oa_ref[...] + ra[...]
            ob_ref[...] = sb[...].astype(F32) + rb[...].astype(F32)
        job.finish(job_in, job_out, job_scr)

    vm = pl.BlockSpec(memory_space=pltpu.VMEM)
    outs = pl.pallas_call(
        body, name="small_all_reduce",
        out_shape=[jax.ShapeDtypeStruct(pa.shape, F32), jax.ShapeDtypeStruct(pb.shape, F32)] + job.out_shape,
        in_specs=[vm, vm] + job.in_specs, out_specs=[vm, vm] + job.out_specs,
        scratch_shapes=[pltpu.VMEM(pa.shape, F32)] * 3 + [pltpu.VMEM(pb.shape, BF16)] * 6
        + [pltpu.SemaphoreType.DMA((2, 3)), pltpu.SemaphoreType.DMA((2, 3))] + job.scratch,
        compiler_params=pltpu.CompilerParams(vmem_limit_bytes=VMEM_LIMIT))(pa, pb, *job.arrays)
    return outs[0], outs[1], list(outs[2:])


def _adamw(w, g, m, v, name):
    r, c = w.shape
    tr = r
    for cand in (512, 256, 128, 64, 32, 16, 8):
        if r % cand == 0 and cand * c * 4 <= (2 << 20):
            tr = cand
            break

    def body(w_ref, g_ref, m_ref, v_ref, d_ref, mo_ref, vo_ref):
        gv = g_ref[...]
        m_new = ADAM_B1 * m_ref[...] + (1.0 - ADAM_B1) * gv
        v_new = ADAM_B2 * v_ref[...] + (1.0 - ADAM_B2) * (gv * gv)
        m_hat = m_new / (1.0 - ADAM_B1 ** ADAM_STEP)
        v_hat = v_new / (1.0 - ADAM_B2 ** ADAM_STEP)
        d_ref[...] = -ADAM_LR * (m_hat / (jnp.sqrt(v_hat) + ADAM_EPS) + ADAM_WD * w_ref[...])
        mo_ref[...] = m_new
        vo_ref[...] = v_new

    spec = pl.BlockSpec((tr, c), lambda i: (i, 0))
    shape = jax.ShapeDtypeStruct((r, c), F32)
    return pl.pallas_call(
        body, name=name, grid=(r // tr,), in_specs=[spec] * 4, out_specs=[spec] * 3, out_shape=[shape] * 3,
        compiler_params=_cparams(1))(w, g, m, v)


def _pack_rows(parts):
    flat = jnp.concatenate([jnp.reshape(p, (-1, D_BR)) for p in parts], axis=0)
    pad = (-flat.shape[0]) % 64
    if pad:
        flat = jnp.concatenate([flat, jnp.zeros((pad, D_BR), F32)], axis=0)
    return flat


def _unpack_rows(flat, shapes):
    out, r0 = [], 0
    for shp in shapes:
        n = 1
        for d in shp:
            n *= d
        rows = n // D_BR
        out.append(jnp.reshape(flat[r0:r0 + rows], shp))
        r0 += rows
    return out


def kernel(x, norm_g, w_in, conv_dw_w, conv_dw_b, conv_ln_g, conv_ln_b, lru_conv_w, lru_conv_b, lru_wa, lru_ba, lru_wx, lru_bx, lru_lambda, w_out, final_g, loss_target, m_norm_g, m_w_in, m_conv_dw_w, m_conv_dw_b, m_conv_ln_g, m_conv_ln_b, m_lru_conv_w, m_lru_conv_b, m_lru_wa, m_lru_ba, m_lru_wx, m_lru_bx, m_lru_lambda, m_w_out, m_final_g, v_norm_g, v_w_in, v_conv_dw_w, v_conv_dw_b, v_conv_ln_g, v_conv_ln_b, v_lru_conv_w, v_lru_conv_b, v_lru_wa, v_lru_ba, v_lru_wx, v_lru_bx, v_lru_lambda, v_w_out, v_final_g):
    n_layers = norm_g.shape[0]
    s = x.shape[1]
    t_rows = min(s, 64)
    xs = jnp.reshape(x, (s, D_MODEL))
    target = jnp.reshape(loss_target, (s, D_MODEL))
    dev = 4 * lax.axis_index("x") + 2 * lax.axis_index("y") + lax.axis_index("c")

    w_in_bf = _cast_bf16(w_in, "cast_w_in")
    w_out_bf = _cast_bf16(w_out, "cast_w_out")
    h, (w_in_l, w31_all, w4_all) = _first_norm(xs, norm_g[0:1], _GatherJob([w_in_bf[0], conv_dw_w, lru_conv_w]))
    w_out_l = None
    w31_full = jnp.reshape(jnp.transpose(w31_all, (1, 2, 0, 3)), (n_layers, KW, D_BR))
    w4_full = jnp.reshape(jnp.transpose(w4_all, (1, 2, 0, 3)), (n_layers, KW4, D_BR))
    row3 = lambda p: jnp.reshape(p, (n_layers, 1, -1))
    mixer_params = (w31_full, row3(conv_dw_b), row3(conv_ln_g), row3(conv_ln_b), w4_full, row3(lru_conv_b),
                    lru_wa.astype(BF16), row3(lru_ba), lru_wx.astype(BF16), row3(lru_bx), row3(lru_lambda))

    saved = []
    act = xs
    for l in range(n_layers):
        wanted = [w_out_bf[0]] if l == 0 else []
        if l + 1 < n_layers:
            wanted += [w_in_bf[l + 1], w_out_bf[l + 1]]
        w_full = _w_in_rows(w_in_l)
        u, gathered = _in_proj(h, w_full, _GatherJob(wanted) if wanted else None)
        if l == 0:
            w_out_l, gathered = gathered[0], gathered[1:]
        y, q_sv, h_sv, hb, *gates_sv = _mixer_fwd(u, mixer_params, l, t_rows)
        kept = [q_sv, h_sv, *gates_sv]
        wo = jnp.reshape(w_out_l, (D_MODEL, D_MODEL))
        saved.append((act, h, u, y, kept, hb, w_full, wo))
        if l + 1 < n_layers:
            act, h = _out_proj(act, y, wo, norm_g[l + 1:l + 2])
            w_in_l, w_out_l = gathered
    loss_part, dx, d_final_g = _out_proj_loss_head(act, y, wo, jnp.reshape(final_g, (1, D_MODEL)), target)
    loss = lax.psum(loss_part[0, 0], AXES)

    pending = None
    reduced_big = [None] * n_layers
    small = [None] * n_layers
    for l in reversed(range(n_layers)):
        x_l, h, u, y, kept, hb, w_full, wo = saved[l]
        dy, dxb, recvs = _out_proj_bwd_x(dx, wo, pending.job() if pending else None)
        if pending:
            pending.absorb(recvs)
            reduced_big[l + 1] = [b[0] for b in pending.bufs]
        g_out, _ = _w_out_grad(y, dxb, None)
        du, pg, dw31, dwa, dwx = _mixer_bwd(u, kept, dy, hb, mixer_params, l, t_rows)
        g_in, _ = _w_in_grad(h, du, None)
        pending = _PendingReduce([g_in, g_out])
        pending.absorb(_run_job(pending.job(), "rs_exchange_c"))
        dx, d_norm, recvs = _in_proj_bwd_x(du, w_full, x_l, norm_g[l:l + 1], dx, pending.job())
        pending.absorb(recvs)
        small[l] = (d_norm, pg, dw31, dwa, dwx)
    grad_x = jnp.reshape(dx, x.shape)

    stack = lambda f: jnp.stack([f(small[l]) for l in range(n_layers)])
    pg_all = stack(lambda t: t[1])
    rep_parts = [
        (stack(lambda t: t[0][0]), norm_g.shape), (pg_all[:, PG_B31], conv_dw_b.shape),
        (pg_all[:, PG_LNG], conv_ln_g.shape), (pg_all[:, PG_LNB], conv_ln_b.shape),
        (pg_all[:, PG_B4], lru_conv_b.shape), (stack(lambda t: t[3]), lru_wa.shape), (pg_all[:, PG_BA], lru_ba.shape),
        (stack(lambda t: t[4]), lru_wx.shape), (pg_all[:, PG_BX], lru_bx.shape), (pg_all[:, PG_LAM], lru_lambda.shape),
        (d_final_g, final_g.shape)]
    shard_parts = [(stack(lambda t: t[2][0:KW]), (n_layers, KW, D_BR)),
                   (pg_all[:, PG_W4:PG_W4 + KW4], (n_layers, KW4, D_BR))]
    gate_w = (5, 7)
    f32_parts = [p for i, p in enumerate(rep_parts) if i not in gate_w] + shard_parts
    bf16_parts = [rep_parts[i] for i in gate_w]
    red_a, red_b, recvs = _all_reduce_small(
        _pack_rows([p for p, _ in f32_parts]), _pack_rows([p for p, _ in bf16_parts]), pending.job())
    pending.absorb(recvs)
    reduced_big[0] = [b[0] for b in pending.bufs]
    grad_w_in = jnp.stack([r[0] for r in reduced_big])
    grad_w_out = jnp.stack([r[1] for r in reduced_big])
    red_a = _unpack_rows(red_a, [shp for _, shp in f32_parts])
    red_b = _unpack_rows(red_b, [shp for _, shp in bf16_parts])
    rep_grads = red_a[:len(rep_parts) - len(gate_w)]
    for i, g in zip(gate_w, red_b):
        rep_grads.insert(i, g)
    grad_dw = lax.dynamic_slice_in_dim(red_a[-2], dev * HD, HD, axis=2)
    grad_w4 = lax.dynamic_slice_in_dim(red_a[-1], dev * HD, HD, axis=2)

    def adam_nd(w, g, m, v, name):
        two_d = (-1, w.shape[-1])
        outs = _adamw(*(jnp.reshape(t, two_d) for t in (w, g, m, v)), name)
        return [jnp.reshape(o, w.shape) for o in outs]

    upd = {}
    upd["w_in"] = adam_nd(w_in, grad_w_in, m_w_in, v_w_in, "adamw_w_in")
    upd["w_out"] = adam_nd(w_out, grad_w_out, m_w_out, v_w_out, "adamw_w_out")
    upd["conv_dw_w"] = adam_nd(conv_dw_w, grad_dw, m_conv_dw_w, v_conv_dw_w, "adamw_conv_dw_w")
    upd["lru_conv_w"] = adam_nd(lru_conv_w, grad_w4, m_lru_conv_w, v_lru_conv_w, "adamw_lru_conv_w")
    rep_w = [norm_g, conv_dw_b, conv_ln_g, conv_ln_b, lru_conv_b, lru_wa, lru_ba, lru_wx, lru_bx, lru_lambda, final_g]
    rep_m = [m_norm_g, m_conv_dw_b, m_conv_ln_g, m_conv_ln_b, m_lru_conv_b, m_lru_wa, m_lru_ba, m_lru_wx, m_lru_bx,
             m_lru_lambda, m_final_g]
    rep_v = [v_norm_g, v_conv_dw_b, v_conv_ln_g, v_conv_ln_b, v_lru_conv_b, v_lru_wa, v_lru_ba, v_lru_wx, v_lru_bx,
             v_lru_lambda, v_final_g]
    rep_keys = ["norm_g", "conv_dw_b", "conv_ln_g", "conv_ln_b", "lru_conv_b", "lru_wa", "lru_ba", "lru_wx", "lru_bx",
                "lru_lambda", "final_g"]
    grads = {"w_in": grad_w_in, "w_out": grad_w_out, "conv_dw_w": grad_dw, "lru_conv_w": grad_w4}
    for i, key in enumerate(rep_keys):
        grads[key] = rep_grads[i]
        upd[key] = adam_nd(rep_w[i], rep_grads[i], rep_m[i], rep_v[i], "adamw_" + key)

    order = ["norm_g", "w_in", "conv_dw_w", "conv_dw_b", "conv_ln_g", "conv_ln_b", "lru_conv_w", "lru_conv_b", "lru_wa",
             "lru_ba", "lru_wx", "lru_bx", "lru_lambda", "w_out", "final_g"]
    return (loss, grad_x, *[grads[k] for k in order], *[upd[k][0] for k in order], *[upd[k][1] for k in order],
            *[upd[k][2] for k in order])
```
